```python
import math
import jax, jax.numpy as jnp
from jax import lax
import numpy as np


D_MODEL = 1024
BATCH = 1
SEQ = 16384
DEPTH = 1

ATTN_HEADS = 8
ATTN_HEAD_DIM = 128
ATTN_WIDTH = ATTN_HEADS * ATTN_HEAD_DIM
MOBA_BLOCK = 256
MOBA_TOPK = 3
Q_CHUNK = 128
REL_BUCKETS = 32
REL_MAX_DIST = 128
SSM_EXPAND = 2
SSM_INNER = SSM_EXPAND * D_MODEL
SSM_HEAD_DIM = 64
SSM_HEADS = SSM_INNER // SSM_HEAD_DIM
SSM_GROUPS = 8
SSM_HEADS_PER_GROUP = SSM_HEADS // SSM_GROUPS
SSM_STATE = 128
SSM_CONV = 4
SSM_CHUNK = 256
SSM_CONV_DIM = SSM_INNER + 2 * SSM_GROUPS * SSM_STATE
FFN_HIDDEN = 2816
FFN_RES = 0.5
N_BRANCH = 2
IN_PROJ_DIM = 3 * ATTN_WIDTH + SSM_INNER + SSM_CONV_DIM + SSM_HEADS + N_BRANCH * D_MODEL
N_MOD = 9
RMS_EPS = 1e-6

kernel_name = 'hybrid_moba_ssd_macaron_block'


def rms_norm(x, g):
    xf = x.astype(jnp.float32)
    y = xf * lax.rsqrt(jnp.mean(xf * xf, axis=-1, keepdims=True) + RMS_EPS)
    return (y * g.astype(jnp.float32)).astype(x.dtype)


def modulate(h, shift, scale):
    return h * (1 + scale[:, None, :]) + shift[:, None, :]


def swiglu(h, w_in, w_out):
    a, b = jnp.split(h @ w_in, 2, axis=-1)
    return (jax.nn.silu(a) * b) @ w_out


def t5_bucket(rel):
    n = jnp.maximum(rel, 0)
    max_exact = REL_BUCKETS // 2
    nf = jnp.maximum(n, 1).astype(jnp.float32)
    large = max_exact + (jnp.log(nf / max_exact) / math.log(REL_MAX_DIST / max_exact)
                         * (REL_BUCKETS - max_exact)).astype(jnp.int32)
    large = jnp.minimum(large, REL_BUCKETS - 1)
    return jnp.where(n < max_exact, n, large)


def moba_attention(q, k, v, rel_bias):
    Bsz, S, H, Dh = q.shape
    nb = -(-S // MOBA_BLOCK)
    pad = nb * MOBA_BLOCK - S
    k = jnp.pad(k, ((0, 0), (0, pad), (0, 0), (0, 0)))
    v = jnp.pad(v, ((0, 0), (0, pad), (0, 0), (0, 0)))
    kb = k.transpose(0, 2, 1, 3).reshape(Bsz, H, nb, MOBA_BLOCK, Dh)
    vb = v.transpose(0, 2, 1, 3).reshape(Bsz, H, nb, MOBA_BLOCK, Dh)
    q = q.transpose(0, 2, 1, 3) * (Dh ** -0.5)
    kmean = jnp.mean(kb.astype(jnp.float32), axis=3)
    score = jnp.einsum('bhsd,bhnd->bhsn', q.astype(jnp.float32), kmean)
    cur = jnp.arange(S) // MOBA_BLOCK
    past = jnp.arange(nb)[None, :] < cur[:, None]
    score = jnp.where(past, score, -jnp.inf)
    topk = min(MOBA_TOPK, nb)
    _, idx = lax.top_k(score, topk)

    tab = rel_bias.T.astype(jnp.float32)
    nqc = S // Q_CHUNK
    q_c = q.reshape(Bsz, H, nqc, Q_CHUNK, Dh).transpose(2, 0, 1, 3, 4)
    idx_c = idx.reshape(Bsz, H, nqc, Q_CHUNK, topk).transpose(2, 0, 1, 3, 4)
    starts = jnp.arange(nqc, dtype=jnp.int32) * Q_CHUNK
    b_ix = jnp.arange(Bsz)[:, None, None, None]
    h_ix = jnp.arange(H)[None, :, None, None]
    offs = jnp.arange(MOBA_BLOCK, dtype=jnp.int32)
    nsel = topk * MOBA_BLOCK

    def one_chunk(args):
        qq, ii, start = args
        blk = start // MOBA_BLOCK
        q_pos = start + jnp.arange(Q_CHUNK, dtype=jnp.int32)
        kg = kb[b_ix, h_ix, ii]
        vg = vb[b_ix, h_ix, ii]
        lg = jnp.einsum('bhqd,bhqjkd->bhqjk', qq, kg).astype(jnp.float32)
        kpos_g = ii[..., None] * MOBA_BLOCK + offs
        rel_g = q_pos[None, None, :, None, None] - kpos_g
        lg = lg + tab[h_ix[..., None], t5_bucket(rel_g)]
        lg = jnp.where((ii < blk)[..., None], lg, -jnp.inf)
        ko = lax.dynamic_index_in_dim(kb, blk, axis=2, keepdims=False)
        vo = lax.dynamic_index_in_dim(vb, blk, axis=2, keepdims=False)
        lo = jnp.einsum('bhqd,bhkd->bhqk', qq, ko).astype(jnp.float32)
        rel_o = q_pos[:, None] - (blk * MOBA_BLOCK + offs)[None, :]
        lo = lo + tab[:, t5_bucket(rel_o)][None]
        lo = jnp.where(rel_o >= 0, lo, -jnp.inf)
        logits = jnp.concatenate([lg.reshape(Bsz, H, Q_CHUNK, nsel), lo], axis=-1)
        p = jax.nn.softmax(logits, axis=-1).astype(vb.dtype)
        pg = p[..., :nsel].reshape(Bsz, H, Q_CHUNK, topk, MOBA_BLOCK)
        po = p[..., nsel:]
        return (jnp.einsum('bhqjk,bhqjkd->bhqd', pg, vg)
                + jnp.einsum('bhqk,bhkd->bhqd', po, vo))

    out = lax.map(one_chunk, (q_c, idx_c, starts))
    return out.transpose(1, 0, 3, 2, 4).reshape(Bsz, S, H * Dh)


def mamba2_ssd(z, xbc, dt_raw, conv_w, conv_b, dt_bias, a_log, d_skip, norm_w):
    Bsz, S, _ = xbc.shape
    G, HG, P, N, L = SSM_GROUPS, SSM_HEADS_PER_GROUP, SSM_HEAD_DIM, SSM_STATE, SSM_CHUNK
    xbc = lax.conv_general_dilated(xbc, conv_w[:, None, :], window_strides=(1,),
                                   padding=[(SSM_CONV - 1, 0)],
                                   dimension_numbers=('NWC', 'WIO', 'NWC'),
                                   feature_group_count=SSM_CONV_DIM) + conv_b
    xbc = jax.nn.silu(xbc)
    xs, bm, cm = jnp.split(xbc, [SSM_INNER, SSM_INNER + G * N], axis=-1)
    dt = jax.nn.softplus(dt_raw.astype(jnp.float32) + dt_bias.astype(jnp.float32))
    a = -jnp.exp(a_log.astype(jnp.float32)).reshape(G, HG)
    nc = -(-S // L)
    pad = nc * L - S

    def to_chunks(t):
        t = jnp.pad(t, [(0, 0), (0, pad)] + [(0, 0)] * (t.ndim - 2))
        return t.reshape((Bsz, nc, L) + t.shape[2:]).swapaxes(0, 1)

    xf = xs.astype(jnp.float32)
    xc = to_chunks(xf.reshape(Bsz, S, G, HG, P))
    bc = to_chunks(bm.astype(jnp.float32).reshape(Bsz, S, G, N))
    cc = to_chunks(cm.astype(jnp.float32).reshape(Bsz, S, G, N))
    dtc = to_chunks(dt.reshape(Bsz, S, G, HG))
    tril = jnp.tril(jnp.ones((L, L), dtype=bool))

    def step(state, inp):
        x_, b_, c_, dt_ = inp
        cum = jnp.cumsum(dt_ * a, axis=1)
        seg = cum[:, :, None] - cum[:, None, :]
        decay = jnp.exp(jnp.where(tril[None, :, :, None, None], seg, -jnp.inf))
        cb = jnp.einsum('btgn,bsgn->btsg', c_, b_)
        w = cb[..., None] * decay * dt_[:, None]
        y = jnp.einsum('btsgh,bsghp->btghp', w, x_)
        y = y + jnp.einsum('btgn,bghpn->btghp', c_, state) * jnp.exp(cum)[..., None]
        to_end = jnp.exp(cum[:, -1:] - cum) * dt_
        state = (jnp.exp(cum[:, -1])[..., None, None] * state
                 + jnp.einsum('bsgn,bsgh,bsghp->bghpn', b_, to_end, x_))
        return state, y

    state0 = jnp.zeros((Bsz, G, HG, P, N), jnp.float32)
    _, ys = lax.scan(step, state0, (xc, bc, cc, dtc))
    y = ys.swapaxes(0, 1).reshape(Bsz, nc * L, SSM_INNER)[:, :S]
    y = y + (xf.reshape(Bsz, S, SSM_HEADS, P)
             * d_skip.astype(jnp.float32)[:, None]).reshape(Bsz, S, SSM_INNER)
    g = (y * jax.nn.silu(z.astype(jnp.float32))).reshape(Bsz, S, G, SSM_INNER // G)
    g = g * lax.rsqrt(jnp.mean(g * g, axis=-1, keepdims=True) + RMS_EPS)
    return (g.reshape(Bsz, S, SSM_INNER) * norm_w.astype(jnp.float32)).astype(z.dtype)


def hybrid_mixer(u, w_in, rel_bias, conv_w, conv_b, dt_bias, a_log, d_skip, ssm_norm_w,
                 proj_a, proj_b, w_out):
    Bsz, S, _ = u.shape
    sizes = (ATTN_WIDTH, ATTN_WIDTH, ATTN_WIDTH, SSM_INNER, SSM_CONV_DIM, SSM_HEADS, D_MODEL, D_MODEL)
    cuts = np.cumsum(sizes)[:-1].tolist()
    q, k, v, z, xbc, dt_raw, gate_a, gate_b = jnp.split(u @ w_in, cuts, axis=-1)
    heads = lambda t: t.reshape(Bsz, S, ATTN_HEADS, ATTN_HEAD_DIM)
    y_a = moba_attention(heads(q), heads(k), heads(v), rel_bias) @ proj_a
    y_b = mamba2_ssd(z, xbc, dt_raw, conv_w, conv_b, dt_bias, a_log, d_skip,
                     ssm_norm_w) @ proj_b
    s_a = jax.nn.sigmoid(gate_a.astype(jnp.float32)).astype(u.dtype)
    s_b = jax.nn.sigmoid(gate_b.astype(jnp.float32)).astype(u.dtype)
    return (s_a * y_a + s_b * y_b) @ w_out


def setup_inputs(seed: int = 0) -> dict:
    key = jax.random.key(seed)
    ks = jax.random.split(key, 26)
    f32 = jnp.float32
    Ld = DEPTH
    nrm = lambda k, shape, s: jax.random.normal(k, shape, f32) * s
    gain = lambda k, shape: 1.0 + 0.05 * jax.random.normal(k, shape, f32)
    dt0 = jnp.exp(jax.random.uniform(ks[13], (Ld, SSM_HEADS), f32, math.log(1e-3), math.log(1e-1)))
    return {
        'x': nrm(ks[0], (BATCH, SEQ, D_MODEL), 1.0),
        'c': nrm(ks[1], (BATCH, D_MODEL), 1.0),
        'w_ada': nrm(ks[2], (Ld, D_MODEL, N_MOD * D_MODEL), 0.5 * D_MODEL ** -0.5),
        'b_ada': nrm(ks[3], (Ld, N_MOD * D_MODEL), 0.02),
        'ffn1_norm_pre': gain(ks[4], (Ld, D_MODEL)),
        'ffn1_w_in': nrm(ks[5], (Ld, D_MODEL, 2 * FFN_HIDDEN), D_MODEL ** -0.5),
        'ffn1_w_out': nrm(ks[6], (Ld, FFN_HIDDEN, D_MODEL), FFN_HIDDEN ** -0.5),
        'ffn1_norm_post': gain(ks[7], (Ld, D_MODEL)),
        'mix_norm_pre': gain(ks[8], (Ld, D_MODEL)),
        'w_in_mix': nrm(ks[9], (Ld, D_MODEL, IN_PROJ_DIM), D_MODEL ** -0.5),
        'rel_bias': nrm(ks[10], (REL_BUCKETS, ATTN_HEADS), 0.5),
        'conv_w': nrm(ks[11], (Ld, SSM_CONV, SSM_CONV_DIM), SSM_CONV ** -0.5),
        'conv_b': nrm(ks[12], (Ld, SSM_CONV_DIM), 0.01),
        'dt_bias': dt0 + jnp.log(-jnp.expm1(-dt0)),
        'a_log': jnp.log(jax.random.uniform(ks[14], (Ld, SSM_HEADS), f32, 1.0, 16.0)),
        'd_skip': gain(ks[15], (Ld, SSM_HEADS)),
        'ssm_norm_w': gain(ks[16], (Ld, SSM_INNER)),
        'proj_a': nrm(ks[17], (Ld, ATTN_WIDTH, D_MODEL), ATTN_WIDTH ** -0.5),
        'proj_b': nrm(ks[18], (Ld, SSM_INNER, D_MODEL), SSM_INNER ** -0.5),
        'w_out_mix': nrm(ks[19], (Ld, D_MODEL, D_MODEL), D_MODEL ** -0.5),
        'mix_norm_post': gain(ks[20], (Ld, D_MODEL)),
        'ffn2_norm_pre': gain(ks[21], (Ld, D_MODEL)),
        'ffn2_w_in': nrm(ks[22], (Ld, D_MODEL, 2 * FFN_HIDDEN), D_MODEL ** -0.5),
        'ffn2_w_out': nrm(ks[23], (Ld, FFN_HIDDEN, D_MODEL), FFN_HIDDEN ** -0.5),
        'ffn2_norm_post': gain(ks[24], (Ld, D_MODEL)),
    }


def reference(x, c, w_ada, b_ada, ffn1_norm_pre, ffn1_w_in, ffn1_w_out, ffn1_norm_post,
              mix_norm_pre, w_in_mix, rel_bias, conv_w, conv_b, dt_bias, a_log, d_skip,
              ssm_norm_w, proj_a, proj_b, w_out_mix, mix_norm_post,
              ffn2_norm_pre, ffn2_w_in, ffn2_w_out, ffn2_norm_post):
    h = x
    cs = jax.nn.silu(c)
    for l in range(DEPTH):
        mod = cs @ w_ada[l] + b_ada[l]
        sh1, sc1, g1, shm, scm, gm, sh2, sc2, g2 = jnp.split(mod, N_MOD, axis=-1)
        u = modulate(rms_norm(h, ffn1_norm_pre[l]), sh1, sc1)
        h = h + FFN_RES * g1[:, None] * rms_norm(swiglu(u, ffn1_w_in[l], ffn1_w_out[l]), ffn1_norm_post[l])
        u = modulate(rms_norm(h, mix_norm_pre[l]), shm, scm)
        y = hybrid_mixer(u, w_in_mix[l], rel_bias, conv_w[l], conv_b[l], dt_bias[l], a_log[l],
                         d_skip[l], ssm_norm_w[l], proj_a[l], proj_b[l], w_out_mix[l])
        h = h + gm[:, None] * rms_norm(y, mix_norm_post[l])
        u = modulate(rms_norm(h, ffn2_norm_pre[l]), sh2, sc2)
        h = h + FFN_RES * g2[:, None] * rms_norm(swiglu(u, ffn2_w_in[l], ffn2_w_out[l]), ffn2_norm_post[l])
    return h
```

```python
import functools
import math

import jax
import jax.numpy as jnp
from jax import lax
from jax.experimental import pallas as pl
from jax.experimental.pallas import tpu as pltpu

F32 = jnp.float32
BF16 = jnp.bfloat16
HIGHEST = lax.Precision.HIGHEST

D_MODEL = 1024
N_MOD = 9
RMS_EPS = 1e-6
FFN_HIDDEN = 2816
FFN_RES = 0.5

ATTN_HEADS = 8
HEAD_DIM = 128
ATTN_WIDTH = ATTN_HEADS * HEAD_DIM
MOBA_BLOCK = 256
MOBA_TOPK = 3
MAX_BLOCKS = 128
AUG_DIM = HEAD_DIM + MAX_BLOCKS
REL_BUCKETS = 32
REL_MAX_DIST = 128
MASKED = -1e30

SSM_INNER = 2048
SSM_HEAD_DIM = 64
SSM_GROUPS = 8
SSM_HEADS = SSM_INNER // SSM_HEAD_DIM
SSM_HPG = SSM_HEADS // SSM_GROUPS
SSM_GROUP_W = SSM_INNER // SSM_GROUPS
SSM_STATE = 128
SSM_CONV = 4
SSM_CHUNK = 256
CONV_HALO = 8

COL_Q = 0
COL_K = COL_Q + ATTN_WIDTH
COL_V = COL_K + ATTN_WIDTH
COL_Z = COL_V + ATTN_WIDTH
COL_X = COL_Z + SSM_INNER
COL_B = COL_X + SSM_INNER
COL_C = COL_B + SSM_GROUPS * SSM_STATE
COL_GA = COL_C + SSM_GROUPS * SSM_STATE
COL_GB = COL_GA + D_MODEL
PROJ_W = COL_GB + D_MODEL
DT_PAD = 128

VMEM_LIMIT = 56 * 1024 * 1024


def _params(sem):
    return pltpu.CompilerParams(dimension_semantics=sem, vmem_limit_bytes=VMEM_LIMIT)


def _silu(x):
    return x / (1.0 + jnp.exp(-x))


def _sigmoid(x):
    return 1.0 / (1.0 + jnp.exp(-x))


def _softplus(x):
    return jnp.maximum(x, 0.0) + jnp.log(1.0 + jnp.exp(-jnp.abs(x)))


def _rms(x, g):
    return x * lax.rsqrt(jnp.mean(x * x, axis=-1, keepdims=True) + RMS_EPS) * g


def _dot(a, b, **kw):
    return jnp.dot(a, b, preferred_element_type=F32, **kw)


def _dot_nt(a, b, **kw):
    return lax.dot_general(a, b, (((1,), (1,)), ((), ())), preferred_element_type=F32, **kw)


def _dot_tn(a, b, **kw):
    return lax.dot_general(a, b, (((0,), (0,)), ((), ())), preferred_element_type=F32, **kw)


def _mod_kernel(c_ref, w_ref, b_ref, o_ref):
    cs = _silu(c_ref[...])
    o_ref[...] = _dot(cs, w_ref[...], precision=HIGHEST) + b_ref[...]


def _mod(c, w_ada, b_ada):
    n = w_ada.shape[1]
    tn = 1024
    c8 = jnp.broadcast_to(c, (8, D_MODEL))
    out = pl.pallas_call(
        _mod_kernel,
        out_shape=jax.ShapeDtypeStruct((8, n), F32),
        grid=(n // tn,),
        in_specs=[pl.BlockSpec((8, D_MODEL), lambda j: (0, 0)),
                  pl.BlockSpec((D_MODEL, tn), lambda j: (0, j)),
                  pl.BlockSpec((1, tn), lambda j: (0, j))],
        out_specs=pl.BlockSpec((8, tn), lambda j: (0, j)),
        compiler_params=_params(("arbitrary",)),
        name="mod",
    )(c8, w_ada, b_ada.reshape(1, n))
    return out[0].reshape(N_MOD, 1, D_MODEL)


def _ffn_kernel(h_ref, gpre_ref, sh_ref, sc_ref, gate_ref, gpost_ref, wa_ref, wb_ref, wo_ref,
                o_ref, u_sc, acc_sc):
    j = pl.program_id(1)

    @pl.when(j == 0)
    def _():
        u = _rms(h_ref[...], gpre_ref[...]) * (1.0 + sc_ref[...]) + sh_ref[...]
        u_sc[...] = u.astype(BF16)
        acc_sc[...] = jnp.zeros_like(acc_sc)

    u = u_sc[...]
    a = _dot(u, wa_ref[...])
    b = _dot(u, wb_ref[...])
    mid = (_silu(a) * b).astype(BF16)
    acc_sc[...] += _dot(mid, wo_ref[...])

    @pl.when(j == pl.num_programs(1) - 1)
    def _():
        y = _rms(acc_sc[...], gpost_ref[...])
        o_ref[...] = h_ref[...] + (FFN_RES * gate_ref[...]) * y


def _ffn(h, gpre, sh, sc, gate, gpost, wa, wb, wo, tm, tf):
    s = h.shape[0]
    row = lambda i, j: (i, 0)
    vec = pl.BlockSpec((1, D_MODEL), lambda i, j: (0, 0))
    return pl.pallas_call(
        _ffn_kernel,
        out_shape=jax.ShapeDtypeStruct((s, D_MODEL), F32),
        grid=(s // tm, FFN_HIDDEN // tf),
        in_specs=[pl.BlockSpec((tm, D_MODEL), row), vec, vec, vec, vec, vec,
                  pl.BlockSpec((D_MODEL, tf), lambda i, j: (0, j)),
                  pl.BlockSpec((D_MODEL, tf), lambda i, j: (0, j)),
                  pl.BlockSpec((tf, D_MODEL), lambda i, j: (j, 0))],
        out_specs=pl.BlockSpec((tm, D_MODEL), row),
        scratch_shapes=[pltpu.VMEM((tm, D_MODEL), BF16), pltpu.VMEM((tm, D_MODEL), F32)],
        compiler_params=_params(("parallel", "arbitrary")),
        name="ffn",
    )(h, gpre, sh, sc, gate, gpost, wa, wb, wo)


def _inproj_kernel(h_ref, gpre_ref, sh_ref, sc_ref, w_ref, wdt_ref, o_ref, dt_ref, u_sc):
    j = pl.program_id(1)

    @pl.when(j == 0)
    def _():
        u = _rms(h_ref[...], gpre_ref[...]) * (1.0 + sc_ref[...]) + sh_ref[...]
        u_sc[...] = u.astype(BF16)
        dt_ref[...] = _dot(u_sc[...], wdt_ref[...])

    o_ref[...] = _dot(u_sc[...], w_ref[...])


def _inproj(h, gpre, sh, sc, w, wdt, tm, tn):
    s = h.shape[0]
    vec = pl.BlockSpec((1, D_MODEL), lambda i, j: (0, 0))
    return pl.pallas_call(
        _inproj_kernel,
        out_shape=(jax.ShapeDtypeStruct((s, PROJ_W), F32),
                   jax.ShapeDtypeStruct((s, DT_PAD), F32)),
        grid=(s // tm, PROJ_W // tn),
        in_specs=[pl.BlockSpec((tm, D_MODEL), lambda i, j: (i, 0)), vec, vec, vec,
                  pl.BlockSpec((D_MODEL, tn), lambda i, j: (0, j)),
                  pl.BlockSpec((D_MODEL, DT_PAD), lambda i, j: (0, 0))],
        out_specs=(pl.BlockSpec((tm, tn), lambda i, j: (i, j)),
                   pl.BlockSpec((tm, DT_PAD), lambda i, j: (i, 0))),
        scratch_shapes=[pltpu.VMEM((tm, D_MODEL), BF16)],
        compiler_params=_params(("parallel", "arbitrary")),
        name="inproj",
    )(h, gpre, sh, sc, w, wdt)


def _kmean_kernel(k_ref, o_ref):
    n = pl.program_id(1)

    @pl.when(n == 0)
    def _():
        o_ref[...] = jnp.zeros_like(o_ref)

    o_ref[0, pl.ds(n, 1), :] = jnp.mean(k_ref[...], axis=0, keepdims=True)


def _kmean(proj, nb):
    return pl.pallas_call(
        _kmean_kernel,
        out_shape=jax.ShapeDtypeStruct((ATTN_HEADS, MAX_BLOCKS, HEAD_DIM), F32),
        grid=(ATTN_HEADS, nb),
        in_specs=[pl.BlockSpec((MOBA_BLOCK, HEAD_DIM), lambda h, n: (n, COL_K // HEAD_DIM + h))],
        out_specs=pl.BlockSpec((1, MAX_BLOCKS, HEAD_DIM), lambda h, n: (h, 0, 0)),
        compiler_params=_params(("parallel", "arbitrary")),
        name="kmean",
    )(proj)


def _prep_kernel(q_ref, k_ref, v_ref, km_ref, qa_ref, ka_ref, va_ref):
    i = pl.program_id(1)
    q = q_ref[...] * (HEAD_DIM ** -0.5)
    score = _dot_nt(q, km_ref[0], precision=HIGHEST)
    lane = lax.broadcasted_iota(jnp.int32, score.shape, 1)
    past = lane < i
    s = jnp.where(past, score, -jnp.inf)
    sel = jnp.zeros(score.shape, jnp.bool_)
    for _ in range(MOBA_TOPK):
        m = jnp.max(s, axis=-1, keepdims=True)
        first = jnp.min(jnp.where(s == m, lane, MAX_BLOCKS), axis=-1, keepdims=True)
        pick = lane == first
        sel = jnp.logical_or(sel, jnp.logical_and(pick, past))
        s = jnp.where(pick, -jnp.inf, s)
    qa_ref[0, :, :HEAD_DIM] = q.astype(BF16)
    qa_ref[0, :, HEAD_DIM:] = jnp.where(sel, 0.0, MASKED).astype(BF16)
    ka_ref[0, :, :HEAD_DIM] = k_ref[...].astype(BF16)
    ka_ref[0, :, HEAD_DIM:] = jnp.where(lane == i, 1.0, 0.0).astype(BF16)
    va_ref[0] = v_ref[...].astype(BF16)


def _prep(proj, kmean, nb):
    s = proj.shape[0]
    blk = lambda col: pl.BlockSpec((MOBA_BLOCK, HEAD_DIM), lambda h, i: (i, col // HEAD_DIM + h))
    return pl.pallas_call(
        _prep_kernel,
        out_shape=(jax.ShapeDtypeStruct((ATTN_HEADS, s, AUG_DIM), BF16),
                   jax.ShapeDtypeStruct((ATTN_HEADS, s, AUG_DIM), BF16),
                   jax.ShapeDtypeStruct((ATTN_HEADS, s, HEAD_DIM), BF16)),
        grid=(ATTN_HEADS, nb),
        in_specs=[blk(COL_Q), blk(COL_K), blk(COL_V),
                  pl.BlockSpec((1, MAX_BLOCKS, HEAD_DIM), lambda h, i: (h, 0, 0))],
        out_specs=(pl.BlockSpec((1, MOBA_BLOCK, AUG_DIM), lambda h, i: (h, i, 0)),
                   pl.BlockSpec((1, MOBA_BLOCK, AUG_DIM), lambda h, i: (h, i, 0)),
                   pl.BlockSpec((1, MOBA_BLOCK, HEAD_DIM), lambda h, i: (h, i, 0))),
        compiler_params=_params(("parallel", "parallel")),
        name="prep",
    )(proj, proj, proj, kmean)


def _t5_bucket(rel):
    n = jnp.maximum(rel, 0)
    max_exact = REL_BUCKETS // 2
    nf = jnp.maximum(n, 1).astype(F32)
    large = max_exact + (jnp.log(nf / max_exact) / math.log(REL_MAX_DIST / max_exact)
                         * (REL_BUCKETS - max_exact)).astype(jnp.int32)
    large = jnp.minimum(large, REL_BUCKETS - 1)
    return jnp.where(n < max_exact, n, large)


def _bias_kernel(tab_ref, o_ref):
    h = pl.program_id(0)
    shape = (MOBA_BLOCK, MOBA_BLOCK)
    qi = lax.broadcasted_iota(jnp.int32, shape, 0)
    ki = lax.broadcasted_iota(jnp.int32, shape, 1)
    for which in range(2):
        rel = qi - ki + which * MOBA_BLOCK
        bucket = _t5_bucket(rel)
        val = jnp.zeros(shape, F32)
        for b in range(REL_BUCKETS):
            val = jnp.where(bucket == b, tab_ref[h * REL_BUCKETS + b], val)
        if which == 0:
            val = jnp.where(rel >= 0, val, MASKED)
        o_ref[0, which] = val


def _bias_tiles(tab_flat):
    return pl.pallas_call(
        _bias_kernel,
        out_shape=jax.ShapeDtypeStruct((ATTN_HEADS, 2, MOBA_BLOCK, MOBA_BLOCK), F32),
        grid=(ATTN_HEADS,),
        in_specs=[pl.BlockSpec(memory_space=pltpu.SMEM)],
        out_specs=pl.BlockSpec((1, 2, MOBA_BLOCK, MOBA_BLOCK), lambda h: (h, 0, 0, 0)),
        compiler_params=_params(("parallel",)),
        name="bias",
    )(tab_flat)


def _attn_kernel(tab_ref, qa_ref, ka_ref, va_ref, t_ref, o_ref, m_sc, l_sc, acc_sc):
    h = pl.program_id(0)
    i = pl.program_id(1)
    qa = qa_ref[0]

    def kv(n):
        start = pl.multiple_of(n * MOBA_BLOCK, MOBA_BLOCK)
        return ka_ref[0, pl.ds(start, MOBA_BLOCK), :], va_ref[0, pl.ds(start, MOBA_BLOCK), :]

    k0, v0 = kv(i)
    s = _dot_nt(qa[:, :HEAD_DIM], k0[:, :HEAD_DIM]) + t_ref[0, 0]
    m = jnp.max(s, axis=-1, keepdims=True)
    p = jnp.exp(s - m)
    m_sc[...] = m
    l_sc[...] = jnp.sum(p, axis=-1, keepdims=True)
    acc_sc[...] = _dot(p.astype(BF16), v0)

    def update(n, bias):
        kn, vn = kv(n)
        s = _dot_nt(qa, kn) + bias
        m_old = m_sc[...]
        m_new = jnp.maximum(m_old, jnp.max(s, axis=-1, keepdims=True))
        alpha = jnp.exp(m_old - m_new)
        p = jnp.exp(s - m_new)
        m_sc[...] = m_new
        l_sc[...] = alpha * l_sc[...] + jnp.sum(p, axis=-1, keepdims=True)
        acc_sc[...] = alpha * acc_sc[...] + _dot(p.astype(BF16), vn)

    @pl.when(i >= 1)
    def _():
        update(i - 1, t_ref[0, 1])

    far_bias = tab_ref[h * REL_BUCKETS + REL_BUCKETS - 1]

    def body(n, carry):
        update(n, far_bias)
        return carry

    lax.fori_loop(0, jnp.maximum(i - 1, 0), body, 0)
    o_ref[...] = (acc_sc[...] / l_sc[...]).astype(o_ref.dtype)


def _attn(tab_flat, qa, ka, va, tiles, nb):
    s = qa.shape[1]
    return pl.pallas_call(
        _attn_kernel,
        out_shape=jax.ShapeDtypeStruct((s, ATTN_WIDTH), BF16),
        grid=(ATTN_HEADS, nb),
        in_specs=[pl.BlockSpec(memory_space=pltpu.SMEM),
                  pl.BlockSpec((1, MOBA_BLOCK, AUG_DIM), lambda h, i: (h, i, 0)),
                  pl.BlockSpec((1, s, AUG_DIM), lambda h, i: (h, 0, 0)),
                  pl.BlockSpec((1, s, HEAD_DIM), lambda h, i: (h, 0, 0)),
                  pl.BlockSpec((1, 2, MOBA_BLOCK, MOBA_BLOCK), lambda h, i: (h, 0, 0, 0))],
        out_specs=pl.BlockSpec((MOBA_BLOCK, HEAD_DIM), lambda h, i: (i, h)),
        scratch_shapes=[pltpu.VMEM((MOBA_BLOCK, 1), F32), pltpu.VMEM((MOBA_BLOCK, 1), F32),
                        pltpu.VMEM((MOBA_BLOCK, HEAD_DIM), F32)],
        compiler_params=_params(("parallel", "arbitrary")),
        name="attn",
    )(tab_flat, qa, ka, va, tiles)


def _expand_heads(d, lane_head):
    out = d[:, SSM_HPG - 1:SSM_HPG]
    for hg in range(SSM_HPG - 2, -1, -1):
        out = jnp.where(lane_head == hg, d[:, hg:hg + 1], out)
    return out


def _ssd_kernel(z_ref, x_ref, b_ref, c_ref, dtr_ref, dtc_ref,
                wx_ref, wb_ref, wc_ref, bx_ref, bb_ref, bc_ref,
                dtb_r_ref, alog_r_ref, dtb_c_ref, alog_c_ref, dskip_ref, nw_ref,
                o_ref, xpx_sc, xpb_sc, xpc_sc, st_sc):
    c = pl.program_id(1)
    L = SSM_CHUNK

    @pl.when(c == 0)
    def _():
        xpx_sc[0:CONV_HALO, :] = jnp.zeros((CONV_HALO, xpx_sc.shape[1]), F32)
        xpb_sc[0:CONV_HALO, :] = jnp.zeros((CONV_HALO, xpb_sc.shape[1]), F32)
        xpc_sc[0:CONV_HALO, :] = jnp.zeros((CONV_HALO, xpc_sc.shape[1]), F32)
        st_sc[...] = jnp.zeros_like(st_sc)

    def conv(src_ref, pad_sc, w_ref, bias_ref):
        pad_sc[CONV_HALO:CONV_HALO + L, :] = src_ref[...]
        acc = bias_ref[...]
        for j in range(SSM_CONV):
            lo = CONV_HALO - (SSM_CONV - 1) + j
            acc = acc + pad_sc[lo:lo + L, :] * w_ref[j:j + 1, :]
        pad_sc[0:CONV_HALO, :] = pad_sc[L:L + CONV_HALO, :]
        return _silu(acc)

    x = conv(x_ref, xpx_sc, wx_ref, bx_ref)
    bm = conv(b_ref, xpb_sc, wb_ref, bb_ref)
    cm = conv(c_ref, xpc_sc, wc_ref, bc_ref)

    dt_r = _softplus(dtr_ref[0] + dtb_r_ref[0])
    dt_c = _softplus(dtc_ref[0] + dtb_c_ref[0])
    a_r = -jnp.exp(alog_r_ref[0])
    a_c = -jnp.exp(alog_c_ref[0])
    row = lax.broadcasted_iota(jnp.int32, (L, L), 0)
    col = lax.broadcasted_iota(jnp.int32, (L, L), 1)
    causal = row >= col
    tril = jnp.where(causal, 1.0, 0.0).astype(F32)
    triu = jnp.where(row <= col, 1.0, 0.0).astype(F32)
    cum_r = _dot(tril, dt_r * a_r, precision=HIGHEST)
    cum_c = _dot(dt_c * a_c, triu, precision=HIGHEST)
    cum_last = cum_r[L - 1:L, :]

    lane_head = lax.broadcasted_iota(jnp.int32, (1, SSM_GROUP_W), 1) // SSM_HEAD_DIM
    dt_x = _expand_heads(dt_r, lane_head)
    grow_x = _expand_heads(jnp.exp(cum_r), lane_head)
    end_x = _expand_heads(jnp.exp(cum_last - cum_r) * dt_r, lane_head)
    last_x = _expand_heads(jnp.exp(cum_last), lane_head)

    cb = _dot_nt(cm.astype(BF16), bm.astype(BF16))
    xdt = x * dt_x
    w_parts, x_parts = [], []
    for hg in range(SSM_HPG):
        seg = cum_r[:, hg:hg + 1] - cum_c[hg:hg + 1, :]
        decay = jnp.exp(jnp.where(causal, seg, -jnp.inf))
        w_parts.append((cb * decay).astype(BF16))
        x_parts.append(jnp.where(lane_head == hg, xdt, 0.0).astype(BF16))
    y = _dot(jnp.concatenate(w_parts, axis=1), jnp.concatenate(x_parts, axis=0))

    st = st_sc[...]
    y = y + _dot(cm.astype(BF16), st.astype(BF16)) * grow_x
    st_sc[...] = last_x * st + _dot_tn(bm.astype(BF16), (x * end_x).astype(BF16))
    y = y + x * dskip_ref[0]

    g = y * _silu(z_ref[...])
    g = g * lax.rsqrt(jnp.mean(g * g, axis=-1, keepdims=True) + RMS_EPS)
    o_ref[...] = (g * nw_ref[0]).astype(o_ref.dtype)


def _ssd(proj, dt_rows, dt_cols, conv_w, conv_b, dtb_r, alog_r, dtb_c, alog_c, dskip_x, norm_w, nc):
    s = proj.shape[0]
    L, GW, NS = SSM_CHUNK, SSM_GROUP_W, SSM_STATE
    xoff, boff, coff = 0, SSM_INNER, SSM_INNER + SSM_GROUPS * NS
    return pl.pallas_call(
        _ssd_kernel,
        out_shape=jax.ShapeDtypeStruct((s, SSM_INNER), BF16),
        grid=(SSM_GROUPS, nc),
        in_specs=[
            pl.BlockSpec((L, GW), lambda g, c: (c, COL_Z // GW + g)),
            pl.BlockSpec((L, GW), lambda g, c: (c, COL_X // GW + g)),
            pl.BlockSpec((L, NS), lambda g, c: (c, COL_B // NS + g)),
            pl.BlockSpec((L, NS), lambda g, c: (c, COL_C // NS + g)),
            pl.BlockSpec((1, L, SSM_HPG), lambda g, c: (g, c, 0)),
            pl.BlockSpec((1, SSM_HPG, L), lambda g, c: (g, 0, c)),
            pl.BlockSpec((SSM_CONV, GW), lambda g, c: (0, xoff // GW + g)),
            pl.BlockSpec((SSM_CONV, NS), lambda g, c: (0, boff // NS + g)),
            pl.BlockSpec((SSM_CONV, NS), lambda g, c: (0, coff // NS + g)),
            pl.BlockSpec((1, GW), lambda g, c: (0, xoff // GW + g)),
            pl.BlockSpec((1, NS), lambda g, c: (0, boff // NS + g)),
            pl.BlockSpec((1, NS), lambda g, c: (0, coff // NS + g)),
            pl.BlockSpec((1, 1, SSM_HPG), lambda g, c: (g, 0, 0)),
            pl.BlockSpec((1, 1, SSM_HPG), lambda g, c: (g, 0, 0)),
            pl.BlockSpec((1, SSM_HPG, 1), lambda g, c: (g, 0, 0)),
            pl.BlockSpec((1, SSM_HPG, 1), lambda g, c: (g, 0, 0)),
            pl.BlockSpec((1, 1, GW), lambda g, c: (g, 0, 0)),
            pl.BlockSpec((1, 1, GW), lambda g, c: (g, 0, 0)),
        ],
        out_specs=pl.BlockSpec((L, GW), lambda g, c: (c, g)),
        scratch_shapes=[pltpu.VMEM((L + CONV_HALO, GW), F32), pltpu.VMEM((L + CONV_HALO, NS), F32),
                        pltpu.VMEM((L + CONV_HALO, NS), F32), pltpu.VMEM((NS, GW), F32)],
        compiler_params=_params(("parallel", "arbitrary")),
        name="ssd",
    )(proj, proj, proj, proj, dt_rows, dt_cols, conv_w, conv_w, conv_w, conv_b, conv_b, conv_b,
      dtb_r, alog_r, dtb_c, alog_c, dskip_x, norm_w)


def _merge_kernel(h_ref, a_ref, b_ref, ga_ref, gb_ref, pa_ref, pb_ref, wo_ref, gate_ref, gpost_ref,
                  o_ref):
    ya = _dot(a_ref[...], pa_ref[...])
    yb = _dot(b_ref[...], pb_ref[...])
    mix = _sigmoid(ga_ref[...]) * ya + _sigmoid(gb_ref[...]) * yb
    y = _dot(mix.astype(BF16), wo_ref[...])
    o_ref[...] = h_ref[...] + gate_ref[...] * _rms(y, gpost_ref[...])


def _merge(h, attn, ssd, proj, pa, pb, wo, gate, gpost, tm):
    s = h.shape[0]
    vec = pl.BlockSpec((1, D_MODEL), lambda i: (0, 0))
    full = lambda a: pl.BlockSpec(a.shape, lambda i: (0, 0))
    return pl.pallas_call(
        _merge_kernel,
        out_shape=jax.ShapeDtypeStruct((s, D_MODEL), F32),
        grid=(s // tm,),
        in_specs=[pl.BlockSpec((tm, D_MODEL), lambda i: (i, 0)),
                  pl.BlockSpec((tm, ATTN_WIDTH), lambda i: (i, 0)),
                  pl.BlockSpec((tm, SSM_INNER), lambda i: (i, 0)),
                  pl.BlockSpec((tm, D_MODEL), lambda i: (i, COL_GA // D_MODEL)),
                  pl.BlockSpec((tm, D_MODEL), lambda i: (i, COL_GB // D_MODEL)),
                  full(pa), full(pb), full(wo), vec, vec],
        out_specs=pl.BlockSpec((tm, D_MODEL), lambda i: (i, 0)),
        compiler_params=_params(("parallel",)),
        name="merge",
    )(h, attn, ssd, proj, proj, pa, pb, wo, gate, gpost)


def _layer(h, mod, rel_bias, p):
    s = h.shape[0]
    nb = s // MOBA_BLOCK
    nc = s // SSM_CHUNK
    tm = min(512, s)
    sh1, sc1, g1, shm, scm, gm, sh2, sc2, g2 = [mod[k] for k in range(N_MOD)]
    vec = lambda a: a.reshape(1, -1)

    def ffn_weights(w_in, w_out):
        return (w_in[:, :FFN_HIDDEN].astype(BF16), w_in[:, FFN_HIDDEN:].astype(BF16),
                w_out.astype(BF16))

    h = _ffn(h, vec(p["ffn1_norm_pre"]), sh1, sc1, g1, vec(p["ffn1_norm_post"]),
             *ffn_weights(p["ffn1_w_in"], p["ffn1_w_out"]), tm=tm, tf=FFN_HIDDEN // 2)

    w_in = p["w_in_mix"]
    dt_lo = COL_GA
    w_main = jnp.concatenate([w_in[:, :dt_lo], w_in[:, dt_lo + SSM_HEADS:]], axis=1).astype(BF16)
    w_dt = jnp.pad(w_in[:, dt_lo:dt_lo + SSM_HEADS], ((0, 0), (0, DT_PAD - SSM_HEADS))).astype(BF16)
    proj, dt_raw = _inproj(h, vec(p["mix_norm_pre"]), shm, scm, w_main, w_dt, tm=tm, tn=1024)

    tab_flat = rel_bias.T.reshape(-1)
    kmean = _kmean(proj, nb)
    qa, ka, va = _prep(proj, kmean, nb)
    tiles = _bias_tiles(tab_flat)
    attn = _attn(tab_flat, qa, ka, va, tiles, nb)

    dt3 = dt_raw[:, :SSM_HEADS].reshape(s, SSM_GROUPS, SSM_HPG)
    per_group = lambda a: a.reshape(SSM_GROUPS, SSM_HPG)
    dskip_x = jnp.repeat(p["d_skip"], SSM_HEAD_DIM).reshape(SSM_GROUPS, 1, SSM_GROUP_W)
    ssd = _ssd(proj, dt3.transpose(1, 0, 2), dt3.transpose(1, 2, 0),
               p["conv_w"], vec(p["conv_b"]),
               per_group(p["dt_bias"])[:, None, :], per_group(p["a_log"])[:, None, :],
               per_group(p["dt_bias"])[:, :, None], per_group(p["a_log"])[:, :, None],
               dskip_x, p["ssm_norm_w"].reshape(SSM_GROUPS, 1, SSM_GROUP_W), nc)

    h = _merge(h, attn, ssd, proj, p["proj_a"].astype(BF16), p["proj_b"].astype(BF16),
               p["w_out_mix"].astype(BF16), gm, vec(p["mix_norm_post"]), tm=tm)

    h = _ffn(h, vec(p["ffn2_norm_pre"]), sh2, sc2, g2, vec(p["ffn2_norm_post"]),
             *ffn_weights(p["ffn2_w_in"], p["ffn2_w_out"]), tm=tm, tf=FFN_HIDDEN // 2)
    return h


_LAYER_KEYS = ("ffn1_norm_pre", "ffn1_w_in", "ffn1_w_out", "ffn1_norm_post", "mix_norm_pre",
               "w_in_mix", "conv_w", "conv_b", "dt_bias", "a_log", "d_skip", "ssm_norm_w",
               "proj_a", "proj_b", "w_out_mix", "mix_norm_post",
               "ffn2_norm_pre", "ffn2_w_in", "ffn2_w_out", "ffn2_norm_post")


def kernel(x, c, w_ada, b_ada, ffn1_norm_pre, ffn1_w_in, ffn1_w_out, ffn1_norm_post, mix_norm_pre,
           w_in_mix, rel_bias, conv_w, conv_b, dt_bias, a_log, d_skip, ssm_norm_w, proj_a, proj_b,
           w_out_mix, mix_norm_post, ffn2_norm_pre, ffn2_w_in, ffn2_w_out, ffn2_norm_post):
    stacked = dict(ffn1_norm_pre=ffn1_norm_pre, ffn1_w_in=ffn1_w_in, ffn1_w_out=ffn1_w_out,
                   ffn1_norm_post=ffn1_norm_post, mix_norm_pre=mix_norm_pre, w_in_mix=w_in_mix,
                   conv_w=conv_w, conv_b=conv_b, dt_bias=dt_bias, a_log=a_log, d_skip=d_skip,
                   ssm_norm_w=ssm_norm_w, proj_a=proj_a, proj_b=proj_b, w_out_mix=w_out_mix,
                   mix_norm_post=mix_norm_post, ffn2_norm_pre=ffn2_norm_pre, ffn2_w_in=ffn2_w_in,
                   ffn2_w_out=ffn2_w_out, ffn2_norm_post=ffn2_norm_post)
    batch, seq, _ = x.shape
    assert seq % MOBA_BLOCK == 0 and seq // MOBA_BLOCK <= MAX_BLOCKS and seq % SSM_CHUNK == 0
    depth = w_ada.shape[0]
    outs = []
    for b in range(batch):
        h = x[b]
        for l in range(depth):
            mod = _mod(c[b:b + 1], w_ada[l], b_ada[l])
            h = _layer(h, mod, rel_bias, {k: stacked[k][l] for k in _LAYER_KEYS})
        outs.append(h)
    return jnp.stack(outs)
```

```python
import functools
import math

import jax
import jax.numpy as jnp
from jax import lax
from jax.experimental import pallas as pl
from jax.experimental.pallas import tpu as pltpu

F32 = jnp.float32
BF16 = jnp.bfloat16
HIGHEST = lax.Precision.HIGHEST

D_MODEL = 1024
N_MOD = 9
RMS_EPS = 1e-6
FFN_HIDDEN = 2816
FFN_RES = 0.5

ATTN_HEADS = 8
HEAD_DIM = 128
ATTN_WIDTH = ATTN_HEADS * HEAD_DIM
MOBA_BLOCK = 256
MOBA_TOPK = 3
MAX_BLOCKS = 128
AUG_DIM = HEAD_DIM + MAX_BLOCKS
REL_BUCKETS = 32
REL_MAX_DIST = 128
MASKED = -1e30
LOG2E = math.log2(math.e)
ATTN_TILE = 512

SSM_INNER = 2048
SSM_HEAD_DIM = 64
SSM_GROUPS = 8
SSM_HEADS = SSM_INNER // SSM_HEAD_DIM
SSM_HPG = SSM_HEADS // SSM_GROUPS
SSM_GROUP_W = SSM_INNER // SSM_GROUPS
SSM_STATE = 128
SSM_CONV = 4
SSM_CHUNK = 256
CONV_HALO = 8

COL_Q = 0
COL_K = COL_Q + ATTN_WIDTH
COL_V = COL_K + ATTN_WIDTH
COL_Z = COL_V + ATTN_WIDTH
COL_X = COL_Z + SSM_INNER
COL_B = COL_X + SSM_INNER
COL_C = COL_B + SSM_GROUPS * SSM_STATE
COL_GA = COL_C + SSM_GROUPS * SSM_STATE
COL_GB = COL_GA + D_MODEL
PROJ_W = COL_GB + D_MODEL
DT_PAD = 128

VMEM_LIMIT = 56 * 1024 * 1024


def _params(sem):
    return pltpu.CompilerParams(dimension_semantics=sem, vmem_limit_bytes=VMEM_LIMIT)


def _silu(x):
    return x / (1.0 + jnp.exp(-x))


def _sigmoid(x):
    return 1.0 / (1.0 + jnp.exp(-x))


def _softplus(x):
    return jnp.maximum(x, 0.0) + jnp.log(1.0 + jnp.exp(-jnp.abs(x)))


def _rms(x, g):
    return x * lax.rsqrt(jnp.mean(x * x, axis=-1, keepdims=True) + RMS_EPS) * g


def _dot(a, b, **kw):
    return jnp.dot(a, b, preferred_element_type=F32, **kw)


def _dot_nt(a, b, **kw):
    return lax.dot_general(a, b, (((1,), (1,)), ((), ())), preferred_element_type=F32, **kw)


def _dot_tn(a, b, **kw):
    return lax.dot_general(a, b, (((0,), (0,)), ((), ())), preferred_element_type=F32, **kw)


def _mod_kernel(c_ref, w_ref, b_ref, o_ref):
    cs = _silu(c_ref[...])
    o_ref[...] = _dot(cs, w_ref[...], precision=HIGHEST) + b_ref[...]


def _mod(c, w_ada, b_ada):
    n = w_ada.shape[1]
    tn = 1024
    c8 = jnp.broadcast_to(c, (8, D_MODEL))
    out = pl.pallas_call(
        _mod_kernel,
        out_shape=jax.ShapeDtypeStruct((8, n), F32),
        grid=(n // tn,),
        in_specs=[pl.BlockSpec((8, D_MODEL), lambda j: (0, 0)),
                  pl.BlockSpec((D_MODEL, tn), lambda j: (0, j)),
                  pl.BlockSpec((1, tn), lambda j: (0, j))],
        out_specs=pl.BlockSpec((8, tn), lambda j: (0, j)),
        compiler_params=_params(("arbitrary",)),
        name="mod",
    )(c8, w_ada, b_ada.reshape(1, n))
    return out[0].reshape(N_MOD, 1, D_MODEL)


def _ffn_kernel(h_ref, gpre_ref, sh_ref, sc_ref, gate_ref, gpost_ref, wa_ref, wb_ref, wo_ref,
                o_ref, u_sc, acc_sc):
    j = pl.program_id(1)

    @pl.when(j == 0)
    def _():
        u = _rms(h_ref[...], gpre_ref[...]) * (1.0 + sc_ref[...]) + sh_ref[...]
        u_sc[...] = u.astype(BF16)
        acc_sc[...] = jnp.zeros_like(acc_sc)

    u = u_sc[...]
    a = _dot(u, wa_ref[...])
    b = _dot(u, wb_ref[...])
    mid = (_silu(a) * b).astype(BF16)
    acc_sc[...] += _dot(mid, wo_ref[...])

    @pl.when(j == pl.num_programs(1) - 1)
    def _():
        y = _rms(acc_sc[...], gpost_ref[...])
        o_ref[...] = h_ref[...] + (FFN_RES * gate_ref[...]) * y


def _ffn(h, gpre, sh, sc, gate, gpost, wa, wb, wo, tm, tf):
    s = h.shape[0]
    row = lambda i, j: (i, 0)
    vec = pl.BlockSpec((1, D_MODEL), lambda i, j: (0, 0))
    return pl.pallas_call(
        _ffn_kernel,
        out_shape=jax.ShapeDtypeStruct((s, D_MODEL), F32),
        grid=(s // tm, FFN_HIDDEN // tf),
        in_specs=[pl.BlockSpec((tm, D_MODEL), row), vec, vec, vec, vec, vec,
                  pl.BlockSpec((D_MODEL, tf), lambda i, j: (0, j)),
                  pl.BlockSpec((D_MODEL, tf), lambda i, j: (0, j)),
                  pl.BlockSpec((tf, D_MODEL), lambda i, j: (j, 0))],
        out_specs=pl.BlockSpec((tm, D_MODEL), row),
        scratch_shapes=[pltpu.VMEM((tm, D_MODEL), BF16), pltpu.VMEM((tm, D_MODEL), F32)],
        compiler_params=_params(("parallel", "arbitrary")),
        name="ffn",
    )(h, gpre, sh, sc, gate, gpost, wa, wb, wo)


def _inproj_kernel(h_ref, gpre_ref, sh_ref, sc_ref, w_ref, wdt_ref, o_ref, dt_ref, u_sc):
    j = pl.program_id(1)

    @pl.when(j == 0)
    def _():
        u = _rms(h_ref[...], gpre_ref[...]) * (1.0 + sc_ref[...]) + sh_ref[...]
        u_sc[...] = u.astype(BF16)
        dt_ref[...] = _dot(u_sc[...], wdt_ref[...])

    o_ref[...] = _dot(u_sc[...], w_ref[...])


def _inproj(h, gpre, sh, sc, w, wdt, tm, tn):
    s = h.shape[0]
    vec = pl.BlockSpec((1, D_MODEL), lambda i, j: (0, 0))
    return pl.pallas_call(
        _inproj_kernel,
        out_shape=(jax.ShapeDtypeStruct((s, PROJ_W), F32),
                   jax.ShapeDtypeStruct((s, DT_PAD), F32)),
        grid=(s // tm, PROJ_W // tn),
        in_specs=[pl.BlockSpec((tm, D_MODEL), lambda i, j: (i, 0)), vec, vec, vec,
                  pl.BlockSpec((D_MODEL, tn), lambda i, j: (0, j)),
                  pl.BlockSpec((D_MODEL, DT_PAD), lambda i, j: (0, 0))],
        out_specs=(pl.BlockSpec((tm, tn), lambda i, j: (i, j)),
                   pl.BlockSpec((tm, DT_PAD), lambda i, j: (i, 0))),
        scratch_shapes=[pltpu.VMEM((tm, D_MODEL), BF16)],
        compiler_params=_params(("parallel", "arbitrary")),
        name="inproj",
    )(h, gpre, sh, sc, w, wdt)


def _kmean_kernel(k_ref, o_ref):
    n = pl.program_id(1)

    @pl.when(n == 0)
    def _():
        o_ref[...] = jnp.zeros_like(o_ref)

    o_ref[0, pl.ds(n, 1), :] = jnp.mean(k_ref[...], axis=0, keepdims=True)


def _kmean(proj, nb):
    return pl.pallas_call(
        _kmean_kernel,
        out_shape=jax.ShapeDtypeStruct((ATTN_HEADS, MAX_BLOCKS, HEAD_DIM), F32),
        grid=(ATTN_HEADS, nb),
        in_specs=[pl.BlockSpec((MOBA_BLOCK, HEAD_DIM), lambda h, n: (n, COL_K // HEAD_DIM + h))],
        out_specs=pl.BlockSpec((1, MAX_BLOCKS, HEAD_DIM), lambda h, n: (h, 0, 0)),
        compiler_params=_params(("parallel", "arbitrary")),
        name="kmean",
    )(proj)


def _prep_kernel(q_ref, k_ref, v_ref, km_ref, qa_ref, ka_ref, va_ref):
    i = pl.program_id(1)
    q = q_ref[...] * (HEAD_DIM ** -0.5 * LOG2E)
    score = _dot_nt(q, km_ref[0], precision=HIGHEST)
    lane = lax.broadcasted_iota(jnp.int32, score.shape, 1)
    past = lane < i
    s = jnp.where(past, score, -jnp.inf)
    sel = lane == i
    for _ in range(MOBA_TOPK):
        m = jnp.max(s, axis=-1, keepdims=True)
        first = jnp.min(jnp.where(s == m, lane, MAX_BLOCKS), axis=-1, keepdims=True)
        pick = lane == first
        sel = jnp.logical_or(sel, jnp.logical_and(pick, past))
        s = jnp.where(pick, -jnp.inf, s)
    qa_ref[0, :, :HEAD_DIM] = q.astype(BF16)
    qa_ref[0, :, HEAD_DIM:] = jnp.where(sel, 0.0, MASKED).astype(BF16)
    ka_ref[0, :, :HEAD_DIM] = k_ref[...].astype(BF16)
    ka_ref[0, :, HEAD_DIM:] = jnp.where(lane == i, 1.0, 0.0).astype(BF16)
    va_ref[0] = v_ref[...].astype(BF16)


def _prep(proj, kmean, nb):
    s = proj.shape[0]
    blk = lambda col: pl.BlockSpec((MOBA_BLOCK, HEAD_DIM), lambda h, i: (i, col // HEAD_DIM + h))
    return pl.pallas_call(
        _prep_kernel,
        out_shape=(jax.ShapeDtypeStruct((ATTN_HEADS, s, AUG_DIM), BF16),
                   jax.ShapeDtypeStruct((ATTN_HEADS, s, AUG_DIM), BF16),
                   jax.ShapeDtypeStruct((ATTN_HEADS, s, HEAD_DIM), BF16)),
        grid=(ATTN_HEADS, nb),
        in_specs=[blk(COL_Q), blk(COL_K), blk(COL_V),
                  pl.BlockSpec((1, MAX_BLOCKS, HEAD_DIM), lambda h, i: (h, 0, 0))],
        out_specs=(pl.BlockSpec((1, MOBA_BLOCK, AUG_DIM), lambda h, i: (h, i, 0)),
                   pl.BlockSpec((1, MOBA_BLOCK, AUG_DIM), lambda h, i: (h, i, 0)),
                   pl.BlockSpec((1, MOBA_BLOCK, HEAD_DIM), lambda h, i: (h, i, 0))),
        compiler_params=_params(("parallel", "parallel")),
        name="prep",
    )(proj, proj, proj, kmean)


def _t5_bucket(rel):
    n = jnp.maximum(rel, 0)
    max_exact = REL_BUCKETS // 2
    nf = jnp.maximum(n, 1).astype(F32)
    large = max_exact + (jnp.log(nf / max_exact) / math.log(REL_MAX_DIST / max_exact)
                         * (REL_BUCKETS - max_exact)).astype(jnp.int32)
    large = jnp.minimum(large, REL_BUCKETS - 1)
    return jnp.where(n < max_exact, n, large)


def _bias_kernel(tab_ref, o_ref):
    h = pl.program_id(0)
    shape = (ATTN_TILE, ATTN_TILE)
    qi = lax.broadcasted_iota(jnp.int32, shape, 0)
    ki = lax.broadcasted_iota(jnp.int32, shape, 1)
    far = tab_ref[h * REL_BUCKETS + REL_BUCKETS - 1]
    for which in range(2):
        rel = qi - ki + which * ATTN_TILE
        bucket = _t5_bucket(rel)
        val = jnp.zeros(shape, F32)
        for b in range(REL_BUCKETS):
            val = jnp.where(bucket == b, tab_ref[h * REL_BUCKETS + b], val)
        val = (val - far) * LOG2E
        if which == 0:
            val = jnp.where(rel >= 0, val, MASKED)
        o_ref[0, which] = val


def _bias_tiles(tab_flat):
    return pl.pallas_call(
        _bias_kernel,
        out_shape=jax.ShapeDtypeStruct((ATTN_HEADS, 2, ATTN_TILE, ATTN_TILE), F32),
        grid=(ATTN_HEADS,),
        in_specs=[pl.BlockSpec(memory_space=pltpu.SMEM)],
        out_specs=pl.BlockSpec((1, 2, ATTN_TILE, ATTN_TILE), lambda h: (h, 0, 0, 0)),
        compiler_params=_params(("parallel",)),
        name="bias",
    )(tab_flat)


def _attn_kernel(qa_ref, ka_ref, va_ref, t_ref, o_ref, m_sc, acc_sc):
    t = pl.program_id(1)
    qa = qa_ref[0]
    ones = jnp.ones((ATTN_TILE, HEAD_DIM), BF16)

    def scores(j):
        start = pl.multiple_of(j * ATTN_TILE, ATTN_TILE)
        return _dot_nt(qa, ka_ref[0, pl.ds(start, ATTN_TILE), :])

    def pv(p, j):
        start = pl.multiple_of(j * ATTN_TILE, ATTN_TILE)
        v1 = jnp.concatenate([va_ref[0, pl.ds(start, ATTN_TILE), :], ones], axis=1)
        return _dot(p.astype(BF16), v1)

    m_sc[...] = jnp.full(m_sc.shape, 4.0 * MASKED, F32)
    acc_sc[...] = jnp.zeros_like(acc_sc)

    def update(s, j):
        m_old = m_sc[...]
        m_new = jnp.maximum(m_old, jnp.max(s, axis=-1, keepdims=True))
        m_sc[...] = m_new
        acc_sc[...] = jnp.exp2(m_old - m_new) * acc_sc[...] + pv(jnp.exp2(s - m_new), j)

    @pl.when(t >= 1)
    def _():
        def body(j, s_cur):
            s_next = scores(j + 1)
            update(s_cur, j)
            return s_next

        s_prev = lax.fori_loop(0, t - 1, body, scores(0))
        update(s_prev + t_ref[0, 1], t - 1)

    update(scores(t) + t_ref[0, 0], t)
    acc = acc_sc[...]
    o_ref[...] = (acc[:, :HEAD_DIM] / acc[:, HEAD_DIM:]).astype(o_ref.dtype)


def _attn(qa, ka, va, tiles):
    s = qa.shape[1]
    return pl.pallas_call(
        _attn_kernel,
        out_shape=jax.ShapeDtypeStruct((s, ATTN_WIDTH), BF16),
        grid=(ATTN_HEADS, s // ATTN_TILE),
        in_specs=[pl.BlockSpec((1, ATTN_TILE, AUG_DIM), lambda h, t: (h, t, 0)),
                  pl.BlockSpec((1, s, AUG_DIM), lambda h, t: (h, 0, 0)),
                  pl.BlockSpec((1, s, HEAD_DIM), lambda h, t: (h, 0, 0)),
                  pl.BlockSpec((1, 2, ATTN_TILE, ATTN_TILE), lambda h, t: (h, 0, 0, 0))],
        out_specs=pl.BlockSpec((ATTN_TILE, HEAD_DIM), lambda h, t: (t, h)),
        scratch_shapes=[pltpu.VMEM((ATTN_TILE, 1), F32),
                        pltpu.VMEM((ATTN_TILE, 2 * HEAD_DIM), F32)],
        compiler_params=_params(("parallel", "arbitrary")),
        name="attn",
    )(qa, ka, va, tiles)


def _expand_heads(d, lane_head):
    out = d[:, SSM_HPG - 1:SSM_HPG]
    for hg in range(SSM_HPG - 2, -1, -1):
        out = jnp.where(lane_head == hg, d[:, hg:hg + 1], out)
    return out


def _ssd_kernel(z_ref, x_ref, b_ref, c_ref, dtr_ref, dtc_ref,
                wx_ref, wb_ref, wc_ref, bx_ref, bb_ref, bc_ref,
                dtb_r_ref, alog_r_ref, dtb_c_ref, alog_c_ref, dskip_ref, nw_ref,
                o_ref, xpx_sc, xpb_sc, xpc_sc, st_sc):
    c = pl.program_id(1)
    L = SSM_CHUNK

    @pl.when(c == 0)
    def _():
        xpx_sc[0:CONV_HALO, :] = jnp.zeros((CONV_HALO, xpx_sc.shape[1]), F32)
        xpb_sc[0:CONV_HALO, :] = jnp.zeros((CONV_HALO, xpb_sc.shape[1]), F32)
        xpc_sc[0:CONV_HALO, :] = jnp.zeros((CONV_HALO, xpc_sc.shape[1]), F32)
        st_sc[...] = jnp.zeros_like(st_sc)

    def conv(src_ref, pad_sc, w_ref, bias_ref):
        pad_sc[CONV_HALO:CONV_HALO + L, :] = src_ref[...]
        acc = bias_ref[...]
        for j in range(SSM_CONV):
            lo = CONV_HALO - (SSM_CONV - 1) + j
            acc = acc + pad_sc[lo:lo + L, :] * w_ref[j:j + 1, :]
        pad_sc[0:CONV_HALO, :] = pad_sc[L:L + CONV_HALO, :]
        return _silu(acc)

    x = conv(x_ref, xpx_sc, wx_ref, bx_ref)
    bm = conv(b_ref, xpb_sc, wb_ref, bb_ref)
    cm = conv(c_ref, xpc_sc, wc_ref, bc_ref)

    dt_r = _softplus(dtr_ref[0] + dtb_r_ref[0])
    dt_c = _softplus(dtc_ref[0] + dtb_c_ref[0])
    a_r = -jnp.exp(alog_r_ref[0])
    a_c = -jnp.exp(alog_c_ref[0])
    row = lax.broadcasted_iota(jnp.int32, (L, L), 0)
    col = lax.broadcasted_iota(jnp.int32, (L, L), 1)
    causal = row >= col
    tril = jnp.where(causal, 1.0, 0.0).astype(F32)
    triu = jnp.where(row <= col, 1.0, 0.0).astype(F32)
    cum_r = _dot(tril, dt_r * a_r, precision=HIGHEST)
    cum_c = _dot(dt_c * a_c, triu, precision=HIGHEST)
    cum_last = cum_r[L - 1:L, :]

    lane_head = lax.broadcasted_iota(jnp.int32, (1, SSM_GROUP_W), 1) // SSM_HEAD_DIM
    dt_x = _expand_heads(dt_r, lane_head)
    grow_x = _expand_heads(jnp.exp(cum_r), lane_head)
    end_x = _expand_heads(jnp.exp(cum_last - cum_r) * dt_r, lane_head)
    last_x = _expand_heads(jnp.exp(cum_last), lane_head)

    cb = _dot_nt(cm.astype(BF16), bm.astype(BF16))
    xdt = x * dt_x
    w_parts, x_parts = [], []
    for hg in range(SSM_HPG):
        seg = cum_r[:, hg:hg + 1] - cum_c[hg:hg + 1, :]
        decay = jnp.exp(jnp.where(causal, seg, -jnp.inf))
        w_parts.append((cb * decay).astype(BF16))
        x_parts.append(jnp.where(lane_head == hg, xdt, 0.0).astype(BF16))
    y = _dot(jnp.concatenate(w_parts, axis=1), jnp.concatenate(x_parts, axis=0))

    st = st_sc[...]
    y = y + _dot(cm.astype(BF16), st.astype(BF16)) * grow_x
    st_sc[...] = last_x * st + _dot_tn(bm.astype(BF16), (x * end_x).astype(BF16))
    y = y + x * dskip_ref[0]

    g = y * _silu(z_ref[...])
    g = g * lax.rsqrt(jnp.mean(g * g, axis=-1, keepdims=True) + RMS_EPS)
    o_ref[...] = (g * nw_ref[0]).astype(o_ref.dtype)


def _ssd(proj, dt_rows, dt_cols, conv_w, conv_b, dtb_r, alog_r, dtb_c, alog_c, dskip_x, norm_w, nc):
    s = proj.shape[0]
    L, GW, NS = SSM_CHUNK, SSM_GROUP_W, SSM_STATE
    xoff, boff, coff = 0, SSM_INNER, SSM_INNER + SSM_GROUPS * NS
    return pl.pallas_call(
        _ssd_kernel,
        out_shape=jax.ShapeDtypeStruct((s, SSM_INNER), BF16),
        grid=(SSM_GROUPS, nc),
        in_specs=[
            pl.BlockSpec((L, GW), lambda g, c: (c, COL_Z // GW + g)),
            pl.BlockSpec((L, GW), lambda g, c: (c, COL_X // GW + g)),
            pl.BlockSpec((L, NS), lambda g, c: (c, COL_B // NS + g)),
            pl.BlockSpec((L, NS), lambda g, c: (c, COL_C // NS + g)),
            pl.BlockSpec((1, L, SSM_HPG), lambda g, c: (g, c, 0)),
            pl.BlockSpec((1, SSM_HPG, L), lambda g, c: (g, 0, c)),
            pl.BlockSpec((SSM_CONV, GW), lambda g, c: (0, xoff // GW + g)),
            pl.BlockSpec((SSM_CONV, NS), lambda g, c: (0, boff // NS + g)),
            pl.BlockSpec((SSM_CONV, NS), lambda g, c: (0, coff // NS + g)),
            pl.BlockSpec((1, GW), lambda g, c: (0, xoff // GW + g)),
            pl.BlockSpec((1, NS), lambda g, c: (0, boff // NS + g)),
            pl.BlockSpec((1, NS), lambda g, c: (0, coff // NS + g)),
            pl.BlockSpec((1, 1, SSM_HPG), lambda g, c: (g, 0, 0)),
            pl.BlockSpec((1, 1, SSM_HPG), lambda g, c: (g, 0, 0)),
            pl.BlockSpec((1, SSM_HPG, 1), lambda g, c: (g, 0, 0)),
            pl.BlockSpec((1, SSM_HPG, 1), lambda g, c: (g, 0, 0)),
            pl.BlockSpec((1, 1, GW), lambda g, c: (g, 0, 0)),
            pl.BlockSpec((1, 1, GW), lambda g, c: (g, 0, 0)),
        ],
        out_specs=pl.BlockSpec((L, GW), lambda g, c: (c, g)),
        scratch_shapes=[pltpu.VMEM((L + CONV_HALO, GW), F32), pltpu.VMEM((L + CONV_HALO, NS), F32),
                        pltpu.VMEM((L + CONV_HALO, NS), F32), pltpu.VMEM((NS, GW), F32)],
        compiler_params=_params(("parallel", "arbitrary")),
        name="ssd",
    )(proj, proj, proj, proj, dt_rows, dt_cols, conv_w, conv_w, conv_w, conv_b, conv_b, conv_b,
      dtb_r, alog_r, dtb_c, alog_c, dskip_x, norm_w)


def _merge_kernel(h_ref, a_ref, b_ref, ga_ref, gb_ref, pa_ref, pb_ref, wo_ref, gate_ref, gpost_ref,
                  o_ref):
    ya = _dot(a_ref[...], pa_ref[...])
    yb = _dot(b_ref[...], pb_ref[...])
    mix = _sigmoid(ga_ref[...]) * ya + _sigmoid(gb_ref[...]) * yb
    y = _dot(mix.astype(BF16), wo_ref[...])
    o_ref[...] = h_ref[...] + gate_ref[...] * _rms(y, gpost_ref[...])


def _merge(h, attn, ssd, proj, pa, pb, wo, gate, gpost, tm):
    s = h.shape[0]
    vec = pl.BlockSpec((1, D_MODEL), lambda i: (0, 0))
    full = lambda a: pl.BlockSpec(a.shape, lambda i: (0, 0))
    return pl.pallas_call(
        _merge_kernel,
        out_shape=jax.ShapeDtypeStruct((s, D_MODEL), F32),
        grid=(s // tm,),
        in_specs=[pl.BlockSpec((tm, D_MODEL), lambda i: (i, 0)),
                  pl.BlockSpec((tm, ATTN_WIDTH), lambda i: (i, 0)),
                  pl.BlockSpec((tm, SSM_INNER), lambda i: (i, 0)),
                  pl.BlockSpec((tm, D_MODEL), lambda i: (i, COL_GA // D_MODEL)),
                  pl.BlockSpec((tm, D_MODEL), lambda i: (i, COL_GB // D_MODEL)),
                  full(pa), full(pb), full(wo), vec, vec],
        out_specs=pl.BlockSpec((tm, D_MODEL), lambda i: (i, 0)),
        compiler_params=_params(("parallel",)),
        name="merge",
    )(h, attn, ssd, proj, proj, pa, pb, wo, gate, gpost)


def _layer(h, mod, rel_bias, p):
    s = h.shape[0]
    nb = s // MOBA_BLOCK
    nc = s // SSM_CHUNK
    tm = min(512, s)
    sh1, sc1, g1, shm, scm, gm, sh2, sc2, g2 = [mod[k] for k in range(N_MOD)]
    vec = lambda a: a.reshape(1, -1)

    def ffn_weights(w_in, w_out):
        return (w_in[:, :FFN_HIDDEN].astype(BF16), w_in[:, FFN_HIDDEN:].astype(BF16),
                w_out.astype(BF16))

    h = _ffn(h, vec(p["ffn1_norm_pre"]), sh1, sc1, g1, vec(p["ffn1_norm_post"]),
             *ffn_weights(p["ffn1_w_in"], p["ffn1_w_out"]), tm=tm, tf=FFN_HIDDEN // 2)

    w_in = p["w_in_mix"]
    dt_lo = COL_GA
    w_main = jnp.concatenate([w_in[:, :dt_lo], w_in[:, dt_lo + SSM_HEADS:]], axis=1).astype(BF16)
    w_dt = jnp.pad(w_in[:, dt_lo:dt_lo + SSM_HEADS], ((0, 0), (0, DT_PAD - SSM_HEADS))).astype(BF16)
    proj, dt_raw = _inproj(h, vec(p["mix_norm_pre"]), shm, scm, w_main, w_dt, tm=tm, tn=1024)

    tab_flat = rel_bias.T.reshape(-1)
    kmean = _kmean(proj, nb)
    qa, ka, va = _prep(proj, kmean, nb)
    tiles = _bias_tiles(tab_flat)
    attn = _attn(qa, ka, va, tiles)

    dt3 = dt_raw[:, :SSM_HEADS].reshape(s, SSM_GROUPS, SSM_HPG)
    per_group = lambda a: a.reshape(SSM_GROUPS, SSM_HPG)
    dskip_x = jnp.repeat(p["d_skip"], SSM_HEAD_DIM).reshape(SSM_GROUPS, 1, SSM_GROUP_W)
    ssd = _ssd(proj, dt3.transpose(1, 0, 2), dt3.transpose(1, 2, 0),
               p["conv_w"], vec(p["conv_b"]),
               per_group(p["dt_bias"])[:, None, :], per_group(p["a_log"])[:, None, :],
               per_group(p["dt_bias"])[:, :, None], per_group(p["a_log"])[:, :, None],
               dskip_x, p["ssm_norm_w"].reshape(SSM_GROUPS, 1, SSM_GROUP_W), nc)

    h = _merge(h, attn, ssd, proj, p["proj_a"].astype(BF16), p["proj_b"].astype(BF16),
               p["w_out_mix"].astype(BF16), gm, vec(p["mix_norm_post"]), tm=tm)

    h = _ffn(h, vec(p["ffn2_norm_pre"]), sh2, sc2, g2, vec(p["ffn2_norm_post"]),
             *ffn_weights(p["ffn2_w_in"], p["ffn2_w_out"]), tm=tm, tf=FFN_HIDDEN // 2)
    return h


_LAYER_KEYS = ("ffn1_norm_pre", "ffn1_w_in", "ffn1_w_out", "ffn1_norm_post", "mix_norm_pre",
               "w_in_mix", "conv_w", "conv_b", "dt_bias", "a_log", "d_skip", "ssm_norm_w",
               "proj_a", "proj_b", "w_out_mix", "mix_norm_post",
               "ffn2_norm_pre", "ffn2_w_in", "ffn2_w_out", "ffn2_norm_post")


def kernel(x, c, w_ada, b_ada, ffn1_norm_pre, ffn1_w_in, ffn1_w_out, ffn1_norm_post, mix_norm_pre,
           w_in_mix, rel_bias, conv_w, conv_b, dt_bias, a_log, d_skip, ssm_norm_w, proj_a, proj_b,
           w_out_mix, mix_norm_post, ffn2_norm_pre, ffn2_w_in, ffn2_w_out, ffn2_norm_post):
    stacked = dict(ffn1_norm_pre=ffn1_norm_pre, ffn1_w_in=ffn1_w_in, ffn1_w_out=ffn1_w_out,
                   ffn1_norm_post=ffn1_norm_post, mix_norm_pre=mix_norm_pre, w_in_mix=w_in_mix,
                   conv_w=conv_w, conv_b=conv_b, dt_bias=dt_bias, a_log=a_log, d_skip=d_skip,
                   ssm_norm_w=ssm_norm_w, proj_a=proj_a, proj_b=proj_b, w_out_mix=w_out_mix,
                   mix_norm_post=mix_norm_post, ffn2_norm_pre=ffn2_norm_pre, ffn2_w_in=ffn2_w_in,
                   ffn2_w_out=ffn2_w_out, ffn2_norm_post=ffn2_norm_post)
    batch, seq, _ = x.shape
    assert seq % ATTN_TILE == 0 and seq // MOBA_BLOCK <= MAX_BLOCKS and seq % SSM_CHUNK == 0
    depth = w_ada.shape[0]
    outs = []
    for b in range(batch):
        h = x[b]
        for l in range(depth):
            mod = _mod(c[b:b + 1], w_ada[l], b_ada[l])
            h = _layer(h, mod, rel_bias, {k: stacked[k][l] for k in _LAYER_KEYS})
        outs.append(h)
    return jnp.stack(outs)
```

```python
import functools
import math

import jax
import jax.numpy as jnp
from jax import lax
from jax.experimental import pallas as pl
from jax.experimental.pallas import tpu as pltpu

F32 = jnp.float32
BF16 = jnp.bfloat16
HIGHEST = lax.Precision.HIGHEST

D_MODEL = 1024
N_MOD = 9
RMS_EPS = 1e-6
FFN_HIDDEN = 2816
FFN_RES = 0.5

ATTN_HEADS = 8
HEAD_DIM = 128
ATTN_WIDTH = ATTN_HEADS * HEAD_DIM
MOBA_BLOCK = 256
MOBA_TOPK = 3
MAX_BLOCKS = 128
AUG_DIM = HEAD_DIM + MAX_BLOCKS
REL_BUCKETS = 32
REL_MAX_DIST = 128
MASKED = -1e30
LOG2E = math.log2(math.e)
ATTN_TILE = 512
BLOCKS_PER_TILE = ATTN_TILE // MOBA_BLOCK
V_ROWS = HEAD_DIM + 16
EXP_ROWS = 64

SSM_INNER = 2048
SSM_HEAD_DIM = 64
SSM_GROUPS = 8
SSM_HEADS = SSM_INNER // SSM_HEAD_DIM
SSM_HPG = SSM_HEADS // SSM_GROUPS
SSM_GROUP_W = SSM_INNER // SSM_GROUPS
SSM_STATE = 128
SSM_CONV = 4
SSM_CHUNK = 256
CONV_HALO = 8

COL_Q = 0
COL_K = COL_Q + ATTN_WIDTH
COL_V = COL_K + ATTN_WIDTH
COL_Z = COL_V + ATTN_WIDTH
COL_X = COL_Z + SSM_INNER
COL_B = COL_X + SSM_INNER
COL_C = COL_B + SSM_GROUPS * SSM_STATE
COL_GA = COL_C + SSM_GROUPS * SSM_STATE
COL_GB = COL_GA + D_MODEL
PROJ_W = COL_GB + D_MODEL
DT_PAD = 128

VMEM_LIMIT = 56 * 1024 * 1024


def _params(sem):
    return pltpu.CompilerParams(dimension_semantics=sem, vmem_limit_bytes=VMEM_LIMIT)


def _silu(x):
    return x / (1.0 + jnp.exp(-x))


def _sigmoid(x):
    return 1.0 / (1.0 + jnp.exp(-x))


def _softplus(x):
    return jnp.maximum(x, 0.0) + jnp.log(1.0 + jnp.exp(-jnp.abs(x)))


def _rms(x, g):
    return x * lax.rsqrt(jnp.mean(x * x, axis=-1, keepdims=True) + RMS_EPS) * g


def _dot(a, b, **kw):
    return jnp.dot(a, b, preferred_element_type=F32, **kw)


def _dot_nt(a, b, **kw):
    return lax.dot_general(a, b, (((1,), (1,)), ((), ())), preferred_element_type=F32, **kw)


def _dot_tn(a, b, **kw):
    return lax.dot_general(a, b, (((0,), (0,)), ((), ())), preferred_element_type=F32, **kw)


def _mod_kernel(c_ref, w_ref, b_ref, o_ref):
    cs = _silu(c_ref[...])
    o_ref[...] = _dot(cs, w_ref[...], precision=HIGHEST) + b_ref[...]


def _mod(c, w_ada, b_ada):
    n = w_ada.shape[1]
    tn = 1024
    c8 = jnp.broadcast_to(c, (8, D_MODEL))
    out = pl.pallas_call(
        _mod_kernel,
        out_shape=jax.ShapeDtypeStruct((8, n), F32),
        grid=(n // tn,),
        in_specs=[pl.BlockSpec((8, D_MODEL), lambda j: (0, 0)),
                  pl.BlockSpec((D_MODEL, tn), lambda j: (0, j)),
                  pl.BlockSpec((1, tn), lambda j: (0, j))],
        out_specs=pl.BlockSpec((8, tn), lambda j: (0, j)),
        compiler_params=_params(("arbitrary",)),
        name="mod",
    )(c8, w_ada, b_ada.reshape(1, n))
    return out[0].reshape(N_MOD, 1, D_MODEL)


def _ffn_kernel(h_ref, gpre_ref, sh_ref, sc_ref, gate_ref, gpost_ref, wa_ref, wb_ref, wo_ref,
                o_ref, u_sc, acc_sc):
    j = pl.program_id(1)

    @pl.when(j == 0)
    def _():
        u = _rms(h_ref[...], gpre_ref[...]) * (1.0 + sc_ref[...]) + sh_ref[...]
        u_sc[...] = u.astype(BF16)
        acc_sc[...] = jnp.zeros_like(acc_sc)

    u = u_sc[...]
    a = _dot(u, wa_ref[...])
    b = _dot(u, wb_ref[...])
    mid = (_silu(a) * b).astype(BF16)
    acc_sc[...] += _dot(mid, wo_ref[...])

    @pl.when(j == pl.num_programs(1) - 1)
    def _():
        y = _rms(acc_sc[...], gpost_ref[...])
        o_ref[...] = h_ref[...] + (FFN_RES * gate_ref[...]) * y


def _ffn(h, gpre, sh, sc, gate, gpost, wa, wb, wo, tm, tf):
    s = h.shape[0]
    row = lambda i, j: (i, 0)
    vec = pl.BlockSpec((1, D_MODEL), lambda i, j: (0, 0))
    return pl.pallas_call(
        _ffn_kernel,
        out_shape=jax.ShapeDtypeStruct((s, D_MODEL), F32),
        grid=(s // tm, FFN_HIDDEN // tf),
        in_specs=[pl.BlockSpec((tm, D_MODEL), row), vec, vec, vec, vec, vec,
                  pl.BlockSpec((D_MODEL, tf), lambda i, j: (0, j)),
                  pl.BlockSpec((D_MODEL, tf), lambda i, j: (0, j)),
                  pl.BlockSpec((tf, D_MODEL), lambda i, j: (j, 0))],
        out_specs=pl.BlockSpec((tm, D_MODEL), row),
        scratch_shapes=[pltpu.VMEM((tm, D_MODEL), BF16), pltpu.VMEM((tm, D_MODEL), F32)],
        compiler_params=_params(("parallel", "arbitrary")),
        name="ffn",
    )(h, gpre, sh, sc, gate, gpost, wa, wb, wo)


def _inproj_kernel(h_ref, gpre_ref, sh_ref, sc_ref, w_ref, wdt_ref, o_ref, dt_ref, u_sc):
    j = pl.program_id(1)

    @pl.when(j == 0)
    def _():
        u = _rms(h_ref[...], gpre_ref[...]) * (1.0 + sc_ref[...]) + sh_ref[...]
        u_sc[...] = u.astype(BF16)
        dt_ref[...] = _dot(u_sc[...], wdt_ref[...])

    o_ref[...] = _dot(u_sc[...], w_ref[...])


def _inproj(h, gpre, sh, sc, w, wdt, tm, tn):
    s = h.shape[0]
    vec = pl.BlockSpec((1, D_MODEL), lambda i, j: (0, 0))
    return pl.pallas_call(
        _inproj_kernel,
        out_shape=(jax.ShapeDtypeStruct((s, PROJ_W), F32),
                   jax.ShapeDtypeStruct((s, DT_PAD), F32)),
        grid=(s // tm, PROJ_W // tn),
        in_specs=[pl.BlockSpec((tm, D_MODEL), lambda i, j: (i, 0)), vec, vec, vec,
                  pl.BlockSpec((D_MODEL, tn), lambda i, j: (0, j)),
                  pl.BlockSpec((D_MODEL, DT_PAD), lambda i, j: (0, 0))],
        out_specs=(pl.BlockSpec((tm, tn), lambda i, j: (i, j)),
                   pl.BlockSpec((tm, DT_PAD), lambda i, j: (i, 0))),
        scratch_shapes=[pltpu.VMEM((tm, D_MODEL), BF16)],
        compiler_params=_params(("parallel", "arbitrary")),
        name="inproj",
    )(h, gpre, sh, sc, w, wdt)


def _prep_kernel(q_ref, k_ref, v_ref, qt_ref, ka_ref, vt_ref, km_sc):
    t = pl.program_id(1)

    @pl.when(t == 0)
    def _():
        km_sc[...] = jnp.zeros_like(km_sc)

    k = k_ref[...]
    for b in range(BLOCKS_PER_TILE):
        km_sc[pl.ds(t * BLOCKS_PER_TILE + b, 1), :] = jnp.mean(
            k[b * MOBA_BLOCK:(b + 1) * MOBA_BLOCK], axis=0, keepdims=True)

    qt = (q_ref[...] * (HEAD_DIM ** -0.5 * LOG2E)).T
    score = _dot(km_sc[...], qt, precision=HIGHEST)
    blk = lax.broadcasted_iota(jnp.int32, score.shape, 0)
    q_blk = t * BLOCKS_PER_TILE + lax.broadcasted_iota(jnp.int32, score.shape, 1) // MOBA_BLOCK
    past = blk < q_blk
    s = jnp.where(past, score, -jnp.inf)
    sel = blk == q_blk
    for _ in range(MOBA_TOPK):
        m = jnp.max(s, axis=0, keepdims=True)
        first = jnp.min(jnp.where(s == m, blk, MAX_BLOCKS), axis=0, keepdims=True)
        pick = blk == first
        sel = jnp.logical_or(sel, jnp.logical_and(pick, past))
        s = jnp.where(pick, -jnp.inf, s)
    qt_ref[0, 0, :HEAD_DIM, :] = qt.astype(BF16)
    qt_ref[0, 0, HEAD_DIM:, :] = jnp.where(sel, 0.0, MASKED).astype(BF16)

    lane = lax.broadcasted_iota(jnp.int32, (ATTN_TILE, MAX_BLOCKS), 1)
    k_blk = t * BLOCKS_PER_TILE + lax.broadcasted_iota(jnp.int32, lane.shape, 0) // MOBA_BLOCK
    ka_ref[0, :, :HEAD_DIM] = k.astype(BF16)
    ka_ref[0, :, HEAD_DIM:] = jnp.where(lane == k_blk, 1.0, 0.0).astype(BF16)

    ones_row = lax.broadcasted_iota(jnp.int32, (V_ROWS - HEAD_DIM, ATTN_TILE), 0) == 0
    vt_ref[0, 0, :HEAD_DIM, :] = v_ref[...].T.astype(BF16)
    vt_ref[0, 0, HEAD_DIM:, :] = jnp.where(ones_row, 1.0, 0.0).astype(BF16)


def _prep(proj):
    s = proj.shape[0]
    nt = s // ATTN_TILE
    blk = lambda col: pl.BlockSpec((ATTN_TILE, HEAD_DIM), lambda h, t: (t, col // HEAD_DIM + h))
    return pl.pallas_call(
        _prep_kernel,
        out_shape=(jax.ShapeDtypeStruct((ATTN_HEADS, nt, AUG_DIM, ATTN_TILE), BF16),
                   jax.ShapeDtypeStruct((ATTN_HEADS, s, AUG_DIM), BF16),
                   jax.ShapeDtypeStruct((ATTN_HEADS, nt, V_ROWS, ATTN_TILE), BF16)),
        grid=(ATTN_HEADS, nt),
        in_specs=[blk(COL_Q), blk(COL_K), blk(COL_V)],
        out_specs=(pl.BlockSpec((1, 1, AUG_DIM, ATTN_TILE), lambda h, t: (h, t, 0, 0)),
                   pl.BlockSpec((1, ATTN_TILE, AUG_DIM), lambda h, t: (h, t, 0)),
                   pl.BlockSpec((1, 1, V_ROWS, ATTN_TILE), lambda h, t: (h, t, 0, 0))),
        scratch_shapes=[pltpu.VMEM((MAX_BLOCKS, HEAD_DIM), F32)],
        compiler_params=_params(("parallel", "arbitrary")),
        name="prep",
    )(proj, proj, proj)


def _t5_bucket(rel):
    n = jnp.maximum(rel, 0)
    max_exact = REL_BUCKETS // 2
    nf = jnp.maximum(n, 1).astype(F32)
    large = max_exact + (jnp.log(nf / max_exact) / math.log(REL_MAX_DIST / max_exact)
                         * (REL_BUCKETS - max_exact)).astype(jnp.int32)
    large = jnp.minimum(large, REL_BUCKETS - 1)
    return jnp.where(n < max_exact, n, large)


def _bias_kernel(tab_ref, o_ref):
    h = pl.program_id(0)
    shape = (ATTN_TILE, ATTN_TILE)
    ki = lax.broadcasted_iota(jnp.int32, shape, 0)
    qi = lax.broadcasted_iota(jnp.int32, shape, 1)
    far = tab_ref[h * REL_BUCKETS + REL_BUCKETS - 1]
    for which in range(2):
        rel = qi - ki + which * ATTN_TILE
        bucket = _t5_bucket(rel)
        val = jnp.zeros(shape, F32)
        for b in range(REL_BUCKETS):
            val = jnp.where(bucket == b, tab_ref[h * REL_BUCKETS + b], val)
        val = (val - far) * LOG2E
        if which == 0:
            val = jnp.where(rel >= 0, val, MASKED)
        o_ref[0, which] = val


def _bias_tiles(tab_flat):
    return pl.pallas_call(
        _bias_kernel,
        out_shape=jax.ShapeDtypeStruct((ATTN_HEADS, 2, ATTN_TILE, ATTN_TILE), F32),
        grid=(ATTN_HEADS,),
        in_specs=[pl.BlockSpec(memory_space=pltpu.SMEM)],
        out_specs=pl.BlockSpec((1, 2, ATTN_TILE, ATTN_TILE), lambda h: (h, 0, 0, 0)),
        compiler_params=_params(("parallel",)),
        name="bias",
    )(tab_flat)


def _attn_kernel(qt_ref, ka_ref, vt_ref, t_ref, o_ref, m_sc, acc_sc, s_sc, p_sc):
    t = pl.program_id(1)
    qt = qt_ref[0, 0]

    def scores(j):
        start = pl.multiple_of(j * ATTN_TILE, ATTN_TILE)
        return _dot(ka_ref[0, pl.ds(start, ATTN_TILE), :], qt)

    m_sc[...] = jnp.full(m_sc.shape, 4.0 * MASKED, F32)
    acc_sc[...] = jnp.zeros_like(acc_sc)

    def colmax(s_ref):
        return jnp.max(s_ref[...], axis=0, keepdims=True)

    def consume(s_ref, s_max, j):
        m_old = m_sc[...]
        m_new = jnp.maximum(m_old, s_max)
        m_sc[...] = m_new
        for r in range(0, ATTN_TILE, EXP_ROWS):
            p_sc[r:r + EXP_ROWS, :] = jnp.exp2(s_ref[r:r + EXP_ROWS, :] - m_new).astype(BF16)
        acc_sc[...] = jnp.exp2(m_old - m_new) * acc_sc[...] + _dot(vt_ref[0, j], p_sc[...])

    def far_step(j, cur, nxt, max_cur):
        nxt[...] = scores(j + 1)
        consume(cur, max_cur, j)
        return colmax(nxt)

    s_a, s_b = s_sc.at[0], s_sc.at[1]

    @pl.when(t >= 1)
    def _():
        n_far = t - 1
        s_a[...] = scores(0)

        def pair(i, max_cur):
            max_cur = far_step(2 * i, s_a, s_b, max_cur)
            return far_step(2 * i + 1, s_b, s_a, max_cur)

        max_cur = lax.fori_loop(0, n_far // 2, pair, colmax(s_a))
        odd = n_far % 2 == 1

        @pl.when(odd)
        def _():
            far_step(n_far - 1, s_a, s_b, max_cur)
            s_b[...] = s_b[...] + t_ref[0, 1]
            consume(s_b, colmax(s_b), t - 1)

        @pl.when(jnp.logical_not(odd))
        def _():
            s_a[...] = s_a[...] + t_ref[0, 1]
            consume(s_a, colmax(s_a), t - 1)

    s_a[...] = scores(t) + t_ref[0, 0]
    consume(s_a, colmax(s_a), t)
    acc = acc_sc[...]
    out = acc[:HEAD_DIM] / acc[HEAD_DIM:HEAD_DIM + 1]
    o_ref[...] = out.T.astype(o_ref.dtype)


def _attn(qt, ka, vt, tiles):
    nt = qt.shape[1]
    s = nt * ATTN_TILE
    return pl.pallas_call(
        _attn_kernel,
        out_shape=jax.ShapeDtypeStruct((s, ATTN_WIDTH), BF16),
        grid=(ATTN_HEADS, nt),
        in_specs=[pl.BlockSpec((1, 1, AUG_DIM, ATTN_TILE), lambda h, t: (h, t, 0, 0)),
                  pl.BlockSpec((1, s, AUG_DIM), lambda h, t: (h, 0, 0)),
                  pl.BlockSpec((1, nt, V_ROWS, ATTN_TILE), lambda h, t: (h, 0, 0, 0)),
                  pl.BlockSpec((1, 2, ATTN_TILE, ATTN_TILE), lambda h, t: (h, 0, 0, 0))],
        out_specs=pl.BlockSpec((ATTN_TILE, HEAD_DIM), lambda h, t: (t, h)),
        scratch_shapes=[pltpu.VMEM((1, ATTN_TILE), F32), pltpu.VMEM((V_ROWS, ATTN_TILE), F32),
                        pltpu.VMEM((2, ATTN_TILE, ATTN_TILE), F32),
                        pltpu.VMEM((ATTN_TILE, ATTN_TILE), BF16)],
        compiler_params=_params(("parallel", "arbitrary")),
        name="attn",
    )(qt, ka, vt, tiles)


def _expand_heads(d, lane_head):
    out = d[:, SSM_HPG - 1:SSM_HPG]
    for hg in range(SSM_HPG - 2, -1, -1):
        out = jnp.where(lane_head == hg, d[:, hg:hg + 1], out)
    return out


def _ssd_kernel(z_ref, x_ref, b_ref, c_ref, dtr_ref, dtc_ref,
                wx_ref, wb_ref, wc_ref, bx_ref, bb_ref, bc_ref,
                dtb_r_ref, alog_r_ref, dtb_c_ref, alog_c_ref, dskip_ref, nw_ref,
                o_ref, xpx_sc, xpb_sc, xpc_sc, st_sc):
    c = pl.program_id(1)
    L = SSM_CHUNK

    @pl.when(c == 0)
    def _():
        xpx_sc[0:CONV_HALO, :] = jnp.zeros((CONV_HALO, xpx_sc.shape[1]), F32)
        xpb_sc[0:CONV_HALO, :] = jnp.zeros((CONV_HALO, xpb_sc.shape[1]), F32)
        xpc_sc[0:CONV_HALO, :] = jnp.zeros((CONV_HALO, xpc_sc.shape[1]), F32)
        st_sc[...] = jnp.zeros_like(st_sc)

    def conv(src_ref, pad_sc, w_ref, bias_ref):
        pad_sc[CONV_HALO:CONV_HALO + L, :] = src_ref[...]
        acc = bias_ref[...]
        for j in range(SSM_CONV):
            lo = CONV_HALO - (SSM_CONV - 1) + j
            acc = acc + pad_sc[lo:lo + L, :] * w_ref[j:j + 1, :]
        pad_sc[0:CONV_HALO, :] = pad_sc[L:L + CONV_HALO, :]
        return _silu(acc)

    x = conv(x_ref, xpx_sc, wx_ref, bx_ref)
    bm = conv(b_ref, xpb_sc, wb_ref, bb_ref)
    cm = conv(c_ref, xpc_sc, wc_ref, bc_ref)

    dt_r = _softplus(dtr_ref[0] + dtb_r_ref[0])
    dt_c = _softplus(dtc_ref[0] + dtb_c_ref[0])
    a_r = -jnp.exp(alog_r_ref[0])
    a_c = -jnp.exp(alog_c_ref[0])
    row = lax.broadcasted_iota(jnp.int32, (L, L), 0)
    col = lax.broadcasted_iota(jnp.int32, (L, L), 1)
    causal = row >= col
    tril = jnp.where(causal, 1.0, 0.0).astype(F32)
    triu = jnp.where(row <= col, 1.0, 0.0).astype(F32)
    cum_r = _dot(tril, dt_r * a_r, precision=HIGHEST)
    cum_c = _dot(dt_c * a_c, triu, precision=HIGHEST)
    cum_last = cum_r[L - 1:L, :]

    lane_head = lax.broadcasted_iota(jnp.int32, (1, SSM_GROUP_W), 1) // SSM_HEAD_DIM
    dt_x = _expand_heads(dt_r, lane_head)
    grow_x = _expand_heads(jnp.exp(cum_r), lane_head)
    end_x = _expand_heads(jnp.exp(cum_last - cum_r) * dt_r, lane_head)
    last_x = _expand_heads(jnp.exp(cum_last), lane_head)

    cb = _dot_nt(cm.astype(BF16), bm.astype(BF16))
    xdt = x * dt_x
    w_parts, x_parts = [], []
    for hg in range(SSM_HPG):
        seg = cum_r[:, hg:hg + 1] - cum_c[hg:hg + 1, :]
        decay = jnp.exp(jnp.where(causal, seg, -jnp.inf))
        w_parts.append((cb * decay).astype(BF16))
        x_parts.append(jnp.where(lane_head == hg, xdt, 0.0).astype(BF16))
    y = _dot(jnp.concatenate(w_parts, axis=1), jnp.concatenate(x_parts, axis=0))

    st = st_sc[...]
    y = y + _dot(cm.astype(BF16), st.astype(BF16)) * grow_x
    st_sc[...] = last_x * st + _dot_tn(bm.astype(BF16), (x * end_x).astype(BF16))
    y = y + x * dskip_ref[0]

    g = y * _silu(z_ref[...])
    g = g * lax.rsqrt(jnp.mean(g * g, axis=-1, keepdims=True) + RMS_EPS)
    o_ref[...] = (g * nw_ref[0]).astype(o_ref.dtype)


def _ssd(proj, dt_rows, dt_cols, conv_w, conv_b, dtb_r, alog_r, dtb_c, alog_c, dskip_x, norm_w, nc):
    s = proj.shape[0]
    L, GW, NS = SSM_CHUNK, SSM_GROUP_W, SSM_STATE
    xoff, boff, coff = 0, SSM_INNER, SSM_INNER + SSM_GROUPS * NS
    return pl.pallas_call(
        _ssd_kernel,
        out_shape=jax.ShapeDtypeStruct((s, SSM_INNER), BF16),
        grid=(SSM_GROUPS, nc),
        in_specs=[
            pl.BlockSpec((L, GW), lambda g, c: (c, COL_Z // GW + g)),
            pl.BlockSpec((L, GW), lambda g, c: (c, COL_X // GW + g)),
            pl.BlockSpec((L, NS), lambda g, c: (c, COL_B // NS + g)),
            pl.BlockSpec((L, NS), lambda g, c: (c, COL_C // NS + g)),
            pl.BlockSpec((1, L, SSM_HPG), lambda g, c: (g, c, 0)),
            pl.BlockSpec((1, SSM_HPG, L), lambda g, c: (g, 0, c)),
            pl.BlockSpec((SSM_CONV, GW), lambda g, c: (0, xoff // GW + g)),
            pl.BlockSpec((SSM_CONV, NS), lambda g, c: (0, boff // NS + g)),
            pl.BlockSpec((SSM_CONV, NS), lambda g, c: (0, coff // NS + g)),
            pl.BlockSpec((1, GW), lambda g, c: (0, xoff // GW + g)),
            pl.BlockSpec((1, NS), lambda g, c: (0, boff // NS + g)),
            pl.BlockSpec((1, NS), lambda g, c: (0, coff // NS + g)),
            pl.BlockSpec((1, 1, SSM_HPG), lambda g, c: (g, 0, 0)),
            pl.BlockSpec((1, 1, SSM_HPG), lambda g, c: (g, 0, 0)),
            pl.BlockSpec((1, SSM_HPG, 1), lambda g, c: (g, 0, 0)),
            pl.BlockSpec((1, SSM_HPG, 1), lambda g, c: (g, 0, 0)),
            pl.BlockSpec((1, 1, GW), lambda g, c: (g, 0, 0)),
            pl.BlockSpec((1, 1, GW), lambda g, c: (g, 0, 0)),
        ],
        out_specs=pl.BlockSpec((L, GW), lambda g, c: (c, g)),
        scratch_shapes=[pltpu.VMEM((L + CONV_HALO, GW), F32), pltpu.VMEM((L + CONV_HALO, NS), F32),
                        pltpu.VMEM((L + CONV_HALO, NS), F32), pltpu.VMEM((NS, GW), F32)],
        compiler_params=_params(("parallel", "arbitrary")),
        name="ssd",
    )(proj, proj, proj, proj, dt_rows, dt_cols, conv_w, conv_w, conv_w, conv_b, conv_b, conv_b,
      dtb_r, alog_r, dtb_c, alog_c, dskip_x, norm_w)


def _merge_kernel(h_ref, a_ref, b_ref, ga_ref, gb_ref, pa_ref, pb_ref, wo_ref, gate_ref, gpost_ref,
                  o_ref):
    ya = _dot(a_ref[...], pa_ref[...])
    yb = _dot(b_ref[...], pb_ref[...])
    mix = _sigmoid(ga_ref[...]) * ya + _sigmoid(gb_ref[...]) * yb
    y = _dot(mix.astype(BF16), wo_ref[...])
    o_ref[...] = h_ref[...] + gate_ref[...] * _rms(y, gpost_ref[...])


def _merge(h, attn, ssd, proj, pa, pb, wo, gate, gpost, tm):
    s = h.shape[0]
    vec = pl.BlockSpec((1, D_MODEL), lambda i: (0, 0))
    full = lambda a: pl.BlockSpec(a.shape, lambda i: (0, 0))
    return pl.pallas_call(
        _merge_kernel,
        out_shape=jax.ShapeDtypeStruct((s, D_MODEL), F32),
        grid=(s // tm,),
        in_specs=[pl.BlockSpec((tm, D_MODEL), lambda i: (i, 0)),
                  pl.BlockSpec((tm, ATTN_WIDTH), lambda i: (i, 0)),
                  pl.BlockSpec((tm, SSM_INNER), lambda i: (i, 0)),
                  pl.BlockSpec((tm, D_MODEL), lambda i: (i, COL_GA // D_MODEL)),
                  pl.BlockSpec((tm, D_MODEL), lambda i: (i, COL_GB // D_MODEL)),
                  full(pa), full(pb), full(wo), vec, vec],
        out_specs=pl.BlockSpec((tm, D_MODEL), lambda i: (i, 0)),
        compiler_params=_params(("parallel",)),
        name="merge",
    )(h, attn, ssd, proj, proj, pa, pb, wo, gate, gpost)


def _layer(h, mod, rel_bias, p):
    s = h.shape[0]
    nb = s // MOBA_BLOCK
    nc = s // SSM_CHUNK
    tm = min(512, s)
    sh1, sc1, g1, shm, scm, gm, sh2, sc2, g2 = [mod[k] for k in range(N_MOD)]
    vec = lambda a: a.reshape(1, -1)

    def ffn_weights(w_in, w_out):
        return (w_in[:, :FFN_HIDDEN].astype(BF16), w_in[:, FFN_HIDDEN:].astype(BF16),
                w_out.astype(BF16))

    h = _ffn(h, vec(p["ffn1_norm_pre"]), sh1, sc1, g1, vec(p["ffn1_norm_post"]),
             *ffn_weights(p["ffn1_w_in"], p["ffn1_w_out"]), tm=tm, tf=FFN_HIDDEN // 2)

    w_in = p["w_in_mix"]
    dt_lo = COL_GA
    w_main = jnp.concatenate([w_in[:, :dt_lo], w_in[:, dt_lo + SSM_HEADS:]], axis=1).astype(BF16)
    w_dt = jnp.pad(w_in[:, dt_lo:dt_lo + SSM_HEADS], ((0, 0), (0, DT_PAD - SSM_HEADS))).astype(BF16)
    proj, dt_raw = _inproj(h, vec(p["mix_norm_pre"]), shm, scm, w_main, w_dt, tm=tm, tn=1024)

    tab_flat = rel_bias.T.reshape(-1)
    qt, ka, vt = _prep(proj)
    tiles = _bias_tiles(tab_flat)
    attn = _attn(qt, ka, vt, tiles)

    dt3 = dt_raw[:, :SSM_HEADS].reshape(s, SSM_GROUPS, SSM_HPG)
    per_group = lambda a: a.reshape(SSM_GROUPS, SSM_HPG)
    dskip_x = jnp.repeat(p["d_skip"], SSM_HEAD_DIM).reshape(SSM_GROUPS, 1, SSM_GROUP_W)
    ssd = _ssd(proj, dt3.transpose(1, 0, 2), dt3.transpose(1, 2, 0),
               p["conv_w"], vec(p["conv_b"]),
               per_group(p["dt_bias"])[:, None, :], per_group(p["a_log"])[:, None, :],
               per_group(p["dt_bias"])[:, :, None], per_group(p["a_log"])[:, :, None],
               dskip_x, p["ssm_norm_w"].reshape(SSM_GROUPS, 1, SSM_GROUP_W), nc)

    h = _merge(h, attn, ssd, proj, p["proj_a"].astype(BF16), p["proj_b"].astype(BF16),
               p["w_out_mix"].astype(BF16), gm, vec(p["mix_norm_post"]), tm=tm)

    h = _ffn(h, vec(p["ffn2_norm_pre"]), sh2, sc2, g2, vec(p["ffn2_norm_post"]),
             *ffn_weights(p["ffn2_w_in"], p["ffn2_w_out"]), tm=tm, tf=FFN_HIDDEN // 2)
    return h


_LAYER_KEYS = ("ffn1_norm_pre", "ffn1_w_in", "ffn1_w_out", "ffn1_norm_post", "mix_norm_pre",
               "w_in_mix", "conv_w", "conv_b", "dt_bias", "a_log", "d_skip", "ssm_norm_w",
               "proj_a", "proj_b", "w_out_mix", "mix_norm_post",
               "ffn2_norm_pre", "ffn2_w_in", "ffn2_w_out", "ffn2_norm_post")


def kernel(x, c, w_ada, b_ada, ffn1_norm_pre, ffn1_w_in, ffn1_w_out, ffn1_norm_post, mix_norm_pre,
           w_in_mix, rel_bias, conv_w, conv_b, dt_bias, a_log, d_skip, ssm_norm_w, proj_a, proj_b,
           w_out_mix, mix_norm_post, ffn2_norm_pre, ffn2_w_in, ffn2_w_out, ffn2_norm_post):
    stacked = dict(ffn1_norm_pre=ffn1_norm_pre, ffn1_w_in=ffn1_w_in, ffn1_w_out=ffn1_w_out,
                   ffn1_norm_post=ffn1_norm_post, mix_norm_pre=mix_norm_pre, w_in_mix=w_in_mix,
                   conv_w=conv_w, conv_b=conv_b, dt_bias=dt_bias, a_log=a_log, d_skip=d_skip,
                   ssm_norm_w=ssm_norm_w, proj_a=proj_a, proj_b=proj_b, w_out_mix=w_out_mix,
                   mix_norm_post=mix_norm_post, ffn2_norm_pre=ffn2_norm_pre, ffn2_w_in=ffn2_w_in,
                   ffn2_w_out=ffn2_w_out, ffn2_norm_post=ffn2_norm_post)
    batch, seq, _ = x.shape
    assert seq % ATTN_TILE == 0 and seq // MOBA_BLOCK <= MAX_BLOCKS and seq % SSM_CHUNK == 0
    depth = w_ada.shape[0]
    outs = []
    for b in range(batch):
        h = x[b]
        for l in range(depth):
            mod = _mod(c[b:b + 1], w_ada[l], b_ada[l])
            h = _layer(h, mod, rel_bias, {k: stacked[k][l] for k in _LAYER_KEYS})
        outs.append(h)
    return jnp.stack(outs)
```

```python
import math

import jax
import jax.numpy as jnp
from jax import lax
from jax.experimental import pallas as pl
from jax.experimental.pallas import tpu as pltpu

F32 = jnp.float32
BF16 = jnp.bfloat16
HIGHEST = lax.Precision.HIGHEST

D_MODEL = 1024
N_MOD = 9
RMS_EPS = 1e-6
FFN_HIDDEN = 2816
FFN_RES = 0.5

ATTN_HEADS = 8
HEAD_DIM = 128
ATTN_WIDTH = ATTN_HEADS * HEAD_DIM
MOBA_BLOCK = 256
MOBA_TOPK = 3
MAX_BLOCKS = 128
AUG_DIM = HEAD_DIM + MAX_BLOCKS
REL_BUCKETS = 32
REL_MAX_DIST = 128
MASKED = -1e30
LOG2E = math.log2(math.e)
ATTN_TILE = 512
BLOCKS_PER_TILE = ATTN_TILE // MOBA_BLOCK
V_ROWS = HEAD_DIM + 16
EXP_ROWS = 64

SSM_INNER = 2048
SSM_HEAD_DIM = 64
SSM_GROUPS = 8
SSM_HEADS = SSM_INNER // SSM_HEAD_DIM
SSM_HPG = SSM_HEADS // SSM_GROUPS
SSM_GROUP_W = SSM_INNER // SSM_GROUPS
SSM_STATE = 128
SSM_CONV = 4
SSM_CHUNK = 256
CONV_HALO = 8
SMALL_ROWS = 128

COL_Q = 0
COL_K = COL_Q + ATTN_WIDTH
COL_V = COL_K + ATTN_WIDTH
COL_Z = COL_V + ATTN_WIDTH
COL_X = COL_Z + SSM_INNER
COL_B = COL_X + SSM_INNER
COL_C = COL_B + SSM_GROUPS * SSM_STATE
COL_GA = COL_C + SSM_GROUPS * SSM_STATE
COL_GB = COL_GA + D_MODEL
PROJ_W = COL_GB + D_MODEL
DT_PAD = 128

VMEM_LIMIT = 56 * 1024 * 1024


def _params(sem):
    return pltpu.CompilerParams(dimension_semantics=sem, vmem_limit_bytes=VMEM_LIMIT)


def _sigmoid(x):
    return 0.5 + 0.5 * jnp.tanh(0.5 * x)


def _silu(x):
    return x * _sigmoid(x)


def _softplus(x):
    return jnp.maximum(x, 0.0) + jnp.log(1.0 + jnp.exp(-jnp.abs(x)))


def _rms(x, g):
    return x * lax.rsqrt(jnp.mean(x * x, axis=-1, keepdims=True) + RMS_EPS) * g


def _dot(a, b, **kw):
    return jnp.dot(a, b, preferred_element_type=F32, **kw)


def _dot_nt(a, b, **kw):
    return lax.dot_general(a, b, (((1,), (1,)), ((), ())), preferred_element_type=F32, **kw)


def _dot_tn(a, b, **kw):
    return lax.dot_general(a, b, (((0,), (0,)), ((), ())), preferred_element_type=F32, **kw)


def _mod_kernel(c_ref, w_ref, b_ref, o_ref):
    cs = _silu(c_ref[...])
    o_ref[...] = _dot(cs, w_ref[...], precision=HIGHEST) + b_ref[...]


def _mod(c, w_ada, b_ada):
    n = w_ada.shape[1]
    tn = 1024
    c8 = jnp.broadcast_to(c, (8, D_MODEL))
    out = pl.pallas_call(
        _mod_kernel,
        out_shape=jax.ShapeDtypeStruct((8, n), F32),
        grid=(n // tn,),
        in_specs=[pl.BlockSpec((8, D_MODEL), lambda j: (0, 0)),
                  pl.BlockSpec((D_MODEL, tn), lambda j: (0, j)),
                  pl.BlockSpec((1, tn), lambda j: (0, j))],
        out_specs=pl.BlockSpec((8, tn), lambda j: (0, j)),
        compiler_params=_params(("arbitrary",)),
        name="mod",
    )(c8, w_ada, b_ada.reshape(1, n))
    return out[0].reshape(N_MOD, 1, D_MODEL)


def _ffn_kernel(h_ref, gpre_ref, sh_ref, sc_ref, gate_ref, gpost_ref, wa_ref, wb_ref, wo_ref,
                o_ref, u_sc, acc_sc):
    j = pl.program_id(1)

    @pl.when(j == 0)
    def _():
        u = _rms(h_ref[...], gpre_ref[...]) * (1.0 + sc_ref[...]) + sh_ref[...]
        u_sc[...] = u.astype(BF16)
        acc_sc[...] = jnp.zeros_like(acc_sc)

    u = u_sc[...]
    a = _dot(u, wa_ref[...])
    b = _dot(u, wb_ref[...])
    mid = (_silu(a) * b).astype(BF16)
    acc_sc[...] += _dot(mid, wo_ref[...])

    @pl.when(j == pl.num_programs(1) - 1)
    def _():
        y = _rms(acc_sc[...], gpost_ref[...])
        o_ref[...] = h_ref[...] + (FFN_RES * gate_ref[...]) * y


def _ffn(h, gpre, sh, sc, gate, gpost, wa, wb, wo, tm, tf):
    s = h.shape[0]
    row = lambda i, j: (i, 0)
    vec = pl.BlockSpec((1, D_MODEL), lambda i, j: (0, 0))
    return pl.pallas_call(
        _ffn_kernel,
        out_shape=jax.ShapeDtypeStruct((s, D_MODEL), F32),
        grid=(s // tm, FFN_HIDDEN // tf),
        in_specs=[pl.BlockSpec((tm, D_MODEL), row), vec, vec, vec, vec, vec,
                  pl.BlockSpec((D_MODEL, tf), lambda i, j: (0, j)),
                  pl.BlockSpec((D_MODEL, tf), lambda i, j: (0, j)),
                  pl.BlockSpec((tf, D_MODEL), lambda i, j: (j, 0))],
        out_specs=pl.BlockSpec((tm, D_MODEL), row),
        scratch_shapes=[pltpu.VMEM((tm, D_MODEL), BF16), pltpu.VMEM((tm, D_MODEL), F32)],
        compiler_params=_params(("parallel", "arbitrary")),
        name="ffn",
    )(h, gpre, sh, sc, gate, gpost, wa, wb, wo)


def _inproj_kernel(h_ref, gpre_ref, sh_ref, sc_ref, w_ref, wdt_ref, o_ref, dt_ref, u_sc):
    j = pl.program_id(1)

    @pl.when(j == 0)
    def _():
        u = _rms(h_ref[...], gpre_ref[...]) * (1.0 + sc_ref[...]) + sh_ref[...]
        u_sc[...] = u.astype(BF16)
        dt_ref[...] = _dot(u_sc[...], wdt_ref[...])

    o_ref[...] = _dot(u_sc[...], w_ref[...]).astype(o_ref.dtype)


def _inproj(h, gpre, sh, sc, w, wdt, tm, tn):
    s = h.shape[0]
    vec = pl.BlockSpec((1, D_MODEL), lambda i, j: (0, 0))
    return pl.pallas_call(
        _inproj_kernel,
        out_shape=(jax.ShapeDtypeStruct((s, PROJ_W), BF16),
                   jax.ShapeDtypeStruct((s, DT_PAD), F32)),
        grid=(s // tm, PROJ_W // tn),
        in_specs=[pl.BlockSpec((tm, D_MODEL), lambda i, j: (i, 0)), vec, vec, vec,
                  pl.BlockSpec((D_MODEL, tn), lambda i, j: (0, j)),
                  pl.BlockSpec((D_MODEL, DT_PAD), lambda i, j: (0, 0))],
        out_specs=(pl.BlockSpec((tm, tn), lambda i, j: (i, j)),
                   pl.BlockSpec((tm, DT_PAD), lambda i, j: (i, 0))),
        scratch_shapes=[pltpu.VMEM((tm, D_MODEL), BF16)],
        compiler_params=_params(("parallel", "arbitrary")),
        name="inproj",
    )(h, gpre, sh, sc, w, wdt)


def _prep_kernel(q_ref, k_ref, v_ref, qt_ref, ka_ref, vt_ref, km_sc):
    t = pl.program_id(1)

    @pl.when(t == 0)
    def _():
        km_sc[...] = jnp.zeros_like(km_sc)

    k = k_ref[...].astype(F32)
    for b in range(BLOCKS_PER_TILE):
        km_sc[pl.ds(t * BLOCKS_PER_TILE + b, 1), :] = jnp.mean(
            k[b * MOBA_BLOCK:(b + 1) * MOBA_BLOCK], axis=0, keepdims=True)

    qt = (q_ref[...].astype(F32) * (HEAD_DIM ** -0.5 * LOG2E)).T
    score = _dot(km_sc[...], qt, precision=HIGHEST)
    blk = lax.broadcasted_iota(jnp.int32, score.shape, 0)
    q_blk = t * BLOCKS_PER_TILE + lax.broadcasted_iota(jnp.int32, score.shape, 1) // MOBA_BLOCK
    past = blk < q_blk
    s = jnp.where(past, score, -jnp.inf)
    sel = blk == q_blk
    for _ in range(MOBA_TOPK):
        m = jnp.max(s, axis=0, keepdims=True)
        first = jnp.min(jnp.where(s == m, blk, MAX_BLOCKS), axis=0, keepdims=True)
        pick = blk == first
        sel = jnp.logical_or(sel, jnp.logical_and(pick, past))
        s = jnp.where(pick, -jnp.inf, s)
    qt_ref[0, 0, :HEAD_DIM, :] = qt.astype(BF16)
    qt_ref[0, 0, HEAD_DIM:, :] = jnp.where(sel, 0.0, MASKED).astype(BF16)

    lane = lax.broadcasted_iota(jnp.int32, (ATTN_TILE, MAX_BLOCKS), 1)
    k_blk = t * BLOCKS_PER_TILE + lax.broadcasted_iota(jnp.int32, lane.shape, 0) // MOBA_BLOCK
    ka_ref[0, :, :HEAD_DIM] = k.astype(BF16)
    ka_ref[0, :, HEAD_DIM:] = jnp.where(lane == k_blk, 1.0, 0.0).astype(BF16)

    ones_row = lax.broadcasted_iota(jnp.int32, (V_ROWS - HEAD_DIM, ATTN_TILE), 0) == 0
    vt_ref[0, 0, :HEAD_DIM, :] = v_ref[...].astype(F32).T.astype(BF16)
    vt_ref[0, 0, HEAD_DIM:, :] = jnp.where(ones_row, 1.0, 0.0).astype(BF16)


def _prep(proj):
    s = proj.shape[0]
    nt = s // ATTN_TILE
    blk = lambda col: pl.BlockSpec((ATTN_TILE, HEAD_DIM), lambda h, t: (t, col // HEAD_DIM + h))
    return pl.pallas_call(
        _prep_kernel,
        out_shape=(jax.ShapeDtypeStruct((ATTN_HEADS, nt, AUG_DIM, ATTN_TILE), BF16),
                   jax.ShapeDtypeStruct((ATTN_HEADS, s, AUG_DIM), BF16),
                   jax.ShapeDtypeStruct((ATTN_HEADS, nt, V_ROWS, ATTN_TILE), BF16)),
        grid=(ATTN_HEADS, nt),
        in_specs=[blk(COL_Q), blk(COL_K), blk(COL_V)],
        out_specs=(pl.BlockSpec((1, 1, AUG_DIM, ATTN_TILE), lambda h, t: (h, t, 0, 0)),
                   pl.BlockSpec((1, ATTN_TILE, AUG_DIM), lambda h, t: (h, t, 0)),
                   pl.BlockSpec((1, 1, V_ROWS, ATTN_TILE), lambda h, t: (h, t, 0, 0))),
        scratch_shapes=[pltpu.VMEM((MAX_BLOCKS, HEAD_DIM), F32)],
        compiler_params=_params(("parallel", "arbitrary")),
        name="prep",
    )(proj, proj, proj)


def _t5_bucket(rel):
    n = jnp.maximum(rel, 0)
    max_exact = REL_BUCKETS // 2
    nf = jnp.maximum(n, 1).astype(F32)
    large = max_exact + (jnp.log(nf / max_exact) / math.log(REL_MAX_DIST / max_exact)
                         * (REL_BUCKETS - max_exact)).astype(jnp.int32)
    large = jnp.minimum(large, REL_BUCKETS - 1)
    return jnp.where(n < max_exact, n, large)


def _bias_kernel(tab_ref, o_ref):
    h = pl.program_id(0)
    shape = (ATTN_TILE, ATTN_TILE)
    ki = lax.broadcasted_iota(jnp.int32, shape, 0)
    qi = lax.broadcasted_iota(jnp.int32, shape, 1)
    far = tab_ref[h * REL_BUCKETS + REL_BUCKETS - 1]
    for which in range(2):
        rel = qi - ki + which * ATTN_TILE
        bucket = _t5_bucket(rel)
        val = jnp.zeros(shape, F32)
        for b in range(REL_BUCKETS):
            val = jnp.where(bucket == b, tab_ref[h * REL_BUCKETS + b], val)
        val = (val - far) * LOG2E
        if which == 0:
            val = jnp.where(rel >= 0, val, MASKED)
        o_ref[0, which] = val


def _bias_tiles(tab_flat):
    return pl.pallas_call(
        _bias_kernel,
        out_shape=jax.ShapeDtypeStruct((ATTN_HEADS, 2, ATTN_TILE, ATTN_TILE), F32),
        grid=(ATTN_HEADS,),
        in_specs=[pl.BlockSpec(memory_space=pltpu.SMEM)],
        out_specs=pl.BlockSpec((1, 2, ATTN_TILE, ATTN_TILE), lambda h: (h, 0, 0, 0)),
        compiler_params=_params(("parallel",)),
        name="bias",
    )(tab_flat)


def _attn_kernel(qt_ref, ka_ref, vt_ref, t_ref, o_ref, m_sc, acc_sc, s_sc, p_sc):
    t = pl.program_id(1)
    qt = qt_ref[0, 0]

    def scores(j):
        start = pl.multiple_of(j * ATTN_TILE, ATTN_TILE)
        return _dot(ka_ref[0, pl.ds(start, ATTN_TILE), :], qt)

    m_sc[...] = jnp.full(m_sc.shape, 4.0 * MASKED, F32)
    acc_sc[...] = jnp.zeros_like(acc_sc)

    def colmax(s_ref):
        return jnp.max(s_ref[...], axis=0, keepdims=True)

    def consume(s_ref, s_max, j):
        m_old = m_sc[...]
        m_new = jnp.maximum(m_old, s_max)
        m_sc[...] = m_new
        for r in range(0, ATTN_TILE, EXP_ROWS):
            p_sc[r:r + EXP_ROWS, :] = jnp.exp2(s_ref[r:r + EXP_ROWS, :] - m_new).astype(BF16)
        acc_sc[...] = jnp.exp2(m_old - m_new) * acc_sc[...] + _dot(vt_ref[0, j], p_sc[...])

    def far_step(j, cur, nxt, max_cur):
        nxt[...] = scores(j + 1)
        consume(cur, max_cur, j)
        return colmax(nxt)

    s_a, s_b = s_sc.at[0], s_sc.at[1]

    @pl.when(t >= 1)
    def _():
        n_far = t - 1
        s_a[...] = scores(0)

        def pair(i, max_cur):
            max_cur = far_step(2 * i, s_a, s_b, max_cur)
            return far_step(2 * i + 1, s_b, s_a, max_cur)

        max_cur = lax.fori_loop(0, n_far // 2, pair, colmax(s_a))
        odd = n_far % 2 == 1

        @pl.when(odd)
        def _():
            far_step(n_far - 1, s_a, s_b, max_cur)
            s_b[...] = s_b[...] + t_ref[0, 1]
            consume(s_b, colmax(s_b), t - 1)

        @pl.when(jnp.logical_not(odd))
        def _():
            s_a[...] = s_a[...] + t_ref[0, 1]
            consume(s_a, colmax(s_a), t - 1)

    s_a[...] = scores(t) + t_ref[0, 0]
    consume(s_a, colmax(s_a), t)
    acc = acc_sc[...]
    out = acc[:HEAD_DIM] / acc[HEAD_DIM:HEAD_DIM + 1]
    o_ref[...] = out.T.astype(o_ref.dtype)


def _attn(qt, ka, vt, tiles):
    nt = qt.shape[1]
    s = nt * ATTN_TILE
    return pl.pallas_call(
        _attn_kernel,
        out_shape=jax.ShapeDtypeStruct((s, ATTN_WIDTH), BF16),
        grid=(ATTN_HEADS, nt),
        in_specs=[pl.BlockSpec((1, 1, AUG_DIM, ATTN_TILE), lambda h, t: (h, t, 0, 0)),
                  pl.BlockSpec((1, s, AUG_DIM), lambda h, t: (h, 0, 0)),
                  pl.BlockSpec((1, nt, V_ROWS, ATTN_TILE), lambda h, t: (h, 0, 0, 0)),
                  pl.BlockSpec((1, 2, ATTN_TILE, ATTN_TILE), lambda h, t: (h, 0, 0, 0))],
        out_specs=pl.BlockSpec((ATTN_TILE, HEAD_DIM), lambda h, t: (t, h)),
        scratch_shapes=[pltpu.VMEM((1, ATTN_TILE), F32), pltpu.VMEM((V_ROWS, ATTN_TILE), F32),
                        pltpu.VMEM((2, ATTN_TILE, ATTN_TILE), F32),
                        pltpu.VMEM((ATTN_TILE, ATTN_TILE), BF16)],
        compiler_params=_params(("parallel", "arbitrary")),
        name="attn",
    )(qt, ka, vt, tiles)


def _expand_heads(d, lane_head):
    out = d[:, SSM_HPG - 1:SSM_HPG]
    for hg in range(SSM_HPG - 2, -1, -1):
        out = jnp.where(lane_head == hg, d[:, hg:hg + 1], out)
    return out


def _ssd_kernel(z_ref, x_ref, b_ref, c_ref, dtc_ref, wx_ref, wb_ref, wc_ref, bx_ref, bb_ref, bc_ref,
                dtb_ref, alog_ref, dskip_ref, nw_ref, o_ref, xpx_sc, xpb_sc, xpc_sc, st_sc):
    c = pl.program_id(1)
    L = SSM_CHUNK

    @pl.when(c == 0)
    def _():
        xpx_sc[0:CONV_HALO, :] = jnp.zeros((CONV_HALO, xpx_sc.shape[1]), F32)
        xpb_sc[0:CONV_HALO, :] = jnp.zeros((CONV_HALO, xpb_sc.shape[1]), F32)
        xpc_sc[0:CONV_HALO, :] = jnp.zeros((CONV_HALO, xpc_sc.shape[1]), F32)
        st_sc[...] = jnp.zeros_like(st_sc)

    def conv(src_ref, pad_sc, w_ref, bias_ref):
        pad_sc[CONV_HALO:CONV_HALO + L, :] = src_ref[...].astype(F32)
        acc = bias_ref[...]
        for j in range(SSM_CONV):
            lo = CONV_HALO - (SSM_CONV - 1) + j
            acc = acc + pad_sc[lo:lo + L, :] * w_ref[j:j + 1, :]
        pad_sc[0:CONV_HALO, :] = pad_sc[L:L + CONV_HALO, :]
        return _silu(acc)

    x = conv(x_ref, xpx_sc, wx_ref, bx_ref)
    bm = conv(b_ref, xpb_sc, wb_ref, bb_ref)
    cm = conv(c_ref, xpc_sc, wc_ref, bc_ref)

    hp = SSM_HPG
    dt_c = _softplus(dtc_ref[0] + dtb_ref[0])
    a_c = -jnp.exp(alog_ref[0])
    row = lax.broadcasted_iota(jnp.int32, (L, L), 0)
    col = lax.broadcasted_iota(jnp.int32, (L, L), 1)
    causal = row >= col
    triu = jnp.where(row <= col, 1.0, 0.0).astype(F32)
    cum_c = _dot(dt_c * a_c, triu, precision=HIGHEST)
    last_c = cum_c[:, L - 1:L]
    small = jnp.concatenate(
        [dt_c, cum_c, jnp.exp(cum_c), jnp.exp(last_c - cum_c) * dt_c,
         jnp.zeros((SMALL_ROWS - 4 * hp, L), F32)], axis=0).T
    cum_r = small[:, hp:2 * hp]

    lane_head = lax.broadcasted_iota(jnp.int32, (1, SSM_GROUP_W), 1) // SSM_HEAD_DIM
    dt_x = _expand_heads(small[:, 0:hp], lane_head)
    grow_x = _expand_heads(small[:, 2 * hp:3 * hp], lane_head)
    end_x = _expand_heads(small[:, 3 * hp:4 * hp], lane_head)
    last_x = _expand_heads(small[L - 1:L, 2 * hp:3 * hp], lane_head)

    cb = _dot_nt(cm.astype(BF16), bm.astype(BF16))
    xdt = x * dt_x
    w_parts, x_parts = [], []
    for hg in range(hp):
        seg = cum_r[:, hg:hg + 1] - cum_c[hg:hg + 1, :]
        w_parts.append(cb * jnp.exp(jnp.where(causal, seg, -jnp.inf)))
        x_parts.append(jnp.where(lane_head == hg, xdt, 0.0))
    y = _dot(jnp.concatenate(w_parts, axis=1).astype(BF16),
             jnp.concatenate(x_parts, axis=0).astype(BF16))

    st = st_sc[...]
    y = y + _dot(cm.astype(BF16), st.astype(BF16)) * grow_x
    st_sc[...] = last_x * st + _dot_tn(bm.astype(BF16), (x * end_x).astype(BF16))
    y = y + x * dskip_ref[0]

    g = y * _silu(z_ref[...].astype(F32))
    g = g * lax.rsqrt(jnp.mean(g * g, axis=-1, keepdims=True) + RMS_EPS)
    o_ref[...] = (g * nw_ref[0]).astype(o_ref.dtype)


def _ssd(proj, dt_cols, conv_w, conv_b, dtb, alog, dskip_x, norm_w, nc):
    s = proj.shape[0]
    L, GW, NS = SSM_CHUNK, SSM_GROUP_W, SSM_STATE
    xoff, boff, coff = 0, SSM_INNER, SSM_INNER + SSM_GROUPS * NS
    return pl.pallas_call(
        _ssd_kernel,
        out_shape=jax.ShapeDtypeStruct((s, SSM_INNER), BF16),
        grid=(SSM_GROUPS, nc),
        in_specs=[
            pl.BlockSpec((L, GW), lambda g, c: (c, COL_Z // GW + g)),
            pl.BlockSpec((L, GW), lambda g, c: (c, COL_X // GW + g)),
            pl.BlockSpec((L, NS), lambda g, c: (c, COL_B // NS + g)),
            pl.BlockSpec((L, NS), lambda g, c: (c, COL_C // NS + g)),
            pl.BlockSpec((1, SSM_HPG, L), lambda g, c: (g, 0, c)),
            pl.BlockSpec((SSM_CONV, GW), lambda g, c: (0, xoff // GW + g)),
            pl.BlockSpec((SSM_CONV, NS), lambda g, c: (0, boff // NS + g)),
            pl.BlockSpec((SSM_CONV, NS), lambda g, c: (0, coff // NS + g)),
            pl.BlockSpec((1, GW), lambda g, c: (0, xoff // GW + g)),
            pl.BlockSpec((1, NS), lambda g, c: (0, boff // NS + g)),
            pl.BlockSpec((1, NS), lambda g, c: (0, coff // NS + g)),
            pl.BlockSpec((1, SSM_HPG, 1), lambda g, c: (g, 0, 0)),
            pl.BlockSpec((1, SSM_HPG, 1), lambda g, c: (g, 0, 0)),
            pl.BlockSpec((1, 1, GW), lambda g, c: (g, 0, 0)),
            pl.BlockSpec((1, 1, GW), lambda g, c: (g, 0, 0)),
        ],
        out_specs=pl.BlockSpec((L, GW), lambda g, c: (c, g)),
        scratch_shapes=[pltpu.VMEM((L + CONV_HALO, GW), F32), pltpu.VMEM((L + CONV_HALO, NS), F32),
                        pltpu.VMEM((L + CONV_HALO, NS), F32), pltpu.VMEM((NS, GW), F32)],
        compiler_params=_params(("parallel", "arbitrary")),
        name="ssd",
    )(proj, proj, proj, proj, dt_cols, conv_w, conv_w, conv_w, conv_b, conv_b, conv_b,
      dtb, alog, dskip_x, norm_w)


def _merge_kernel(h_ref, a_ref, b_ref, ga_ref, gb_ref, pa_ref, pb_ref, wo_ref, gate_ref, gpost_ref,
                  o_ref):
    ya = _dot(a_ref[...], pa_ref[...])
    yb = _dot(b_ref[...], pb_ref[...])
    mix = _sigmoid(ga_ref[...].astype(F32)) * ya + _sigmoid(gb_ref[...].astype(F32)) * yb
    y = _dot(mix.astype(BF16), wo_ref[...])
    o_ref[...] = h_ref[...] + gate_ref[...] * _rms(y, gpost_ref[...])


def _merge(h, attn, ssd, proj, pa, pb, wo, gate, gpost, tm):
    s = h.shape[0]
    vec = pl.BlockSpec((1, D_MODEL), lambda i: (0, 0))
    full = lambda a: pl.BlockSpec(a.shape, lambda i: (0, 0))
    return pl.pallas_call(
        _merge_kernel,
        out_shape=jax.ShapeDtypeStruct((s, D_MODEL), F32),
        grid=(s // tm,),
        in_specs=[pl.BlockSpec((tm, D_MODEL), lambda i: (i, 0)),
                  pl.BlockSpec((tm, ATTN_WIDTH), lambda i: (i, 0)),
                  pl.BlockSpec((tm, SSM_INNER), lambda i: (i, 0)),
                  pl.BlockSpec((tm, D_MODEL), lambda i: (i, COL_GA // D_MODEL)),
                  pl.BlockSpec((tm, D_MODEL), lambda i: (i, COL_GB // D_MODEL)),
                  full(pa), full(pb), full(wo), vec, vec],
        out_specs=pl.BlockSpec((tm, D_MODEL), lambda i: (i, 0)),
        compiler_params=_params(("parallel",)),
        name="merge",
    )(h, attn, ssd, proj, proj, pa, pb, wo, gate, gpost)


def _layer(h, mod, rel_bias, p):
    s = h.shape[0]
    nb = s // MOBA_BLOCK
    nc = s // SSM_CHUNK
    tm = min(512, s)
    sh1, sc1, g1, shm, scm, gm, sh2, sc2, g2 = [mod[k] for k in range(N_MOD)]
    vec = lambda a: a.reshape(1, -1)

    def ffn_weights(w_in, w_out):
        return (w_in[:, :FFN_HIDDEN].astype(BF16), w_in[:, FFN_HIDDEN:].astype(BF16),
                w_out.astype(BF16))

    h = _ffn(h, vec(p["ffn1_norm_pre"]), sh1, sc1, g1, vec(p["ffn1_norm_post"]),
             *ffn_weights(p["ffn1_w_in"], p["ffn1_w_out"]), tm=tm, tf=FFN_HIDDEN // 2)

    w_in = p["w_in_mix"]
    dt_lo = COL_GA
    w_main = jnp.concatenate([w_in[:, :dt_lo], w_in[:, dt_lo + SSM_HEADS:]], axis=1).astype(BF16)
    w_dt = jnp.pad(w_in[:, dt_lo:dt_lo + SSM_HEADS], ((0, 0), (0, DT_PAD - SSM_HEADS))).astype(BF16)
    proj, dt_raw = _inproj(h, vec(p["mix_norm_pre"]), shm, scm, w_main, w_dt, tm=min(1024, s), tn=1024)

    tab_flat = rel_bias.T.reshape(-1)
    qt, ka, vt = _prep(proj)
    tiles = _bias_tiles(tab_flat)
    attn = _attn(qt, ka, vt, tiles)

    dt_cols = dt_raw[:, :SSM_HEADS].reshape(s, SSM_GROUPS, SSM_HPG).transpose(1, 2, 0)
    per_group = lambda a: a.reshape(SSM_GROUPS, SSM_HPG, 1)
    dskip_x = jnp.repeat(p["d_skip"], SSM_HEAD_DIM).reshape(SSM_GROUPS, 1, SSM_GROUP_W)
    ssd = _ssd(proj, dt_cols, p["conv_w"], vec(p["conv_b"]),
               per_group(p["dt_bias"]), per_group(p["a_log"]),
               dskip_x, p["ssm_norm_w"].reshape(SSM_GROUPS, 1, SSM_GROUP_W), nc)

    h = _merge(h, attn, ssd, proj, p["proj_a"].astype(BF16), p["proj_b"].astype(BF16),
               p["w_out_mix"].astype(BF16), gm, vec(p["mix_norm_post"]), tm=tm)

    h = _ffn(h, vec(p["ffn2_norm_pre"]), sh2, sc2, g2, vec(p["ffn2_norm_post"]),
             *ffn_weights(p["ffn2_w_in"], p["ffn2_w_out"]), tm=tm, tf=FFN_HIDDEN // 2)
    return h


_LAYER_KEYS = ("ffn1_norm_pre", "ffn1_w_in", "ffn1_w_out", "ffn1_norm_post", "mix_norm_pre",
               "w_in_mix", "conv_w", "conv_b", "dt_bias", "a_log", "d_skip", "ssm_norm_w",
               "proj_a", "proj_b", "w_out_mix", "mix_norm_post",
               "ffn2_norm_pre", "ffn2_w_in", "ffn2_w_out", "ffn2_norm_post")


def kernel(x, c, w_ada, b_ada, ffn1_norm_pre, ffn1_w_in, ffn1_w_out, ffn1_norm_post, mix_norm_pre,
           w_in_mix, rel_bias, conv_w, conv_b, dt_bias, a_log, d_skip, ssm_norm_w, proj_a, proj_b,
           w_out_mix, mix_norm_post, ffn2_norm_pre, ffn2_w_in, ffn2_w_out, ffn2_norm_post):
    stacked = dict(ffn1_norm_pre=ffn1_norm_pre, ffn1_w_in=ffn1_w_in, ffn1_w_out=ffn1_w_out,
                   ffn1_norm_post=ffn1_norm_post, mix_norm_pre=mix_norm_pre, w_in_mix=w_in_mix,
                   conv_w=conv_w, conv_b=conv_b, dt_bias=dt_bias, a_log=a_log, d_skip=d_skip,
                   ssm_norm_w=ssm_norm_w, proj_a=proj_a, proj_b=proj_b, w_out_mix=w_out_mix,
                   mix_norm_post=mix_norm_post, ffn2_norm_pre=ffn2_norm_pre, ffn2_w_in=ffn2_w_in,
                   ffn2_w_out=ffn2_w_out, ffn2_norm_post=ffn2_norm_post)
    batch, seq, _ = x.shape
    assert seq % ATTN_TILE == 0 and seq // MOBA_BLOCK <= MAX_BLOCKS and seq % SSM_CHUNK == 0
    depth = w_ada.shape[0]
    outs = []
    for b in range(batch):
        h = x[b]
        for l in range(depth):
            mod = _mod(c[b:b + 1], w_ada[l], b_ada[l])
            h = _layer(h, mod, rel_bias, {k: stacked[k][l] for k in _LAYER_KEYS})
        outs.append(h)
    return jnp.stack(outs)
```

```python
import functools
import math

import jax
import jax.numpy as jnp
from jax import lax
from jax.experimental import pallas as pl
from jax.experimental.pallas import tpu as pltpu

F32 = jnp.float32
BF16 = jnp.bfloat16
HIGHEST = lax.Precision.HIGHEST

D_MODEL = 1024
N_MOD = 9
RMS_EPS = 1e-6
FFN_HIDDEN = 2816
FFN_RES = 0.5

ATTN_HEADS = 8
HEAD_DIM = 128
ATTN_WIDTH = ATTN_HEADS * HEAD_DIM
MOBA_BLOCK = 256
MOBA_TOPK = 3
MAX_BLOCKS = 128
AUG_DIM = HEAD_DIM + MAX_BLOCKS
REL_BUCKETS = 32
REL_MAX_DIST = 128
MASKED = -1e30
LOG2E = math.log2(math.e)
ATTN_TILE = 512
BLOCKS_PER_TILE = ATTN_TILE // MOBA_BLOCK
BF16_SUBLANES = 16
V_ROWS = HEAD_DIM + BF16_SUBLANES
EXP_ROWS = 64

SSM_INNER = 2048
SSM_HEAD_DIM = 64
SSM_GROUPS = 8
SSM_HEADS = SSM_INNER // SSM_HEAD_DIM
SSM_HPG = SSM_HEADS // SSM_GROUPS
SSM_GROUP_W = SSM_INNER // SSM_GROUPS
SSM_STATE = 128
SSM_CONV = 4
SSM_CHUNK = 256
CONV_HALO = 8
SMALL_ROWS = 128
SSD_GROUPS_PER_STEP = 2

COL_Q = 0
COL_K = COL_Q + ATTN_WIDTH
COL_V = COL_K + ATTN_WIDTH
COL_Z = COL_V + ATTN_WIDTH
COL_X = COL_Z + SSM_INNER
COL_B = COL_X + SSM_INNER
COL_C = COL_B + SSM_GROUPS * SSM_STATE
COL_GA = COL_C + SSM_GROUPS * SSM_STATE
COL_GB = COL_GA + D_MODEL
PROJ_W = COL_GB + D_MODEL
DT_PAD = 128

VMEM_LIMIT = 56 * 1024 * 1024


def _params(sem):
    return pltpu.CompilerParams(dimension_semantics=sem, vmem_limit_bytes=VMEM_LIMIT)


def _sigmoid(x):
    return 0.5 + 0.5 * jnp.tanh(0.5 * x)


def _silu(x):
    return x * _sigmoid(x)


def _softplus(x):
    return jnp.maximum(x, 0.0) + jnp.log(1.0 + jnp.exp(-jnp.abs(x)))


def _rms(x, g):
    return x * lax.rsqrt(jnp.mean(x * x, axis=-1, keepdims=True) + RMS_EPS) * g


def _dot(a, b, **kw):
    return jnp.dot(a, b, preferred_element_type=F32, **kw)


def _dot_nt(a, b, **kw):
    return lax.dot_general(a, b, (((1,), (1,)), ((), ())), preferred_element_type=F32, **kw)


def _dot_tn(a, b, **kw):
    return lax.dot_general(a, b, (((0,), (0,)), ((), ())), preferred_element_type=F32, **kw)


def _mod_kernel(c_ref, w_ref, b_ref, o_ref):
    cs = _silu(c_ref[...])
    o_ref[...] = _dot(cs, w_ref[...], precision=HIGHEST) + b_ref[...]


def _mod(c, w_ada, b_ada):
    n = w_ada.shape[1]
    tn = 1024
    c8 = jnp.broadcast_to(c, (8, D_MODEL))
    out = pl.pallas_call(
        _mod_kernel,
        out_shape=jax.ShapeDtypeStruct((8, n), F32),
        grid=(n // tn,),
        in_specs=[pl.BlockSpec((8, D_MODEL), lambda j: (0, 0)),
                  pl.BlockSpec((D_MODEL, tn), lambda j: (0, j)),
                  pl.BlockSpec((1, tn), lambda j: (0, j))],
        out_specs=pl.BlockSpec((8, tn), lambda j: (0, j)),
        compiler_params=_params(("arbitrary",)),
        name="mod",
    )(c8, w_ada, b_ada.reshape(1, n))
    return out[0].reshape(N_MOD, 1, D_MODEL)


def _ffn_kernel(h_ref, gpre_ref, sh_ref, sc_ref, gate_ref, gpost_ref, wa_ref, wb_ref, wo_ref,
                o_ref, u_sc, acc_sc):
    j = pl.program_id(1)

    @pl.when(j == 0)
    def _():
        u = _rms(h_ref[...], gpre_ref[...]) * (1.0 + sc_ref[...]) + sh_ref[...]
        u_sc[...] = u.astype(BF16)
        acc_sc[...] = jnp.zeros_like(acc_sc)

    u = u_sc[...]
    a = _dot(u, wa_ref[...])
    b = _dot(u, wb_ref[...])
    mid = (_silu(a) * b).astype(BF16)
    acc_sc[...] += _dot(mid, wo_ref[...])

    @pl.when(j == pl.num_programs(1) - 1)
    def _():
        y = _rms(acc_sc[...], gpost_ref[...])
        o_ref[...] = h_ref[...] + (FFN_RES * gate_ref[...]) * y


def _ffn(h, gpre, sh, sc, gate, gpost, wa, wb, wo, tm, tf):
    s = h.shape[0]
    row = lambda i, j: (i, 0)
    vec = pl.BlockSpec((1, D_MODEL), lambda i, j: (0, 0))
    return pl.pallas_call(
        _ffn_kernel,
        out_shape=jax.ShapeDtypeStruct((s, D_MODEL), F32),
        grid=(s // tm, FFN_HIDDEN // tf),
        in_specs=[pl.BlockSpec((tm, D_MODEL), row), vec, vec, vec, vec, vec,
                  pl.BlockSpec((D_MODEL, tf), lambda i, j: (0, j)),
                  pl.BlockSpec((D_MODEL, tf), lambda i, j: (0, j)),
                  pl.BlockSpec((tf, D_MODEL), lambda i, j: (j, 0))],
        out_specs=pl.BlockSpec((tm, D_MODEL), row),
        scratch_shapes=[pltpu.VMEM((tm, D_MODEL), BF16), pltpu.VMEM((tm, D_MODEL), F32)],
        compiler_params=_params(("parallel", "arbitrary")),
        name="ffn",
    )(h, gpre, sh, sc, gate, gpost, wa, wb, wo)


def _inproj_kernel(h_ref, gpre_ref, sh_ref, sc_ref, w_ref, wdt_ref, o_ref, dt_ref, u_sc):
    j = pl.program_id(1)

    @pl.when(j == 0)
    def _():
        u = _rms(h_ref[...], gpre_ref[...]) * (1.0 + sc_ref[...]) + sh_ref[...]
        u_sc[...] = u.astype(BF16)
        dt_ref[...] = _dot(u_sc[...], wdt_ref[...])

    o_ref[...] = _dot(u_sc[...], w_ref[...]).astype(o_ref.dtype)


def _inproj(h, gpre, sh, sc, w, wdt, tm, tn):
    s = h.shape[0]
    vec = pl.BlockSpec((1, D_MODEL), lambda i, j: (0, 0))
    return pl.pallas_call(
        _inproj_kernel,
        out_shape=(jax.ShapeDtypeStruct((s, PROJ_W), BF16),
                   jax.ShapeDtypeStruct((s, DT_PAD), F32)),
        grid=(s // tm, PROJ_W // tn),
        in_specs=[pl.BlockSpec((tm, D_MODEL), lambda i, j: (i, 0)), vec, vec, vec,
                  pl.BlockSpec((D_MODEL, tn), lambda i, j: (0, j)),
                  pl.BlockSpec((D_MODEL, DT_PAD), lambda i, j: (0, 0))],
        out_specs=(pl.BlockSpec((tm, tn), lambda i, j: (i, j)),
                   pl.BlockSpec((tm, DT_PAD), lambda i, j: (i, 0))),
        scratch_shapes=[pltpu.VMEM((tm, D_MODEL), BF16)],
        compiler_params=_params(("parallel", "arbitrary")),
        name="inproj",
    )(h, gpre, sh, sc, w, wdt)


def _prep_kernel(n_sel, q_ref, k_ref, v_ref, qt_ref, ka_ref, vt_ref, km_sc):
    t = pl.program_id(1)

    @pl.when(t == 0)
    def _():
        km_sc[...] = jnp.zeros_like(km_sc)

    k = k_ref[...].astype(F32)
    for b in range(BLOCKS_PER_TILE):
        km_sc[pl.ds(t * BLOCKS_PER_TILE + b, 1), :] = jnp.mean(
            k[b * MOBA_BLOCK:(b + 1) * MOBA_BLOCK], axis=0, keepdims=True)

    qt = (q_ref[...].astype(F32) * (HEAD_DIM ** -0.5 * LOG2E)).T
    score = _dot(km_sc[:n_sel, :], qt, precision=HIGHEST)
    blk = lax.broadcasted_iota(jnp.int32, score.shape, 0)
    q_blk = t * BLOCKS_PER_TILE + lax.broadcasted_iota(jnp.int32, score.shape, 1) // MOBA_BLOCK
    s = jnp.where(blk < q_blk, score, -jnp.inf)
    pen = jnp.full(score.shape, MASKED, F32)
    for _ in range(MOBA_TOPK):
        m = jnp.max(s, axis=0, keepdims=True)
        first = jnp.min(jnp.where(s == m, blk, n_sel), axis=0, keepdims=True)
        first = jnp.where(m > -jnp.inf, first, n_sel)
        pick = blk == first
        pen = jnp.where(pick, 0.0, pen)
        s = jnp.where(pick, -jnp.inf, s)
    pen = jnp.where(blk == q_blk, 0.0, pen)
    qt_ref[0, 0, :HEAD_DIM, :] = qt.astype(BF16)
    qt_ref[0, 0, HEAD_DIM:HEAD_DIM + n_sel, :] = pen.astype(BF16)
    if n_sel < MAX_BLOCKS:
        qt_ref[0, 0, HEAD_DIM + n_sel:, :] = jnp.zeros((MAX_BLOCKS - n_sel, ATTN_TILE), BF16)

    lane = lax.broadcasted_iota(jnp.int32, (ATTN_TILE, MAX_BLOCKS), 1)
    k_blk = t * BLOCKS_PER_TILE + lax.broadcasted_iota(jnp.int32, lane.shape, 0) // MOBA_BLOCK
    ka_ref[0, :, :HEAD_DIM] = k.astype(BF16)
    ka_ref[0, :, HEAD_DIM:] = jnp.where(lane == k_blk, 1.0, 0.0).astype(BF16)

    ones_row = lax.broadcasted_iota(jnp.int32, (V_ROWS - HEAD_DIM, ATTN_TILE), 0) == 0
    vt_ref[0, 0, :HEAD_DIM, :] = v_ref[...].astype(F32).T.astype(BF16)
    vt_ref[0, 0, HEAD_DIM:, :] = jnp.where(ones_row, 1.0, 0.0).astype(BF16)


def _prep(proj):
    s = proj.shape[0]
    nt = s // ATTN_TILE
    blk = lambda col: pl.BlockSpec((ATTN_TILE, HEAD_DIM), lambda h, t: (t, col // HEAD_DIM + h))
    n_sel = -(-(s // MOBA_BLOCK) // BF16_SUBLANES) * BF16_SUBLANES
    return pl.pallas_call(
        functools.partial(_prep_kernel, n_sel),
        out_shape=(jax.ShapeDtypeStruct((ATTN_HEADS, nt, AUG_DIM, ATTN_TILE), BF16),
                   jax.ShapeDtypeStruct((ATTN_HEADS, s, AUG_DIM), BF16),
                   jax.ShapeDtypeStruct((ATTN_HEADS, nt, V_ROWS, ATTN_TILE), BF16)),
        grid=(ATTN_HEADS, nt),
        in_specs=[blk(COL_Q), blk(COL_K), blk(COL_V)],
        out_specs=(pl.BlockSpec((1, 1, AUG_DIM, ATTN_TILE), lambda h, t: (h, t, 0, 0)),
                   pl.BlockSpec((1, ATTN_TILE, AUG_DIM), lambda h, t: (h, t, 0)),
                   pl.BlockSpec((1, 1, V_ROWS, ATTN_TILE), lambda h, t: (h, t, 0, 0))),
        scratch_shapes=[pltpu.VMEM((MAX_BLOCKS, HEAD_DIM), F32)],
        compiler_params=_params(("parallel", "arbitrary")),
        name="prep",
    )(proj, proj, proj)


def _t5_bucket(rel):
    n = jnp.maximum(rel, 0)
    max_exact = REL_BUCKETS // 2
    nf = jnp.maximum(n, 1).astype(F32)
    large = max_exact + (jnp.log(nf / max_exact) / math.log(REL_MAX_DIST / max_exact)
                         * (REL_BUCKETS - max_exact)).astype(jnp.int32)
    large = jnp.minimum(large, REL_BUCKETS - 1)
    return jnp.where(n < max_exact, n, large)


def _bias_kernel(tab_ref, o_ref):
    h = pl.program_id(0)
    shape = (ATTN_TILE, ATTN_TILE)
    ki = lax.broadcasted_iota(jnp.int32, shape, 0)
    qi = lax.broadcasted_iota(jnp.int32, shape, 1)
    far = tab_ref[h * REL_BUCKETS + REL_BUCKETS - 1]
    for which in range(2):
        rel = qi - ki + which * ATTN_TILE
        bucket = _t5_bucket(rel)
        val = jnp.zeros(shape, F32)
        for b in range(REL_BUCKETS):
            val = jnp.where(bucket == b, tab_ref[h * REL_BUCKETS + b], val)
        val = (val - far) * LOG2E
        if which == 0:
            val = jnp.where(rel >= 0, val, MASKED)
        o_ref[0, which] = val


def _bias_tiles(tab_flat):
    return pl.pallas_call(
        _bias_kernel,
        out_shape=jax.ShapeDtypeStruct((ATTN_HEADS, 2, ATTN_TILE, ATTN_TILE), F32),
        grid=(ATTN_HEADS,),
        in_specs=[pl.BlockSpec(memory_space=pltpu.SMEM)],
        out_specs=pl.BlockSpec((1, 2, ATTN_TILE, ATTN_TILE), lambda h: (h, 0, 0, 0)),
        compiler_params=_params(("parallel",)),
        name="bias",
    )(tab_flat)


def _attn_kernel(qt_ref, ka_ref, vt_ref, t_ref, o_ref, m_sc, acc_sc, s_sc, p_sc):
    t = pl.program_id(1)
    qt = qt_ref[0, 0]

    def scores(j):
        start = pl.multiple_of(j * ATTN_TILE, ATTN_TILE)
        return _dot(ka_ref[0, pl.ds(start, ATTN_TILE), :], qt)

    m_sc[...] = jnp.full(m_sc.shape, 4.0 * MASKED, F32)
    acc_sc[...] = jnp.zeros_like(acc_sc)

    def colmax(s_ref):
        return jnp.max(s_ref[...], axis=0, keepdims=True)

    def consume(s_ref, s_max, j):
        m_old = m_sc[...]
        m_new = jnp.maximum(m_old, s_max)
        m_sc[...] = m_new
        for r in range(0, ATTN_TILE, EXP_ROWS):
            p_sc[r:r + EXP_ROWS, :] = jnp.exp2(s_ref[r:r + EXP_ROWS, :] - m_new).astype(BF16)
        acc_sc[...] = jnp.exp2(m_old - m_new) * acc_sc[...] + _dot(vt_ref[0, j], p_sc[...])

    def far_step(j, cur, nxt, max_cur):
        nxt[...] = scores(j + 1)
        consume(cur, max_cur, j)
        return colmax(nxt)

    s_a, s_b = s_sc.at[0], s_sc.at[1]

    @pl.when(t >= 1)
    def _():
        n_far = t - 1
        s_a[...] = scores(0)

        def pair(i, max_cur):
            max_cur = far_step(2 * i, s_a, s_b, max_cur)
            return far_step(2 * i + 1, s_b, s_a, max_cur)

        max_cur = lax.fori_loop(0, n_far // 2, pair, colmax(s_a))
        odd = n_far % 2 == 1

        @pl.when(odd)
        def _():
            far_step(n_far - 1, s_a, s_b, max_cur)
            s_b[...] = s_b[...] + t_ref[0, 1]
            consume(s_b, colmax(s_b), t - 1)

        @pl.when(jnp.logical_not(odd))
        def _():
            s_a[...] = s_a[...] + t_ref[0, 1]
            consume(s_a, colmax(s_a), t - 1)

    s_a[...] = scores(t) + t_ref[0, 0]
    consume(s_a, colmax(s_a), t)
    acc = acc_sc[...]
    out = acc[:HEAD_DIM] / acc[HEAD_DIM:HEAD_DIM + 1]
    o_ref[...] = out.T.astype(o_ref.dtype)


def _attn(qt, ka, vt, tiles):
    nt = qt.shape[1]
    s = nt * ATTN_TILE
    return pl.pallas_call(
        _attn_kernel,
        out_shape=jax.ShapeDtypeStruct((s, ATTN_WIDTH), BF16),
        grid=(ATTN_HEADS, nt),
        in_specs=[pl.BlockSpec((1, 1, AUG_DIM, ATTN_TILE), lambda h, t: (h, t, 0, 0)),
                  pl.BlockSpec((1, s, AUG_DIM), lambda h, t: (h, 0, 0)),
                  pl.BlockSpec((1, nt, V_ROWS, ATTN_TILE), lambda h, t: (h, 0, 0, 0)),
                  pl.BlockSpec((1, 2, ATTN_TILE, ATTN_TILE), lambda h, t: (h, 0, 0, 0))],
        out_specs=pl.BlockSpec((ATTN_TILE, HEAD_DIM), lambda h, t: (t, h)),
        scratch_shapes=[pltpu.VMEM((1, ATTN_TILE), F32), pltpu.VMEM((V_ROWS, ATTN_TILE), F32),
                        pltpu.VMEM((2, ATTN_TILE, ATTN_TILE), F32),
                        pltpu.VMEM((ATTN_TILE, ATTN_TILE), BF16)],
        compiler_params=_params(("parallel", "arbitrary")),
        name="attn",
    )(qt, ka, vt, tiles)


def _expand_heads(d, lane_head):
    out = d[:, SSM_HPG - 1:SSM_HPG]
    for hg in range(SSM_HPG - 2, -1, -1):
        out = jnp.where(lane_head == hg, d[:, hg:hg + 1], out)
    return out


def _ssd_kernel(z_ref, x_ref, b_ref, c_ref, dtc_ref, wx_ref, wb_ref, wc_ref, bx_ref, bb_ref, bc_ref,
                dtb_ref, alog_ref, dskip_ref, nw_ref, o_ref, xpx_sc, xpb_sc, xpc_sc, st_sc):
    c = pl.program_id(1)
    L, GW, NS, hp = SSM_CHUNK, SSM_GROUP_W, SSM_STATE, SSM_HPG

    @pl.when(c == 0)
    def _():
        xpx_sc[0:CONV_HALO, :] = jnp.zeros((CONV_HALO, xpx_sc.shape[1]), F32)
        xpb_sc[0:CONV_HALO, :] = jnp.zeros((CONV_HALO, xpb_sc.shape[1]), F32)
        xpc_sc[0:CONV_HALO, :] = jnp.zeros((CONV_HALO, xpc_sc.shape[1]), F32)
        st_sc[...] = jnp.zeros_like(st_sc)

    def conv(src_ref, pad_sc, w_ref, bias_ref):
        pad_sc[CONV_HALO:CONV_HALO + L, :] = src_ref[...].astype(F32)
        acc = bias_ref[...]
        for j in range(SSM_CONV):
            lo = CONV_HALO - (SSM_CONV - 1) + j
            acc = acc + pad_sc[lo:lo + L, :] * w_ref[j:j + 1, :]
        pad_sc[0:CONV_HALO, :] = pad_sc[L:L + CONV_HALO, :]
        return _silu(acc)

    x_all = conv(x_ref, xpx_sc, wx_ref, bx_ref)
    b_all = conv(b_ref, xpb_sc, wb_ref, bb_ref)
    c_all = conv(c_ref, xpc_sc, wc_ref, bc_ref)

    row = lax.broadcasted_iota(jnp.int32, (L, L), 0)
    col = lax.broadcasted_iota(jnp.int32, (L, L), 1)
    causal = row >= col
    triu = jnp.where(row <= col, 1.0, 0.0).astype(F32)
    lane_head = lax.broadcasted_iota(jnp.int32, (1, GW), 1) // SSM_HEAD_DIM

    for gi in range(SSD_GROUPS_PER_STEP):
        x = x_all[:, gi * GW:(gi + 1) * GW]
        bm = b_all[:, gi * NS:(gi + 1) * NS].astype(BF16)
        cm = c_all[:, gi * NS:(gi + 1) * NS].astype(BF16)

        dt_c = _softplus(dtc_ref[gi] + dtb_ref[gi])
        a_c = -jnp.exp(alog_ref[gi])
        cum_c = _dot(dt_c * a_c, triu, precision=HIGHEST)
        last_c = cum_c[:, L - 1:L]
        small = jnp.concatenate(
            [dt_c, cum_c, jnp.exp(cum_c), jnp.exp(last_c - cum_c) * dt_c,
             jnp.zeros((SMALL_ROWS - 4 * hp, L), F32)], axis=0).T
        cum_r = small[:, hp:2 * hp]
        dt_x = _expand_heads(small[:, 0:hp], lane_head)
        grow_x = _expand_heads(small[:, 2 * hp:3 * hp], lane_head)
        end_x = _expand_heads(small[:, 3 * hp:4 * hp], lane_head)
        last_x = _expand_heads(small[L - 1:L, 2 * hp:3 * hp], lane_head)

        cb = _dot_nt(cm, bm)
        xdt = x * dt_x
        w_parts, x_parts = [], []
        for hg in range(hp):
            seg = cum_r[:, hg:hg + 1] - cum_c[hg:hg + 1, :]
            w_parts.append(cb * jnp.exp(jnp.where(causal, seg, -jnp.inf)))
            x_parts.append(jnp.where(lane_head == hg, xdt, 0.0))
        y = _dot(jnp.concatenate(w_parts, axis=1).astype(BF16),
                 jnp.concatenate(x_parts, axis=0).astype(BF16))

        st = st_sc[gi]
        y = y + _dot(cm, st.astype(BF16)) * grow_x
        st_sc[gi] = last_x * st + _dot_tn(bm, (x * end_x).astype(BF16))
        y = y + x * dskip_ref[gi]

        g = y * _silu(z_ref[:, gi * GW:(gi + 1) * GW].astype(F32))
        g = g * lax.rsqrt(jnp.mean(g * g, axis=-1, keepdims=True) + RMS_EPS)
        o_ref[:, gi * GW:(gi + 1) * GW] = (g * nw_ref[gi]).astype(o_ref.dtype)


def _ssd(proj, dt_cols, conv_w, conv_b, dtb, alog, dskip_x, norm_w, nc):
    s = proj.shape[0]
    n = SSD_GROUPS_PER_STEP
    L, GW, NS = SSM_CHUNK, n * SSM_GROUP_W, n * SSM_STATE
    xoff, boff, coff = 0, SSM_INNER, SSM_INNER + SSM_GROUPS * SSM_STATE
    per_step = lambda a: pl.BlockSpec((n,) + a.shape[1:], lambda g, c: (g, 0, 0))
    return pl.pallas_call(
        _ssd_kernel,
        out_shape=jax.ShapeDtypeStruct((s, SSM_INNER), BF16),
        grid=(SSM_GROUPS // n, nc),
        in_specs=[
            pl.BlockSpec((L, GW), lambda g, c: (c, COL_Z // GW + g)),
            pl.BlockSpec((L, GW), lambda g, c: (c, COL_X // GW + g)),
            pl.BlockSpec((L, NS), lambda g, c: (c, COL_B // NS + g)),
            pl.BlockSpec((L, NS), lambda g, c: (c, COL_C // NS + g)),
            pl.BlockSpec((n, SSM_HPG, L), lambda g, c: (g, 0, c)),
            pl.BlockSpec((SSM_CONV, GW), lambda g, c: (0, xoff // GW + g)),
            pl.BlockSpec((SSM_CONV, NS), lambda g, c: (0, boff // NS + g)),
            pl.BlockSpec((SSM_CONV, NS), lambda g, c: (0, coff // NS + g)),
            pl.BlockSpec((1, GW), lambda g, c: (0, xoff // GW + g)),
            pl.BlockSpec((1, NS), lambda g, c: (0, boff // NS + g)),
            pl.BlockSpec((1, NS), lambda g, c: (0, coff // NS + g)),
            per_step(dtb), per_step(alog), per_step(dskip_x), per_step(norm_w),
        ],
        out_specs=pl.BlockSpec((L, GW), lambda g, c: (c, g)),
        scratch_shapes=[pltpu.VMEM((L + CONV_HALO, GW), F32), pltpu.VMEM((L + CONV_HALO, NS), F32),
                        pltpu.VMEM((L + CONV_HALO, NS), F32),
                        pltpu.VMEM((n, SSM_STATE, SSM_GROUP_W), F32)],
        compiler_params=_params(("parallel", "arbitrary")),
        name="ssd",
    )(proj, proj, proj, proj, dt_cols, conv_w, conv_w, conv_w, conv_b, conv_b, conv_b,
      dtb, alog, dskip_x, norm_w)


def _merge_kernel(h_ref, a_ref, b_ref, ga_ref, gb_ref, pa_ref, pb_ref, wo_ref, gate_ref, gpost_ref,
                  o_ref):
    ya = _dot(a_ref[...], pa_ref[...])
    yb = _dot(b_ref[...], pb_ref[...])
    mix = _sigmoid(ga_ref[...].astype(F32)) * ya + _sigmoid(gb_ref[...].astype(F32)) * yb
    y = _dot(mix.astype(BF16), wo_ref[...])
    o_ref[...] = h_ref[...] + gate_ref[...] * _rms(y, gpost_ref[...])


def _merge(h, attn, ssd, proj, pa, pb, wo, gate, gpost, tm):
    s = h.shape[0]
    vec = pl.BlockSpec((1, D_MODEL), lambda i: (0, 0))
    full = lambda a: pl.BlockSpec(a.shape, lambda i: (0, 0))
    return pl.pallas_call(
        _merge_kernel,
        out_shape=jax.ShapeDtypeStruct((s, D_MODEL), F32),
        grid=(s // tm,),
        in_specs=[pl.BlockSpec((tm, D_MODEL), lambda i: (i, 0)),
                  pl.BlockSpec((tm, ATTN_WIDTH), lambda i: (i, 0)),
                  pl.BlockSpec((tm, SSM_INNER), lambda i: (i, 0)),
                  pl.BlockSpec((tm, D_MODEL), lambda i: (i, COL_GA // D_MODEL)),
                  pl.BlockSpec((tm, D_MODEL), lambda i: (i, COL_GB // D_MODEL)),
                  full(pa), full(pb), full(wo), vec, vec],
        out_specs=pl.BlockSpec((tm, D_MODEL), lambda i: (i, 0)),
        compiler_params=_params(("parallel",)),
        name="merge",
    )(h, attn, ssd, proj, proj, pa, pb, wo, gate, gpost)


def _layer(h, mod, rel_bias, p):
    s = h.shape[0]
    nb = s // MOBA_BLOCK
    nc = s // SSM_CHUNK
    tm = min(512, s)
    sh1, sc1, g1, shm, scm, gm, sh2, sc2, g2 = [mod[k] for k in range(N_MOD)]
    vec = lambda a: a.reshape(1, -1)

    def ffn_weights(w_in, w_out):
        return (w_in[:, :FFN_HIDDEN].astype(BF16), w_in[:, FFN_HIDDEN:].astype(BF16),
                w_out.astype(BF16))

    h = _ffn(h, vec(p["ffn1_norm_pre"]), sh1, sc1, g1, vec(p["ffn1_norm_post"]),
             *ffn_weights(p["ffn1_w_in"], p["ffn1_w_out"]), tm=tm, tf=FFN_HIDDEN // 2)

    w_in = p["w_in_mix"]
    dt_lo = COL_GA
    w_main = jnp.concatenate([w_in[:, :dt_lo], w_in[:, dt_lo + SSM_HEADS:]], axis=1).astype(BF16)
    w_dt = jnp.pad(w_in[:, dt_lo:dt_lo + SSM_HEADS], ((0, 0), (0, DT_PAD - SSM_HEADS))).astype(BF16)
    proj, dt_raw = _inproj(h, vec(p["mix_norm_pre"]), shm, scm, w_main, w_dt, tm=min(1024, s), tn=1024)

    tab_flat = rel_bias.T.reshape(-1)
    qt, ka, vt = _prep(proj)
    tiles = _bias_tiles(tab_flat)
    attn = _attn(qt, ka, vt, tiles)

    dt_cols = dt_raw[:, :SSM_HEADS].reshape(s, SSM_GROUPS, SSM_HPG).transpose(1, 2, 0)
    per_group = lambda a: a.reshape(SSM_GROUPS, SSM_HPG, 1)
    dskip_x = jnp.repeat(p["d_skip"], SSM_HEAD_DIM).reshape(SSM_GROUPS, 1, SSM_GROUP_W)
    ssd = _ssd(proj, dt_cols, p["conv_w"], vec(p["conv_b"]),
               per_group(p["dt_bias"]), per_group(p["a_log"]),
               dskip_x, p["ssm_norm_w"].reshape(SSM_GROUPS, 1, SSM_GROUP_W), nc)

    h = _merge(h, attn, ssd, proj, p["proj_a"].astype(BF16), p["proj_b"].astype(BF16),
               p["w_out_mix"].astype(BF16), gm, vec(p["mix_norm_post"]), tm=tm)

    h = _ffn(h, vec(p["ffn2_norm_pre"]), sh2, sc2, g2, vec(p["ffn2_norm_post"]),
             *ffn_weights(p["ffn2_w_in"], p["ffn2_w_out"]), tm=tm, tf=FFN_HIDDEN // 2)
    return h


_LAYER_KEYS = ("ffn1_norm_pre", "ffn1_w_in", "ffn1_w_out", "ffn1_norm_post", "mix_norm_pre",
               "w_in_mix", "conv_w", "conv_b", "dt_bias", "a_log", "d_skip", "ssm_norm_w",
               "proj_a", "proj_b", "w_out_mix", "mix_norm_post",
               "ffn2_norm_pre", "ffn2_w_in", "ffn2_w_out", "ffn2_norm_post")


def kernel(x, c, w_ada, b_ada, ffn1_norm_pre, ffn1_w_in, ffn1_w_out, ffn1_norm_post, mix_norm_pre,
           w_in_mix, rel_bias, conv_w, conv_b, dt_bias, a_log, d_skip, ssm_norm_w, proj_a, proj_b,
           w_out_mix, mix_norm_post, ffn2_norm_pre, ffn2_w_in, ffn2_w_out, ffn2_norm_post):
    stacked = dict(ffn1_norm_pre=ffn1_norm_pre, ffn1_w_in=ffn1_w_in, ffn1_w_out=ffn1_w_out,
                   ffn1_norm_post=ffn1_norm_post, mix_norm_pre=mix_norm_pre, w_in_mix=w_in_mix,
                   conv_w=conv_w, conv_b=conv_b, dt_bias=dt_bias, a_log=a_log, d_skip=d_skip,
                   ssm_norm_w=ssm_norm_w, proj_a=proj_a, proj_b=proj_b, w_out_mix=w_out_mix,
                   mix_norm_post=mix_norm_post, ffn2_norm_pre=ffn2_norm_pre, ffn2_w_in=ffn2_w_in,
                   ffn2_w_out=ffn2_w_out, ffn2_norm_post=ffn2_norm_post)
    batch, seq, _ = x.shape
    assert seq % ATTN_TILE == 0 and seq // MOBA_BLOCK <= MAX_BLOCKS and seq % SSM_CHUNK == 0
    depth = w_ada.shape[0]
    outs = []
    for b in range(batch):
        h = x[b]
        for l in range(depth):
            mod = _mod(c[b:b + 1], w_ada[l], b_ada[l])
            h = _layer(h, mod, rel_bias, {k: stacked[k][l] for k in _LAYER_KEYS})
        outs.append(h)
    return jnp.stack(outs)
```

```python
import functools
import math

import jax
import jax.numpy as jnp
from jax import lax
from jax.experimental import pallas as pl
from jax.experimental.pallas import tpu as pltpu

F32 = jnp.float32
BF16 = jnp.bfloat16
HIGHEST = lax.Precision.HIGHEST

D_MODEL = 1024
N_MOD = 9
RMS_EPS = 1e-6
FFN_HIDDEN = 2816
FFN_RES = 0.5

ATTN_HEADS = 8
HEAD_DIM = 128
ATTN_WIDTH = ATTN_HEADS * HEAD_DIM
MOBA_BLOCK = 256
MOBA_TOPK = 3
MAX_BLOCKS = 128
AUG_DIM = HEAD_DIM + MAX_BLOCKS
REL_BUCKETS = 32
REL_MAX_DIST = 128
MASKED = -1e30
LOG2E = math.log2(math.e)
ATTN_TILE = 512
BLOCKS_PER_TILE = ATTN_TILE // MOBA_BLOCK
BF16_SUBLANES = 16
V_ROWS = HEAD_DIM + BF16_SUBLANES
EXP_ROWS = 64

SSM_INNER = 2048
SSM_HEAD_DIM = 64
SSM_GROUPS = 8
SSM_HEADS = SSM_INNER // SSM_HEAD_DIM
SSM_HPG = SSM_HEADS // SSM_GROUPS
SSM_GROUP_W = SSM_INNER // SSM_GROUPS
SSM_STATE = 128
SSM_CONV = 4
SSM_CHUNK = 256
CONV_HALO = 8
SMALL_ROWS = 128
SSD_GROUPS_PER_STEP = 2

PROJ_TILE = 1024
TILE_Q, TILE_K, TILE_V = 0, 1, 2
TILE_Z0 = 3
TILE_CONV0 = TILE_Z0 + SSM_INNER // PROJ_TILE
N_CONV_TILES = (SSM_INNER + 2 * SSM_GROUPS * SSM_STATE) // PROJ_TILE
TILE_GATE0 = TILE_CONV0 + N_CONV_TILES
N_PROJ_TILES = TILE_GATE0 + 2 * D_MODEL // PROJ_TILE
DT_COL = TILE_GATE0 * PROJ_TILE
REST_Z = 0
REST_X = REST_Z + SSM_INNER
REST_B = REST_X + SSM_INNER
REST_C = REST_B + SSM_GROUPS * SSM_STATE
REST_GA = REST_C + SSM_GROUPS * SSM_STATE
REST_GB = REST_GA + D_MODEL
REST_W = REST_GB + D_MODEL
DT_PAD = 128

VMEM_LIMIT = 56 * 1024 * 1024


def _params(sem):
    return pltpu.CompilerParams(dimension_semantics=sem, vmem_limit_bytes=VMEM_LIMIT)


def _sigmoid(x):
    return 0.5 + 0.5 * jnp.tanh(0.5 * x)


def _silu(x):
    return x * _sigmoid(x)


def _softplus(x):
    return jnp.maximum(x, 0.0) + jnp.log(1.0 + jnp.exp(-jnp.abs(x)))


def _rms(x, g):
    return x * lax.rsqrt(jnp.mean(x * x, axis=-1, keepdims=True) + RMS_EPS) * g


def _dot(a, b, **kw):
    return jnp.dot(a, b, preferred_element_type=F32, **kw)


def _dot_nt(a, b, **kw):
    return lax.dot_general(a, b, (((1,), (1,)), ((), ())), preferred_element_type=F32, **kw)


def _dot_tn(a, b, **kw):
    return lax.dot_general(a, b, (((0,), (0,)), ((), ())), preferred_element_type=F32, **kw)


def _mod_kernel(c_ref, w_ref, b_ref, o_ref):
    cs = _silu(c_ref[...])
    o_ref[...] = _dot(cs, w_ref[...], precision=HIGHEST) + b_ref[...]


def _mod(c, w_ada, b_ada):
    n = w_ada.shape[1]
    tn = 1024
    c8 = jnp.broadcast_to(c, (8, D_MODEL))
    out = pl.pallas_call(
        _mod_kernel,
        out_shape=jax.ShapeDtypeStruct((8, n), F32),
        grid=(n // tn,),
        in_specs=[pl.BlockSpec((8, D_MODEL), lambda j: (0, 0)),
                  pl.BlockSpec((D_MODEL, tn), lambda j: (0, j)),
                  pl.BlockSpec((1, tn), lambda j: (0, j))],
        out_specs=pl.BlockSpec((8, tn), lambda j: (0, j)),
        compiler_params=_params(("arbitrary",)),
        name="mod",
    )(c8, w_ada, b_ada.reshape(1, n))
    return out[0].reshape(N_MOD, 1, D_MODEL)


def _ffn_kernel(h_ref, gpre_ref, sh_ref, sc_ref, gate_ref, gpost_ref, wa_ref, wb_ref, wo_ref,
                o_ref, u_sc, acc_sc):
    j = pl.program_id(1)

    @pl.when(j == 0)
    def _():
        u = _rms(h_ref[...], gpre_ref[...]) * (1.0 + sc_ref[...]) + sh_ref[...]
        u_sc[...] = u.astype(BF16)
        acc_sc[...] = jnp.zeros_like(acc_sc)

    u = u_sc[...]
    a = _dot(u, wa_ref[...])
    b = _dot(u, wb_ref[...])
    mid = (_silu(a) * b).astype(BF16)
    acc_sc[...] += _dot(mid, wo_ref[...])

    @pl.when(j == pl.num_programs(1) - 1)
    def _():
        y = _rms(acc_sc[...], gpost_ref[...])
        o_ref[...] = h_ref[...] + (FFN_RES * gate_ref[...]) * y


def _ffn(h, gpre, sh, sc, gate, gpost, wa, wb, wo, tm, tf):
    s = h.shape[0]
    row = lambda i, j: (i, 0)
    vec = pl.BlockSpec((1, D_MODEL), lambda i, j: (0, 0))
    return pl.pallas_call(
        _ffn_kernel,
        out_shape=jax.ShapeDtypeStruct((s, D_MODEL), F32),
        grid=(s // tm, FFN_HIDDEN // tf),
        in_specs=[pl.BlockSpec((tm, D_MODEL), row), vec, vec, vec, vec, vec,
                  pl.BlockSpec((D_MODEL, tf), lambda i, j: (0, j)),
                  pl.BlockSpec((D_MODEL, tf), lambda i, j: (0, j)),
                  pl.BlockSpec((tf, D_MODEL), lambda i, j: (j, 0))],
        out_specs=pl.BlockSpec((tm, D_MODEL), row),
        scratch_shapes=[pltpu.VMEM((tm, D_MODEL), BF16), pltpu.VMEM((tm, D_MODEL), F32)],
        compiler_params=_params(("parallel", "arbitrary")),
        name="ffn",
    )(h, gpre, sh, sc, gate, gpost, wa, wb, wo)


def _inproj_kernel(h_ref, gpre_ref, sh_ref, sc_ref, w_ref, wg_ref, wdt_ref, cw_ref, cb_ref,
                   q_ref, k_ref, v_ref, o_ref, dt_ref, u_sc, pad_sc, halo_sc):
    i = pl.program_id(0)
    j = pl.program_id(1)
    tm = u_sc.shape[0]

    @pl.when(j == 0)
    def _():
        u = _rms(h_ref[...], gpre_ref[...]) * (1.0 + sc_ref[...]) + sh_ref[...]
        u_sc[...] = u.astype(BF16)
        dt_ref[...] = _dot(u_sc[...], wdt_ref[...])

    for tile, dst in ((TILE_Q, q_ref), (TILE_K, k_ref), (TILE_V, v_ref)):
        @pl.when(j == tile)
        def _(dst=dst):
            r = _dot(u_sc[...], w_ref[...]).astype(BF16)
            for hd in range(ATTN_HEADS):
                dst[hd] = r[:, hd * HEAD_DIM:(hd + 1) * HEAD_DIM]

    @pl.when(jnp.logical_and(j >= TILE_Z0, j < TILE_CONV0))
    def _():
        o_ref[...] = _silu(_dot(u_sc[...], w_ref[...])).astype(o_ref.dtype)

    @pl.when(jnp.logical_and(j >= TILE_CONV0, j < TILE_GATE0))
    def _():
        raw = _dot(u_sc[...], w_ref[...])
        c = j - TILE_CONV0
        pad_sc[0:CONV_HALO, :] = jnp.where(i == 0, 0.0, halo_sc[c])
        pad_sc[CONV_HALO:CONV_HALO + tm, :] = raw
        halo_sc[c] = raw[tm - CONV_HALO:tm, :]
        acc = cb_ref[...]
        for tap in range(SSM_CONV):
            lo = CONV_HALO - (SSM_CONV - 1) + tap
            acc = acc + pad_sc[lo:lo + tm, :] * cw_ref[tap:tap + 1, :]
        o_ref[...] = _silu(acc).astype(o_ref.dtype)

    @pl.when(j >= TILE_GATE0)
    def _():
        o_ref[...] = _sigmoid(_dot(u_sc[...], wg_ref[...])).astype(o_ref.dtype)


def _inproj(h, gpre, sh, sc, w, wg, wdt, conv_w, conv_b, tm):
    s = h.shape[0]
    tn = PROJ_TILE
    vec = pl.BlockSpec((1, D_MODEL), lambda i, j: (0, 0))
    head_major = jax.ShapeDtypeStruct((ATTN_HEADS, s, HEAD_DIM), BF16)
    head_spec = pl.BlockSpec((ATTN_HEADS, tm, HEAD_DIM), lambda i, j: (0, i, 0))
    conv_tile = lambda i, j: (0, jnp.clip(j - TILE_CONV0, 0, N_CONV_TILES - 1))
    return pl.pallas_call(
        _inproj_kernel,
        out_shape=(head_major, head_major, head_major,
                   jax.ShapeDtypeStruct((s, REST_W), BF16),
                   jax.ShapeDtypeStruct((s, DT_PAD), F32)),
        grid=(s // tm, N_PROJ_TILES),
        in_specs=[pl.BlockSpec((tm, D_MODEL), lambda i, j: (i, 0)), vec, vec, vec,
                  pl.BlockSpec((D_MODEL, tn), lambda i, j: (0, jnp.minimum(j, TILE_GATE0 - 1))),
                  pl.BlockSpec((D_MODEL, tn), lambda i, j: (0, jnp.maximum(j - TILE_GATE0, 0))),
                  pl.BlockSpec((D_MODEL, DT_PAD), lambda i, j: (0, 0)),
                  pl.BlockSpec((SSM_CONV, tn), conv_tile),
                  pl.BlockSpec((1, tn), conv_tile)],
        out_specs=(head_spec, head_spec, head_spec,
                   pl.BlockSpec((tm, tn), lambda i, j: (i, jnp.maximum(j - TILE_Z0, 0))),
                   pl.BlockSpec((tm, DT_PAD), lambda i, j: (i, 0))),
        scratch_shapes=[pltpu.VMEM((tm, D_MODEL), BF16),
                        pltpu.VMEM((tm + CONV_HALO, tn), F32),
                        pltpu.VMEM((N_CONV_TILES, CONV_HALO, tn), F32)],
        compiler_params=_params(("arbitrary", "arbitrary")),
        name="inproj",
    )(h, gpre, sh, sc, w, wg, wdt, conv_w, conv_b)


def _prep_kernel(n_sel, q_ref, k_ref, v_ref, qt_ref, ka_ref, vt_ref, km_sc):
    t = pl.program_id(1)

    @pl.when(t == 0)
    def _():
        km_sc[...] = jnp.zeros_like(km_sc)

    k = k_ref[0].astype(F32)
    for b in range(BLOCKS_PER_TILE):
        km_sc[pl.ds(t * BLOCKS_PER_TILE + b, 1), :] = jnp.mean(
            k[b * MOBA_BLOCK:(b + 1) * MOBA_BLOCK], axis=0, keepdims=True)

    qt = (q_ref[0].astype(F32) * (HEAD_DIM ** -0.5 * LOG2E)).T
    score = _dot(km_sc[:n_sel, :], qt, precision=HIGHEST)
    blk = lax.broadcasted_iota(jnp.int32, score.shape, 0)
    q_blk = t * BLOCKS_PER_TILE + lax.broadcasted_iota(jnp.int32, score.shape, 1) // MOBA_BLOCK
    s = jnp.where(blk < q_blk, score, -jnp.inf)
    pen = jnp.full(score.shape, MASKED, F32)
    for _ in range(MOBA_TOPK):
        m = jnp.max(s, axis=0, keepdims=True)
        first = jnp.min(jnp.where(s == m, blk, n_sel), axis=0, keepdims=True)
        first = jnp.where(m > -jnp.inf, first, n_sel)
        pick = blk == first
        pen = jnp.where(pick, 0.0, pen)
        s = jnp.where(pick, -jnp.inf, s)
    pen = jnp.where(blk == q_blk, 0.0, pen)
    qt_ref[0, 0, :HEAD_DIM, :] = qt.astype(BF16)
    qt_ref[0, 0, HEAD_DIM:HEAD_DIM + n_sel, :] = pen.astype(BF16)
    if n_sel < MAX_BLOCKS:
        qt_ref[0, 0, HEAD_DIM + n_sel:, :] = jnp.zeros((MAX_BLOCKS - n_sel, ATTN_TILE), BF16)

    lane = lax.broadcasted_iota(jnp.int32, (ATTN_TILE, MAX_BLOCKS), 1)
    k_blk = t * BLOCKS_PER_TILE + lax.broadcasted_iota(jnp.int32, lane.shape, 0) // MOBA_BLOCK
    ka_ref[0, :, :HEAD_DIM] = k.astype(BF16)
    ka_ref[0, :, HEAD_DIM:] = jnp.where(lane == k_blk, 1.0, 0.0).astype(BF16)

    ones_row = lax.broadcasted_iota(jnp.int32, (V_ROWS - HEAD_DIM, ATTN_TILE), 0) == 0
    vt_ref[0, 0, :HEAD_DIM, :] = v_ref[0].astype(F32).T.astype(BF16)
    vt_ref[0, 0, HEAD_DIM:, :] = jnp.where(ones_row, 1.0, 0.0).astype(BF16)


def _prep(qh, kh, vh):
    s = qh.shape[1]
    nt = s // ATTN_TILE
    blk = pl.BlockSpec((1, ATTN_TILE, HEAD_DIM), lambda h, t: (h, t, 0))
    n_sel = -(-(s // MOBA_BLOCK) // BF16_SUBLANES) * BF16_SUBLANES
    return pl.pallas_call(
        functools.partial(_prep_kernel, n_sel),
        out_shape=(jax.ShapeDtypeStruct((ATTN_HEADS, nt, AUG_DIM, ATTN_TILE), BF16),
                   jax.ShapeDtypeStruct((ATTN_HEADS, s, AUG_DIM), BF16),
                   jax.ShapeDtypeStruct((ATTN_HEADS, nt, V_ROWS, ATTN_TILE), BF16)),
        grid=(ATTN_HEADS, nt),
        in_specs=[blk, blk, blk],
        out_specs=(pl.BlockSpec((1, 1, AUG_DIM, ATTN_TILE), lambda h, t: (h, t, 0, 0)),
                   pl.BlockSpec((1, ATTN_TILE, AUG_DIM), lambda h, t: (h, t, 0)),
                   pl.BlockSpec((1, 1, V_ROWS, ATTN_TILE), lambda h, t: (h, t, 0, 0))),
        scratch_shapes=[pltpu.VMEM((MAX_BLOCKS, HEAD_DIM), F32)],
        compiler_params=_params(("parallel", "arbitrary")),
        name="prep",
    )(qh, kh, vh)


def _t5_bucket(rel):
    n = jnp.maximum(rel, 0)
    max_exact = REL_BUCKETS // 2
    nf = jnp.maximum(n, 1).astype(F32)
    large = max_exact + (jnp.log(nf / max_exact) / math.log(REL_MAX_DIST / max_exact)
                         * (REL_BUCKETS - max_exact)).astype(jnp.int32)
    large = jnp.minimum(large, REL_BUCKETS - 1)
    return jnp.where(n < max_exact, n, large)


def _bias_kernel(tab_ref, o_ref):
    h = pl.program_id(0)
    shape = (ATTN_TILE, ATTN_TILE)
    ki = lax.broadcasted_iota(jnp.int32, shape, 0)
    qi = lax.broadcasted_iota(jnp.int32, shape, 1)
    far = tab_ref[h * REL_BUCKETS + REL_BUCKETS - 1]
    for which in range(2):
        rel = qi - ki + which * ATTN_TILE
        bucket = _t5_bucket(rel)
        val = jnp.zeros(shape, F32)
        for b in range(REL_BUCKETS):
            val = jnp.where(bucket == b, tab_ref[h * REL_BUCKETS + b], val)
        val = (val - far) * LOG2E
        if which == 0:
            val = jnp.where(rel >= 0, val, MASKED)
        o_ref[0, which] = val


def _bias_tiles(tab_flat):
    return pl.pallas_call(
        _bias_kernel,
        out_shape=jax.ShapeDtypeStruct((ATTN_HEADS, 2, ATTN_TILE, ATTN_TILE), F32),
        grid=(ATTN_HEADS,),
        in_specs=[pl.BlockSpec(memory_space=pltpu.SMEM)],
        out_specs=pl.BlockSpec((1, 2, ATTN_TILE, ATTN_TILE), lambda h: (h, 0, 0, 0)),
        compiler_params=_params(("parallel",)),
        name="bias",
    )(tab_flat)


def _attn_kernel(qt_ref, ka_ref, vt_ref, t_ref, o_ref, m_sc, acc_sc, s_sc, p_sc):
    t = pl.program_id(1)
    qt = qt_ref[0, 0]

    def scores(j):
        start = pl.multiple_of(j * ATTN_TILE, ATTN_TILE)
        return _dot(ka_ref[0, pl.ds(start, ATTN_TILE), :], qt)

    m_sc[...] = jnp.full(m_sc.shape, 4.0 * MASKED, F32)
    acc_sc[...] = jnp.zeros_like(acc_sc)

    def colmax(s_ref):
        return jnp.max(s_ref[...], axis=0, keepdims=True)

    def consume(s_ref, s_max, j):
        m_old = m_sc[...]
        m_new = jnp.maximum(m_old, s_max)
        m_sc[...] = m_new
        for r in range(0, ATTN_TILE, EXP_ROWS):
            p_sc[r:r + EXP_ROWS, :] = jnp.exp2(s_ref[r:r + EXP_ROWS, :] - m_new).astype(BF16)
        acc_sc[...] = jnp.exp2(m_old - m_new) * acc_sc[...] + _dot(vt_ref[0, j], p_sc[...])

    def far_step(j, cur, nxt, max_cur):
        nxt[...] = scores(j + 1)
        consume(cur, max_cur, j)
        return colmax(nxt)

    s_a, s_b = s_sc.at[0], s_sc.at[1]

    @pl.when(t >= 1)
    def _():
        n_far = t - 1
        s_a[...] = scores(0)

        def pair(i, max_cur):
            max_cur = far_step(2 * i, s_a, s_b, max_cur)
            return far_step(2 * i + 1, s_b, s_a, max_cur)

        max_cur = lax.fori_loop(0, n_far // 2, pair, colmax(s_a))
        odd = n_far % 2 == 1

        @pl.when(odd)
        def _():
            far_step(n_far - 1, s_a, s_b, max_cur)
            s_b[...] = s_b[...] + t_ref[0, 1]
            consume(s_b, colmax(s_b), t - 1)

        @pl.when(jnp.logical_not(odd))
        def _():
            s_a[...] = s_a[...] + t_ref[0, 1]
            consume(s_a, colmax(s_a), t - 1)

    s_a[...] = scores(t) + t_ref[0, 0]
    consume(s_a, colmax(s_a), t)
    acc = acc_sc[...]
    out = acc[:HEAD_DIM] / acc[HEAD_DIM:HEAD_DIM + 1]
    o_ref[...] = out.T.astype(o_ref.dtype)


def _attn(qt, ka, vt, tiles):
    nt = qt.shape[1]
    s = nt * ATTN_TILE
    return pl.pallas_call(
        _attn_kernel,
        out_shape=jax.ShapeDtypeStruct((s, ATTN_WIDTH), BF16),
        grid=(ATTN_HEADS, nt),
        in_specs=[pl.BlockSpec((1, 1, AUG_DIM, ATTN_TILE), lambda h, t: (h, t, 0, 0)),
                  pl.BlockSpec((1, s, AUG_DIM), lambda h, t: (h, 0, 0)),
                  pl.BlockSpec((1, nt, V_ROWS, ATTN_TILE), lambda h, t: (h, 0, 0, 0)),
                  pl.BlockSpec((1, 2, ATTN_TILE, ATTN_TILE), lambda h, t: (h, 0, 0, 0))],
        out_specs=pl.BlockSpec((ATTN_TILE, HEAD_DIM), lambda h, t: (t, h)),
        scratch_shapes=[pltpu.VMEM((1, ATTN_TILE), F32), pltpu.VMEM((V_ROWS, ATTN_TILE), F32),
                        pltpu.VMEM((2, ATTN_TILE, ATTN_TILE), F32),
                        pltpu.VMEM((ATTN_TILE, ATTN_TILE), BF16)],
        compiler_params=_params(("parallel", "arbitrary")),
        name="attn",
    )(qt, ka, vt, tiles)


def _expand_heads(d, lane_head):
    out = d[:, SSM_HPG - 1:SSM_HPG]
    for hg in range(SSM_HPG - 2, -1, -1):
        out = jnp.where(lane_head == hg, d[:, hg:hg + 1], out)
    return out


def _ssd_kernel(z_ref, x_ref, b_ref, c_ref, dtc_ref, dtb_ref, alog_ref, dskip_ref, nw_ref, o_ref, st_sc):
    c = pl.program_id(1)
    L, GW, NS, hp = SSM_CHUNK, SSM_GROUP_W, SSM_STATE, SSM_HPG

    @pl.when(c == 0)
    def _():
        st_sc[...] = jnp.zeros_like(st_sc)

    row = lax.broadcasted_iota(jnp.int32, (L, L), 0)
    col = lax.broadcasted_iota(jnp.int32, (L, L), 1)
    causal = row >= col
    triu = jnp.where(row <= col, 1.0, 0.0).astype(F32)
    lane_head = lax.broadcasted_iota(jnp.int32, (1, GW), 1) // SSM_HEAD_DIM

    for gi in range(SSD_GROUPS_PER_STEP):
        x = x_ref[:, gi * GW:(gi + 1) * GW].astype(F32)
        bm = b_ref[:, gi * NS:(gi + 1) * NS]
        cm = c_ref[:, gi * NS:(gi + 1) * NS]

        dt_c = _softplus(dtc_ref[gi] + dtb_ref[gi])
        a_c = -jnp.exp(alog_ref[gi])
        cum_c = _dot(dt_c * a_c, triu, precision=HIGHEST)
        last_c = cum_c[:, L - 1:L]
        small = jnp.concatenate(
            [dt_c, cum_c, jnp.exp(cum_c), jnp.exp(last_c - cum_c) * dt_c,
             jnp.zeros((SMALL_ROWS - 4 * hp, L), F32)], axis=0).T
        cum_r = small[:, hp:2 * hp]
        dt_x = _expand_heads(small[:, 0:hp], lane_head)
        grow_x = _expand_heads(small[:, 2 * hp:3 * hp], lane_head)
        end_x = _expand_heads(small[:, 3 * hp:4 * hp], lane_head)
        last_x = _expand_heads(small[L - 1:L, 2 * hp:3 * hp], lane_head)

        cb = _dot_nt(cm, bm)
        xdt = x * dt_x
        w_parts, x_parts = [], []
        for hg in range(hp):
            seg = cum_r[:, hg:hg + 1] - cum_c[hg:hg + 1, :]
            w_parts.append(cb * jnp.exp(jnp.where(causal, seg, -jnp.inf)))
            x_parts.append(jnp.where(lane_head == hg, xdt, 0.0))
        y = _dot(jnp.concatenate(w_parts, axis=1).astype(BF16),
                 jnp.concatenate(x_parts, axis=0).astype(BF16))

        st = st_sc[gi]
        y = y + _dot(cm, st.astype(BF16)) * grow_x
        st_sc[gi] = last_x * st + _dot_tn(bm, (x * end_x).astype(BF16))
        y = y + x * dskip_ref[gi]

        g = y * z_ref[:, gi * GW:(gi + 1) * GW].astype(F32)
        g = g * lax.rsqrt(jnp.mean(g * g, axis=-1, keepdims=True) + RMS_EPS)
        o_ref[:, gi * GW:(gi + 1) * GW] = (g * nw_ref[gi]).astype(o_ref.dtype)


def _ssd(rest, dt_cols, dtb, alog, dskip_x, norm_w, nc):
    s = rest.shape[0]
    n = SSD_GROUPS_PER_STEP
    L, GW, NS = SSM_CHUNK, n * SSM_GROUP_W, n * SSM_STATE
    per_step = lambda a: pl.BlockSpec((n,) + a.shape[1:], lambda g, c: (g, 0, 0))
    return pl.pallas_call(
        _ssd_kernel,
        out_shape=jax.ShapeDtypeStruct((s, SSM_INNER), BF16),
        grid=(SSM_GROUPS // n, nc),
        in_specs=[
            pl.BlockSpec((L, GW), lambda g, c: (c, REST_Z // GW + g)),
            pl.BlockSpec((L, GW), lambda g, c: (c, REST_X // GW + g)),
            pl.BlockSpec((L, NS), lambda g, c: (c, REST_B // NS + g)),
            pl.BlockSpec((L, NS), lambda g, c: (c, REST_C // NS + g)),
            pl.BlockSpec((n, SSM_HPG, L), lambda g, c: (g, 0, c)),
            per_step(dtb), per_step(alog), per_step(dskip_x), per_step(norm_w),
        ],
        out_specs=pl.BlockSpec((L, GW), lambda g, c: (c, g)),
        scratch_shapes=[pltpu.VMEM((n, SSM_STATE, SSM_GROUP_W), F32)],
        compiler_params=_params(("parallel", "arbitrary")),
        name="ssd",
    )(rest, rest, rest, rest, dt_cols, dtb, alog, dskip_x, norm_w)


def _merge_kernel(h_ref, a_ref, b_ref, ga_ref, gb_ref, pa_ref, pb_ref, wo_ref, gate_ref, gpost_ref,
                  o_ref):
    ya = _dot(a_ref[...], pa_ref[...])
    yb = _dot(b_ref[...], pb_ref[...])
    mix = ga_ref[...].astype(F32) * ya + gb_ref[...].astype(F32) * yb
    y = _dot(mix.astype(BF16), wo_ref[...])
    o_ref[...] = h_ref[...] + gate_ref[...] * _rms(y, gpost_ref[...])


def _merge(h, attn, ssd, rest, pa, pb, wo, gate, gpost, tm):
    s = h.shape[0]
    vec = pl.BlockSpec((1, D_MODEL), lambda i: (0, 0))
    full = lambda a: pl.BlockSpec(a.shape, lambda i: (0, 0))
    return pl.pallas_call(
        _merge_kernel,
        out_shape=jax.ShapeDtypeStruct((s, D_MODEL), F32),
        grid=(s // tm,),
        in_specs=[pl.BlockSpec((tm, D_MODEL), lambda i: (i, 0)),
                  pl.BlockSpec((tm, ATTN_WIDTH), lambda i: (i, 0)),
                  pl.BlockSpec((tm, SSM_INNER), lambda i: (i, 0)),
                  pl.BlockSpec((tm, D_MODEL), lambda i: (i, REST_GA // D_MODEL)),
                  pl.BlockSpec((tm, D_MODEL), lambda i: (i, REST_GB // D_MODEL)),
                  full(pa), full(pb), full(wo), vec, vec],
        out_specs=pl.BlockSpec((tm, D_MODEL), lambda i: (i, 0)),
        compiler_params=_params(("parallel",)),
        name="merge",
    )(h, attn, ssd, rest, rest, pa, pb, wo, gate, gpost)


def _layer(h, mod, rel_bias, p):
    s = h.shape[0]
    nc = s // SSM_CHUNK
    tm = min(512, s)
    sh1, sc1, g1, shm, scm, gm, sh2, sc2, g2 = [mod[k] for k in range(N_MOD)]
    vec = lambda a: a.reshape(1, -1)

    def ffn_weights(w_in, w_out):
        return (w_in[:, :FFN_HIDDEN].astype(BF16), w_in[:, FFN_HIDDEN:].astype(BF16),
                w_out.astype(BF16))

    h = _ffn(h, vec(p["ffn1_norm_pre"]), sh1, sc1, g1, vec(p["ffn1_norm_post"]),
             *ffn_weights(p["ffn1_w_in"], p["ffn1_w_out"]), tm=tm, tf=FFN_HIDDEN // 2)

    w_in = p["w_in_mix"].astype(BF16)
    w_gates = w_in[:, DT_COL + SSM_HEADS:]
    w_dt = jnp.pad(w_in[:, DT_COL:DT_COL + SSM_HEADS], ((0, 0), (0, DT_PAD - SSM_HEADS)))
    qh, kh, vh, rest, dt_raw = _inproj(h, vec(p["mix_norm_pre"]), shm, scm, w_in, w_gates, w_dt,
                                       p["conv_w"], vec(p["conv_b"]), tm=min(1024, s))

    tab_flat = rel_bias.T.reshape(-1)
    qt, ka, vt = _prep(qh, kh, vh)
    tiles = _bias_tiles(tab_flat)
    attn = _attn(qt, ka, vt, tiles)

    dt_cols = dt_raw[:, :SSM_HEADS].reshape(s, SSM_GROUPS, SSM_HPG).transpose(1, 2, 0)
    per_group = lambda a: a.reshape(SSM_GROUPS, SSM_HPG, 1)
    dskip_x = jnp.repeat(p["d_skip"], SSM_HEAD_DIM).reshape(SSM_GROUPS, 1, SSM_GROUP_W)
    ssd = _ssd(rest, dt_cols, per_group(p["dt_bias"]), per_group(p["a_log"]),
               dskip_x, p["ssm_norm_w"].reshape(SSM_GROUPS, 1, SSM_GROUP_W), nc)

    h = _merge(h, attn, ssd, rest, p["proj_a"].astype(BF16), p["proj_b"].astype(BF16),
               p["w_out_mix"].astype(BF16), gm, vec(p["mix_norm_post"]), tm=tm)

    h = _ffn(h, vec(p["ffn2_norm_pre"]), sh2, sc2, g2, vec(p["ffn2_norm_post"]),
             *ffn_weights(p["ffn2_w_in"], p["ffn2_w_out"]), tm=tm, tf=FFN_HIDDEN // 2)
    return h


_LAYER_KEYS = ("ffn1_norm_pre", "ffn1_w_in", "ffn1_w_out", "ffn1_norm_post", "mix_norm_pre",
               "w_in_mix", "conv_w", "conv_b", "dt_bias", "a_log", "d_skip", "ssm_norm_w",
               "proj_a", "proj_b", "w_out_mix", "mix_norm_post",
               "ffn2_norm_pre", "ffn2_w_in", "ffn2_w_out", "ffn2_norm_post")


def kernel(x, c, w_ada, b_ada, ffn1_norm_pre, ffn1_w_in, ffn1_w_out, ffn1_norm_post, mix_norm_pre,
           w_in_mix, rel_bias, conv_w, conv_b, dt_bias, a_log, d_skip, ssm_norm_w, proj_a, proj_b,
           w_out_mix, mix_norm_post, ffn2_norm_pre, ffn2_w_in, ffn2_w_out, ffn2_norm_post):
    stacked = dict(ffn1_norm_pre=ffn1_norm_pre, ffn1_w_in=ffn1_w_in, ffn1_w_out=ffn1_w_out,
                   ffn1_norm_post=ffn1_norm_post, mix_norm_pre=mix_norm_pre, w_in_mix=w_in_mix,
                   conv_w=conv_w, conv_b=conv_b, dt_bias=dt_bias, a_log=a_log, d_skip=d_skip,
                   ssm_norm_w=ssm_norm_w, proj_a=proj_a, proj_b=proj_b, w_out_mix=w_out_mix,
                   mix_norm_post=mix_norm_post, ffn2_norm_pre=ffn2_norm_pre, ffn2_w_in=ffn2_w_in,
                   ffn2_w_out=ffn2_w_out, ffn2_norm_post=ffn2_norm_post)
    batch, seq, _ = x.shape
    assert seq % ATTN_TILE == 0 and seq // MOBA_BLOCK <= MAX_BLOCKS and seq % SSM_CHUNK == 0
    depth = w_ada.shape[0]
    outs = []
    for b in range(batch):
        h = x[b]
        for l in range(depth):
            mod = _mod(c[b:b + 1], w_ada[l], b_ada[l])
            h = _layer(h, mod, rel_bias, {k: stacked[k][l] for k in _LAYER_KEYS})
        outs.append(h)
    return outs[0][None] if batch == 1 else jnp.stack(outs)
```

```python
import functools
import math

import jax
import jax.numpy as jnp
from jax import lax
from jax.experimental import pallas as pl
from jax.experimental.pallas import tpu as pltpu

F32 = jnp.float32
BF16 = jnp.bfloat16
HIGHEST = lax.Precision.HIGHEST

D_MODEL = 1024
N_MOD = 9
RMS_EPS = 1e-6
FFN_HIDDEN = 2816
FFN_RES = 0.5
FFN_CHUNK = FFN_HIDDEN // 2

ATTN_HEADS = 8
HEAD_DIM = 128
ATTN_WIDTH = ATTN_HEADS * HEAD_DIM
MOBA_BLOCK = 256
MOBA_TOPK = 3
MAX_BLOCKS = 128
AUG_DIM = HEAD_DIM + MAX_BLOCKS
REL_BUCKETS = 32
REL_MAX_DIST = 128
MASKED = -1e30
LOG2E = math.log2(math.e)
ATTN_TILE = 512
BLOCKS_PER_TILE = ATTN_TILE // MOBA_BLOCK
BF16_SUBLANES = 16
V_ROWS = HEAD_DIM + BF16_SUBLANES
EXP_ROWS = 64

SSM_INNER = 2048
SSM_HEAD_DIM = 64
SSM_GROUPS = 8
SSM_HEADS = SSM_INNER // SSM_HEAD_DIM
SSM_HPG = SSM_HEADS // SSM_GROUPS
SSM_GROUP_W = SSM_INNER // SSM_GROUPS
SSM_STATE = 128
SSM_CONV = 4
SSM_CHUNK = 256
CONV_HALO = 8
SMALL_ROWS = 128
SSD_GROUPS_PER_STEP = 2

COL_Q = 0
COL_K = COL_Q + ATTN_WIDTH
COL_V = COL_K + ATTN_WIDTH
COL_Z = COL_V + ATTN_WIDTH
COL_X = COL_Z + SSM_INNER
COL_B = COL_X + SSM_INNER
COL_C = COL_B + SSM_GROUPS * SSM_STATE
COL_GA = COL_C + SSM_GROUPS * SSM_STATE
COL_GB = COL_GA + D_MODEL
PROJ_W = COL_GB + D_MODEL
DT_PAD = 128

VMEM_LIMIT = 56 * 1024 * 1024


def _params(sem):
    return pltpu.CompilerParams(dimension_semantics=sem, vmem_limit_bytes=VMEM_LIMIT)


def _sigmoid(x):
    return 0.5 + 0.5 * jnp.tanh(0.5 * x)


def _silu(x):
    return x * _sigmoid(x)


def _softplus(x):
    return jnp.maximum(x, 0.0) + jnp.log(1.0 + jnp.exp(-jnp.abs(x)))


def _rms(x, g):
    return x * lax.rsqrt(jnp.mean(x * x, axis=-1, keepdims=True) + RMS_EPS) * g


def _dot(a, b, **kw):
    return jnp.dot(a, b, preferred_element_type=F32, **kw)


def _dot_nt(a, b, **kw):
    return lax.dot_general(a, b, (((1,), (1,)), ((), ())), preferred_element_type=F32, **kw)


def _dot_tn(a, b, **kw):
    return lax.dot_general(a, b, (((0,), (0,)), ((), ())), preferred_element_type=F32, **kw)


def _mod_kernel(c_ref, w_ref, b_ref, o_ref):
    cs = _silu(c_ref[...])
    o_ref[...] = _dot(cs, w_ref[...], precision=HIGHEST) + b_ref[...]


def _mod(c, w_ada, b_ada):
    n = w_ada.shape[1]
    tn = 1024
    c8 = jnp.broadcast_to(c, (8, D_MODEL))
    out = pl.pallas_call(
        _mod_kernel,
        out_shape=jax.ShapeDtypeStruct((8, n), F32),
        grid=(n // tn,),
        in_specs=[pl.BlockSpec((8, D_MODEL), lambda j: (0, 0)),
                  pl.BlockSpec((D_MODEL, tn), lambda j: (0, j)),
                  pl.BlockSpec((1, tn), lambda j: (0, j))],
        out_specs=pl.BlockSpec((8, tn), lambda j: (0, j)),
        compiler_params=_params(("arbitrary",)),
        name="mod",
    )(c8, w_ada, b_ada.reshape(1, n))
    return out[0].reshape(N_MOD, 1, D_MODEL)


def _ffn_kernel(h_ref, gpre_ref, sh_ref, sc_ref, gate_ref, gpost_ref, wa_ref, wb_ref, wo_ref, o_ref):
    h = h_ref[...]
    u = (_rms(h, gpre_ref[...]) * (1.0 + sc_ref[...]) + sh_ref[...]).astype(BF16)
    acc = None
    for lo in range(0, FFN_HIDDEN, FFN_CHUNK):
        a = _dot(u, wa_ref[:, lo:lo + FFN_CHUNK])
        b = _dot(u, wb_ref[:, lo:lo + FFN_CHUNK])
        part = _dot((_silu(a) * b).astype(BF16), wo_ref[lo:lo + FFN_CHUNK, :])
        acc = part if acc is None else acc + part
    o_ref[...] = h + (FFN_RES * gate_ref[...]) * _rms(acc, gpost_ref[...])


def _ffn(h, gpre, sh, sc, gate, gpost, wa, wb, wo, tm):
    s = h.shape[0]
    row = lambda i: (i, 0)
    vec = pl.BlockSpec((1, D_MODEL), lambda i: (0, 0))
    resident = lambda a: pl.BlockSpec(a.shape, lambda i: (0, 0), pipeline_mode=pl.Buffered(1))
    return pl.pallas_call(
        _ffn_kernel,
        out_shape=jax.ShapeDtypeStruct((s, D_MODEL), F32),
        grid=(s // tm,),
        in_specs=[pl.BlockSpec((tm, D_MODEL), row), vec, vec, vec, vec, vec,
                  resident(wa), resident(wb), resident(wo)],
        out_specs=pl.BlockSpec((tm, D_MODEL), row),
        compiler_params=_params(("parallel",)),
        name="ffn",
    )(h, gpre, sh, sc, gate, gpost, wa, wb, wo)


def _inproj_kernel(h_ref, gpre_ref, sh_ref, sc_ref, w_ref, wdt_ref, o_ref, dt_ref, u_sc):
    j = pl.program_id(1)

    @pl.when(j == 0)
    def _():
        u = _rms(h_ref[...], gpre_ref[...]) * (1.0 + sc_ref[...]) + sh_ref[...]
        u_sc[...] = u.astype(BF16)
        dt_ref[...] = _dot(u_sc[...], wdt_ref[...])

    o_ref[...] = _dot(u_sc[...], w_ref[...]).astype(o_ref.dtype)


def _inproj(h, gpre, sh, sc, w, wdt, tm, tn):
    s = h.shape[0]
    vec = pl.BlockSpec((1, D_MODEL), lambda i, j: (0, 0))
    return pl.pallas_call(
        _inproj_kernel,
        out_shape=(jax.ShapeDtypeStruct((s, PROJ_W), BF16),
                   jax.ShapeDtypeStruct((s, DT_PAD), F32)),
        grid=(s // tm, PROJ_W // tn),
        in_specs=[pl.BlockSpec((tm, D_MODEL), lambda i, j: (i, 0)), vec, vec, vec,
                  pl.BlockSpec((D_MODEL, tn), lambda i, j: (0, j)),
                  pl.BlockSpec((D_MODEL, DT_PAD), lambda i, j: (0, 0))],
        out_specs=(pl.BlockSpec((tm, tn), lambda i, j: (i, j)),
                   pl.BlockSpec((tm, DT_PAD), lambda i, j: (i, 0))),
        scratch_shapes=[pltpu.VMEM((tm, D_MODEL), BF16)],
        compiler_params=_params(("parallel", "arbitrary")),
        name="inproj",
    )(h, gpre, sh, sc, w, wdt)


def _prep_kernel(n_sel, q_ref, k_ref, v_ref, qt_ref, ka_ref, vt_ref, km_sc):
    t = pl.program_id(1)

    @pl.when(t == 0)
    def _():
        km_sc[...] = jnp.zeros_like(km_sc)

    k = k_ref[...].astype(F32)
    for b in range(BLOCKS_PER_TILE):
        km_sc[pl.ds(t * BLOCKS_PER_TILE + b, 1), :] = jnp.mean(
            k[b * MOBA_BLOCK:(b + 1) * MOBA_BLOCK], axis=0, keepdims=True)

    qt = (q_ref[...].astype(F32) * (HEAD_DIM ** -0.5 * LOG2E)).T
    score = _dot(km_sc[:n_sel, :], qt, precision=HIGHEST)
    blk = lax.broadcasted_iota(jnp.int32, score.shape, 0)
    q_blk = t * BLOCKS_PER_TILE + lax.broadcasted_iota(jnp.int32, score.shape, 1) // MOBA_BLOCK
    s = jnp.where(blk < q_blk, score, -jnp.inf)
    pen = jnp.full(score.shape, MASKED, F32)
    for _ in range(MOBA_TOPK):
        m = jnp.max(s, axis=0, keepdims=True)
        first = jnp.min(jnp.where(s == m, blk, n_sel), axis=0, keepdims=True)
        first = jnp.where(m > -jnp.inf, first, n_sel)
        pick = blk == first
        pen = jnp.where(pick, 0.0, pen)
        s = jnp.where(pick, -jnp.inf, s)
    pen = jnp.where(blk == q_blk, 0.0, pen)
    qt_ref[0, 0, :HEAD_DIM, :] = qt.astype(BF16)
    qt_ref[0, 0, HEAD_DIM:HEAD_DIM + n_sel, :] = pen.astype(BF16)
    if n_sel < MAX_BLOCKS:
        qt_ref[0, 0, HEAD_DIM + n_sel:, :] = jnp.zeros((MAX_BLOCKS - n_sel, ATTN_TILE), BF16)

    lane = lax.broadcasted_iota(jnp.int32, (ATTN_TILE, MAX_BLOCKS), 1)
    k_blk = t * BLOCKS_PER_TILE + lax.broadcasted_iota(jnp.int32, lane.shape, 0) // MOBA_BLOCK
    ka_ref[0, :, :HEAD_DIM] = k.astype(BF16)
    ka_ref[0, :, HEAD_DIM:] = jnp.where(lane == k_blk, 1.0, 0.0).astype(BF16)

    ones_row = lax.broadcasted_iota(jnp.int32, (V_ROWS - HEAD_DIM, ATTN_TILE), 0) == 0
    vt_ref[0, 0, :HEAD_DIM, :] = v_ref[...].astype(F32).T.astype(BF16)
    vt_ref[0, 0, HEAD_DIM:, :] = jnp.where(ones_row, 1.0, 0.0).astype(BF16)


def _prep(proj):
    s = proj.shape[0]
    nt = s // ATTN_TILE
    blk = lambda col: pl.BlockSpec((ATTN_TILE, HEAD_DIM), lambda h, t: (t, col // HEAD_DIM + h))
    n_sel = -(-(s // MOBA_BLOCK) // BF16_SUBLANES) * BF16_SUBLANES
    return pl.pallas_call(
        functools.partial(_prep_kernel, n_sel),
        out_shape=(jax.ShapeDtypeStruct((ATTN_HEADS, nt, AUG_DIM, ATTN_TILE), BF16),
                   jax.ShapeDtypeStruct((ATTN_HEADS, s, AUG_DIM), BF16),
                   jax.ShapeDtypeStruct((ATTN_HEADS, nt, V_ROWS, ATTN_TILE), BF16)),
        grid=(ATTN_HEADS, nt),
        in_specs=[blk(COL_Q), blk(COL_K), blk(COL_V)],
        out_specs=(pl.BlockSpec((1, 1, AUG_DIM, ATTN_TILE), lambda h, t: (h, t, 0, 0)),
                   pl.BlockSpec((1, ATTN_TILE, AUG_DIM), lambda h, t: (h, t, 0)),
                   pl.BlockSpec((1, 1, V_ROWS, ATTN_TILE), lambda h, t: (h, t, 0, 0))),
        scratch_shapes=[pltpu.VMEM((MAX_BLOCKS, HEAD_DIM), F32)],
        compiler_params=_params(("parallel", "arbitrary")),
        name="prep",
    )(proj, proj, proj)


def _t5_bucket(rel):
    n = jnp.maximum(rel, 0)
    max_exact = REL_BUCKETS // 2
    nf = jnp.maximum(n, 1).astype(F32)
    large = max_exact + (jnp.log(nf / max_exact) / math.log(REL_MAX_DIST / max_exact)
                         * (REL_BUCKETS - max_exact)).astype(jnp.int32)
    large = jnp.minimum(large, REL_BUCKETS - 1)
    return jnp.where(n < max_exact, n, large)


def _bias_kernel(tab_ref, o_ref):
    h = pl.program_id(0)
    shape = (ATTN_TILE, ATTN_TILE)
    ki = lax.broadcasted_iota(jnp.int32, shape, 0)
    qi = lax.broadcasted_iota(jnp.int32, shape, 1)
    far = tab_ref[h * REL_BUCKETS + REL_BUCKETS - 1]
    for which in range(2):
        rel = qi - ki + which * ATTN_TILE
        bucket = _t5_bucket(rel)
        val = jnp.zeros(shape, F32)
        for b in range(REL_BUCKETS):
            val = jnp.where(bucket == b, tab_ref[h * REL_BUCKETS + b], val)
        val = (val - far) * LOG2E
        if which == 0:
            val = jnp.where(rel >= 0, val, MASKED)
        o_ref[0, which] = val


def _bias_tiles(tab_flat):
    return pl.pallas_call(
        _bias_kernel,
        out_shape=jax.ShapeDtypeStruct((ATTN_HEADS, 2, ATTN_TILE, ATTN_TILE), F32),
        grid=(ATTN_HEADS,),
        in_specs=[pl.BlockSpec(memory_space=pltpu.SMEM)],
        out_specs=pl.BlockSpec((1, 2, ATTN_TILE, ATTN_TILE), lambda h: (h, 0, 0, 0)),
        compiler_params=_params(("parallel",)),
        name="bias",
    )(tab_flat)


def _attn_kernel(qt_ref, ka_ref, vt_ref, t_ref, o_ref, m_sc, acc_sc, s_sc, p_sc):
    t = pl.program_id(1)
    qt = qt_ref[0, 0]

    def scores(j):
        start = pl.multiple_of(j * ATTN_TILE, ATTN_TILE)
        return _dot(ka_ref[0, pl.ds(start, ATTN_TILE), :], qt)

    m_sc[...] = jnp.full(m_sc.shape, 4.0 * MASKED, F32)
    acc_sc[...] = jnp.zeros_like(acc_sc)

    def colmax(s_ref):
        return jnp.max(s_ref[...], axis=0, keepdims=True)

    def consume(s_ref, s_max, j):
        m_old = m_sc[...]
        m_new = jnp.maximum(m_old, s_max)
        m_sc[...] = m_new
        for r in range(0, ATTN_TILE, EXP_ROWS):
            p_sc[r:r + EXP_ROWS, :] = jnp.exp2(s_ref[r:r + EXP_ROWS, :] - m_new).astype(BF16)
        acc_sc[...] = jnp.exp2(m_old - m_new) * acc_sc[...] + _dot(vt_ref[0, j], p_sc[...])

    def far_step(j, cur, nxt, max_cur):
        nxt[...] = scores(j + 1)
        consume(cur, max_cur, j)
        return colmax(nxt)

    s_a, s_b = s_sc.at[0], s_sc.at[1]

    @pl.when(t >= 1)
    def _():
        n_far = t - 1
        s_a[...] = scores(0)

        def pair(i, max_cur):
            max_cur = far_step(2 * i, s_a, s_b, max_cur)
            return far_step(2 * i + 1, s_b, s_a, max_cur)

        max_cur = lax.fori_loop(0, n_far // 2, pair, colmax(s_a))
        odd = n_far % 2 == 1

        @pl.when(odd)
        def _():
            far_step(n_far - 1, s_a, s_b, max_cur)
            s_b[...] = s_b[...] + t_ref[0, 1]
            consume(s_b, colmax(s_b), t - 1)

        @pl.when(jnp.logical_not(odd))
        def _():
            s_a[...] = s_a[...] + t_ref[0, 1]
            consume(s_a, colmax(s_a), t - 1)

    s_a[...] = scores(t) + t_ref[0, 0]
    consume(s_a, colmax(s_a), t)
    acc = acc_sc[...]
    out = acc[:HEAD_DIM] / acc[HEAD_DIM:HEAD_DIM + 1]
    o_ref[...] = out.T.astype(o_ref.dtype)


def _attn(qt, ka, vt, tiles):
    nt = qt.shape[1]
    s = nt * ATTN_TILE
    return pl.pallas_call(
        _attn_kernel,
        out_shape=jax.ShapeDtypeStruct((s, ATTN_WIDTH), BF16),
        grid=(ATTN_HEADS, nt),
        in_specs=[pl.BlockSpec((1, 1, AUG_DIM, ATTN_TILE), lambda h, t: (h, t, 0, 0)),
                  pl.BlockSpec((1, s, AUG_DIM), lambda h, t: (h, 0, 0)),
                  pl.BlockSpec((1, nt, V_ROWS, ATTN_TILE), lambda h, t: (h, 0, 0, 0)),
                  pl.BlockSpec((1, 2, ATTN_TILE, ATTN_TILE), lambda h, t: (h, 0, 0, 0))],
        out_specs=pl.BlockSpec((ATTN_TILE, HEAD_DIM), lambda h, t: (t, h)),
        scratch_shapes=[pltpu.VMEM((1, ATTN_TILE), F32), pltpu.VMEM((V_ROWS, ATTN_TILE), F32),
                        pltpu.VMEM((2, ATTN_TILE, ATTN_TILE), F32),
                        pltpu.VMEM((ATTN_TILE, ATTN_TILE), BF16)],
        compiler_params=_params(("parallel", "arbitrary")),
        name="attn",
    )(qt, ka, vt, tiles)


def _expand_heads(d, lane_head):
    out = d[:, SSM_HPG - 1:SSM_HPG]
    for hg in range(SSM_HPG - 2, -1, -1):
        out = jnp.where(lane_head == hg, d[:, hg:hg + 1], out)
    return out


def _ssd_kernel(z_ref, x_ref, b_ref, c_ref, dtc_ref, wx_ref, wb_ref, wc_ref, bx_ref, bb_ref, bc_ref,
                dtb_ref, alog_ref, dskip_ref, nw_ref, o_ref, xpx_sc, xpb_sc, xpc_sc, st_sc):
    c = pl.program_id(1)
    L, GW, NS, hp = SSM_CHUNK, SSM_GROUP_W, SSM_STATE, SSM_HPG

    @pl.when(c == 0)
    def _():
        xpx_sc[0:CONV_HALO, :] = jnp.zeros((CONV_HALO, xpx_sc.shape[1]), F32)
        xpb_sc[0:CONV_HALO, :] = jnp.zeros((CONV_HALO, xpb_sc.shape[1]), F32)
        xpc_sc[0:CONV_HALO, :] = jnp.zeros((CONV_HALO, xpc_sc.shape[1]), F32)
        st_sc[...] = jnp.zeros_like(st_sc)

    def conv(src_ref, pad_sc, w_ref, bias_ref):
        pad_sc[CONV_HALO:CONV_HALO + L, :] = src_ref[...].astype(F32)
        acc = bias_ref[...]
        for j in range(SSM_CONV):
            lo = CONV_HALO - (SSM_CONV - 1) + j
            acc = acc + pad_sc[lo:lo + L, :] * w_ref[j:j + 1, :]
        pad_sc[0:CONV_HALO, :] = pad_sc[L:L + CONV_HALO, :]
        return _silu(acc)

    x_all = conv(x_ref, xpx_sc, wx_ref, bx_ref)
    b_all = conv(b_ref, xpb_sc, wb_ref, bb_ref)
    c_all = conv(c_ref, xpc_sc, wc_ref, bc_ref)

    row = lax.broadcasted_iota(jnp.int32, (L, L), 0)
    col = lax.broadcasted_iota(jnp.int32, (L, L), 1)
    causal = row >= col
    triu = jnp.where(row <= col, 1.0, 0.0).astype(F32)
    lane_head = lax.broadcasted_iota(jnp.int32, (1, GW), 1) // SSM_HEAD_DIM

    for gi in range(SSD_GROUPS_PER_STEP):
        x = x_all[:, gi * GW:(gi + 1) * GW]
        bm = b_all[:, gi * NS:(gi + 1) * NS].astype(BF16)
        cm = c_all[:, gi * NS:(gi + 1) * NS].astype(BF16)

        dt_c = _softplus(dtc_ref[gi] + dtb_ref[gi])
        a_c = -jnp.exp(alog_ref[gi])
        cum_c = _dot(dt_c * a_c, triu, precision=HIGHEST)
        last_c = cum_c[:, L - 1:L]
        small = jnp.concatenate(
            [dt_c, cum_c, jnp.exp(cum_c), jnp.exp(last_c - cum_c) * dt_c,
             jnp.zeros((SMALL_ROWS - 4 * hp, L), F32)], axis=0).T
        cum_r = small[:, hp:2 * hp]
        dt_x = _expand_heads(small[:, 0:hp], lane_head)
        grow_x = _expand_heads(small[:, 2 * hp:3 * hp], lane_head)
        end_x = _expand_heads(small[:, 3 * hp:4 * hp], lane_head)
        last_x = _expand_heads(small[L - 1:L, 2 * hp:3 * hp], lane_head)

        cb = _dot_nt(cm, bm)
        xdt = x * dt_x
        w_parts, x_parts = [], []
        for hg in range(hp):
            seg = cum_r[:, hg:hg + 1] - cum_c[hg:hg + 1, :]
            w_parts.append(cb * jnp.exp(jnp.where(causal, seg, -jnp.inf)))
            x_parts.append(jnp.where(lane_head == hg, xdt, 0.0))
        y = _dot(jnp.concatenate(w_parts, axis=1).astype(BF16),
                 jnp.concatenate(x_parts, axis=0).astype(BF16))

        st = st_sc[gi]
        y = y + _dot(cm, st.astype(BF16)) * grow_x
        st_sc[gi] = last_x * st + _dot_tn(bm, (x * end_x).astype(BF16))
        y = y + x * dskip_ref[gi]

        g = y * _silu(z_ref[:, gi * GW:(gi + 1) * GW].astype(F32))
        g = g * lax.rsqrt(jnp.mean(g * g, axis=-1, keepdims=True) + RMS_EPS)
        o_ref[:, gi * GW:(gi + 1) * GW] = (g * nw_ref[gi]).astype(o_ref.dtype)


def _ssd(proj, dt_cols, conv_w, conv_b, dtb, alog, dskip_x, norm_w, nc):
    s = proj.shape[0]
    n = SSD_GROUPS_PER_STEP
    L, GW, NS = SSM_CHUNK, n * SSM_GROUP_W, n * SSM_STATE
    xoff, boff, coff = 0, SSM_INNER, SSM_INNER + SSM_GROUPS * SSM_STATE
    per_step = lambda a: pl.BlockSpec((n,) + a.shape[1:], lambda g, c: (g, 0, 0))
    return pl.pallas_call(
        _ssd_kernel,
        out_shape=jax.ShapeDtypeStruct((s, SSM_INNER), BF16),
        grid=(SSM_GROUPS // n, nc),
        in_specs=[
            pl.BlockSpec((L, GW), lambda g, c: (c, COL_Z // GW + g)),
            pl.BlockSpec((L, GW), lambda g, c: (c, COL_X // GW + g)),
            pl.BlockSpec((L, NS), lambda g, c: (c, COL_B // NS + g)),
            pl.BlockSpec((L, NS), lambda g, c: (c, COL_C // NS + g)),
            pl.BlockSpec((n, SSM_HPG, L), lambda g, c: (g, 0, c)),
            pl.BlockSpec((SSM_CONV, GW), lambda g, c: (0, xoff // GW + g)),
            pl.BlockSpec((SSM_CONV, NS), lambda g, c: (0, boff // NS + g)),
            pl.BlockSpec((SSM_CONV, NS), lambda g, c: (0, coff // NS + g)),
            pl.BlockSpec((1, GW), lambda g, c: (0, xoff // GW + g)),
            pl.BlockSpec((1, NS), lambda g, c: (0, boff // NS + g)),
            pl.BlockSpec((1, NS), lambda g, c: (0, coff // NS + g)),
            per_step(dtb), per_step(alog), per_step(dskip_x), per_step(norm_w),
        ],
        out_specs=pl.BlockSpec((L, GW), lambda g, c: (c, g)),
        scratch_shapes=[pltpu.VMEM((L + CONV_HALO, GW), F32), pltpu.VMEM((L + CONV_HALO, NS), F32),
                        pltpu.VMEM((L + CONV_HALO, NS), F32),
                        pltpu.VMEM((n, SSM_STATE, SSM_GROUP_W), F32)],
        compiler_params=_params(("parallel", "arbitrary")),
        name="ssd",
    )(proj, proj, proj, proj, dt_cols, conv_w, conv_w, conv_w, conv_b, conv_b, conv_b,
      dtb, alog, dskip_x, norm_w)


def _merge_kernel(h_ref, a_ref, b_ref, ga_ref, gb_ref, pa_ref, pb_ref, wo_ref, gate_ref, gpost_ref,
                  o_ref):
    ya = _dot(a_ref[...], pa_ref[...])
    yb = _dot(b_ref[...], pb_ref[...])
    mix = _sigmoid(ga_ref[...].astype(F32)) * ya + _sigmoid(gb_ref[...].astype(F32)) * yb
    y = _dot(mix.astype(BF16), wo_ref[...])
    o_ref[...] = h_ref[...] + gate_ref[...] * _rms(y, gpost_ref[...])


def _merge(h, attn, ssd, proj, pa, pb, wo, gate, gpost, tm):
    s = h.shape[0]
    vec = pl.BlockSpec((1, D_MODEL), lambda i: (0, 0))
    full = lambda a: pl.BlockSpec(a.shape, lambda i: (0, 0))
    return pl.pallas_call(
        _merge_kernel,
        out_shape=jax.ShapeDtypeStruct((s, D_MODEL), F32),
        grid=(s // tm,),
        in_specs=[pl.BlockSpec((tm, D_MODEL), lambda i: (i, 0)),
                  pl.BlockSpec((tm, ATTN_WIDTH), lambda i: (i, 0)),
                  pl.BlockSpec((tm, SSM_INNER), lambda i: (i, 0)),
                  pl.BlockSpec((tm, D_MODEL), lambda i: (i, COL_GA // D_MODEL)),
                  pl.BlockSpec((tm, D_MODEL), lambda i: (i, COL_GB // D_MODEL)),
                  full(pa), full(pb), full(wo), vec, vec],
        out_specs=pl.BlockSpec((tm, D_MODEL), lambda i: (i, 0)),
        compiler_params=_params(("parallel",)),
        name="merge",
    )(h, attn, ssd, proj, proj, pa, pb, wo, gate, gpost)


def _layer(h, mod, rel_bias, p):
    s = h.shape[0]
    nc = s // SSM_CHUNK
    tm = min(512, s)
    tm_wide = min(1024, s)
    sh1, sc1, g1, shm, scm, gm, sh2, sc2, g2 = [mod[k] for k in range(N_MOD)]
    vec = lambda a: a.reshape(1, -1)

    def ffn_weights(w_in, w_out):
        return (w_in[:, :FFN_HIDDEN].astype(BF16), w_in[:, FFN_HIDDEN:].astype(BF16),
                w_out.astype(BF16))

    h = _ffn(h, vec(p["ffn1_norm_pre"]), sh1, sc1, g1, vec(p["ffn1_norm_post"]),
             *ffn_weights(p["ffn1_w_in"], p["ffn1_w_out"]), tm=tm_wide)

    w_in = p["w_in_mix"]
    dt_lo = COL_GA
    w_main = jnp.concatenate([w_in[:, :dt_lo], w_in[:, dt_lo + SSM_HEADS:]], axis=1).astype(BF16)
    w_dt = jnp.pad(w_in[:, dt_lo:dt_lo + SSM_HEADS], ((0, 0), (0, DT_PAD - SSM_HEADS))).astype(BF16)
    proj, dt_raw = _inproj(h, vec(p["mix_norm_pre"]), shm, scm, w_main, w_dt, tm=tm_wide, tn=1024)

    tab_flat = rel_bias.T.reshape(-1)
    qt, ka, vt = _prep(proj)
    tiles = _bias_tiles(tab_flat)
    attn = _attn(qt, ka, vt, tiles)

    dt_cols = dt_raw[:, :SSM_HEADS].reshape(s, SSM_GROUPS, SSM_HPG).transpose(1, 2, 0)
    per_group = lambda a: a.reshape(SSM_GROUPS, SSM_HPG, 1)
    dskip_x = jnp.repeat(p["d_skip"], SSM_HEAD_DIM).reshape(SSM_GROUPS, 1, SSM_GROUP_W)
    ssd = _ssd(proj, dt_cols, p["conv_w"], vec(p["conv_b"]),
               per_group(p["dt_bias"]), per_group(p["a_log"]),
               dskip_x, p["ssm_norm_w"].reshape(SSM_GROUPS, 1, SSM_GROUP_W), nc)

    h = _merge(h, attn, ssd, proj, p["proj_a"].astype(BF16), p["proj_b"].astype(BF16),
               p["w_out_mix"].astype(BF16), gm, vec(p["mix_norm_post"]), tm=tm)

    h = _ffn(h, vec(p["ffn2_norm_pre"]), sh2, sc2, g2, vec(p["ffn2_norm_post"]),
             *ffn_weights(p["ffn2_w_in"], p["ffn2_w_out"]), tm=tm_wide)
    return h


_LAYER_KEYS = ("ffn1_norm_pre", "ffn1_w_in", "ffn1_w_out", "ffn1_norm_post", "mix_norm_pre",
               "w_in_mix", "conv_w", "conv_b", "dt_bias", "a_log", "d_skip", "ssm_norm_w",
               "proj_a", "proj_b", "w_out_mix", "mix_norm_post",
               "ffn2_norm_pre", "ffn2_w_in", "ffn2_w_out", "ffn2_norm_post")


def kernel(x, c, w_ada, b_ada, ffn1_norm_pre, ffn1_w_in, ffn1_w_out, ffn1_norm_post, mix_norm_pre,
           w_in_mix, rel_bias, conv_w, conv_b, dt_bias, a_log, d_skip, ssm_norm_w, proj_a, proj_b,
           w_out_mix, mix_norm_post, ffn2_norm_pre, ffn2_w_in, ffn2_w_out, ffn2_norm_post):
    stacked = dict(ffn1_norm_pre=ffn1_norm_pre, ffn1_w_in=ffn1_w_in, ffn1_w_out=ffn1_w_out,
                   ffn1_norm_post=ffn1_norm_post, mix_norm_pre=mix_norm_pre, w_in_mix=w_in_mix,
                   conv_w=conv_w, conv_b=conv_b, dt_bias=dt_bias, a_log=a_log, d_skip=d_skip,
                   ssm_norm_w=ssm_norm_w, proj_a=proj_a, proj_b=proj_b, w_out_mix=w_out_mix,
                   mix_norm_post=mix_norm_post, ffn2_norm_pre=ffn2_norm_pre, ffn2_w_in=ffn2_w_in,
                   ffn2_w_out=ffn2_w_out, ffn2_norm_post=ffn2_norm_post)
    batch, seq, _ = x.shape
    assert seq % ATTN_TILE == 0 and seq // MOBA_BLOCK <= MAX_BLOCKS and seq % SSM_CHUNK == 0
    depth = w_ada.shape[0]
    outs = []
    for b in range(batch):
        h = x[b]
        for l in range(depth):
            mod = _mod(c[b:b + 1], w_ada[l], b_ada[l])
            h = _layer(h, mod, rel_bias, {k: stacked[k][l] for k in _LAYER_KEYS})
        outs.append(h)
    return outs[0][None] if batch == 1 else jnp.stack(outs)
```

```python
import functools
import math

import jax
import jax.numpy as jnp
from jax import lax
from jax.experimental import pallas as pl
from jax.experimental.pallas import tpu as pltpu

F32 = jnp.float32
BF16 = jnp.bfloat16
HIGHEST = lax.Precision.HIGHEST

D_MODEL = 1024
N_MOD = 9
RMS_EPS = 1e-6
FFN_HIDDEN = 2816
FFN_RES = 0.5
FFN_CHUNK = FFN_HIDDEN // 2

ATTN_HEADS = 8
HEAD_DIM = 128
ATTN_WIDTH = ATTN_HEADS * HEAD_DIM
MOBA_BLOCK = 256
MOBA_TOPK = 3
MAX_BLOCKS = 128
AUG_DIM = HEAD_DIM + MAX_BLOCKS
REL_BUCKETS = 32
REL_MAX_DIST = 128
MASKED = -1e30
LOG2E = math.log2(math.e)
ATTN_TILE = 512
BLOCKS_PER_TILE = ATTN_TILE // MOBA_BLOCK
BF16_SUBLANES = 16
V_ROWS = HEAD_DIM + BF16_SUBLANES
EXP_ROWS = 64

SSM_INNER = 2048
SSM_HEAD_DIM = 64
SSM_GROUPS = 8
SSM_HEADS = SSM_INNER // SSM_HEAD_DIM
SSM_HPG = SSM_HEADS // SSM_GROUPS
SSM_GROUP_W = SSM_INNER // SSM_GROUPS
SSM_STATE = 128
SSM_CONV = 4
SSM_CHUNK = 256
CONV_HALO = 8
SMALL_ROWS = 128
SSD_GROUPS_PER_STEP = 2

COL_Q = 0
COL_K = COL_Q + ATTN_WIDTH
COL_V = COL_K + ATTN_WIDTH
COL_Z = COL_V + ATTN_WIDTH
COL_X = COL_Z + SSM_INNER
COL_B = COL_X + SSM_INNER
COL_C = COL_B + SSM_GROUPS * SSM_STATE
COL_GA = COL_C + SSM_GROUPS * SSM_STATE
COL_GB = COL_GA + D_MODEL
PROJ_W = COL_GB + D_MODEL
PROJ_TILE = 1024
MAIN_TILES = COL_GA // PROJ_TILE
DT_PAD = 128

VMEM_LIMIT = 56 * 1024 * 1024


def _params(sem):
    return pltpu.CompilerParams(dimension_semantics=sem, vmem_limit_bytes=VMEM_LIMIT)


def _sigmoid(x):
    return 0.5 + 0.5 * jnp.tanh(0.5 * x)


def _silu(x):
    return x * _sigmoid(x)


def _softplus(x):
    return jnp.maximum(x, 0.0) + jnp.log(1.0 + jnp.exp(-jnp.abs(x)))


def _rms(x, g):
    return x * lax.rsqrt(jnp.mean(x * x, axis=-1, keepdims=True) + RMS_EPS) * g


def _dot(a, b, **kw):
    return jnp.dot(a, b, preferred_element_type=F32, **kw)


def _dot_nt(a, b, **kw):
    return lax.dot_general(a, b, (((1,), (1,)), ((), ())), preferred_element_type=F32, **kw)


def _dot_tn(a, b, **kw):
    return lax.dot_general(a, b, (((0,), (0,)), ((), ())), preferred_element_type=F32, **kw)


def _mod_kernel(c_ref, w_ref, b_ref, o_ref):
    cs = _silu(c_ref[...])
    o_ref[...] = _dot(cs, w_ref[...], precision=HIGHEST) + b_ref[...]


def _mod(c, w_ada, b_ada):
    n = w_ada.shape[1]
    tn = 1024
    c8 = jnp.broadcast_to(c, (8, D_MODEL))
    out = pl.pallas_call(
        _mod_kernel,
        out_shape=jax.ShapeDtypeStruct((8, n), F32),
        grid=(n // tn,),
        in_specs=[pl.BlockSpec((8, D_MODEL), lambda j: (0, 0)),
                  pl.BlockSpec((D_MODEL, tn), lambda j: (0, j)),
                  pl.BlockSpec((1, tn), lambda j: (0, j))],
        out_specs=pl.BlockSpec((8, tn), lambda j: (0, j)),
        compiler_params=_params(("arbitrary",)),
        name="mod",
    )(c8, w_ada, b_ada.reshape(1, n))
    return out[0].reshape(N_MOD, 1, D_MODEL)


def _ffn_kernel(h_ref, gpre_ref, sh_ref, sc_ref, gate_ref, gpost_ref, wi_ref, wo_ref, o_ref):
    h = h_ref[...]
    u = (_rms(h, gpre_ref[...]) * (1.0 + sc_ref[...]) + sh_ref[...]).astype(BF16)
    acc = None
    for lo in range(0, FFN_HIDDEN, FFN_CHUNK):
        a = _dot(u, wi_ref[:, lo:lo + FFN_CHUNK])
        b = _dot(u, wi_ref[:, FFN_HIDDEN + lo:FFN_HIDDEN + lo + FFN_CHUNK])
        part = _dot((_silu(a) * b).astype(BF16), wo_ref[lo:lo + FFN_CHUNK, :])
        acc = part if acc is None else acc + part
    o_ref[...] = h + (FFN_RES * gate_ref[...]) * _rms(acc, gpost_ref[...])


def _ffn(h, gpre, sh, sc, gate, gpost, wi, wo, tm):
    s = h.shape[0]
    row = lambda i: (i, 0)
    vec = pl.BlockSpec((1, D_MODEL), lambda i: (0, 0))
    resident = lambda a: pl.BlockSpec(a.shape, lambda i: (0, 0), pipeline_mode=pl.Buffered(1))
    return pl.pallas_call(
        _ffn_kernel,
        out_shape=jax.ShapeDtypeStruct((s, D_MODEL), F32),
        grid=(s // tm,),
        in_specs=[pl.BlockSpec((tm, D_MODEL), row), vec, vec, vec, vec, vec,
                  resident(wi), resident(wo)],
        out_specs=pl.BlockSpec((tm, D_MODEL), row),
        compiler_params=_params(("parallel",)),
        name="ffn",
    )(h, gpre, sh, sc, gate, gpost, wi, wo)


def _inproj_kernel(h_ref, gpre_ref, sh_ref, sc_ref, w_ref, wg_ref, wdt_ref, o_ref, dt_ref, u_sc):
    j = pl.program_id(1)

    @pl.when(j == 0)
    def _():
        u = (_rms(h_ref[...], gpre_ref[...]) * (1.0 + sc_ref[...]) + sh_ref[...]).astype(BF16)
        u_sc[...] = u
        dt_ref[...] = _dot(u, wdt_ref[...])
        o_ref[...] = _dot(u, w_ref[...]).astype(o_ref.dtype)

    @pl.when(jnp.logical_and(j > 0, j < MAIN_TILES))
    def _():
        o_ref[...] = _dot(u_sc[...], w_ref[...]).astype(o_ref.dtype)

    @pl.when(j >= MAIN_TILES)
    def _():
        o_ref[...] = _dot(u_sc[...], wg_ref[...]).astype(o_ref.dtype)


def _inproj(h, gpre, sh, sc, w, wg, wdt, tm):
    s = h.shape[0]
    tn = PROJ_TILE
    vec = pl.BlockSpec((1, D_MODEL), lambda i, j: (0, 0))
    return pl.pallas_call(
        _inproj_kernel,
        out_shape=(jax.ShapeDtypeStruct((s, PROJ_W), BF16),
                   jax.ShapeDtypeStruct((s, DT_PAD), F32)),
        grid=(s // tm, PROJ_W // tn),
        in_specs=[pl.BlockSpec((tm, D_MODEL), lambda i, j: (i, 0)), vec, vec, vec,
                  pl.BlockSpec((D_MODEL, tn), lambda i, j: (0, jnp.minimum(j, MAIN_TILES - 1))),
                  pl.BlockSpec((D_MODEL, tn), lambda i, j: (0, jnp.maximum(j - MAIN_TILES, 0))),
                  pl.BlockSpec((D_MODEL, DT_PAD), lambda i, j: (0, 0))],
        out_specs=(pl.BlockSpec((tm, tn), lambda i, j: (i, j)),
                   pl.BlockSpec((tm, DT_PAD), lambda i, j: (i, 0))),
        scratch_shapes=[pltpu.VMEM((tm, D_MODEL), BF16)],
        compiler_params=_params(("parallel", "arbitrary")),
        name="inproj",
    )(h, gpre, sh, sc, w, wg, wdt)


def _prep_kernel(n_sel, q_ref, k_ref, v_ref, qt_ref, ka_ref, vt_ref, km_sc):
    t = pl.program_id(1)

    @pl.when(t == 0)
    def _():
        km_sc[...] = jnp.zeros_like(km_sc)

    k = k_ref[...].astype(F32)
    for b in range(BLOCKS_PER_TILE):
        km_sc[pl.ds(t * BLOCKS_PER_TILE + b, 1), :] = jnp.mean(
            k[b * MOBA_BLOCK:(b + 1) * MOBA_BLOCK], axis=0, keepdims=True)

    qt = (q_ref[...].astype(F32) * (HEAD_DIM ** -0.5 * LOG2E)).T
    score = _dot(km_sc[:n_sel, :], qt, precision=HIGHEST)
    blk = lax.broadcasted_iota(jnp.int32, score.shape, 0)
    q_blk = t * BLOCKS_PER_TILE + lax.broadcasted_iota(jnp.int32, score.shape, 1) // MOBA_BLOCK
    s = jnp.where(blk < q_blk, score, -jnp.inf)
    pen = jnp.full(score.shape, MASKED, F32)
    for _ in range(MOBA_TOPK):
        m = jnp.max(s, axis=0, keepdims=True)
        first = jnp.min(jnp.where(s == m, blk, n_sel), axis=0, keepdims=True)
        first = jnp.where(m > -jnp.inf, first, n_sel)
        pick = blk == first
        pen = jnp.where(pick, 0.0, pen)
        s = jnp.where(pick, -jnp.inf, s)
    pen = jnp.where(blk == q_blk, 0.0, pen)
    qt_ref[0, 0, :HEAD_DIM, :] = qt.astype(BF16)
    qt_ref[0, 0, HEAD_DIM:HEAD_DIM + n_sel, :] = pen.astype(BF16)
    if n_sel < MAX_BLOCKS:
        qt_ref[0, 0, HEAD_DIM + n_sel:, :] = jnp.zeros((MAX_BLOCKS - n_sel, ATTN_TILE), BF16)

    lane = lax.broadcasted_iota(jnp.int32, (ATTN_TILE, MAX_BLOCKS), 1)
    k_blk = t * BLOCKS_PER_TILE + lax.broadcasted_iota(jnp.int32, lane.shape, 0) // MOBA_BLOCK
    ka_ref[0, :, :HEAD_DIM] = k.astype(BF16)
    ka_ref[0, :, HEAD_DIM:] = jnp.where(lane == k_blk, 1.0, 0.0).astype(BF16)

    ones_row = lax.broadcasted_iota(jnp.int32, (V_ROWS - HEAD_DIM, ATTN_TILE), 0) == 0
    vt_ref[0, 0, :HEAD_DIM, :] = v_ref[...].astype(F32).T.astype(BF16)
    vt_ref[0, 0, HEAD_DIM:, :] = jnp.where(ones_row, 1.0, 0.0).astype(BF16)


def _prep(proj):
    s = proj.shape[0]
    nt = s // ATTN_TILE
    blk = lambda col: pl.BlockSpec((ATTN_TILE, HEAD_DIM), lambda h, t: (t, col // HEAD_DIM + h))
    n_sel = -(-(s // MOBA_BLOCK) // BF16_SUBLANES) * BF16_SUBLANES
    return pl.pallas_call(
        functools.partial(_prep_kernel, n_sel),
        out_shape=(jax.ShapeDtypeStruct((ATTN_HEADS, nt, AUG_DIM, ATTN_TILE), BF16),
                   jax.ShapeDtypeStruct((ATTN_HEADS, s, AUG_DIM), BF16),
                   jax.ShapeDtypeStruct((ATTN_HEADS, nt, V_ROWS, ATTN_TILE), BF16)),
        grid=(ATTN_HEADS, nt),
        in_specs=[blk(COL_Q), blk(COL_K), blk(COL_V)],
        out_specs=(pl.BlockSpec((1, 1, AUG_DIM, ATTN_TILE), lambda h, t: (h, t, 0, 0)),
                   pl.BlockSpec((1, ATTN_TILE, AUG_DIM), lambda h, t: (h, t, 0)),
                   pl.BlockSpec((1, 1, V_ROWS, ATTN_TILE), lambda h, t: (h, t, 0, 0))),
        scratch_shapes=[pltpu.VMEM((MAX_BLOCKS, HEAD_DIM), F32)],
        compiler_params=_params(("parallel", "arbitrary")),
        name="prep",
    )(proj, proj, proj)


def _t5_bucket(rel):
    n = jnp.maximum(rel, 0)
    max_exact = REL_BUCKETS // 2
    nf = jnp.maximum(n, 1).astype(F32)
    large = max_exact + (jnp.log(nf / max_exact) / math.log(REL_MAX_DIST / max_exact)
                         * (REL_BUCKETS - max_exact)).astype(jnp.int32)
    large = jnp.minimum(large, REL_BUCKETS - 1)
    return jnp.where(n < max_exact, n, large)


def _bias_kernel(tab_ref, o_ref):
    h = pl.program_id(0)
    shape = (ATTN_TILE, ATTN_TILE)
    ki = lax.broadcasted_iota(jnp.int32, shape, 0)
    qi = lax.broadcasted_iota(jnp.int32, shape, 1)
    far = tab_ref[h * REL_BUCKETS + REL_BUCKETS - 1]
    for which in range(2):
        rel = qi - ki + which * ATTN_TILE
        bucket = _t5_bucket(rel)
        val = jnp.zeros(shape, F32)
        for b in range(REL_BUCKETS):
            val = jnp.where(bucket == b, tab_ref[h * REL_BUCKETS + b], val)
        val = (val - far) * LOG2E
        if which == 0:
            val = jnp.where(rel >= 0, val, MASKED)
        o_ref[0, which] = val


def _bias_tiles(tab_flat):
    return pl.pallas_call(
        _bias_kernel,
        out_shape=jax.ShapeDtypeStruct((ATTN_HEADS, 2, ATTN_TILE, ATTN_TILE), F32),
        grid=(ATTN_HEADS,),
        in_specs=[pl.BlockSpec(memory_space=pltpu.SMEM)],
        out_specs=pl.BlockSpec((1, 2, ATTN_TILE, ATTN_TILE), lambda h: (h, 0, 0, 0)),
        compiler_params=_params(("parallel",)),
        name="bias",
    )(tab_flat)


def _attn_kernel(qt_ref, ka_ref, vt_ref, t_ref, o_ref, m_sc, acc_sc, s_sc, p_sc):
    t = pl.program_id(1)
    qt = qt_ref[0, 0]

    def scores(j):
        start = pl.multiple_of(j * ATTN_TILE, ATTN_TILE)
        return _dot(ka_ref[0, pl.ds(start, ATTN_TILE), :], qt)

    m_sc[...] = jnp.full(m_sc.shape, 4.0 * MASKED, F32)
    acc_sc[...] = jnp.zeros_like(acc_sc)

    def colmax(s_ref):
        return jnp.max(s_ref[...], axis=0, keepdims=True)

    def consume(s_ref, s_max, j):
        m_old = m_sc[...]
        m_new = jnp.maximum(m_old, s_max)
        m_sc[...] = m_new
        for r in range(0, ATTN_TILE, EXP_ROWS):
            p_sc[r:r + EXP_ROWS, :] = jnp.exp2(s_ref[r:r + EXP_ROWS, :] - m_new).astype(BF16)
        acc_sc[...] = jnp.exp2(m_old - m_new) * acc_sc[...] + _dot(vt_ref[0, j], p_sc[...])

    def far_step(j, cur, nxt, max_cur):
        nxt[...] = scores(j + 1)
        consume(cur, max_cur, j)
        return colmax(nxt)

    s_a, s_b = s_sc.at[0], s_sc.at[1]

    @pl.when(t >= 1)
    def _():
        n_far = t - 1
        s_a[...] = scores(0)

        def pair(i, max_cur):
            max_cur = far_step(2 * i, s_a, s_b, max_cur)
            return far_step(2 * i + 1, s_b, s_a, max_cur)

        max_cur = lax.fori_loop(0, n_far // 2, pair, colmax(s_a))
        odd = n_far % 2 == 1

        @pl.when(odd)
        def _():
            far_step(n_far - 1, s_a, s_b, max_cur)
            s_b[...] = s_b[...] + t_ref[0, 1]
            consume(s_b, colmax(s_b), t - 1)

        @pl.when(jnp.logical_not(odd))
        def _():
            s_a[...] = s_a[...] + t_ref[0, 1]
            consume(s_a, colmax(s_a), t - 1)

    s_a[...] = scores(t) + t_ref[0, 0]
    consume(s_a, colmax(s_a), t)
    acc = acc_sc[...]
    out = acc[:HEAD_DIM] / acc[HEAD_DIM:HEAD_DIM + 1]
    o_ref[...] = out.T.astype(o_ref.dtype)


def _attn(qt, ka, vt, tiles):
    nt = qt.shape[1]
    s = nt * ATTN_TILE
    return pl.pallas_call(
        _attn_kernel,
        out_shape=jax.ShapeDtypeStruct((s, ATTN_WIDTH), BF16),
        grid=(ATTN_HEADS, nt),
        in_specs=[pl.BlockSpec((1, 1, AUG_DIM, ATTN_TILE), lambda h, t: (h, t, 0, 0)),
                  pl.BlockSpec((1, s, AUG_DIM), lambda h, t: (h, 0, 0)),
                  pl.BlockSpec((1, nt, V_ROWS, ATTN_TILE), lambda h, t: (h, 0, 0, 0)),
                  pl.BlockSpec((1, 2, ATTN_TILE, ATTN_TILE), lambda h, t: (h, 0, 0, 0))],
        out_specs=pl.BlockSpec((ATTN_TILE, HEAD_DIM), lambda h, t: (t, h)),
        scratch_shapes=[pltpu.VMEM((1, ATTN_TILE), F32), pltpu.VMEM((V_ROWS, ATTN_TILE), F32),
                        pltpu.VMEM((2, ATTN_TILE, ATTN_TILE), F32),
                        pltpu.VMEM((ATTN_TILE, ATTN_TILE), BF16)],
        compiler_params=_params(("parallel", "arbitrary")),
        name="attn",
    )(qt, ka, vt, tiles)


def _expand_heads(d, lane_head):
    out = d[:, SSM_HPG - 1:SSM_HPG]
    for hg in range(SSM_HPG - 2, -1, -1):
        out = jnp.where(lane_head == hg, d[:, hg:hg + 1], out)
    return out


def _ssd_kernel(z_ref, x_ref, b_ref, c_ref, dtc_ref, wx_ref, wb_ref, wc_ref, bx_ref, bb_ref, bc_ref,
                dtb_ref, alog_ref, dskip_ref, nw_ref, o_ref, xpx_sc, xpb_sc, xpc_sc, st_sc):
    c = pl.program_id(1)
    L, GW, NS, hp = SSM_CHUNK, SSM_GROUP_W, SSM_STATE, SSM_HPG

    @pl.when(c == 0)
    def _():
        xpx_sc[0:CONV_HALO, :] = jnp.zeros((CONV_HALO, xpx_sc.shape[1]), F32)
        xpb_sc[0:CONV_HALO, :] = jnp.zeros((CONV_HALO, xpb_sc.shape[1]), F32)
        xpc_sc[0:CONV_HALO, :] = jnp.zeros((CONV_HALO, xpc_sc.shape[1]), F32)
        st_sc[...] = jnp.zeros_like(st_sc)

    def conv(src_ref, pad_sc, w_ref, bias_ref):
        pad_sc[CONV_HALO:CONV_HALO + L, :] = src_ref[...].astype(F32)
        acc = bias_ref[...]
        for j in range(SSM_CONV):
            lo = CONV_HALO - (SSM_CONV - 1) + j
            acc = acc + pad_sc[lo:lo + L, :] * w_ref[j:j + 1, :]
        pad_sc[0:CONV_HALO, :] = pad_sc[L:L + CONV_HALO, :]
        return _silu(acc)

    x_all = conv(x_ref, xpx_sc, wx_ref, bx_ref)
    b_all = conv(b_ref, xpb_sc, wb_ref, bb_ref)
    c_all = conv(c_ref, xpc_sc, wc_ref, bc_ref)

    row = lax.broadcasted_iota(jnp.int32, (L, L), 0)
    col = lax.broadcasted_iota(jnp.int32, (L, L), 1)
    causal = row >= col
    triu = jnp.where(row <= col, 1.0, 0.0).astype(F32)
    lane_head = lax.broadcasted_iota(jnp.int32, (1, GW), 1) // SSM_HEAD_DIM

    for gi in range(SSD_GROUPS_PER_STEP):
        x = x_all[:, gi * GW:(gi + 1) * GW]
        bm = b_all[:, gi * NS:(gi + 1) * NS].astype(BF16)
        cm = c_all[:, gi * NS:(gi + 1) * NS].astype(BF16)

        dt_c = _softplus(dtc_ref[gi] + dtb_ref[gi])
        a_c = -jnp.exp(alog_ref[gi])
        cum_c = _dot(dt_c * a_c, triu, precision=HIGHEST)
        last_c = cum_c[:, L - 1:L]
        small = jnp.concatenate(
            [dt_c, cum_c, jnp.exp(cum_c), jnp.exp(last_c - cum_c) * dt_c,
             jnp.zeros((SMALL_ROWS - 4 * hp, L), F32)], axis=0).T
        cum_r = small[:, hp:2 * hp]
        dt_x = _expand_heads(small[:, 0:hp], lane_head)
        grow_x = _expand_heads(small[:, 2 * hp:3 * hp], lane_head)
        end_x = _expand_heads(small[:, 3 * hp:4 * hp], lane_head)
        last_x = _expand_heads(small[L - 1:L, 2 * hp:3 * hp], lane_head)

        cb = _dot_nt(cm, bm)
        xdt = x * dt_x
        w_parts, x_parts = [], []
        for hg in range(hp):
            seg = cum_r[:, hg:hg + 1] - cum_c[hg:hg + 1, :]
            w_parts.append(cb * jnp.exp(jnp.where(causal, seg, -jnp.inf)))
            x_parts.append(jnp.where(lane_head == hg, xdt, 0.0))
        y = _dot(jnp.concatenate(w_parts, axis=1).astype(BF16),
                 jnp.concatenate(x_parts, axis=0).astype(BF16))

        st = st_sc[gi]
        y = y + _dot(cm, st.astype(BF16)) * grow_x
        st_sc[gi] = last_x * st + _dot_tn(bm, (x * end_x).astype(BF16))
        y = y + x * dskip_ref[gi]

        g = y * _silu(z_ref[:, gi * GW:(gi + 1) * GW].astype(F32))
        g = g * lax.rsqrt(jnp.mean(g * g, axis=-1, keepdims=True) + RMS_EPS)
        o_ref[:, gi * GW:(gi + 1) * GW] = (g * nw_ref[gi]).astype(o_ref.dtype)


def _ssd(proj, dt_cols, conv_w, conv_b, dtb, alog, dskip_x, norm_w, nc):
    s = proj.shape[0]
    n = SSD_GROUPS_PER_STEP
    L, GW, NS = SSM_CHUNK, n * SSM_GROUP_W, n * SSM_STATE
    xoff, boff, coff = 0, SSM_INNER, SSM_INNER + SSM_GROUPS * SSM_STATE
    per_step = lambda a: pl.BlockSpec((n,) + a.shape[1:], lambda g, c: (g, 0, 0))
    return pl.pallas_call(
        _ssd_kernel,
        out_shape=jax.ShapeDtypeStruct((s, SSM_INNER), BF16),
        grid=(SSM_GROUPS // n, nc),
        in_specs=[
            pl.BlockSpec((L, GW), lambda g, c: (c, COL_Z // GW + g)),
            pl.BlockSpec((L, GW), lambda g, c: (c, COL_X // GW + g)),
            pl.BlockSpec((L, NS), lambda g, c: (c, COL_B // NS + g)),
            pl.BlockSpec((L, NS), lambda g, c: (c, COL_C // NS + g)),
            pl.BlockSpec((n, SSM_HPG, L), lambda g, c: (g, 0, c)),
            pl.BlockSpec((SSM_CONV, GW), lambda g, c: (0, xoff // GW + g)),
            pl.BlockSpec((SSM_CONV, NS), lambda g, c: (0, boff // NS + g)),
            pl.BlockSpec((SSM_CONV, NS), lambda g, c: (0, coff // NS + g)),
            pl.BlockSpec((1, GW), lambda g, c: (0, xoff // GW + g)),
            pl.BlockSpec((1, NS), lambda g, c: (0, boff // NS + g)),
            pl.BlockSpec((1, NS), lambda g, c: (0, coff // NS + g)),
            per_step(dtb), per_step(alog), per_step(dskip_x), per_step(norm_w),
        ],
        out_specs=pl.BlockSpec((L, GW), lambda g, c: (c, g)),
        scratch_shapes=[pltpu.VMEM((L + CONV_HALO, GW), F32), pltpu.VMEM((L + CONV_HALO, NS), F32),
                        pltpu.VMEM((L + CONV_HALO, NS), F32),
                        pltpu.VMEM((n, SSM_STATE, SSM_GROUP_W), F32)],
        compiler_params=_params(("parallel", "arbitrary")),
        name="ssd",
    )(proj, proj, proj, proj, dt_cols, conv_w, conv_w, conv_w, conv_b, conv_b, conv_b,
      dtb, alog, dskip_x, norm_w)


def _merge_kernel(h_ref, a_ref, b_ref, ga_ref, gb_ref, pa_ref, pb_ref, wo_ref, gate_ref, gpost_ref,
                  o_ref):
    ya = _dot(a_ref[...], pa_ref[...])
    yb = _dot(b_ref[...], pb_ref[...])
    mix = _sigmoid(ga_ref[...].astype(F32)) * ya + _sigmoid(gb_ref[...].astype(F32)) * yb
    y = _dot(mix.astype(BF16), wo_ref[...])
    o_ref[...] = h_ref[...] + gate_ref[...] * _rms(y, gpost_ref[...])


def _merge(h, attn, ssd, proj, pa, pb, wo, gate, gpost, tm):
    s = h.shape[0]
    vec = pl.BlockSpec((1, D_MODEL), lambda i: (0, 0))
    full = lambda a: pl.BlockSpec(a.shape, lambda i: (0, 0))
    return pl.pallas_call(
        _merge_kernel,
        out_shape=jax.ShapeDtypeStruct((s, D_MODEL), F32),
        grid=(s // tm,),
        in_specs=[pl.BlockSpec((tm, D_MODEL), lambda i: (i, 0)),
                  pl.BlockSpec((tm, ATTN_WIDTH), lambda i: (i, 0)),
                  pl.BlockSpec((tm, SSM_INNER), lambda i: (i, 0)),
                  pl.BlockSpec((tm, D_MODEL), lambda i: (i, COL_GA // D_MODEL)),
                  pl.BlockSpec((tm, D_MODEL), lambda i: (i, COL_GB // D_MODEL)),
                  full(pa), full(pb), full(wo), vec, vec],
        out_specs=pl.BlockSpec((tm, D_MODEL), lambda i: (i, 0)),
        compiler_params=_params(("parallel",)),
        name="merge",
    )(h, attn, ssd, proj, proj, pa, pb, wo, gate, gpost)


def _layer(h, mod, rel_bias, p):
    s = h.shape[0]
    nc = s // SSM_CHUNK
    tm = min(512, s)
    tm_wide = min(1024, s)
    sh1, sc1, g1, shm, scm, gm, sh2, sc2, g2 = [mod[k] for k in range(N_MOD)]
    vec = lambda a: a.reshape(1, -1)

    h = _ffn(h, vec(p["ffn1_norm_pre"]), sh1, sc1, g1, vec(p["ffn1_norm_post"]),
             p["ffn1_w_in"].astype(BF16), p["ffn1_w_out"].astype(BF16), tm=tm_wide)

    w_in = p["w_in_mix"].astype(BF16)
    dt_lo = COL_GA
    w_gates = w_in[:, dt_lo + SSM_HEADS:]
    w_dt = jnp.pad(w_in[:, dt_lo:dt_lo + SSM_HEADS], ((0, 0), (0, DT_PAD - SSM_HEADS)))
    proj, dt_raw = _inproj(h, vec(p["mix_norm_pre"]), shm, scm, w_in, w_gates, w_dt, tm=tm_wide)

    tab_flat = rel_bias.T.reshape(-1)
    qt, ka, vt = _prep(proj)
    tiles = _bias_tiles(tab_flat)
    attn = _attn(qt, ka, vt, tiles)

    dt_cols = dt_raw[:, :SSM_HEADS].reshape(s, SSM_GROUPS, SSM_HPG).transpose(1, 2, 0)
    per_group = lambda a: a.reshape(SSM_GROUPS, SSM_HPG, 1)
    dskip_x = jnp.repeat(p["d_skip"], SSM_HEAD_DIM).reshape(SSM_GROUPS, 1, SSM_GROUP_W)
    ssd = _ssd(proj, dt_cols, p["conv_w"], vec(p["conv_b"]),
               per_group(p["dt_bias"]), per_group(p["a_log"]),
               dskip_x, p["ssm_norm_w"].reshape(SSM_GROUPS, 1, SSM_GROUP_W), nc)

    h = _merge(h, attn, ssd, proj, p["proj_a"].astype(BF16), p["proj_b"].astype(BF16),
               p["w_out_mix"].astype(BF16), gm, vec(p["mix_norm_post"]), tm=tm)

    h = _ffn(h, vec(p["ffn2_norm_pre"]), sh2, sc2, g2, vec(p["ffn2_norm_post"]),
             p["ffn2_w_in"].astype(BF16), p["ffn2_w_out"].astype(BF16), tm=tm_wide)
    return h


_LAYER_KEYS = ("ffn1_norm_pre", "ffn1_w_in", "ffn1_w_out", "ffn1_norm_post", "mix_norm_pre",
               "w_in_mix", "conv_w", "conv_b", "dt_bias", "a_log", "d_skip", "ssm_norm_w",
               "proj_a", "proj_b", "w_out_mix", "mix_norm_post",
               "ffn2_norm_pre", "ffn2_w_in", "ffn2_w_out", "ffn2_norm_post")


def kernel(x, c, w_ada, b_ada, ffn1_norm_pre, ffn1_w_in, ffn1_w_out, ffn1_norm_post, mix_norm_pre,
           w_in_mix, rel_bias, conv_w, conv_b, dt_bias, a_log, d_skip, ssm_norm_w, proj_a, proj_b,
           w_out_mix, mix_norm_post, ffn2_norm_pre, ffn2_w_in, ffn2_w_out, ffn2_norm_post):
    stacked = dict(ffn1_norm_pre=ffn1_norm_pre, ffn1_w_in=ffn1_w_in, ffn1_w_out=ffn1_w_out,
                   ffn1_norm_post=ffn1_norm_post, mix_norm_pre=mix_norm_pre, w_in_mix=w_in_mix,
                   conv_w=conv_w, conv_b=conv_b, dt_bias=dt_bias, a_log=a_log, d_skip=d_skip,
                   ssm_norm_w=ssm_norm_w, proj_a=proj_a, proj_b=proj_b, w_out_mix=w_out_mix,
                   mix_norm_post=mix_norm_post, ffn2_norm_pre=ffn2_norm_pre, ffn2_w_in=ffn2_w_in,
                   ffn2_w_out=ffn2_w_out, ffn2_norm_post=ffn2_norm_post)
    batch, seq, _ = x.shape
    assert seq % ATTN_TILE == 0 and seq // MOBA_BLOCK <= MAX_BLOCKS and seq % SSM_CHUNK == 0
    depth = w_ada.shape[0]
    outs = []
    for b in range(batch):
        h = x[b]
        for l in range(depth):
            mod = _mod(c[b:b + 1], w_ada[l], b_ada[l])
            h = _layer(h, mod, rel_bias, {k: stacked[k][l] for k in _LAYER_KEYS})
        outs.append(h)
    return outs[0][None] if batch == 1 else jnp.stack(outs)
```

```python
import functools
import math

import jax
import jax.numpy as jnp
from jax import lax
from jax.experimental import pallas as pl
from jax.experimental.pallas import tpu as pltpu

F32 = jnp.float32
BF16 = jnp.bfloat16
HIGHEST = lax.Precision.HIGHEST

D_MODEL = 1024
N_MOD = 9
RMS_EPS = 1e-6
FFN_HIDDEN = 2816
FFN_RES = 0.5
FFN_CHUNK = FFN_HIDDEN // 2

ATTN_HEADS = 8
HEAD_DIM = 128
ATTN_WIDTH = ATTN_HEADS * HEAD_DIM
MOBA_BLOCK = 256
MOBA_TOPK = 3
MAX_BLOCKS = 128
AUG_DIM = HEAD_DIM + MAX_BLOCKS
REL_BUCKETS = 32
REL_MAX_DIST = 128
MASKED = -1e30
LOG2E = math.log2(math.e)
ATTN_TILE = 512
BLOCKS_PER_TILE = ATTN_TILE // MOBA_BLOCK
BF16_SUBLANES = 16
V_ROWS = HEAD_DIM + BF16_SUBLANES
EXP_ROWS = 64

SSM_INNER = 2048
SSM_HEAD_DIM = 64
SSM_GROUPS = 8
SSM_HEADS = SSM_INNER // SSM_HEAD_DIM
SSM_HPG = SSM_HEADS // SSM_GROUPS
SSM_GROUP_W = SSM_INNER // SSM_GROUPS
SSM_STATE = 128
SSM_CONV = 4
SSM_CHUNK = 256
CONV_HALO = 8
SMALL_ROWS = 128
SSD_GROUPS_PER_STEP = 2

COL_Q = 0
COL_K = COL_Q + ATTN_WIDTH
COL_V = COL_K + ATTN_WIDTH
COL_Z = COL_V + ATTN_WIDTH
COL_X = COL_Z + SSM_INNER
COL_B = COL_X + SSM_INNER
COL_C = COL_B + SSM_GROUPS * SSM_STATE
COL_GA = COL_C + SSM_GROUPS * SSM_STATE
COL_GB = COL_GA + D_MODEL
PROJ_W = COL_GB + D_MODEL
PROJ_TILE = 1024
MAIN_TILES = COL_GA // PROJ_TILE
DT_PAD = 128

VMEM_LIMIT = 56 * 1024 * 1024


def _params(sem):
    return pltpu.CompilerParams(dimension_semantics=sem, vmem_limit_bytes=VMEM_LIMIT)


def _sigmoid(x):
    return 0.5 + 0.5 * jnp.tanh(0.5 * x)


def _silu(x):
    return x * _sigmoid(x)


def _softplus(x):
    return jnp.maximum(x, 0.0) + jnp.log(1.0 + jnp.exp(-jnp.abs(x)))


def _rms(x, g):
    return x * lax.rsqrt(jnp.mean(x * x, axis=-1, keepdims=True) + RMS_EPS) * g


def _dot(a, b, **kw):
    return jnp.dot(a, b, preferred_element_type=F32, **kw)


def _dot_nt(a, b, **kw):
    return lax.dot_general(a, b, (((1,), (1,)), ((), ())), preferred_element_type=F32, **kw)


def _dot_tn(a, b, **kw):
    return lax.dot_general(a, b, (((0,), (0,)), ((), ())), preferred_element_type=F32, **kw)


def _mod_kernel(c_ref, w_ref, b_ref, o_ref):
    cs = _silu(c_ref[...])
    o_ref[...] = _dot(cs, w_ref[...], precision=HIGHEST) + b_ref[...]


def _mod(c, w_ada, b_ada):
    n = w_ada.shape[1]
    tn = 1024
    c8 = jnp.broadcast_to(c, (8, D_MODEL))
    out = pl.pallas_call(
        _mod_kernel,
        out_shape=jax.ShapeDtypeStruct((8, n), F32),
        grid=(n // tn,),
        in_specs=[pl.BlockSpec((8, D_MODEL), lambda j: (0, 0)),
                  pl.BlockSpec((D_MODEL, tn), lambda j: (0, j)),
                  pl.BlockSpec((1, tn), lambda j: (0, j))],
        out_specs=pl.BlockSpec((8, tn), lambda j: (0, j)),
        compiler_params=_params(("arbitrary",)),
        name="mod",
    )(c8, w_ada, b_ada.reshape(1, n))
    return out[0].reshape(N_MOD, 1, D_MODEL)


def _ffn_kernel(h_ref, gpre_ref, sh_ref, sc_ref, gate_ref, gpost_ref, wi_ref, wo_ref, o_ref):
    h = h_ref[...]
    u = (_rms(h, gpre_ref[...]) * (1.0 + sc_ref[...]) + sh_ref[...]).astype(BF16)
    acc = None
    for lo in range(0, FFN_HIDDEN, FFN_CHUNK):
        a = _dot(u, wi_ref[:, lo:lo + FFN_CHUNK])
        b = _dot(u, wi_ref[:, FFN_HIDDEN + lo:FFN_HIDDEN + lo + FFN_CHUNK])
        part = _dot((_silu(a) * b).astype(BF16), wo_ref[lo:lo + FFN_CHUNK, :])
        acc = part if acc is None else acc + part
    o_ref[...] = h + (FFN_RES * gate_ref[...]) * _rms(acc, gpost_ref[...])


def _ffn(h, gpre, sh, sc, gate, gpost, wi, wo, tm):
    s = h.shape[0]
    row = lambda i: (i, 0)
    vec = pl.BlockSpec((1, D_MODEL), lambda i: (0, 0))
    resident = lambda a: pl.BlockSpec(a.shape, lambda i: (0, 0), pipeline_mode=pl.Buffered(1))
    return pl.pallas_call(
        _ffn_kernel,
        out_shape=jax.ShapeDtypeStruct((s, D_MODEL), F32),
        grid=(s // tm,),
        in_specs=[pl.BlockSpec((tm, D_MODEL), row), vec, vec, vec, vec, vec,
                  resident(wi), resident(wo)],
        out_specs=pl.BlockSpec((tm, D_MODEL), row),
        compiler_params=_params(("parallel",)),
        name="ffn",
    )(h, gpre, sh, sc, gate, gpost, wi, wo)


def _inproj_kernel(h_ref, gpre_ref, sh_ref, sc_ref, w_ref, wg_ref, wdt_ref, o_ref, dt_ref, u_sc):
    j = pl.program_id(1)

    @pl.when(j == 0)
    def _():
        u = (_rms(h_ref[...], gpre_ref[...]) * (1.0 + sc_ref[...]) + sh_ref[...]).astype(BF16)
        u_sc[...] = u
        dt_ref[...] = _dot(u, wdt_ref[...])
        o_ref[...] = _dot(u, w_ref[...]).astype(o_ref.dtype)

    @pl.when(jnp.logical_and(j > 0, j < MAIN_TILES))
    def _():
        o_ref[...] = _dot(u_sc[...], w_ref[...]).astype(o_ref.dtype)

    @pl.when(j >= MAIN_TILES)
    def _():
        o_ref[...] = _dot(u_sc[...], wg_ref[...]).astype(o_ref.dtype)


def _inproj(h, gpre, sh, sc, w, wg, wdt, tm):
    s = h.shape[0]
    tn = PROJ_TILE
    vec = pl.BlockSpec((1, D_MODEL), lambda i, j: (0, 0))
    return pl.pallas_call(
        _inproj_kernel,
        out_shape=(jax.ShapeDtypeStruct((s, PROJ_W), BF16),
                   jax.ShapeDtypeStruct((s, DT_PAD), F32)),
        grid=(s // tm, PROJ_W // tn),
        in_specs=[pl.BlockSpec((tm, D_MODEL), lambda i, j: (i, 0)), vec, vec, vec,
                  pl.BlockSpec((D_MODEL, tn), lambda i, j: (0, jnp.minimum(j, MAIN_TILES - 1))),
                  pl.BlockSpec((D_MODEL, tn), lambda i, j: (0, jnp.maximum(j - MAIN_TILES, 0))),
                  pl.BlockSpec((D_MODEL, DT_PAD), lambda i, j: (0, 0))],
        out_specs=(pl.BlockSpec((tm, tn), lambda i, j: (i, j)),
                   pl.BlockSpec((tm, DT_PAD), lambda i, j: (i, 0))),
        scratch_shapes=[pltpu.VMEM((tm, D_MODEL), BF16)],
        compiler_params=_params(("parallel", "arbitrary")),
        name="inproj",
    )(h, gpre, sh, sc, w, wg, wdt)


def _prep_kernel(n_sel, q_ref, k_ref, v_ref, qt_ref, ka_ref, vt_ref, km_sc):
    t = pl.program_id(1)

    @pl.when(t == 0)
    def _():
        km_sc[...] = jnp.zeros_like(km_sc)

    k = k_ref[...].astype(F32)
    for b in range(BLOCKS_PER_TILE):
        km_sc[pl.ds(t * BLOCKS_PER_TILE + b, 1), :] = jnp.mean(
            k[b * MOBA_BLOCK:(b + 1) * MOBA_BLOCK], axis=0, keepdims=True)

    qt = (q_ref[...].astype(F32) * (HEAD_DIM ** -0.5 * LOG2E)).T
    score = _dot(km_sc[:n_sel, :], qt, precision=HIGHEST)
    blk = lax.broadcasted_iota(jnp.int32, score.shape, 0)
    q_blk = t * BLOCKS_PER_TILE + lax.broadcasted_iota(jnp.int32, score.shape, 1) // MOBA_BLOCK
    s = jnp.where(blk < q_blk, score, -jnp.inf)
    pen = jnp.full(score.shape, MASKED, F32)
    for _ in range(MOBA_TOPK):
        m = jnp.max(s, axis=0, keepdims=True)
        first = jnp.min(jnp.where(s == m, blk, n_sel), axis=0, keepdims=True)
        first = jnp.where(m > -jnp.inf, first, n_sel)
        pick = blk == first
        pen = jnp.where(pick, 0.0, pen)
        s = jnp.where(pick, -jnp.inf, s)
    pen = jnp.where(blk == q_blk, 0.0, pen)
    qt_ref[0, 0, :HEAD_DIM, :] = qt.astype(BF16)
    qt_ref[0, 0, HEAD_DIM:HEAD_DIM + n_sel, :] = pen.astype(BF16)
    if n_sel < MAX_BLOCKS:
        qt_ref[0, 0, HEAD_DIM + n_sel:, :] = jnp.zeros((MAX_BLOCKS - n_sel, ATTN_TILE), BF16)

    lane = lax.broadcasted_iota(jnp.int32, (ATTN_TILE, MAX_BLOCKS), 1)
    k_blk = t * BLOCKS_PER_TILE + lax.broadcasted_iota(jnp.int32, lane.shape, 0) // MOBA_BLOCK
    ka_ref[0, :, :HEAD_DIM] = k.astype(BF16)
    ka_ref[0, :, HEAD_DIM:] = jnp.where(lane == k_blk, 1.0, 0.0).astype(BF16)

    ones_row = lax.broadcasted_iota(jnp.int32, (V_ROWS - HEAD_DIM, ATTN_TILE), 0) == 0
    vt_ref[0, 0, :HEAD_DIM, :] = v_ref[...].astype(F32).T.astype(BF16)
    vt_ref[0, 0, HEAD_DIM:, :] = jnp.where(ones_row, 1.0, 0.0).astype(BF16)


def _prep(proj):
    s = proj.shape[0]
    nt = s // ATTN_TILE
    blk = lambda col: pl.BlockSpec((ATTN_TILE, HEAD_DIM), lambda h, t: (t, col // HEAD_DIM + h))
    n_sel = -(-(s // MOBA_BLOCK) // BF16_SUBLANES) * BF16_SUBLANES
    return pl.pallas_call(
        functools.partial(_prep_kernel, n_sel),
        out_shape=(jax.ShapeDtypeStruct((ATTN_HEADS, nt, AUG_DIM, ATTN_TILE), BF16),
                   jax.ShapeDtypeStruct((ATTN_HEADS, s, AUG_DIM), BF16),
                   jax.ShapeDtypeStruct((ATTN_HEADS, nt, V_ROWS, ATTN_TILE), BF16)),
        grid=(ATTN_HEADS, nt),
        in_specs=[blk(COL_Q), blk(COL_K), blk(COL_V)],
        out_specs=(pl.BlockSpec((1, 1, AUG_DIM, ATTN_TILE), lambda h, t: (h, t, 0, 0)),
                   pl.BlockSpec((1, ATTN_TILE, AUG_DIM), lambda h, t: (h, t, 0)),
                   pl.BlockSpec((1, 1, V_ROWS, ATTN_TILE), lambda h, t: (h, t, 0, 0))),
        scratch_shapes=[pltpu.VMEM((MAX_BLOCKS, HEAD_DIM), F32)],
        compiler_params=_params(("parallel", "arbitrary")),
        name="prep",
    )(proj, proj, proj)


def _t5_bucket(rel):
    n = jnp.maximum(rel, 0)
    max_exact = REL_BUCKETS // 2
    nf = jnp.maximum(n, 1).astype(F32)
    large = max_exact + (jnp.log(nf / max_exact) / math.log(REL_MAX_DIST / max_exact)
                         * (REL_BUCKETS - max_exact)).astype(jnp.int32)
    large = jnp.minimum(large, REL_BUCKETS - 1)
    return jnp.where(n < max_exact, n, large)


def _bias_kernel(tab_ref, o_ref):
    h = pl.program_id(0)
    shape = (ATTN_TILE, ATTN_TILE)
    ki = lax.broadcasted_iota(jnp.int32, shape, 0)
    qi = lax.broadcasted_iota(jnp.int32, shape, 1)
    far = tab_ref[h * REL_BUCKETS + REL_BUCKETS - 1]
    for which in range(2):
        rel = qi - ki + which * ATTN_TILE
        bucket = _t5_bucket(rel)
        val = jnp.zeros(shape, F32)
        for b in range(REL_BUCKETS):
            val = jnp.where(bucket == b, tab_ref[h * REL_BUCKETS + b], val)
        val = (val - far) * LOG2E
        if which == 0:
            val = jnp.where(rel >= 0, val, MASKED)
        o_ref[0, which] = val


def _bias_tiles(tab_flat):
    return pl.pallas_call(
        _bias_kernel,
        out_shape=jax.ShapeDtypeStruct((ATTN_HEADS, 2, ATTN_TILE, ATTN_TILE), F32),
        grid=(ATTN_HEADS,),
        in_specs=[pl.BlockSpec(memory_space=pltpu.SMEM)],
        out_specs=pl.BlockSpec((1, 2, ATTN_TILE, ATTN_TILE), lambda h: (h, 0, 0, 0)),
        compiler_params=_params(("parallel",)),
        name="bias",
    )(tab_flat)


def _attn_kernel(qt_ref, ka_ref, vt_ref, t_ref, o_ref, m_sc, acc_sc, s_sc, p_sc):
    t = pl.program_id(1)
    qt = qt_ref[0, 0]

    def scores(j):
        start = pl.multiple_of(j * ATTN_TILE, ATTN_TILE)
        return _dot(ka_ref[0, pl.ds(start, ATTN_TILE), :], qt)

    m_sc[...] = jnp.full(m_sc.shape, 4.0 * MASKED, F32)
    acc_sc[...] = jnp.zeros_like(acc_sc)

    def colmax(s_ref):
        return jnp.max(s_ref[...], axis=0, keepdims=True)

    def consume(s_ref, s_max, j):
        m_old = m_sc[...]
        m_new = jnp.maximum(m_old, s_max)
        m_sc[...] = m_new
        for r in range(0, ATTN_TILE, EXP_ROWS):
            p_sc[r:r + EXP_ROWS, :] = jnp.exp2(s_ref[r:r + EXP_ROWS, :] - m_new).astype(BF16)
        acc_sc[...] = jnp.exp2(m_old - m_new) * acc_sc[...] + _dot(vt_ref[0, j], p_sc[...])

    def far_step(j, cur, nxt, max_cur):
        nxt[...] = scores(j + 1)
        consume(cur, max_cur, j)
        return colmax(nxt)

    s_a, s_b = s_sc.at[0], s_sc.at[1]

    @pl.when(t >= 1)
    def _():
        n_far = t - 1
        s_a[...] = scores(0)

        def pair(i, max_cur):
            max_cur = far_step(2 * i, s_a, s_b, max_cur)
            return far_step(2 * i + 1, s_b, s_a, max_cur)

        max_cur = lax.fori_loop(0, n_far // 2, pair, colmax(s_a))
        odd = n_far % 2 == 1

        @pl.when(odd)
        def _():
            far_step(n_far - 1, s_a, s_b, max_cur)
            s_b[...] = s_b[...] + t_ref[0, 1]
            consume(s_b, colmax(s_b), t - 1)

        @pl.when(jnp.logical_not(odd))
        def _():
            s_a[...] = s_a[...] + t_ref[0, 1]
            consume(s_a, colmax(s_a), t - 1)

    s_a[...] = scores(t) + t_ref[0, 0]
    consume(s_a, colmax(s_a), t)
    acc = acc_sc[...]
    out = acc[:HEAD_DIM] / acc[HEAD_DIM:HEAD_DIM + 1]
    o_ref[...] = out.T.astype(o_ref.dtype)


def _attn(qt, ka, vt, tiles):
    nt = qt.shape[1]
    s = nt * ATTN_TILE
    return pl.pallas_call(
        _attn_kernel,
        out_shape=jax.ShapeDtypeStruct((s, ATTN_WIDTH), BF16),
        grid=(ATTN_HEADS, nt),
        in_specs=[pl.BlockSpec((1, 1, AUG_DIM, ATTN_TILE), lambda h, t: (h, t, 0, 0)),
                  pl.BlockSpec((1, s, AUG_DIM), lambda h, t: (h, 0, 0)),
                  pl.BlockSpec((1, nt, V_ROWS, ATTN_TILE), lambda h, t: (h, 0, 0, 0)),
                  pl.BlockSpec((1, 2, ATTN_TILE, ATTN_TILE), lambda h, t: (h, 0, 0, 0))],
        out_specs=pl.BlockSpec((ATTN_TILE, HEAD_DIM), lambda h, t: (t, h)),
        scratch_shapes=[pltpu.VMEM((1, ATTN_TILE), F32), pltpu.VMEM((V_ROWS, ATTN_TILE), F32),
                        pltpu.VMEM((2, ATTN_TILE, ATTN_TILE), F32),
                        pltpu.VMEM((ATTN_TILE, ATTN_TILE), BF16)],
        compiler_params=_params(("parallel", "arbitrary")),
        name="attn",
    )(qt, ka, vt, tiles)


def _expand_heads(d, lane_head):
    out = d[:, SSM_HPG - 1:SSM_HPG]
    for hg in range(SSM_HPG - 2, -1, -1):
        out = jnp.where(lane_head == hg, d[:, hg:hg + 1], out)
    return out


def _cumsum_lanes(x, triu):
    hi = x.astype(BF16)
    rest = x - hi.astype(F32)
    mid = rest.astype(BF16)
    lo = (rest - mid.astype(F32)).astype(BF16)
    return _dot(hi, triu) + _dot(mid, triu) + _dot(lo, triu)


def _ssd_kernel(z_ref, x_ref, b_ref, c_ref, dtc_ref, wx_ref, wb_ref, wc_ref, bx_ref, bb_ref, bc_ref,
                dtb_ref, alog_ref, dskip_ref, nw_ref, o_ref, xpx_sc, xpb_sc, xpc_sc, st_sc):
    c = pl.program_id(1)
    L, GW, NS, hp = SSM_CHUNK, SSM_GROUP_W, SSM_STATE, SSM_HPG

    @pl.when(c == 0)
    def _():
        xpx_sc[0:CONV_HALO, :] = jnp.zeros((CONV_HALO, xpx_sc.shape[1]), F32)
        xpb_sc[0:CONV_HALO, :] = jnp.zeros((CONV_HALO, xpb_sc.shape[1]), F32)
        xpc_sc[0:CONV_HALO, :] = jnp.zeros((CONV_HALO, xpc_sc.shape[1]), F32)
        st_sc[...] = jnp.zeros_like(st_sc)

    def conv(src_ref, pad_sc, w_ref, bias_ref):
        pad_sc[CONV_HALO:CONV_HALO + L, :] = src_ref[...].astype(F32)
        acc = bias_ref[...]
        for j in range(SSM_CONV):
            lo = CONV_HALO - (SSM_CONV - 1) + j
            acc = acc + pad_sc[lo:lo + L, :] * w_ref[j:j + 1, :]
        pad_sc[0:CONV_HALO, :] = pad_sc[L:L + CONV_HALO, :]
        return _silu(acc)

    x_all = conv(x_ref, xpx_sc, wx_ref, bx_ref)
    b_all = conv(b_ref, xpb_sc, wb_ref, bb_ref)
    c_all = conv(c_ref, xpc_sc, wc_ref, bc_ref)

    row = lax.broadcasted_iota(jnp.int32, (L, L), 0)
    col = lax.broadcasted_iota(jnp.int32, (L, L), 1)
    causal = row >= col
    triu = jnp.where(row <= col, 1.0, 0.0).astype(BF16)
    lane_head = lax.broadcasted_iota(jnp.int32, (1, GW), 1) // SSM_HEAD_DIM

    for gi in range(SSD_GROUPS_PER_STEP):
        x = x_all[:, gi * GW:(gi + 1) * GW]
        bm = b_all[:, gi * NS:(gi + 1) * NS].astype(BF16)
        cm = c_all[:, gi * NS:(gi + 1) * NS].astype(BF16)

        dt_c = _softplus(dtc_ref[gi] + dtb_ref[gi])
        a_c = -jnp.exp(alog_ref[gi])
        cum_c = _cumsum_lanes(dt_c * a_c, triu)
        last_c = cum_c[:, L - 1:L]
        small = jnp.concatenate(
            [cum_c, jnp.exp(cum_c), jnp.exp(last_c - cum_c) * dt_c,
             jnp.zeros((SMALL_ROWS - 3 * hp, L), F32)], axis=0).T
        cum_r = small[:, 0:hp]
        grow_x = _expand_heads(small[:, hp:2 * hp], lane_head)
        end_x = _expand_heads(small[:, 2 * hp:3 * hp], lane_head)
        last_x = _expand_heads(small[L - 1:L, hp:2 * hp], lane_head)
        src_c = cum_c - jnp.log(dt_c)

        cb = _dot_nt(cm, bm)
        w_parts, x_parts = [], []
        for hg in range(hp):
            seg = cum_r[:, hg:hg + 1] - src_c[hg:hg + 1, :]
            w_parts.append(cb * jnp.exp(jnp.where(causal, seg, -jnp.inf)))
            x_parts.append(jnp.where(lane_head == hg, x, 0.0))
        y = _dot(jnp.concatenate(w_parts, axis=1).astype(BF16),
                 jnp.concatenate(x_parts, axis=0).astype(BF16))

        st = st_sc[gi]
        y = y + _dot(cm, st.astype(BF16)) * grow_x
        st_sc[gi] = last_x * st + _dot_tn(bm, (x * end_x).astype(BF16))
        y = y + x * dskip_ref[gi]

        g = y * _silu(z_ref[:, gi * GW:(gi + 1) * GW].astype(F32))
        g = g * lax.rsqrt(jnp.mean(g * g, axis=-1, keepdims=True) + RMS_EPS)
        o_ref[:, gi * GW:(gi + 1) * GW] = (g * nw_ref[gi]).astype(o_ref.dtype)


def _ssd(proj, dt_cols, conv_w, conv_b, dtb, alog, dskip_x, norm_w, nc):
    s = proj.shape[0]
    n = SSD_GROUPS_PER_STEP
    L, GW, NS = SSM_CHUNK, n * SSM_GROUP_W, n * SSM_STATE
    xoff, boff, coff = 0, SSM_INNER, SSM_INNER + SSM_GROUPS * SSM_STATE
    per_step = lambda a: pl.BlockSpec((n,) + a.shape[1:], lambda g, c: (g, 0, 0))
    return pl.pallas_call(
        _ssd_kernel,
        out_shape=jax.ShapeDtypeStruct((s, SSM_INNER), BF16),
        grid=(SSM_GROUPS // n, nc),
        in_specs=[
            pl.BlockSpec((L, GW), lambda g, c: (c, COL_Z // GW + g)),
            pl.BlockSpec((L, GW), lambda g, c: (c, COL_X // GW + g)),
            pl.BlockSpec((L, NS), lambda g, c: (c, COL_B // NS + g)),
            pl.BlockSpec((L, NS), lambda g, c: (c, COL_C // NS + g)),
            pl.BlockSpec((n, SSM_HPG, L), lambda g, c: (g, 0, c)),
            pl.BlockSpec((SSM_CONV, GW), lambda g, c: (0, xoff // GW + g)),
            pl.BlockSpec((SSM_CONV, NS), lambda g, c: (0, boff // NS + g)),
            pl.BlockSpec((SSM_CONV, NS), lambda g, c: (0, coff // NS + g)),
            pl.BlockSpec((1, GW), lambda g, c: (0, xoff // GW + g)),
            pl.BlockSpec((1, NS), lambda g, c: (0, boff // NS + g)),
            pl.BlockSpec((1, NS), lambda g, c: (0, coff // NS + g)),
            per_step(dtb), per_step(alog), per_step(dskip_x), per_step(norm_w),
        ],
        out_specs=pl.BlockSpec((L, GW), lambda g, c: (c, g)),
        scratch_shapes=[pltpu.VMEM((L + CONV_HALO, GW), F32), pltpu.VMEM((L + CONV_HALO, NS), F32),
                        pltpu.VMEM((L + CONV_HALO, NS), F32),
                        pltpu.VMEM((n, SSM_STATE, SSM_GROUP_W), F32)],
        compiler_params=_params(("parallel", "arbitrary")),
        name="ssd",
    )(proj, proj, proj, proj, dt_cols, conv_w, conv_w, conv_w, conv_b, conv_b, conv_b,
      dtb, alog, dskip_x, norm_w)


def _merge_kernel(h_ref, a_ref, b_ref, ga_ref, gb_ref, pa_ref, pb_ref, wo_ref, gate_ref, gpost_ref,
                  o_ref):
    ya = _dot(a_ref[...], pa_ref[...])
    yb = _dot(b_ref[...], pb_ref[...])
    mix = _sigmoid(ga_ref[...].astype(F32)) * ya + _sigmoid(gb_ref[...].astype(F32)) * yb
    y = _dot(mix.astype(BF16), wo_ref[...])
    o_ref[...] = h_ref[...] + gate_ref[...] * _rms(y, gpost_ref[...])


def _merge(h, attn, ssd, proj, pa, pb, wo, gate, gpost, tm):
    s = h.shape[0]
    vec = pl.BlockSpec((1, D_MODEL), lambda i: (0, 0))
    full = lambda a: pl.BlockSpec(a.shape, lambda i: (0, 0))
    return pl.pallas_call(
        _merge_kernel,
        out_shape=jax.ShapeDtypeStruct((s, D_MODEL), F32),
        grid=(s // tm,),
        in_specs=[pl.BlockSpec((tm, D_MODEL), lambda i: (i, 0)),
                  pl.BlockSpec((tm, ATTN_WIDTH), lambda i: (i, 0)),
                  pl.BlockSpec((tm, SSM_INNER), lambda i: (i, 0)),
                  pl.BlockSpec((tm, D_MODEL), lambda i: (i, COL_GA // D_MODEL)),
                  pl.BlockSpec((tm, D_MODEL), lambda i: (i, COL_GB // D_MODEL)),
                  full(pa), full(pb), full(wo), vec, vec],
        out_specs=pl.BlockSpec((tm, D_MODEL), lambda i: (i, 0)),
        compiler_params=_params(("parallel",)),
        name="merge",
    )(h, attn, ssd, proj, proj, pa, pb, wo, gate, gpost)


def _layer(h, mod, rel_bias, p):
    s = h.shape[0]
    nc = s // SSM_CHUNK
    tm = min(512, s)
    tm_wide = min(1024, s)
    sh1, sc1, g1, shm, scm, gm, sh2, sc2, g2 = [mod[k] for k in range(N_MOD)]
    vec = lambda a: a.reshape(1, -1)

    h = _ffn(h, vec(p["ffn1_norm_pre"]), sh1, sc1, g1, vec(p["ffn1_norm_post"]),
             p["ffn1_w_in"].astype(BF16), p["ffn1_w_out"].astype(BF16), tm=tm_wide)

    w_in = p["w_in_mix"].astype(BF16)
    dt_lo = COL_GA
    w_gates = w_in[:, dt_lo + SSM_HEADS:]
    w_dt = jnp.pad(w_in[:, dt_lo:dt_lo + SSM_HEADS], ((0, 0), (0, DT_PAD - SSM_HEADS)))
    proj, dt_raw = _inproj(h, vec(p["mix_norm_pre"]), shm, scm, w_in, w_gates, w_dt, tm=tm_wide)

    tab_flat = rel_bias.T.reshape(-1)
    qt, ka, vt = _prep(proj)
    tiles = _bias_tiles(tab_flat)
    attn = _attn(qt, ka, vt, tiles)

    dt_cols = dt_raw[:, :SSM_HEADS].reshape(s, SSM_GROUPS, SSM_HPG).transpose(1, 2, 0)
    per_group = lambda a: a.reshape(SSM_GROUPS, SSM_HPG, 1)
    dskip_x = jnp.repeat(p["d_skip"], SSM_HEAD_DIM).reshape(SSM_GROUPS, 1, SSM_GROUP_W)
    ssd = _ssd(proj, dt_cols, p["conv_w"], vec(p["conv_b"]),
               per_group(p["dt_bias"]), per_group(p["a_log"]),
               dskip_x, p["ssm_norm_w"].reshape(SSM_GROUPS, 1, SSM_GROUP_W), nc)

    h = _merge(h, attn, ssd, proj, p["proj_a"].astype(BF16), p["proj_b"].astype(BF16),
               p["w_out_mix"].astype(BF16), gm, vec(p["mix_norm_post"]), tm=tm)

    h = _ffn(h, vec(p["ffn2_norm_pre"]), sh2, sc2, g2, vec(p["ffn2_norm_post"]),
             p["ffn2_w_in"].astype(BF16), p["ffn2_w_out"].astype(BF16), tm=tm_wide)
    return h


_LAYER_KEYS = ("ffn1_norm_pre", "ffn1_w_in", "ffn1_w_out", "ffn1_norm_post", "mix_norm_pre",
               "w_in_mix", "conv_w", "conv_b", "dt_bias", "a_log", "d_skip", "ssm_norm_w",
               "proj_a", "proj_b", "w_out_mix", "mix_norm_post",
               "ffn2_norm_pre", "ffn2_w_in", "ffn2_w_out", "ffn2_norm_post")


def kernel(x, c, w_ada, b_ada, ffn1_norm_pre, ffn1_w_in, ffn1_w_out, ffn1_norm_post, mix_norm_pre,
           w_in_mix, rel_bias, conv_w, conv_b, dt_bias, a_log, d_skip, ssm_norm_w, proj_a, proj_b,
           w_out_mix, mix_norm_post, ffn2_norm_pre, ffn2_w_in, ffn2_w_out, ffn2_norm_post):
    stacked = dict(ffn1_norm_pre=ffn1_norm_pre, ffn1_w_in=ffn1_w_in, ffn1_w_out=ffn1_w_out,
                   ffn1_norm_post=ffn1_norm_post, mix_norm_pre=mix_norm_pre, w_in_mix=w_in_mix,
                   conv_w=conv_w, conv_b=conv_b, dt_bias=dt_bias, a_log=a_log, d_skip=d_skip,
                   ssm_norm_w=ssm_norm_w, proj_a=proj_a, proj_b=proj_b, w_out_mix=w_out_mix,
                   mix_norm_post=mix_norm_post, ffn2_norm_pre=ffn2_norm_pre, ffn2_w_in=ffn2_w_in,
                   ffn2_w_out=ffn2_w_out, ffn2_norm_post=ffn2_norm_post)
    batch, seq, _ = x.shape
    assert seq % ATTN_TILE == 0 and seq // MOBA_BLOCK <= MAX_BLOCKS and seq % SSM_CHUNK == 0
    depth = w_ada.shape[0]
    outs = []
    for b in range(batch):
        h = x[b]
        for l in range(depth):
            mod = _mod(c[b:b + 1], w_ada[l], b_ada[l])
            h = _layer(h, mod, rel_bias, {k: stacked[k][l] for k in _LAYER_KEYS})
        outs.append(h)
    return outs[0][None] if batch == 1 else jnp.stack(outs)
```

```python
import functools
import math

import jax
import jax.numpy as jnp
from jax import lax
from jax.experimental import pallas as pl
from jax.experimental.pallas import tpu as pltpu

F32 = jnp.float32
BF16 = jnp.bfloat16
HIGHEST = lax.Precision.HIGHEST

D_MODEL = 1024
N_MOD = 9
RMS_EPS = 1e-6
FFN_HIDDEN = 2816
FFN_RES = 0.5
FFN_CHUNK = FFN_HIDDEN // 2

ATTN_HEADS = 8
HEAD_DIM = 128
ATTN_WIDTH = ATTN_HEADS * HEAD_DIM
MOBA_BLOCK = 256
MOBA_TOPK = 3
MAX_BLOCKS = 128
AUG_DIM = HEAD_DIM + MAX_BLOCKS
REL_BUCKETS = 32
REL_MAX_DIST = 128
MASKED = -1e30
LOG2E = math.log2(math.e)
ATTN_TILE = 512
BLOCKS_PER_TILE = ATTN_TILE // MOBA_BLOCK
BF16_SUBLANES = 16
V_ROWS = HEAD_DIM + BF16_SUBLANES
PREP_TILES_PER_STEP = 2

SSM_INNER = 2048
SSM_HEAD_DIM = 64
SSM_GROUPS = 8
SSM_HEADS = SSM_INNER // SSM_HEAD_DIM
SSM_HPG = SSM_HEADS // SSM_GROUPS
SSM_GROUP_W = SSM_INNER // SSM_GROUPS
SSM_STATE = 128
SSM_CONV = 4
SSM_CHUNK = 256
CONV_HALO = 8
SMALL_ROWS = 128
SSD_GROUPS_PER_STEP = 2

COL_Q = 0
COL_K = COL_Q + ATTN_WIDTH
COL_V = COL_K + ATTN_WIDTH
COL_Z = COL_V + ATTN_WIDTH
COL_X = COL_Z + SSM_INNER
COL_B = COL_X + SSM_INNER
COL_C = COL_B + SSM_GROUPS * SSM_STATE
COL_GA = COL_C + SSM_GROUPS * SSM_STATE
COL_GB = COL_GA + D_MODEL
PROJ_W = COL_GB + D_MODEL
PROJ_TILE = 1024
MAIN_TILES = COL_GA // PROJ_TILE
DT_PAD = 128

VMEM_LIMIT = 56 * 1024 * 1024


def _params(sem):
    return pltpu.CompilerParams(dimension_semantics=sem, vmem_limit_bytes=VMEM_LIMIT)


def _sigmoid(x):
    return 0.5 + 0.5 * jnp.tanh(0.5 * x)


def _silu(x):
    return x * _sigmoid(x)


def _softplus(x):
    return jnp.maximum(x, 0.0) + jnp.log(1.0 + jnp.exp(-jnp.abs(x)))


def _rms(x, g):
    return x * lax.rsqrt(jnp.mean(x * x, axis=-1, keepdims=True) + RMS_EPS) * g


def _dot(a, b, **kw):
    return jnp.dot(a, b, preferred_element_type=F32, **kw)


def _dot_nt(a, b, **kw):
    return lax.dot_general(a, b, (((1,), (1,)), ((), ())), preferred_element_type=F32, **kw)


def _dot_tn(a, b, **kw):
    return lax.dot_general(a, b, (((0,), (0,)), ((), ())), preferred_element_type=F32, **kw)


def _mod_kernel(c_ref, w_ref, b_ref, o_ref):
    cs = _silu(c_ref[...])
    o_ref[...] = _dot(cs, w_ref[...], precision=HIGHEST) + b_ref[...]


def _mod(c, w_ada, b_ada):
    n = w_ada.shape[1]
    tn = 1024
    c8 = jnp.broadcast_to(c, (8, D_MODEL))
    out = pl.pallas_call(
        _mod_kernel,
        out_shape=jax.ShapeDtypeStruct((8, n), F32),
        grid=(n // tn,),
        in_specs=[pl.BlockSpec((8, D_MODEL), lambda j: (0, 0)),
                  pl.BlockSpec((D_MODEL, tn), lambda j: (0, j)),
                  pl.BlockSpec((1, tn), lambda j: (0, j))],
        out_specs=pl.BlockSpec((8, tn), lambda j: (0, j)),
        compiler_params=_params(("arbitrary",)),
        name="mod",
    )(c8, w_ada, b_ada.reshape(1, n))
    return out[0].reshape(N_MOD, 1, D_MODEL)


def _ffn_kernel(h_ref, gpre_ref, sh_ref, sc_ref, gate_ref, gpost_ref, wi_ref, wo_ref, o_ref):
    h = h_ref[...]
    u = (_rms(h, gpre_ref[...]) * (1.0 + sc_ref[...]) + sh_ref[...]).astype(BF16)
    acc = None
    for lo in range(0, FFN_HIDDEN, FFN_CHUNK):
        a = _dot(u, wi_ref[:, lo:lo + FFN_CHUNK])
        b = _dot(u, wi_ref[:, FFN_HIDDEN + lo:FFN_HIDDEN + lo + FFN_CHUNK])
        part = _dot((_silu(a) * b).astype(BF16), wo_ref[lo:lo + FFN_CHUNK, :])
        acc = part if acc is None else acc + part
    o_ref[...] = h + (FFN_RES * gate_ref[...]) * _rms(acc, gpost_ref[...])


def _ffn(h, gpre, sh, sc, gate, gpost, wi, wo, tm):
    s = h.shape[0]
    row = lambda i: (i, 0)
    vec = pl.BlockSpec((1, D_MODEL), lambda i: (0, 0))
    resident = lambda a: pl.BlockSpec(a.shape, lambda i: (0, 0), pipeline_mode=pl.Buffered(1))
    return pl.pallas_call(
        _ffn_kernel,
        out_shape=jax.ShapeDtypeStruct((s, D_MODEL), F32),
        grid=(s // tm,),
        in_specs=[pl.BlockSpec((tm, D_MODEL), row), vec, vec, vec, vec, vec,
                  resident(wi), resident(wo)],
        out_specs=pl.BlockSpec((tm, D_MODEL), row),
        compiler_params=_params(("parallel",)),
        name="ffn",
    )(h, gpre, sh, sc, gate, gpost, wi, wo)


def _inproj_kernel(h_ref, gpre_ref, sh_ref, sc_ref, w_ref, wg_ref, wdt_ref, o_ref, dt_ref, u_sc):
    j = pl.program_id(1)

    @pl.when(j == 0)
    def _():
        u = (_rms(h_ref[...], gpre_ref[...]) * (1.0 + sc_ref[...]) + sh_ref[...]).astype(BF16)
        u_sc[...] = u
        dt_ref[...] = _dot(u, wdt_ref[...])
        o_ref[...] = _dot(u, w_ref[...]).astype(o_ref.dtype)

    @pl.when(jnp.logical_and(j > 0, j < MAIN_TILES))
    def _():
        o_ref[...] = _dot(u_sc[...], w_ref[...]).astype(o_ref.dtype)

    @pl.when(j >= MAIN_TILES)
    def _():
        o_ref[...] = _dot(u_sc[...], wg_ref[...]).astype(o_ref.dtype)


def _inproj(h, gpre, sh, sc, w, wg, wdt, tm):
    s = h.shape[0]
    tn = PROJ_TILE
    vec = pl.BlockSpec((1, D_MODEL), lambda i, j: (0, 0))
    return pl.pallas_call(
        _inproj_kernel,
        out_shape=(jax.ShapeDtypeStruct((s, PROJ_W), BF16),
                   jax.ShapeDtypeStruct((s, DT_PAD), F32)),
        grid=(s // tm, PROJ_W // tn),
        in_specs=[pl.BlockSpec((tm, D_MODEL), lambda i, j: (i, 0)), vec, vec, vec,
                  pl.BlockSpec((D_MODEL, tn), lambda i, j: (0, jnp.minimum(j, MAIN_TILES - 1))),
                  pl.BlockSpec((D_MODEL, tn), lambda i, j: (0, jnp.maximum(j - MAIN_TILES, 0))),
                  pl.BlockSpec((D_MODEL, DT_PAD), lambda i, j: (0, 0))],
        out_specs=(pl.BlockSpec((tm, tn), lambda i, j: (i, j)),
                   pl.BlockSpec((tm, DT_PAD), lambda i, j: (i, 0))),
        scratch_shapes=[pltpu.VMEM((tm, D_MODEL), BF16)],
        compiler_params=_params(("parallel", "arbitrary")),
        name="inproj",
    )(h, gpre, sh, sc, w, wg, wdt)


def _prep_kernel(n_sel, n_tiles, q_ref, k_ref, v_ref, qt_ref, ka_ref, vt_ref, km_sc):
    @pl.when(pl.program_id(1) == 0)
    def _():
        km_sc[...] = jnp.zeros_like(km_sc)

    for u in range(n_tiles):
        t = pl.program_id(1) * n_tiles + u
        rows = slice(u * ATTN_TILE, (u + 1) * ATTN_TILE)
        k = k_ref[rows, :].astype(F32)
        for b in range(BLOCKS_PER_TILE):
            km_sc[pl.ds(t * BLOCKS_PER_TILE + b, 1), :] = jnp.mean(
                k[b * MOBA_BLOCK:(b + 1) * MOBA_BLOCK], axis=0, keepdims=True)

        qt = (q_ref[rows, :].astype(F32) * (HEAD_DIM ** -0.5 * LOG2E)).T
        score = _dot(km_sc[:n_sel, :], qt, precision=HIGHEST)
        blk = lax.broadcasted_iota(jnp.int32, score.shape, 0)
        q_blk = t * BLOCKS_PER_TILE + lax.broadcasted_iota(jnp.int32, score.shape, 1) // MOBA_BLOCK
        s = jnp.where(blk < q_blk, score, -jnp.inf)
        pen = jnp.full(score.shape, MASKED, F32)
        for _ in range(MOBA_TOPK):
            m = jnp.max(s, axis=0, keepdims=True)
            first = jnp.min(jnp.where(s == m, blk, n_sel), axis=0, keepdims=True)
            first = jnp.where(m > -jnp.inf, first, n_sel)
            pick = blk == first
            pen = jnp.where(pick, 0.0, pen)
            s = jnp.where(pick, -jnp.inf, s)
        pen = jnp.where(blk == q_blk, 0.0, pen)
        qt_ref[0, u, :HEAD_DIM, :] = qt.astype(BF16)
        qt_ref[0, u, HEAD_DIM:HEAD_DIM + n_sel, :] = pen.astype(BF16)
        if n_sel < MAX_BLOCKS:
            qt_ref[0, u, HEAD_DIM + n_sel:, :] = jnp.zeros((MAX_BLOCKS - n_sel, ATTN_TILE), BF16)

        lane = lax.broadcasted_iota(jnp.int32, (ATTN_TILE, MAX_BLOCKS), 1)
        k_blk = t * BLOCKS_PER_TILE + lax.broadcasted_iota(jnp.int32, lane.shape, 0) // MOBA_BLOCK
        ka_ref[0, rows, :HEAD_DIM] = k.astype(BF16)
        ka_ref[0, rows, HEAD_DIM:] = jnp.where(lane == k_blk, 1.0, 0.0).astype(BF16)

        ones_row = lax.broadcasted_iota(jnp.int32, (V_ROWS - HEAD_DIM, ATTN_TILE), 0) == 0
        vt_ref[0, u, :HEAD_DIM, :] = v_ref[rows, :].astype(F32).T.astype(BF16)
        vt_ref[0, u, HEAD_DIM:, :] = jnp.where(ones_row, 1.0, 0.0).astype(BF16)


def _prep(proj):
    s = proj.shape[0]
    nt = s // ATTN_TILE
    n = PREP_TILES_PER_STEP if nt % PREP_TILES_PER_STEP == 0 else 1
    blk = lambda col: pl.BlockSpec((n * ATTN_TILE, HEAD_DIM), lambda h, t: (t, col // HEAD_DIM + h))
    n_sel = -(-(s // MOBA_BLOCK) // BF16_SUBLANES) * BF16_SUBLANES
    return pl.pallas_call(
        functools.partial(_prep_kernel, n_sel, n),
        out_shape=(jax.ShapeDtypeStruct((ATTN_HEADS, nt, AUG_DIM, ATTN_TILE), BF16),
                   jax.ShapeDtypeStruct((ATTN_HEADS, s, AUG_DIM), BF16),
                   jax.ShapeDtypeStruct((ATTN_HEADS, nt, V_ROWS, ATTN_TILE), BF16)),
        grid=(ATTN_HEADS, nt // n),
        in_specs=[blk(COL_Q), blk(COL_K), blk(COL_V)],
        out_specs=(pl.BlockSpec((1, n, AUG_DIM, ATTN_TILE), lambda h, t: (h, t, 0, 0)),
                   pl.BlockSpec((1, n * ATTN_TILE, AUG_DIM), lambda h, t: (h, t, 0)),
                   pl.BlockSpec((1, n, V_ROWS, ATTN_TILE), lambda h, t: (h, t, 0, 0))),
        scratch_shapes=[pltpu.VMEM((MAX_BLOCKS, HEAD_DIM), F32)],
        compiler_params=_params(("parallel", "arbitrary")),
        name="prep",
    )(proj, proj, proj)


def _t5_bucket(rel):
    n = jnp.maximum(rel, 0)
    max_exact = REL_BUCKETS // 2
    nf = jnp.maximum(n, 1).astype(F32)
    large = max_exact + (jnp.log(nf / max_exact) / math.log(REL_MAX_DIST / max_exact)
                         * (REL_BUCKETS - max_exact)).astype(jnp.int32)
    large = jnp.minimum(large, REL_BUCKETS - 1)
    return jnp.where(n < max_exact, n, large)


def _bias_kernel(tab_ref, o_ref):
    h = pl.program_id(0)
    shape = (ATTN_TILE, ATTN_TILE)
    ki = lax.broadcasted_iota(jnp.int32, shape, 0)
    qi = lax.broadcasted_iota(jnp.int32, shape, 1)
    far = tab_ref[h * REL_BUCKETS + REL_BUCKETS - 1]
    for which in range(2):
        rel = qi - ki + which * ATTN_TILE
        bucket = _t5_bucket(rel)
        val = jnp.zeros(shape, F32)
        for b in range(REL_BUCKETS):
            val = jnp.where(bucket == b, tab_ref[h * REL_BUCKETS + b], val)
        val = (val - far) * LOG2E
        if which == 0:
            val = jnp.where(rel >= 0, val, MASKED)
        o_ref[0, which] = val


def _bias_tiles(tab_flat):
    return pl.pallas_call(
        _bias_kernel,
        out_shape=jax.ShapeDtypeStruct((ATTN_HEADS, 2, ATTN_TILE, ATTN_TILE), F32),
        grid=(ATTN_HEADS,),
        in_specs=[pl.BlockSpec(memory_space=pltpu.SMEM)],
        out_specs=pl.BlockSpec((1, 2, ATTN_TILE, ATTN_TILE), lambda h: (h, 0, 0, 0)),
        compiler_params=_params(("parallel",)),
        name="bias",
    )(tab_flat)


def _attn_kernel(qt_ref, ka_ref, vt_ref, t_ref, o_ref, m_sc, acc_sc, s_sc, p_sc, p2_sc):
    t = pl.program_id(1)
    qt = qt_ref[0, 0]

    def scores(j):
        start = pl.multiple_of(j * ATTN_TILE, ATTN_TILE)
        return _dot(ka_ref[0, pl.ds(start, ATTN_TILE), :], qt)

    m_sc[...] = jnp.full(m_sc.shape, 4.0 * MASKED, F32)
    acc_sc[...] = jnp.zeros_like(acc_sc)

    def colmax(s_ref):
        return jnp.max(s_ref[...], axis=0, keepdims=True)

    def accumulate(alpha, terms):
        acc = alpha * acc_sc[...]
        for j, p_ref in terms:
            acc = acc + _dot(vt_ref[0, j], p_ref[...])
        acc_sc[...] = acc

    def consume(s_ref, s_max, j):
        m_old = m_sc[...]
        m_new = jnp.maximum(m_old, s_max)
        m_sc[...] = m_new
        p_sc[...] = jnp.exp2(s_ref[...] - m_new).astype(BF16)
        accumulate(jnp.exp2(m_old - m_new), [(j, p_sc)])

    def consume_near(s_prev, s_diag):
        m_old = m_sc[...]
        m_new = jnp.maximum(m_old, jnp.maximum(colmax(s_prev), colmax(s_diag)))
        m_sc[...] = m_new
        p_sc[...] = jnp.exp2(s_prev[...] - m_new).astype(BF16)
        p2_sc[...] = jnp.exp2(s_diag[...] - m_new).astype(BF16)
        accumulate(jnp.exp2(m_old - m_new), [(t - 1, p_sc), (t, p2_sc)])

    def far_step(j, cur, nxt, max_cur):
        nxt[...] = scores(j + 1)
        consume(cur, max_cur, j)
        return colmax(nxt)

    s_a, s_b = s_sc.at[0], s_sc.at[1]

    @pl.when(t == 0)
    def _():
        s_a[...] = scores(0) + t_ref[0, 0]
        consume(s_a, colmax(s_a), 0)

    @pl.when(t >= 1)
    def _():
        n_far = t - 1
        s_a[...] = scores(0)

        def pair(i, max_cur):
            max_cur = far_step(2 * i, s_a, s_b, max_cur)
            return far_step(2 * i + 1, s_b, s_a, max_cur)

        max_cur = lax.fori_loop(0, n_far // 2, pair, colmax(s_a))
        odd = n_far % 2 == 1

        @pl.when(odd)
        def _():
            far_step(n_far - 1, s_a, s_b, max_cur)
            s_a[...] = scores(t) + t_ref[0, 0]
            s_b[...] = s_b[...] + t_ref[0, 1]
            consume_near(s_b, s_a)

        @pl.when(jnp.logical_not(odd))
        def _():
            s_b[...] = scores(t) + t_ref[0, 0]
            s_a[...] = s_a[...] + t_ref[0, 1]
            consume_near(s_a, s_b)

    acc = acc_sc[...]
    out = acc[:HEAD_DIM] / acc[HEAD_DIM:HEAD_DIM + 1]
    o_ref[...] = out.T.astype(o_ref.dtype)


def _attn(qt, ka, vt, tiles):
    nt = qt.shape[1]
    s = nt * ATTN_TILE
    return pl.pallas_call(
        _attn_kernel,
        out_shape=jax.ShapeDtypeStruct((s, ATTN_WIDTH), BF16),
        grid=(ATTN_HEADS, nt),
        in_specs=[pl.BlockSpec((1, 1, AUG_DIM, ATTN_TILE), lambda h, t: (h, t, 0, 0)),
                  pl.BlockSpec((1, s, AUG_DIM), lambda h, t: (h, 0, 0)),
                  pl.BlockSpec((1, nt, V_ROWS, ATTN_TILE), lambda h, t: (h, 0, 0, 0)),
                  pl.BlockSpec((1, 2, ATTN_TILE, ATTN_TILE), lambda h, t: (h, 0, 0, 0))],
        out_specs=pl.BlockSpec((ATTN_TILE, HEAD_DIM), lambda h, t: (t, h)),
        scratch_shapes=[pltpu.VMEM((1, ATTN_TILE), F32), pltpu.VMEM((V_ROWS, ATTN_TILE), F32),
                        pltpu.VMEM((2, ATTN_TILE, ATTN_TILE), F32),
                        pltpu.VMEM((ATTN_TILE, ATTN_TILE), BF16), pltpu.VMEM((ATTN_TILE, ATTN_TILE), BF16)],
        compiler_params=_params(("parallel", "arbitrary")),
        name="attn",
    )(qt, ka, vt, tiles)


def _expand_heads(d, lane_head):
    out = d[:, SSM_HPG - 1:SSM_HPG]
    for hg in range(SSM_HPG - 2, -1, -1):
        out = jnp.where(lane_head == hg, d[:, hg:hg + 1], out)
    return out


def _cumsum_lanes(x, triu):
    hi = x.astype(BF16)
    rest = x - hi.astype(F32)
    mid = rest.astype(BF16)
    lo = (rest - mid.astype(F32)).astype(BF16)
    return _dot(hi, triu) + _dot(mid, triu) + _dot(lo, triu)


def _ssd_kernel(z_ref, x_ref, b_ref, c_ref, dtc_ref, wx_ref, wb_ref, wc_ref, bx_ref, bb_ref, bc_ref,
                dtb_ref, alog_ref, dskip_ref, nw_ref, o_ref, xpx_sc, xpb_sc, xpc_sc, st_sc):
    c = pl.program_id(1)
    L, GW, NS, hp = SSM_CHUNK, SSM_GROUP_W, SSM_STATE, SSM_HPG

    @pl.when(c == 0)
    def _():
        xpx_sc[0:CONV_HALO, :] = jnp.zeros((CONV_HALO, xpx_sc.shape[1]), F32)
        xpb_sc[0:CONV_HALO, :] = jnp.zeros((CONV_HALO, xpb_sc.shape[1]), F32)
        xpc_sc[0:CONV_HALO, :] = jnp.zeros((CONV_HALO, xpc_sc.shape[1]), F32)
        st_sc[...] = jnp.zeros_like(st_sc)

    def conv(src_ref, pad_sc, w_ref, bias_ref):
        pad_sc[CONV_HALO:CONV_HALO + L, :] = src_ref[...].astype(F32)
        acc = bias_ref[...]
        for j in range(SSM_CONV):
            lo = CONV_HALO - (SSM_CONV - 1) + j
            acc = acc + pad_sc[lo:lo + L, :] * w_ref[j:j + 1, :]
        pad_sc[0:CONV_HALO, :] = pad_sc[L:L + CONV_HALO, :]
        return _silu(acc)

    x_all = conv(x_ref, xpx_sc, wx_ref, bx_ref)
    b_all = conv(b_ref, xpb_sc, wb_ref, bb_ref)
    c_all = conv(c_ref, xpc_sc, wc_ref, bc_ref)

    row = lax.broadcasted_iota(jnp.int32, (L, L), 0)
    col = lax.broadcasted_iota(jnp.int32, (L, L), 1)
    causal = row >= col
    triu = jnp.where(row <= col, 1.0, 0.0).astype(BF16)
    lane_head = lax.broadcasted_iota(jnp.int32, (1, GW), 1) // SSM_HEAD_DIM

    for gi in range(SSD_GROUPS_PER_STEP):
        x = x_all[:, gi * GW:(gi + 1) * GW]
        bm = b_all[:, gi * NS:(gi + 1) * NS].astype(BF16)
        cm = c_all[:, gi * NS:(gi + 1) * NS].astype(BF16)

        dt_c = _softplus(dtc_ref[gi] + dtb_ref[gi])
        a_c = -jnp.exp(alog_ref[gi])
        cum_c = _cumsum_lanes(dt_c * a_c, triu)
        last_c = cum_c[:, L - 1:L]
        small = jnp.concatenate(
            [cum_c, jnp.exp(cum_c), jnp.exp(last_c - cum_c) * dt_c,
             jnp.zeros((SMALL_ROWS - 3 * hp, L), F32)], axis=0).T
        cum_r = small[:, 0:hp]
        grow_x = _expand_heads(small[:, hp:2 * hp], lane_head)
        end_x = _expand_heads(small[:, 2 * hp:3 * hp], lane_head)
        last_x = _expand_heads(small[L - 1:L, hp:2 * hp], lane_head)
        src_c = cum_c - jnp.log(dt_c)

        cb = _dot_nt(cm, bm)
        w_parts, x_parts = [], []
        for hg in range(hp):
            seg = cum_r[:, hg:hg + 1] - src_c[hg:hg + 1, :]
            w_parts.append(cb * jnp.exp(jnp.where(causal, seg, -jnp.inf)))
            x_parts.append(jnp.where(lane_head == hg, x, 0.0))
        y = _dot(jnp.concatenate(w_parts, axis=1).astype(BF16),
                 jnp.concatenate(x_parts, axis=0).astype(BF16))

        st = st_sc[gi]
        y = y + _dot(cm, st.astype(BF16)) * grow_x
        st_sc[gi] = last_x * st + _dot_tn(bm, (x * end_x).astype(BF16))
        y = y + x * dskip_ref[gi]

        g = y * _silu(z_ref[:, gi * GW:(gi + 1) * GW].astype(F32))
        g = g * lax.rsqrt(jnp.mean(g * g, axis=-1, keepdims=True) + RMS_EPS)
        o_ref[:, gi * GW:(gi + 1) * GW] = (g * nw_ref[gi]).astype(o_ref.dtype)


def _ssd(proj, dt_cols, conv_w, conv_b, dtb, alog, dskip_x, norm_w, nc):
    s = proj.shape[0]
    n = SSD_GROUPS_PER_STEP
    L, GW, NS = SSM_CHUNK, n * SSM_GROUP_W, n * SSM_STATE
    xoff, boff, coff = 0, SSM_INNER, SSM_INNER + SSM_GROUPS * SSM_STATE
    per_step = lambda a: pl.BlockSpec((n,) + a.shape[1:], lambda g, c: (g, 0, 0))
    return pl.pallas_call(
        _ssd_kernel,
        out_shape=jax.ShapeDtypeStruct((s, SSM_INNER), BF16),
        grid=(SSM_GROUPS // n, nc),
        in_specs=[
            pl.BlockSpec((L, GW), lambda g, c: (c, COL_Z // GW + g)),
            pl.BlockSpec((L, GW), lambda g, c: (c, COL_X // GW + g)),
            pl.BlockSpec((L, NS), lambda g, c: (c, COL_B // NS + g)),
            pl.BlockSpec((L, NS), lambda g, c: (c, COL_C // NS + g)),
            pl.BlockSpec((n, SSM_HPG, L), lambda g, c: (g, 0, c)),
            pl.BlockSpec((SSM_CONV, GW), lambda g, c: (0, xoff // GW + g)),
            pl.BlockSpec((SSM_CONV, NS), lambda g, c: (0, boff // NS + g)),
            pl.BlockSpec((SSM_CONV, NS), lambda g, c: (0, coff // NS + g)),
            pl.BlockSpec((1, GW), lambda g, c: (0, xoff // GW + g)),
            pl.BlockSpec((1, NS), lambda g, c: (0, boff // NS + g)),
            pl.BlockSpec((1, NS), lambda g, c: (0, coff // NS + g)),
            per_step(dtb), per_step(alog), per_step(dskip_x), per_step(norm_w),
        ],
        out_specs=pl.BlockSpec((L, GW), lambda g, c: (c, g)),
        scratch_shapes=[pltpu.VMEM((L + CONV_HALO, GW), F32), pltpu.VMEM((L + CONV_HALO, NS), F32),
                        pltpu.VMEM((L + CONV_HALO, NS), F32),
                        pltpu.VMEM((n, SSM_STATE, SSM_GROUP_W), F32)],
        compiler_params=_params(("parallel", "arbitrary")),
        name="ssd",
    )(proj, proj, proj, proj, dt_cols, conv_w, conv_w, conv_w, conv_b, conv_b, conv_b,
      dtb, alog, dskip_x, norm_w)


def _merge_kernel(h_ref, a_ref, b_ref, ga_ref, gb_ref, pa_ref, pb_ref, wo_ref, gate_ref, gpost_ref,
                  o_ref):
    ya = _dot(a_ref[...], pa_ref[...])
    yb = _dot(b_ref[...], pb_ref[...])
    mix = _sigmoid(ga_ref[...].astype(F32)) * ya + _sigmoid(gb_ref[...].astype(F32)) * yb
    y = _dot(mix.astype(BF16), wo_ref[...])
    o_ref[...] = h_ref[...] + gate_ref[...] * _rms(y, gpost_ref[...])


def _merge(h, attn, ssd, proj, pa, pb, wo, gate, gpost, tm):
    s = h.shape[0]
    vec = pl.BlockSpec((1, D_MODEL), lambda i: (0, 0))
    full = lambda a: pl.BlockSpec(a.shape, lambda i: (0, 0))
    return pl.pallas_call(
        _merge_kernel,
        out_shape=jax.ShapeDtypeStruct((s, D_MODEL), F32),
        grid=(s // tm,),
        in_specs=[pl.BlockSpec((tm, D_MODEL), lambda i: (i, 0)),
                  pl.BlockSpec((tm, ATTN_WIDTH), lambda i: (i, 0)),
                  pl.BlockSpec((tm, SSM_INNER), lambda i: (i, 0)),
                  pl.BlockSpec((tm, D_MODEL), lambda i: (i, COL_GA // D_MODEL)),
                  pl.BlockSpec((tm, D_MODEL), lambda i: (i, COL_GB // D_MODEL)),
                  full(pa), full(pb), full(wo), vec, vec],
        out_specs=pl.BlockSpec((tm, D_MODEL), lambda i: (i, 0)),
        compiler_params=_params(("parallel",)),
        name="merge",
    )(h, attn, ssd, proj, proj, pa, pb, wo, gate, gpost)


def _layer(h, mod, rel_bias, p):
    s = h.shape[0]
    nc = s // SSM_CHUNK
    tm = min(512, s)
    tm_wide = min(1024, s)
    sh1, sc1, g1, shm, scm, gm, sh2, sc2, g2 = [mod[k] for k in range(N_MOD)]
    vec = lambda a: a.reshape(1, -1)

    h = _ffn(h, vec(p["ffn1_norm_pre"]), sh1, sc1, g1, vec(p["ffn1_norm_post"]),
             p["ffn1_w_in"].astype(BF16), p["ffn1_w_out"].astype(BF16), tm=tm_wide)

    w_in = p["w_in_mix"].astype(BF16)
    dt_lo = COL_GA
    w_gates = w_in[:, dt_lo + SSM_HEADS:]
    w_dt = jnp.pad(w_in[:, dt_lo:dt_lo + SSM_HEADS], ((0, 0), (0, DT_PAD - SSM_HEADS)))
    proj, dt_raw = _inproj(h, vec(p["mix_norm_pre"]), shm, scm, w_in, w_gates, w_dt, tm=tm_wide)

    tab_flat = rel_bias.T.reshape(-1)
    qt, ka, vt = _prep(proj)
    tiles = _bias_tiles(tab_flat)
    attn = _attn(qt, ka, vt, tiles)

    dt_cols = dt_raw[:, :SSM_HEADS].reshape(s, SSM_GROUPS, SSM_HPG).transpose(1, 2, 0)
    per_group = lambda a: a.reshape(SSM_GROUPS, SSM_HPG, 1)
    dskip_x = jnp.repeat(p["d_skip"], SSM_HEAD_DIM).reshape(SSM_GROUPS, 1, SSM_GROUP_W)
    ssd = _ssd(proj, dt_cols, p["conv_w"], vec(p["conv_b"]),
               per_group(p["dt_bias"]), per_group(p["a_log"]),
               dskip_x, p["ssm_norm_w"].reshape(SSM_GROUPS, 1, SSM_GROUP_W), nc)

    h = _merge(h, attn, ssd, proj, p["proj_a"].astype(BF16), p["proj_b"].astype(BF16),
               p["w_out_mix"].astype(BF16), gm, vec(p["mix_norm_post"]), tm=tm)

    h = _ffn(h, vec(p["ffn2_norm_pre"]), sh2, sc2, g2, vec(p["ffn2_norm_post"]),
             p["ffn2_w_in"].astype(BF16), p["ffn2_w_out"].astype(BF16), tm=tm_wide)
    return h


_LAYER_KEYS = ("ffn1_norm_pre", "ffn1_w_in", "ffn1_w_out", "ffn1_norm_post", "mix_norm_pre",
               "w_in_mix", "conv_w", "conv_b", "dt_bias", "a_log", "d_skip", "ssm_norm_w",
               "proj_a", "proj_b", "w_out_mix", "mix_norm_post",
               "ffn2_norm_pre", "ffn2_w_in", "ffn2_w_out", "ffn2_norm_post")


def kernel(x, c, w_ada, b_ada, ffn1_norm_pre, ffn1_w_in, ffn1_w_out, ffn1_norm_post, mix_norm_pre,
           w_in_mix, rel_bias, conv_w, conv_b, dt_bias, a_log, d_skip, ssm_norm_w, proj_a, proj_b,
           w_out_mix, mix_norm_post, ffn2_norm_pre, ffn2_w_in, ffn2_w_out, ffn2_norm_post):
    stacked = dict(ffn1_norm_pre=ffn1_norm_pre, ffn1_w_in=ffn1_w_in, ffn1_w_out=ffn1_w_out,
                   ffn1_norm_post=ffn1_norm_post, mix_norm_pre=mix_norm_pre, w_in_mix=w_in_mix,
                   conv_w=conv_w, conv_b=conv_b, dt_bias=dt_bias, a_log=a_log, d_skip=d_skip,
                   ssm_norm_w=ssm_norm_w, proj_a=proj_a, proj_b=proj_b, w_out_mix=w_out_mix,
                   mix_norm_post=mix_norm_post, ffn2_norm_pre=ffn2_norm_pre, ffn2_w_in=ffn2_w_in,
                   ffn2_w_out=ffn2_w_out, ffn2_norm_post=ffn2_norm_post)
    batch, seq, _ = x.shape
    assert seq % ATTN_TILE == 0 and seq // MOBA_BLOCK <= MAX_BLOCKS and seq % SSM_CHUNK == 0
    depth = w_ada.shape[0]
    outs = []
    for b in range(batch):
        h = x[b]
        for l in range(depth):
            mod = _mod(c[b:b + 1], w_ada[l], b_ada[l])
            h = _layer(h, mod, rel_bias, {k: stacked[k][l] for k in _LAYER_KEYS})
        outs.append(h)
    return outs[0][None] if batch == 1 else jnp.stack(outs)
```

```python
import functools
import math

import jax
import jax.numpy as jnp
from jax import lax
from jax.experimental import pallas as pl
from jax.experimental.pallas import tpu as pltpu

F32 = jnp.float32
BF16 = jnp.bfloat16
HIGHEST = lax.Precision.HIGHEST

D_MODEL = 1024
N_MOD = 9
RMS_EPS = 1e-6
FFN_HIDDEN = 2816
FFN_RES = 0.5
FFN_CHUNK = FFN_HIDDEN // 2

ATTN_HEADS = 8
HEAD_DIM = 128
ATTN_WIDTH = ATTN_HEADS * HEAD_DIM
MOBA_BLOCK = 256
MOBA_TOPK = 3
MAX_BLOCKS = 128
AUG_DIM = HEAD_DIM + MAX_BLOCKS
REL_BUCKETS = 32
REL_MAX_DIST = 128
MASKED = -1e30
LOG2E = math.log2(math.e)
ATTN_TILE = 512
BLOCKS_PER_TILE = ATTN_TILE // MOBA_BLOCK
BF16_SUBLANES = 16
V_ROWS = HEAD_DIM + BF16_SUBLANES
PREP_TILES_PER_STEP = 4

SSM_INNER = 2048
SSM_HEAD_DIM = 64
SSM_GROUPS = 8
SSM_HEADS = SSM_INNER // SSM_HEAD_DIM
SSM_HPG = SSM_HEADS // SSM_GROUPS
SSM_GROUP_W = SSM_INNER // SSM_GROUPS
SSM_STATE = 128
SSM_CONV = 4
SSM_CHUNK = 256
CONV_HALO = 8
SMALL_ROWS = 128
SSD_GROUPS_PER_STEP = 2

COL_Q = 0
COL_K = COL_Q + ATTN_WIDTH
COL_V = COL_K + ATTN_WIDTH
COL_Z = COL_V + ATTN_WIDTH
COL_X = COL_Z + SSM_INNER
COL_B = COL_X + SSM_INNER
COL_C = COL_B + SSM_GROUPS * SSM_STATE
COL_GA = COL_C + SSM_GROUPS * SSM_STATE
COL_GB = COL_GA + D_MODEL
PROJ_W = COL_GB + D_MODEL
PROJ_TILE = 1024
MAIN_TILES = COL_GA // PROJ_TILE
DT_PAD = 128

VMEM_LIMIT = 56 * 1024 * 1024


def _params(sem):
    return pltpu.CompilerParams(dimension_semantics=sem, vmem_limit_bytes=VMEM_LIMIT)


def _sigmoid(x):
    return 0.5 + 0.5 * jnp.tanh(0.5 * x)


def _silu(x):
    return x * _sigmoid(x)


def _softplus(x):
    return jnp.maximum(x, 0.0) + jnp.log(1.0 + jnp.exp(-jnp.abs(x)))


def _rms(x, g):
    return x * lax.rsqrt(jnp.mean(x * x, axis=-1, keepdims=True) + RMS_EPS) * g


def _dot(a, b, **kw):
    return jnp.dot(a, b, preferred_element_type=F32, **kw)


def _dot_nt(a, b, **kw):
    return lax.dot_general(a, b, (((1,), (1,)), ((), ())), preferred_element_type=F32, **kw)


def _dot_tn(a, b, **kw):
    return lax.dot_general(a, b, (((0,), (0,)), ((), ())), preferred_element_type=F32, **kw)


def _mod_kernel(c_ref, w_ref, b_ref, o_ref):
    cs = _silu(c_ref[...])
    o_ref[...] = _dot(cs, w_ref[...], precision=HIGHEST) + b_ref[...]


def _mod(c, w_ada, b_ada):
    n = w_ada.shape[1]
    tn = 1024
    c8 = jnp.broadcast_to(c, (8, D_MODEL))
    out = pl.pallas_call(
        _mod_kernel,
        out_shape=jax.ShapeDtypeStruct((8, n), F32),
        grid=(n // tn,),
        in_specs=[pl.BlockSpec((8, D_MODEL), lambda j: (0, 0)),
                  pl.BlockSpec((D_MODEL, tn), lambda j: (0, j)),
                  pl.BlockSpec((1, tn), lambda j: (0, j))],
        out_specs=pl.BlockSpec((8, tn), lambda j: (0, j)),
        compiler_params=_params(("arbitrary",)),
        name="mod",
    )(c8, w_ada, b_ada.reshape(1, n))
    return out[0].reshape(N_MOD, 1, D_MODEL)


def _ffn_kernel(h_ref, gpre_ref, sh_ref, sc_ref, gate_ref, gpost_ref, wi_ref, wo_ref, o_ref):
    h = h_ref[...]
    u = (_rms(h, gpre_ref[...]) * (1.0 + sc_ref[...]) + sh_ref[...]).astype(BF16)
    acc = None
    for lo in range(0, FFN_HIDDEN, FFN_CHUNK):
        a = _dot(u, wi_ref[:, lo:lo + FFN_CHUNK])
        b = _dot(u, wi_ref[:, FFN_HIDDEN + lo:FFN_HIDDEN + lo + FFN_CHUNK])
        part = _dot((_silu(a) * b).astype(BF16), wo_ref[lo:lo + FFN_CHUNK, :])
        acc = part if acc is None else acc + part
    o_ref[...] = h + (FFN_RES * gate_ref[...]) * _rms(acc, gpost_ref[...])


def _ffn(h, gpre, sh, sc, gate, gpost, wi, wo, tm):
    s = h.shape[0]
    row = lambda i: (i, 0)
    vec = pl.BlockSpec((1, D_MODEL), lambda i: (0, 0))
    resident = lambda a: pl.BlockSpec(a.shape, lambda i: (0, 0), pipeline_mode=pl.Buffered(1))
    return pl.pallas_call(
        _ffn_kernel,
        out_shape=jax.ShapeDtypeStruct((s, D_MODEL), F32),
        grid=(s // tm,),
        in_specs=[pl.BlockSpec((tm, D_MODEL), row), vec, vec, vec, vec, vec,
                  resident(wi), resident(wo)],
        out_specs=pl.BlockSpec((tm, D_MODEL), row),
        compiler_params=_params(("parallel",)),
        name="ffn",
    )(h, gpre, sh, sc, gate, gpost, wi, wo)


def _inproj_kernel(h_ref, gpre_ref, sh_ref, sc_ref, w_ref, wg_ref, wdt_ref, o_ref, dt_ref, u_sc):
    j = pl.program_id(1)

    @pl.when(j == 0)
    def _():
        u = (_rms(h_ref[...], gpre_ref[...]) * (1.0 + sc_ref[...]) + sh_ref[...]).astype(BF16)
        u_sc[...] = u
        dt_ref[...] = _dot(u, wdt_ref[...])
        o_ref[...] = _dot(u, w_ref[...]).astype(o_ref.dtype)

    @pl.when(jnp.logical_and(j > 0, j < MAIN_TILES))
    def _():
        o_ref[...] = _dot(u_sc[...], w_ref[...]).astype(o_ref.dtype)

    @pl.when(j >= MAIN_TILES)
    def _():
        o_ref[...] = _dot(u_sc[...], wg_ref[...]).astype(o_ref.dtype)


def _inproj(h, gpre, sh, sc, w, wg, wdt, tm):
    s = h.shape[0]
    tn = PROJ_TILE
    vec = pl.BlockSpec((1, D_MODEL), lambda i, j: (0, 0))
    return pl.pallas_call(
        _inproj_kernel,
        out_shape=(jax.ShapeDtypeStruct((s, PROJ_W), BF16),
                   jax.ShapeDtypeStruct((s, DT_PAD), F32)),
        grid=(s // tm, PROJ_W // tn),
        in_specs=[pl.BlockSpec((tm, D_MODEL), lambda i, j: (i, 0)), vec, vec, vec,
                  pl.BlockSpec((D_MODEL, tn), lambda i, j: (0, jnp.minimum(j, MAIN_TILES - 1))),
                  pl.BlockSpec((D_MODEL, tn), lambda i, j: (0, jnp.maximum(j - MAIN_TILES, 0))),
                  pl.BlockSpec((D_MODEL, DT_PAD), lambda i, j: (0, 0))],
        out_specs=(pl.BlockSpec((tm, tn), lambda i, j: (i, j)),
                   pl.BlockSpec((tm, DT_PAD), lambda i, j: (i, 0))),
        scratch_shapes=[pltpu.VMEM((tm, D_MODEL), BF16)],
        compiler_params=_params(("parallel", "arbitrary")),
        name="inproj",
    )(h, gpre, sh, sc, w, wg, wdt)


def _prep_kernel(n_sel, n_tiles, q_ref, k_ref, v_ref, qt_ref, ka_ref, vt_ref, km_sc):
    @pl.when(pl.program_id(1) == 0)
    def _():
        km_sc[...] = jnp.zeros_like(km_sc)

    for u in range(n_tiles):
        t = pl.program_id(1) * n_tiles + u
        rows = slice(u * ATTN_TILE, (u + 1) * ATTN_TILE)
        k = k_ref[rows, :].astype(F32)
        for b in range(BLOCKS_PER_TILE):
            km_sc[pl.ds(t * BLOCKS_PER_TILE + b, 1), :] = jnp.mean(
                k[b * MOBA_BLOCK:(b + 1) * MOBA_BLOCK], axis=0, keepdims=True)

        qt = (q_ref[rows, :].astype(F32) * (HEAD_DIM ** -0.5 * LOG2E)).T
        score = _dot(km_sc[:n_sel, :], qt, precision=HIGHEST)
        blk = lax.broadcasted_iota(jnp.int32, score.shape, 0)
        q_blk = t * BLOCKS_PER_TILE + lax.broadcasted_iota(jnp.int32, score.shape, 1) // MOBA_BLOCK
        s = jnp.where(blk < q_blk, score, -jnp.inf)
        pen = jnp.full(score.shape, MASKED, F32)
        for _ in range(MOBA_TOPK):
            m = jnp.max(s, axis=0, keepdims=True)
            first = jnp.min(jnp.where(s == m, blk, n_sel), axis=0, keepdims=True)
            first = jnp.where(m > -jnp.inf, first, n_sel)
            pick = blk == first
            pen = jnp.where(pick, 0.0, pen)
            s = jnp.where(pick, -jnp.inf, s)
        pen = jnp.where(blk == q_blk, 0.0, pen)
        qt_ref[0, u, :HEAD_DIM, :] = qt.astype(BF16)
        qt_ref[0, u, HEAD_DIM:HEAD_DIM + n_sel, :] = pen.astype(BF16)
        if n_sel < MAX_BLOCKS:
            qt_ref[0, u, HEAD_DIM + n_sel:, :] = jnp.zeros((MAX_BLOCKS - n_sel, ATTN_TILE), BF16)

        lane = lax.broadcasted_iota(jnp.int32, (ATTN_TILE, MAX_BLOCKS), 1)
        k_blk = t * BLOCKS_PER_TILE + lax.broadcasted_iota(jnp.int32, lane.shape, 0) // MOBA_BLOCK
        ka_ref[0, rows, :HEAD_DIM] = k.astype(BF16)
        ka_ref[0, rows, HEAD_DIM:] = jnp.where(lane == k_blk, 1.0, 0.0).astype(BF16)

        ones_row = lax.broadcasted_iota(jnp.int32, (V_ROWS - HEAD_DIM, ATTN_TILE), 0) == 0
        vt_ref[0, u, :HEAD_DIM, :] = v_ref[rows, :].astype(F32).T.astype(BF16)
        vt_ref[0, u, HEAD_DIM:, :] = jnp.where(ones_row, 1.0, 0.0).astype(BF16)


def _prep(proj):
    s = proj.shape[0]
    nt = s // ATTN_TILE
    n = PREP_TILES_PER_STEP if nt % PREP_TILES_PER_STEP == 0 else 1
    blk = lambda col: pl.BlockSpec((n * ATTN_TILE, HEAD_DIM), lambda h, t: (t, col // HEAD_DIM + h))
    n_sel = -(-(s // MOBA_BLOCK) // BF16_SUBLANES) * BF16_SUBLANES
    return pl.pallas_call(
        functools.partial(_prep_kernel, n_sel, n),
        out_shape=(jax.ShapeDtypeStruct((ATTN_HEADS, nt, AUG_DIM, ATTN_TILE), BF16),
                   jax.ShapeDtypeStruct((ATTN_HEADS, s, AUG_DIM), BF16),
                   jax.ShapeDtypeStruct((ATTN_HEADS, nt, V_ROWS, ATTN_TILE), BF16)),
        grid=(ATTN_HEADS, nt // n),
        in_specs=[blk(COL_Q), blk(COL_K), blk(COL_V)],
        out_specs=(pl.BlockSpec((1, n, AUG_DIM, ATTN_TILE), lambda h, t: (h, t, 0, 0)),
                   pl.BlockSpec((1, n * ATTN_TILE, AUG_DIM), lambda h, t: (h, t, 0)),
                   pl.BlockSpec((1, n, V_ROWS, ATTN_TILE), lambda h, t: (h, t, 0, 0))),
        scratch_shapes=[pltpu.VMEM((MAX_BLOCKS, HEAD_DIM), F32)],
        compiler_params=_params(("parallel", "arbitrary")),
        name="prep",
    )(proj, proj, proj)


def _t5_bucket(rel):
    n = jnp.maximum(rel, 0)
    max_exact = REL_BUCKETS // 2
    nf = jnp.maximum(n, 1).astype(F32)
    large = max_exact + (jnp.log(nf / max_exact) / math.log(REL_MAX_DIST / max_exact)
                         * (REL_BUCKETS - max_exact)).astype(jnp.int32)
    large = jnp.minimum(large, REL_BUCKETS - 1)
    return jnp.where(n < max_exact, n, large)


def _bias_kernel(tab_ref, o_ref):
    h = pl.program_id(0)
    shape = (ATTN_TILE, ATTN_TILE)
    far = tab_ref[h * REL_BUCKETS + REL_BUCKETS - 1]
    bucket = _t5_bucket(lax.broadcasted_iota(jnp.int32, (1, ATTN_TILE), 1))
    val = jnp.zeros((1, ATTN_TILE), F32)
    for b in range(REL_BUCKETS):
        val = jnp.where(bucket == b, tab_ref[h * REL_BUCKETS + b], val)
    val = (val - far) * LOG2E
    toeplitz = pltpu.roll(jnp.broadcast_to(val, shape), 0, 1, stride=1, stride_axis=0)
    ki = lax.broadcasted_iota(jnp.int32, shape, 0)
    qi = lax.broadcasted_iota(jnp.int32, shape, 1)
    o_ref[0, 0] = jnp.where(qi >= ki, toeplitz, MASKED)
    o_ref[0, 1] = jnp.where(qi < ki, toeplitz, 0.0)


def _bias_tiles(tab_flat):
    return pl.pallas_call(
        _bias_kernel,
        out_shape=jax.ShapeDtypeStruct((ATTN_HEADS, 2, ATTN_TILE, ATTN_TILE), F32),
        grid=(ATTN_HEADS,),
        in_specs=[pl.BlockSpec(memory_space=pltpu.SMEM)],
        out_specs=pl.BlockSpec((1, 2, ATTN_TILE, ATTN_TILE), lambda h: (h, 0, 0, 0)),
        compiler_params=_params(("parallel",)),
        name="bias",
    )(tab_flat)


def _attn_kernel(qt_ref, ka_ref, vt_ref, t_ref, o_ref, m_sc, acc_sc, s_sc, p_sc, p2_sc):
    t = pl.program_id(1)
    qt = qt_ref[0, 0]

    def scores(j):
        start = pl.multiple_of(j * ATTN_TILE, ATTN_TILE)
        return _dot(ka_ref[0, pl.ds(start, ATTN_TILE), :], qt)

    m_sc[...] = jnp.full(m_sc.shape, 4.0 * MASKED, F32)
    acc_sc[...] = jnp.zeros_like(acc_sc)

    def colmax(s_ref):
        return jnp.max(s_ref[...], axis=0, keepdims=True)

    def accumulate(alpha, terms):
        acc = alpha * acc_sc[...]
        for j, p_ref in terms:
            acc = acc + _dot(vt_ref[0, j], p_ref[...])
        acc_sc[...] = acc

    def consume(s_ref, s_max, j):
        m_old = m_sc[...]
        m_new = jnp.maximum(m_old, s_max)
        m_sc[...] = m_new
        p_sc[...] = jnp.exp2(s_ref[...] - m_new).astype(BF16)
        accumulate(jnp.exp2(m_old - m_new), [(j, p_sc)])

    def consume_near(s_prev, s_diag):
        m_old = m_sc[...]
        m_new = jnp.maximum(m_old, jnp.maximum(colmax(s_prev), colmax(s_diag)))
        m_sc[...] = m_new
        p_sc[...] = jnp.exp2(s_prev[...] - m_new).astype(BF16)
        p2_sc[...] = jnp.exp2(s_diag[...] - m_new).astype(BF16)
        accumulate(jnp.exp2(m_old - m_new), [(t - 1, p_sc), (t, p2_sc)])

    def far_step(j, cur, nxt, max_cur):
        nxt[...] = scores(j + 1)
        consume(cur, max_cur, j)
        return colmax(nxt)

    s_a, s_b = s_sc.at[0], s_sc.at[1]

    @pl.when(t == 0)
    def _():
        s_a[...] = scores(0) + t_ref[0, 0]
        consume(s_a, colmax(s_a), 0)

    @pl.when(t >= 1)
    def _():
        n_far = t - 1
        s_a[...] = scores(0)

        def pair(i, max_cur):
            max_cur = far_step(2 * i, s_a, s_b, max_cur)
            return far_step(2 * i + 1, s_b, s_a, max_cur)

        max_cur = lax.fori_loop(0, n_far // 2, pair, colmax(s_a))
        odd = n_far % 2 == 1

        @pl.when(odd)
        def _():
            far_step(n_far - 1, s_a, s_b, max_cur)
            s_a[...] = scores(t) + t_ref[0, 0]
            s_b[...] = s_b[...] + t_ref[0, 1]
            consume_near(s_b, s_a)

        @pl.when(jnp.logical_not(odd))
        def _():
            s_b[...] = scores(t) + t_ref[0, 0]
            s_a[...] = s_a[...] + t_ref[0, 1]
            consume_near(s_a, s_b)

    acc = acc_sc[...]
    out = acc[:HEAD_DIM] / acc[HEAD_DIM:HEAD_DIM + 1]
    o_ref[...] = out.T.astype(o_ref.dtype)


def _attn(qt, ka, vt, tiles):
    nt = qt.shape[1]
    s = nt * ATTN_TILE
    return pl.pallas_call(
        _attn_kernel,
        out_shape=jax.ShapeDtypeStruct((s, ATTN_WIDTH), BF16),
        grid=(ATTN_HEADS, nt),
        in_specs=[pl.BlockSpec((1, 1, AUG_DIM, ATTN_TILE), lambda h, t: (h, t, 0, 0)),
                  pl.BlockSpec((1, s, AUG_DIM), lambda h, t: (h, 0, 0)),
                  pl.BlockSpec((1, nt, V_ROWS, ATTN_TILE), lambda h, t: (h, 0, 0, 0)),
                  pl.BlockSpec((1, 2, ATTN_TILE, ATTN_TILE), lambda h, t: (h, 0, 0, 0))],
        out_specs=pl.BlockSpec((ATTN_TILE, HEAD_DIM), lambda h, t: (t, h)),
        scratch_shapes=[pltpu.VMEM((1, ATTN_TILE), F32), pltpu.VMEM((V_ROWS, ATTN_TILE), F32),
                        pltpu.VMEM((2, ATTN_TILE, ATTN_TILE), F32),
                        pltpu.VMEM((ATTN_TILE, ATTN_TILE), BF16), pltpu.VMEM((ATTN_TILE, ATTN_TILE), BF16)],
        compiler_params=_params(("parallel", "arbitrary")),
        name="attn",
    )(qt, ka, vt, tiles)


def _expand_heads(d, lane_head):
    out = d[:, SSM_HPG - 1:SSM_HPG]
    for hg in range(SSM_HPG - 2, -1, -1):
        out = jnp.where(lane_head == hg, d[:, hg:hg + 1], out)
    return out


def _cumsum_lanes(x, triu):
    hi = x.astype(BF16)
    rest = x - hi.astype(F32)
    mid = rest.astype(BF16)
    lo = (rest - mid.astype(F32)).astype(BF16)
    return _dot(hi, triu) + _dot(mid, triu) + _dot(lo, triu)


def _ssd_kernel(z_ref, x_ref, b_ref, c_ref, dtc_ref, wx_ref, wb_ref, wc_ref, bx_ref, bb_ref, bc_ref,
                dtb_ref, alog_ref, dskip_ref, nw_ref, o_ref, xpx_sc, xpb_sc, xpc_sc, st_sc):
    c = pl.program_id(1)
    L, GW, NS, hp = SSM_CHUNK, SSM_GROUP_W, SSM_STATE, SSM_HPG

    @pl.when(c == 0)
    def _():
        xpx_sc[0:CONV_HALO, :] = jnp.zeros((CONV_HALO, xpx_sc.shape[1]), F32)
        xpb_sc[0:CONV_HALO, :] = jnp.zeros((CONV_HALO, xpb_sc.shape[1]), F32)
        xpc_sc[0:CONV_HALO, :] = jnp.zeros((CONV_HALO, xpc_sc.shape[1]), F32)
        st_sc[...] = jnp.zeros_like(st_sc)

    def conv(src_ref, pad_sc, w_ref, bias_ref):
        pad_sc[CONV_HALO:CONV_HALO + L, :] = src_ref[...].astype(F32)
        acc = bias_ref[...]
        for j in range(SSM_CONV):
            lo = CONV_HALO - (SSM_CONV - 1) + j
            acc = acc + pad_sc[lo:lo + L, :] * w_ref[j:j + 1, :]
        pad_sc[0:CONV_HALO, :] = pad_sc[L:L + CONV_HALO, :]
        return _silu(acc)

    x_all = conv(x_ref, xpx_sc, wx_ref, bx_ref)
    b_all = conv(b_ref, xpb_sc, wb_ref, bb_ref)
    c_all = conv(c_ref, xpc_sc, wc_ref, bc_ref)

    row = lax.broadcasted_iota(jnp.int32, (L, L), 0)
    col = lax.broadcasted_iota(jnp.int32, (L, L), 1)
    causal = row >= col
    triu = jnp.where(row <= col, 1.0, 0.0).astype(BF16)
    lane_head = lax.broadcasted_iota(jnp.int32, (1, GW), 1) // SSM_HEAD_DIM

    for gi in range(SSD_GROUPS_PER_STEP):
        x = x_all[:, gi * GW:(gi + 1) * GW]
        bm = b_all[:, gi * NS:(gi + 1) * NS].astype(BF16)
        cm = c_all[:, gi * NS:(gi + 1) * NS].astype(BF16)

        dt_c = _softplus(dtc_ref[gi] + dtb_ref[gi])
        a_c = -jnp.exp(alog_ref[gi])
        cum_c = _cumsum_lanes(dt_c * a_c, triu)
        last_c = cum_c[:, L - 1:L]
        small = jnp.concatenate(
            [cum_c, jnp.exp(cum_c), jnp.exp(last_c - cum_c) * dt_c,
             jnp.zeros((SMALL_ROWS - 3 * hp, L), F32)], axis=0).T
        cum_r = small[:, 0:hp]
        grow_x = _expand_heads(small[:, hp:2 * hp], lane_head)
        end_x = _expand_heads(small[:, 2 * hp:3 * hp], lane_head)
        last_x = _expand_heads(small[L - 1:L, hp:2 * hp], lane_head)
        src_c = cum_c - jnp.log(dt_c)

        cb = _dot_nt(cm, bm)
        w_parts, x_parts = [], []
        for hg in range(hp):
            seg = cum_r[:, hg:hg + 1] - src_c[hg:hg + 1, :]
            w_parts.append(cb * jnp.exp(jnp.where(causal, seg, -jnp.inf)))
            x_parts.append(jnp.where(lane_head == hg, x, 0.0))
        y = _dot(jnp.concatenate(w_parts, axis=1).astype(BF16),
                 jnp.concatenate(x_parts, axis=0).astype(BF16))

        st = st_sc[gi]
        y = y + _dot(cm, st.astype(BF16)) * grow_x
        st_sc[gi] = last_x * st + _dot_tn(bm, (x * end_x).astype(BF16))
        y = y + x * dskip_ref[gi]

        g = y * _silu(z_ref[:, gi * GW:(gi + 1) * GW].astype(F32))
        g = g * lax.rsqrt(jnp.mean(g * g, axis=-1, keepdims=True) + RMS_EPS)
        o_ref[:, gi * GW:(gi + 1) * GW] = (g * nw_ref[gi]).astype(o_ref.dtype)


def _ssd(proj, dt_cols, conv_w, conv_b, dtb, alog, dskip_x, norm_w, nc):
    s = proj.shape[0]
    n = SSD_GROUPS_PER_STEP
    L, GW, NS = SSM_CHUNK, n * SSM_GROUP_W, n * SSM_STATE
    xoff, boff, coff = 0, SSM_INNER, SSM_INNER + SSM_GROUPS * SSM_STATE
    per_step = lambda a: pl.BlockSpec((n,) + a.shape[1:], lambda g, c: (g, 0, 0))
    return pl.pallas_call(
        _ssd_kernel,
        out_shape=jax.ShapeDtypeStruct((s, SSM_INNER), BF16),
        grid=(SSM_GROUPS // n, nc),
        in_specs=[
            pl.BlockSpec((L, GW), lambda g, c: (c, COL_Z // GW + g)),
            pl.BlockSpec((L, GW), lambda g, c: (c, COL_X // GW + g)),
            pl.BlockSpec((L, NS), lambda g, c: (c, COL_B // NS + g)),
            pl.BlockSpec((L, NS), lambda g, c: (c, COL_C // NS + g)),
            pl.BlockSpec((n, SSM_HPG, L), lambda g, c: (g, 0, c)),
            pl.BlockSpec((SSM_CONV, GW), lambda g, c: (0, xoff // GW + g)),
            pl.BlockSpec((SSM_CONV, NS), lambda g, c: (0, boff // NS + g)),
            pl.BlockSpec((SSM_CONV, NS), lambda g, c: (0, coff // NS + g)),
            pl.BlockSpec((1, GW), lambda g, c: (0, xoff // GW + g)),
            pl.BlockSpec((1, NS), lambda g, c: (0, boff // NS + g)),
            pl.BlockSpec((1, NS), lambda g, c: (0, coff // NS + g)),
            per_step(dtb), per_step(alog), per_step(dskip_x), per_step(norm_w),
        ],
        out_specs=pl.BlockSpec((L, GW), lambda g, c: (c, g)),
        scratch_shapes=[pltpu.VMEM((L + CONV_HALO, GW), F32), pltpu.VMEM((L + CONV_HALO, NS), F32),
                        pltpu.VMEM((L + CONV_HALO, NS), F32),
                        pltpu.VMEM((n, SSM_STATE, SSM_GROUP_W), F32)],
        compiler_params=_params(("parallel", "arbitrary")),
        name="ssd",
    )(proj, proj, proj, proj, dt_cols, conv_w, conv_w, conv_w, conv_b, conv_b, conv_b,
      dtb, alog, dskip_x, norm_w)


def _merge_kernel(h_ref, a_ref, b_ref, ga_ref, gb_ref, pa_ref, pb_ref, wo_ref, gate_ref, gpost_ref,
                  o_ref):
    ya = _dot(a_ref[...], pa_ref[...])
    yb = _dot(b_ref[...], pb_ref[...])
    mix = _sigmoid(ga_ref[...].astype(F32)) * ya + _sigmoid(gb_ref[...].astype(F32)) * yb
    y = _dot(mix.astype(BF16), wo_ref[...])
    o_ref[...] = h_ref[...] + gate_ref[...] * _rms(y, gpost_ref[...])


def _merge(h, attn, ssd, proj, pa, pb, wo, gate, gpost, tm):
    s = h.shape[0]
    vec = pl.BlockSpec((1, D_MODEL), lambda i: (0, 0))
    full = lambda a: pl.BlockSpec(a.shape, lambda i: (0, 0))
    return pl.pallas_call(
        _merge_kernel,
        out_shape=jax.ShapeDtypeStruct((s, D_MODEL), F32),
        grid=(s // tm,),
        in_specs=[pl.BlockSpec((tm, D_MODEL), lambda i: (i, 0)),
                  pl.BlockSpec((tm, ATTN_WIDTH), lambda i: (i, 0)),
                  pl.BlockSpec((tm, SSM_INNER), lambda i: (i, 0)),
                  pl.BlockSpec((tm, D_MODEL), lambda i: (i, COL_GA // D_MODEL)),
                  pl.BlockSpec((tm, D_MODEL), lambda i: (i, COL_GB // D_MODEL)),
                  full(pa), full(pb), full(wo), vec, vec],
        out_specs=pl.BlockSpec((tm, D_MODEL), lambda i: (i, 0)),
        compiler_params=_params(("parallel",)),
        name="merge",
    )(h, attn, ssd, proj, proj, pa, pb, wo, gate, gpost)


def _layer(h, mod, rel_bias, p):
    s = h.shape[0]
    nc = s // SSM_CHUNK
    tm = min(512, s)
    tm_wide = min(1024, s)
    sh1, sc1, g1, shm, scm, gm, sh2, sc2, g2 = [mod[k] for k in range(N_MOD)]
    vec = lambda a: a.reshape(1, -1)

    h = _ffn(h, vec(p["ffn1_norm_pre"]), sh1, sc1, g1, vec(p["ffn1_norm_post"]),
             p["ffn1_w_in"].astype(BF16), p["ffn1_w_out"].astype(BF16), tm=tm_wide)

    w_in = p["w_in_mix"].astype(BF16)
    dt_lo = COL_GA
    w_gates = w_in[:, dt_lo + SSM_HEADS:]
    w_dt = jnp.pad(w_in[:, dt_lo:dt_lo + SSM_HEADS], ((0, 0), (0, DT_PAD - SSM_HEADS)))
    proj, dt_raw = _inproj(h, vec(p["mix_norm_pre"]), shm, scm, w_in, w_gates, w_dt, tm=min(2048, s))

    tab_flat = rel_bias.T.reshape(-1)
    qt, ka, vt = _prep(proj)
    tiles = _bias_tiles(tab_flat)
    attn = _attn(qt, ka, vt, tiles)

    dt_cols = dt_raw[:, :SSM_HEADS].reshape(s, SSM_GROUPS, SSM_HPG).transpose(1, 2, 0)
    per_group = lambda a: a.reshape(SSM_GROUPS, SSM_HPG, 1)
    dskip_x = jnp.repeat(p["d_skip"], SSM_HEAD_DIM).reshape(SSM_GROUPS, 1, SSM_GROUP_W)
    ssd = _ssd(proj, dt_cols, p["conv_w"], vec(p["conv_b"]),
               per_group(p["dt_bias"]), per_group(p["a_log"]),
               dskip_x, p["ssm_norm_w"].reshape(SSM_GROUPS, 1, SSM_GROUP_W), nc)

    h = _merge(h, attn, ssd, proj, p["proj_a"].astype(BF16), p["proj_b"].astype(BF16),
               p["w_out_mix"].astype(BF16), gm, vec(p["mix_norm_post"]), tm=tm)

    h = _ffn(h, vec(p["ffn2_norm_pre"]), sh2, sc2, g2, vec(p["ffn2_norm_post"]),
             p["ffn2_w_in"].astype(BF16), p["ffn2_w_out"].astype(BF16), tm=tm_wide)
    return h


_LAYER_KEYS = ("ffn1_norm_pre", "ffn1_w_in", "ffn1_w_out", "ffn1_norm_post", "mix_norm_pre",
               "w_in_mix", "conv_w", "conv_b", "dt_bias", "a_log", "d_skip", "ssm_norm_w",
               "proj_a", "proj_b", "w_out_mix", "mix_norm_post",
               "ffn2_norm_pre", "ffn2_w_in", "ffn2_w_out", "ffn2_norm_post")


def kernel(x, c, w_ada, b_ada, ffn1_norm_pre, ffn1_w_in, ffn1_w_out, ffn1_norm_post, mix_norm_pre,
           w_in_mix, rel_bias, conv_w, conv_b, dt_bias, a_log, d_skip, ssm_norm_w, proj_a, proj_b,
           w_out_mix, mix_norm_post, ffn2_norm_pre, ffn2_w_in, ffn2_w_out, ffn2_norm_post):
    stacked = dict(ffn1_norm_pre=ffn1_norm_pre, ffn1_w_in=ffn1_w_in, ffn1_w_out=ffn1_w_out,
                   ffn1_norm_post=ffn1_norm_post, mix_norm_pre=mix_norm_pre, w_in_mix=w_in_mix,
                   conv_w=conv_w, conv_b=conv_b, dt_bias=dt_bias, a_log=a_log, d_skip=d_skip,
                   ssm_norm_w=ssm_norm_w, proj_a=proj_a, proj_b=proj_b, w_out_mix=w_out_mix,
                   mix_norm_post=mix_norm_post, ffn2_norm_pre=ffn2_norm_pre, ffn2_w_in=ffn2_w_in,
                   ffn2_w_out=ffn2_w_out, ffn2_norm_post=ffn2_norm_post)
    batch, seq, _ = x.shape
    assert seq % ATTN_TILE == 0 and seq // MOBA_BLOCK <= MAX_BLOCKS and seq % SSM_CHUNK == 0
    depth = w_ada.shape[0]
    outs = []
    for b in range(batch):
        h = x[b]
        for l in range(depth):
            mod = _mod(c[b:b + 1], w_ada[l], b_ada[l])
            h = _layer(h, mod, rel_bias, {k: stacked[k][l] for k in _LAYER_KEYS})
        outs.append(h)
    return outs[0][None] if batch == 1 else jnp.stack(outs)
```

```python
import functools
import math

import jax
import jax.numpy as jnp
from jax import lax
from jax.experimental import pallas as pl
from jax.experimental.pallas import tpu as pltpu

F32 = jnp.float32
BF16 = jnp.bfloat16
HIGHEST = lax.Precision.HIGHEST

D_MODEL = 1024
N_MOD = 9
RMS_EPS = 1e-6
FFN_HIDDEN = 2816
FFN_RES = 0.5
FFN_CHUNK = FFN_HIDDEN // 2

ATTN_HEADS = 8
HEAD_DIM = 128
ATTN_WIDTH = ATTN_HEADS * HEAD_DIM
MOBA_BLOCK = 256
MOBA_TOPK = 3
MAX_BLOCKS = 128
AUG_DIM = HEAD_DIM + MAX_BLOCKS
REL_BUCKETS = 32
REL_MAX_DIST = 128
MASKED = -1e30
LOG2E = math.log2(math.e)
ATTN_TILE = 512
BLOCKS_PER_TILE = ATTN_TILE // MOBA_BLOCK
BF16_SUBLANES = 16
V_ROWS = HEAD_DIM + BF16_SUBLANES
FAR_UNROLL = 4
PREP_TILES_PER_STEP = 4

SSM_INNER = 2048
SSM_HEAD_DIM = 64
SSM_GROUPS = 8
SSM_HEADS = SSM_INNER // SSM_HEAD_DIM
SSM_HPG = SSM_HEADS // SSM_GROUPS
SSM_GROUP_W = SSM_INNER // SSM_GROUPS
SSM_STATE = 128
SSM_CONV = 4
SSM_CHUNK = 256
CONV_HALO = 8
SMALL_ROWS = 128
SSD_GROUPS_PER_STEP = 2

COL_Q = 0
COL_K = COL_Q + ATTN_WIDTH
COL_V = COL_K + ATTN_WIDTH
COL_Z = COL_V + ATTN_WIDTH
COL_X = COL_Z + SSM_INNER
COL_B = COL_X + SSM_INNER
COL_C = COL_B + SSM_GROUPS * SSM_STATE
COL_GA = COL_C + SSM_GROUPS * SSM_STATE
COL_GB = COL_GA + D_MODEL
PROJ_W = COL_GB + D_MODEL
PROJ_TILE = 1024
MAIN_TILES = COL_GA // PROJ_TILE
DT_PAD = 128

VMEM_LIMIT = 56 * 1024 * 1024


def _params(sem):
    return pltpu.CompilerParams(dimension_semantics=sem, vmem_limit_bytes=VMEM_LIMIT)


def _sigmoid(x):
    return 0.5 + 0.5 * jnp.tanh(0.5 * x)


def _silu(x):
    return x * _sigmoid(x)


def _softplus(x):
    return jnp.maximum(x, 0.0) + jnp.log(1.0 + jnp.exp(-jnp.abs(x)))


def _rms(x, g):
    return x * lax.rsqrt(jnp.mean(x * x, axis=-1, keepdims=True) + RMS_EPS) * g


def _dot(a, b, **kw):
    return jnp.dot(a, b, preferred_element_type=F32, **kw)


def _dot_nt(a, b, **kw):
    return lax.dot_general(a, b, (((1,), (1,)), ((), ())), preferred_element_type=F32, **kw)


def _dot_tn(a, b, **kw):
    return lax.dot_general(a, b, (((0,), (0,)), ((), ())), preferred_element_type=F32, **kw)


def _mod_kernel(c_ref, w_ref, b_ref, o_ref):
    cs = _silu(c_ref[...])
    o_ref[...] = _dot(cs, w_ref[...], precision=HIGHEST) + b_ref[...]


def _mod(c, w_ada, b_ada):
    n = w_ada.shape[1]
    tn = 1024
    c8 = jnp.broadcast_to(c, (8, D_MODEL))
    out = pl.pallas_call(
        _mod_kernel,
        out_shape=jax.ShapeDtypeStruct((8, n), F32),
        grid=(n // tn,),
        in_specs=[pl.BlockSpec((8, D_MODEL), lambda j: (0, 0)),
                  pl.BlockSpec((D_MODEL, tn), lambda j: (0, j)),
                  pl.BlockSpec((1, tn), lambda j: (0, j))],
        out_specs=pl.BlockSpec((8, tn), lambda j: (0, j)),
        compiler_params=_params(("arbitrary",)),
        name="mod",
    )(c8, w_ada, b_ada.reshape(1, n))
    return out[0].reshape(N_MOD, 1, D_MODEL)


def _ffn_kernel(h_ref, gpre_ref, sh_ref, sc_ref, gate_ref, gpost_ref, wi_ref, wo_ref, o_ref):
    h = h_ref[...]
    u = (_rms(h, gpre_ref[...]) * (1.0 + sc_ref[...]) + sh_ref[...]).astype(BF16)
    acc = None
    for lo in range(0, FFN_HIDDEN, FFN_CHUNK):
        a = _dot(u, wi_ref[:, lo:lo + FFN_CHUNK])
        b = _dot(u, wi_ref[:, FFN_HIDDEN + lo:FFN_HIDDEN + lo + FFN_CHUNK])
        part = _dot((_silu(a) * b).astype(BF16), wo_ref[lo:lo + FFN_CHUNK, :])
        acc = part if acc is None else acc + part
    o_ref[...] = h + (FFN_RES * gate_ref[...]) * _rms(acc, gpost_ref[...])


def _ffn(h, gpre, sh, sc, gate, gpost, wi, wo, tm):
    s = h.shape[0]
    row = lambda i: (i, 0)
    vec = pl.BlockSpec((1, D_MODEL), lambda i: (0, 0))
    resident = lambda a: pl.BlockSpec(a.shape, lambda i: (0, 0), pipeline_mode=pl.Buffered(1))
    return pl.pallas_call(
        _ffn_kernel,
        out_shape=jax.ShapeDtypeStruct((s, D_MODEL), F32),
        grid=(s // tm,),
        in_specs=[pl.BlockSpec((tm, D_MODEL), row), vec, vec, vec, vec, vec,
                  resident(wi), resident(wo)],
        out_specs=pl.BlockSpec((tm, D_MODEL), row),
        compiler_params=_params(("parallel",)),
        name="ffn",
    )(h, gpre, sh, sc, gate, gpost, wi, wo)


def _inproj_kernel(h_ref, gpre_ref, sh_ref, sc_ref, w_ref, wg_ref, wdt_ref, o_ref, dt_ref, u_sc):
    j = pl.program_id(1)

    @pl.when(j == 0)
    def _():
        u = (_rms(h_ref[...], gpre_ref[...]) * (1.0 + sc_ref[...]) + sh_ref[...]).astype(BF16)
        u_sc[...] = u
        dt_ref[...] = _dot(u, wdt_ref[...])
        o_ref[...] = _dot(u, w_ref[...]).astype(o_ref.dtype)

    @pl.when(jnp.logical_and(j > 0, j < MAIN_TILES))
    def _():
        o_ref[...] = _dot(u_sc[...], w_ref[...]).astype(o_ref.dtype)

    @pl.when(j >= MAIN_TILES)
    def _():
        o_ref[...] = _dot(u_sc[...], wg_ref[...]).astype(o_ref.dtype)


def _inproj(h, gpre, sh, sc, w, wg, wdt, tm):
    s = h.shape[0]
    tn = PROJ_TILE
    vec = pl.BlockSpec((1, D_MODEL), lambda i, j: (0, 0))
    return pl.pallas_call(
        _inproj_kernel,
        out_shape=(jax.ShapeDtypeStruct((s, PROJ_W), BF16),
                   jax.ShapeDtypeStruct((s, DT_PAD), F32)),
        grid=(s // tm, PROJ_W // tn),
        in_specs=[pl.BlockSpec((tm, D_MODEL), lambda i, j: (i, 0)), vec, vec, vec,
                  pl.BlockSpec((D_MODEL, tn), lambda i, j: (0, jnp.minimum(j, MAIN_TILES - 1))),
                  pl.BlockSpec((D_MODEL, tn), lambda i, j: (0, jnp.maximum(j - MAIN_TILES, 0))),
                  pl.BlockSpec((D_MODEL, DT_PAD), lambda i, j: (0, 0))],
        out_specs=(pl.BlockSpec((tm, tn), lambda i, j: (i, j)),
                   pl.BlockSpec((tm, DT_PAD), lambda i, j: (i, 0))),
        scratch_shapes=[pltpu.VMEM((tm, D_MODEL), BF16)],
        compiler_params=_params(("parallel", "arbitrary")),
        name="inproj",
    )(h, gpre, sh, sc, w, wg, wdt)


def _prep_kernel(n_sel, n_tiles, q_ref, k_ref, v_ref, qt_ref, ka_ref, vt_ref, km_sc):
    @pl.when(pl.program_id(1) == 0)
    def _():
        km_sc[...] = jnp.zeros_like(km_sc)

    for u in range(n_tiles):
        t = pl.program_id(1) * n_tiles + u
        rows = slice(u * ATTN_TILE, (u + 1) * ATTN_TILE)
        k = k_ref[rows, :].astype(F32)
        for b in range(BLOCKS_PER_TILE):
            km_sc[pl.ds(t * BLOCKS_PER_TILE + b, 1), :] = jnp.mean(
                k[b * MOBA_BLOCK:(b + 1) * MOBA_BLOCK], axis=0, keepdims=True)

        qt = (q_ref[rows, :].astype(F32) * (HEAD_DIM ** -0.5 * LOG2E)).T
        score = _dot(km_sc[:n_sel, :], qt, precision=HIGHEST)
        blk = lax.broadcasted_iota(jnp.int32, score.shape, 0)
        q_blk = t * BLOCKS_PER_TILE + lax.broadcasted_iota(jnp.int32, score.shape, 1) // MOBA_BLOCK
        s = jnp.where(blk < q_blk, score, -jnp.inf)
        pen = jnp.full(score.shape, MASKED, F32)
        for _ in range(MOBA_TOPK):
            m = jnp.max(s, axis=0, keepdims=True)
            first = jnp.min(jnp.where(s == m, blk, n_sel), axis=0, keepdims=True)
            first = jnp.where(m > -jnp.inf, first, n_sel)
            pick = blk == first
            pen = jnp.where(pick, 0.0, pen)
            s = jnp.where(pick, -jnp.inf, s)
        pen = jnp.where(blk == q_blk, 0.0, pen)
        qt_ref[0, u, :HEAD_DIM, :] = qt.astype(BF16)
        qt_ref[0, u, HEAD_DIM:HEAD_DIM + n_sel, :] = pen.astype(BF16)
        if n_sel < MAX_BLOCKS:
            qt_ref[0, u, HEAD_DIM + n_sel:, :] = jnp.zeros((MAX_BLOCKS - n_sel, ATTN_TILE), BF16)

        lane = lax.broadcasted_iota(jnp.int32, (ATTN_TILE, MAX_BLOCKS), 1)
        k_blk = t * BLOCKS_PER_TILE + lax.broadcasted_iota(jnp.int32, lane.shape, 0) // MOBA_BLOCK
        ka_ref[0, rows, :HEAD_DIM] = k.astype(BF16)
        ka_ref[0, rows, HEAD_DIM:] = jnp.where(lane == k_blk, 1.0, 0.0).astype(BF16)

        ones_row = lax.broadcasted_iota(jnp.int32, (V_ROWS - HEAD_DIM, ATTN_TILE), 0) == 0
        vt_ref[0, u, :HEAD_DIM, :] = v_ref[rows, :].astype(F32).T.astype(BF16)
        vt_ref[0, u, HEAD_DIM:, :] = jnp.where(ones_row, 1.0, 0.0).astype(BF16)


def _prep(proj):
    s = proj.shape[0]
    nt = s // ATTN_TILE
    n = PREP_TILES_PER_STEP if nt % PREP_TILES_PER_STEP == 0 else 1
    blk = lambda col: pl.BlockSpec((n * ATTN_TILE, HEAD_DIM), lambda h, t: (t, col // HEAD_DIM + h))
    n_sel = -(-(s // MOBA_BLOCK) // BF16_SUBLANES) * BF16_SUBLANES
    return pl.pallas_call(
        functools.partial(_prep_kernel, n_sel, n),
        out_shape=(jax.ShapeDtypeStruct((ATTN_HEADS, nt, AUG_DIM, ATTN_TILE), BF16),
                   jax.ShapeDtypeStruct((ATTN_HEADS, s, AUG_DIM), BF16),
                   jax.ShapeDtypeStruct((ATTN_HEADS, nt, V_ROWS, ATTN_TILE), BF16)),
        grid=(ATTN_HEADS, nt // n),
        in_specs=[blk(COL_Q), blk(COL_K), blk(COL_V)],
        out_specs=(pl.BlockSpec((1, n, AUG_DIM, ATTN_TILE), lambda h, t: (h, t, 0, 0)),
                   pl.BlockSpec((1, n * ATTN_TILE, AUG_DIM), lambda h, t: (h, t, 0)),
                   pl.BlockSpec((1, n, V_ROWS, ATTN_TILE), lambda h, t: (h, t, 0, 0))),
        scratch_shapes=[pltpu.VMEM((MAX_BLOCKS, HEAD_DIM), F32)],
        compiler_params=_params(("parallel", "arbitrary")),
        name="prep",
    )(proj, proj, proj)


def _t5_bucket(rel):
    n = jnp.maximum(rel, 0)
    max_exact = REL_BUCKETS // 2
    nf = jnp.maximum(n, 1).astype(F32)
    large = max_exact + (jnp.log(nf / max_exact) / math.log(REL_MAX_DIST / max_exact)
                         * (REL_BUCKETS - max_exact)).astype(jnp.int32)
    large = jnp.minimum(large, REL_BUCKETS - 1)
    return jnp.where(n < max_exact, n, large)


def _bias_kernel(tab_ref, o_ref):
    h = pl.program_id(0)
    shape = (ATTN_TILE, ATTN_TILE)
    far = tab_ref[h * REL_BUCKETS + REL_BUCKETS - 1]
    bucket = _t5_bucket(lax.broadcasted_iota(jnp.int32, (1, ATTN_TILE), 1))
    val = jnp.zeros((1, ATTN_TILE), F32)
    for b in range(REL_BUCKETS):
        val = jnp.where(bucket == b, tab_ref[h * REL_BUCKETS + b], val)
    val = (val - far) * LOG2E
    toeplitz = pltpu.roll(jnp.broadcast_to(val, shape), 0, 1, stride=1, stride_axis=0)
    ki = lax.broadcasted_iota(jnp.int32, shape, 0)
    qi = lax.broadcasted_iota(jnp.int32, shape, 1)
    o_ref[0, 0] = jnp.where(qi >= ki, toeplitz, MASKED)
    o_ref[0, 1] = jnp.where(qi < ki, toeplitz, 0.0)


def _bias_tiles(tab_flat):
    return pl.pallas_call(
        _bias_kernel,
        out_shape=jax.ShapeDtypeStruct((ATTN_HEADS, 2, ATTN_TILE, ATTN_TILE), F32),
        grid=(ATTN_HEADS,),
        in_specs=[pl.BlockSpec(memory_space=pltpu.SMEM)],
        out_specs=pl.BlockSpec((1, 2, ATTN_TILE, ATTN_TILE), lambda h: (h, 0, 0, 0)),
        compiler_params=_params(("parallel",)),
        name="bias",
    )(tab_flat)


def _attn_kernel(qt_ref, ka_ref, vt_ref, t_ref, o_ref, m_sc, acc_sc, s_sc, p_sc, p2_sc):
    t = pl.program_id(1)
    qt = qt_ref[0, 0]

    def scores(j):
        start = pl.multiple_of(j * ATTN_TILE, ATTN_TILE)
        return _dot(ka_ref[0, pl.ds(start, ATTN_TILE), :], qt)

    m_sc[...] = jnp.full(m_sc.shape, 4.0 * MASKED, F32)
    acc_sc[...] = jnp.zeros_like(acc_sc)

    def colmax(s_ref):
        return jnp.max(s_ref[...], axis=0, keepdims=True)

    def accumulate(alpha, terms):
        acc = alpha * acc_sc[...]
        for j, p_ref in terms:
            acc = acc + _dot(vt_ref[0, j], p_ref[...])
        acc_sc[...] = acc

    def consume(s_ref, s_max, j):
        m_old = m_sc[...]
        m_new = jnp.maximum(m_old, s_max)
        m_sc[...] = m_new
        p_sc[...] = jnp.exp2(s_ref[...] - m_new).astype(BF16)
        accumulate(jnp.exp2(m_old - m_new), [(j, p_sc)])

    def consume_near(s_prev, s_diag):
        m_old = m_sc[...]
        m_new = jnp.maximum(m_old, jnp.maximum(colmax(s_prev), colmax(s_diag)))
        m_sc[...] = m_new
        p_sc[...] = jnp.exp2(s_prev[...] - m_new).astype(BF16)
        p2_sc[...] = jnp.exp2(s_diag[...] - m_new).astype(BF16)
        accumulate(jnp.exp2(m_old - m_new), [(t - 1, p_sc), (t, p2_sc)])

    def far_step(j, cur, nxt, max_cur):
        nxt[...] = scores(j + 1)
        consume(cur, max_cur, j)
        return colmax(nxt)

    s_a, s_b = s_sc.at[0], s_sc.at[1]

    @pl.when(t == 0)
    def _():
        s_a[...] = scores(0) + t_ref[0, 0]
        consume(s_a, colmax(s_a), 0)

    @pl.when(t >= 1)
    def _():
        n_far = t - 1
        s_a[...] = scores(0)

        bufs = (s_a, s_b)

        def steps(first, count, max_cur):
            for k in range(count):
                max_cur = far_step(first + k, bufs[k % 2], bufs[(k + 1) % 2], max_cur)
            return max_cur

        max_cur = lax.fori_loop(0, n_far // FAR_UNROLL,
                                lambda i, m: steps(FAR_UNROLL * i, FAR_UNROLL, m), colmax(s_a))
        done = n_far // FAR_UNROLL * FAR_UNROLL

        for rem in range(FAR_UNROLL):
            @pl.when(n_far - done == rem)
            def _(rem=rem):
                steps(done, rem, max_cur)
                prev, free = bufs[rem % 2], bufs[(rem + 1) % 2]
                free[...] = scores(t) + t_ref[0, 0]
                prev[...] = prev[...] + t_ref[0, 1]
                consume_near(prev, free)

    acc = acc_sc[...]
    out = acc[:HEAD_DIM] / acc[HEAD_DIM:HEAD_DIM + 1]
    o_ref[...] = out.T.astype(o_ref.dtype)


def _attn(qt, ka, vt, tiles):
    nt = qt.shape[1]
    s = nt * ATTN_TILE
    return pl.pallas_call(
        _attn_kernel,
        out_shape=jax.ShapeDtypeStruct((s, ATTN_WIDTH), BF16),
        grid=(ATTN_HEADS, nt),
        in_specs=[pl.BlockSpec((1, 1, AUG_DIM, ATTN_TILE), lambda h, t: (h, t, 0, 0)),
                  pl.BlockSpec((1, s, AUG_DIM), lambda h, t: (h, 0, 0)),
                  pl.BlockSpec((1, nt, V_ROWS, ATTN_TILE), lambda h, t: (h, 0, 0, 0)),
                  pl.BlockSpec((1, 2, ATTN_TILE, ATTN_TILE), lambda h, t: (h, 0, 0, 0))],
        out_specs=pl.BlockSpec((ATTN_TILE, HEAD_DIM), lambda h, t: (t, h)),
        scratch_shapes=[pltpu.VMEM((1, ATTN_TILE), F32), pltpu.VMEM((V_ROWS, ATTN_TILE), F32),
                        pltpu.VMEM((2, ATTN_TILE, ATTN_TILE), F32),
                        pltpu.VMEM((ATTN_TILE, ATTN_TILE), BF16), pltpu.VMEM((ATTN_TILE, ATTN_TILE), BF16)],
        compiler_params=_params(("parallel", "arbitrary")),
        name="attn",
    )(qt, ka, vt, tiles)


def _expand_heads(d, lane_head):
    out = d[:, SSM_HPG - 1:SSM_HPG]
    for hg in range(SSM_HPG - 2, -1, -1):
        out = jnp.where(lane_head == hg, d[:, hg:hg + 1], out)
    return out


def _cumsum_lanes(x, triu):
    hi = x.astype(BF16)
    rest = x - hi.astype(F32)
    mid = rest.astype(BF16)
    lo = (rest - mid.astype(F32)).astype(BF16)
    return _dot(hi, triu) + _dot(mid, triu) + _dot(lo, triu)


def _ssd_kernel(z_ref, x_ref, b_ref, c_ref, dtc_ref, wx_ref, wb_ref, wc_ref, bx_ref, bb_ref, bc_ref,
                dtb_ref, alog_ref, dskip_ref, nw_ref, o_ref, xpx_sc, xpb_sc, xpc_sc, st_sc):
    c = pl.program_id(1)
    L, GW, NS, hp = SSM_CHUNK, SSM_GROUP_W, SSM_STATE, SSM_HPG

    @pl.when(c == 0)
    def _():
        xpx_sc[0:CONV_HALO, :] = jnp.zeros((CONV_HALO, xpx_sc.shape[1]), F32)
        xpb_sc[0:CONV_HALO, :] = jnp.zeros((CONV_HALO, xpb_sc.shape[1]), F32)
        xpc_sc[0:CONV_HALO, :] = jnp.zeros((CONV_HALO, xpc_sc.shape[1]), F32)
        st_sc[...] = jnp.zeros_like(st_sc)

    def conv(src_ref, pad_sc, w_ref, bias_ref):
        pad_sc[CONV_HALO:CONV_HALO + L, :] = src_ref[...].astype(F32)
        acc = bias_ref[...]
        for j in range(SSM_CONV):
            lo = CONV_HALO - (SSM_CONV - 1) + j
            acc = acc + pad_sc[lo:lo + L, :] * w_ref[j:j + 1, :]
        pad_sc[0:CONV_HALO, :] = pad_sc[L:L + CONV_HALO, :]
        return _silu(acc)

    x_all = conv(x_ref, xpx_sc, wx_ref, bx_ref)
    b_all = conv(b_ref, xpb_sc, wb_ref, bb_ref)
    c_all = conv(c_ref, xpc_sc, wc_ref, bc_ref)

    row = lax.broadcasted_iota(jnp.int32, (L, L), 0)
    col = lax.broadcasted_iota(jnp.int32, (L, L), 1)
    causal = row >= col
    triu = jnp.where(row <= col, 1.0, 0.0).astype(BF16)
    lane_head = lax.broadcasted_iota(jnp.int32, (1, GW), 1) // SSM_HEAD_DIM

    for gi in range(SSD_GROUPS_PER_STEP):
        x = x_all[:, gi * GW:(gi + 1) * GW]
        bm = b_all[:, gi * NS:(gi + 1) * NS].astype(BF16)
        cm = c_all[:, gi * NS:(gi + 1) * NS].astype(BF16)

        dt_c = _softplus(dtc_ref[gi] + dtb_ref[gi])
        a_c = -jnp.exp(alog_ref[gi])
        cum_c = _cumsum_lanes(dt_c * a_c, triu)
        last_c = cum_c[:, L - 1:L]
        small = jnp.concatenate(
            [cum_c, jnp.exp(cum_c), jnp.exp(last_c - cum_c) * dt_c,
             jnp.zeros((SMALL_ROWS - 3 * hp, L), F32)], axis=0).T
        cum_r = small[:, 0:hp]
        grow_x = _expand_heads(small[:, hp:2 * hp], lane_head)
        end_x = _expand_heads(small[:, 2 * hp:3 * hp], lane_head)
        last_x = _expand_heads(small[L - 1:L, hp:2 * hp], lane_head)
        src_c = cum_c - jnp.log(dt_c)

        cb = _dot_nt(cm, bm)
        w_parts, x_parts = [], []
        for hg in range(hp):
            seg = cum_r[:, hg:hg + 1] - src_c[hg:hg + 1, :]
            w_parts.append(cb * jnp.exp(jnp.where(causal, seg, -jnp.inf)))
            x_parts.append(jnp.where(lane_head == hg, x, 0.0))
        y = _dot(jnp.concatenate(w_parts, axis=1).astype(BF16),
                 jnp.concatenate(x_parts, axis=0).astype(BF16))

        st = st_sc[gi]
        y = y + _dot(cm, st.astype(BF16)) * grow_x
        st_sc[gi] = last_x * st + _dot_tn(bm, (x * end_x).astype(BF16))
        y = y + x * dskip_ref[gi]

        g = y * _silu(z_ref[:, gi * GW:(gi + 1) * GW].astype(F32))
        g = g * lax.rsqrt(jnp.mean(g * g, axis=-1, keepdims=True) + RMS_EPS)
        o_ref[:, gi * GW:(gi + 1) * GW] = (g * nw_ref[gi]).astype(o_ref.dtype)


def _ssd(proj, dt_cols, conv_w, conv_b, dtb, alog, dskip_x, norm_w, nc):
    s = proj.shape[0]
    n = SSD_GROUPS_PER_STEP
    L, GW, NS = SSM_CHUNK, n * SSM_GROUP_W, n * SSM_STATE
    xoff, boff, coff = 0, SSM_INNER, SSM_INNER + SSM_GROUPS * SSM_STATE
    per_step = lambda a: pl.BlockSpec((n,) + a.shape[1:], lambda g, c: (g, 0, 0))
    return pl.pallas_call(
        _ssd_kernel,
        out_shape=jax.ShapeDtypeStruct((s, SSM_INNER), BF16),
        grid=(SSM_GROUPS // n, nc),
        in_specs=[
            pl.BlockSpec((L, GW), lambda g, c: (c, COL_Z // GW + g)),
            pl.BlockSpec((L, GW), lambda g, c: (c, COL_X // GW + g)),
            pl.BlockSpec((L, NS), lambda g, c: (c, COL_B // NS + g)),
            pl.BlockSpec((L, NS), lambda g, c: (c, COL_C // NS + g)),
            pl.BlockSpec((n, SSM_HPG, L), lambda g, c: (g, 0, c)),
            pl.BlockSpec((SSM_CONV, GW), lambda g, c: (0, xoff // GW + g)),
            pl.BlockSpec((SSM_CONV, NS), lambda g, c: (0, boff // NS + g)),
            pl.BlockSpec((SSM_CONV, NS), lambda g, c: (0, coff // NS + g)),
            pl.BlockSpec((1, GW), lambda g, c: (0, xoff // GW + g)),
            pl.BlockSpec((1, NS), lambda g, c: (0, boff // NS + g)),
            pl.BlockSpec((1, NS), lambda g, c: (0, coff // NS + g)),
            per_step(dtb), per_step(alog), per_step(dskip_x), per_step(norm_w),
        ],
        out_specs=pl.BlockSpec((L, GW), lambda g, c: (c, g)),
        scratch_shapes=[pltpu.VMEM((L + CONV_HALO, GW), F32), pltpu.VMEM((L + CONV_HALO, NS), F32),
                        pltpu.VMEM((L + CONV_HALO, NS), F32),
                        pltpu.VMEM((n, SSM_STATE, SSM_GROUP_W), F32)],
        compiler_params=_params(("parallel", "arbitrary")),
        name="ssd",
    )(proj, proj, proj, proj, dt_cols, conv_w, conv_w, conv_w, conv_b, conv_b, conv_b,
      dtb, alog, dskip_x, norm_w)


def _merge_kernel(h_ref, a_ref, b_ref, ga_ref, gb_ref, pa_ref, pb_ref, wo_ref, gate_ref, gpost_ref,
                  o_ref):
    ya = _dot(a_ref[...], pa_ref[...])
    yb = _dot(b_ref[...], pb_ref[...])
    mix = _sigmoid(ga_ref[...].astype(F32)) * ya + _sigmoid(gb_ref[...].astype(F32)) * yb
    y = _dot(mix.astype(BF16), wo_ref[...])
    o_ref[...] = h_ref[...] + gate_ref[...] * _rms(y, gpost_ref[...])


def _merge(h, attn, ssd, proj, pa, pb, wo, gate, gpost, tm):
    s = h.shape[0]
    vec = pl.BlockSpec((1, D_MODEL), lambda i: (0, 0))
    full = lambda a: pl.BlockSpec(a.shape, lambda i: (0, 0))
    return pl.pallas_call(
        _merge_kernel,
        out_shape=jax.ShapeDtypeStruct((s, D_MODEL), F32),
        grid=(s // tm,),
        in_specs=[pl.BlockSpec((tm, D_MODEL), lambda i: (i, 0)),
                  pl.BlockSpec((tm, ATTN_WIDTH), lambda i: (i, 0)),
                  pl.BlockSpec((tm, SSM_INNER), lambda i: (i, 0)),
                  pl.BlockSpec((tm, D_MODEL), lambda i: (i, COL_GA // D_MODEL)),
                  pl.BlockSpec((tm, D_MODEL), lambda i: (i, COL_GB // D_MODEL)),
                  full(pa), full(pb), full(wo), vec, vec],
        out_specs=pl.BlockSpec((tm, D_MODEL), lambda i: (i, 0)),
        compiler_params=_params(("parallel",)),
        name="merge",
    )(h, attn, ssd, proj, proj, pa, pb, wo, gate, gpost)


def _layer(h, mod, rel_bias, p):
    s = h.shape[0]
    nc = s // SSM_CHUNK
    tm = min(512, s)
    tm_wide = min(1024, s)
    sh1, sc1, g1, shm, scm, gm, sh2, sc2, g2 = [mod[k] for k in range(N_MOD)]
    vec = lambda a: a.reshape(1, -1)

    h = _ffn(h, vec(p["ffn1_norm_pre"]), sh1, sc1, g1, vec(p["ffn1_norm_post"]),
             p["ffn1_w_in"].astype(BF16), p["ffn1_w_out"].astype(BF16), tm=tm_wide)

    w_in = p["w_in_mix"].astype(BF16)
    dt_lo = COL_GA
    w_gates = w_in[:, dt_lo + SSM_HEADS:]
    w_dt = jnp.pad(w_in[:, dt_lo:dt_lo + SSM_HEADS], ((0, 0), (0, DT_PAD - SSM_HEADS)))
    proj, dt_raw = _inproj(h, vec(p["mix_norm_pre"]), shm, scm, w_in, w_gates, w_dt, tm=min(2048, s))

    tab_flat = rel_bias.T.reshape(-1)
    qt, ka, vt = _prep(proj)
    tiles = _bias_tiles(tab_flat)
    attn = _attn(qt, ka, vt, tiles)

    dt_cols = dt_raw[:, :SSM_HEADS].reshape(s, SSM_GROUPS, SSM_HPG).transpose(1, 2, 0)
    per_group = lambda a: a.reshape(SSM_GROUPS, SSM_HPG, 1)
    dskip_x = jnp.repeat(p["d_skip"], SSM_HEAD_DIM).reshape(SSM_GROUPS, 1, SSM_GROUP_W)
    ssd = _ssd(proj, dt_cols, p["conv_w"], vec(p["conv_b"]),
               per_group(p["dt_bias"]), per_group(p["a_log"]),
               dskip_x, p["ssm_norm_w"].reshape(SSM_GROUPS, 1, SSM_GROUP_W), nc)

    h = _merge(h, attn, ssd, proj, p["proj_a"].astype(BF16), p["proj_b"].astype(BF16),
               p["w_out_mix"].astype(BF16), gm, vec(p["mix_norm_post"]), tm=tm)

    h = _ffn(h, vec(p["ffn2_norm_pre"]), sh2, sc2, g2, vec(p["ffn2_norm_post"]),
             p["ffn2_w_in"].astype(BF16), p["ffn2_w_out"].astype(BF16), tm=tm_wide)
    return h


_LAYER_KEYS = ("ffn1_norm_pre", "ffn1_w_in", "ffn1_w_out", "ffn1_norm_post", "mix_norm_pre",
               "w_in_mix", "conv_w", "conv_b", "dt_bias", "a_log", "d_skip", "ssm_norm_w",
               "proj_a", "proj_b", "w_out_mix", "mix_norm_post",
               "ffn2_norm_pre", "ffn2_w_in", "ffn2_w_out", "ffn2_norm_post")


def kernel(x, c, w_ada, b_ada, ffn1_norm_pre, ffn1_w_in, ffn1_w_out, ffn1_norm_post, mix_norm_pre,
           w_in_mix, rel_bias, conv_w, conv_b, dt_bias, a_log, d_skip, ssm_norm_w, proj_a, proj_b,
           w_out_mix, mix_norm_post, ffn2_norm_pre, ffn2_w_in, ffn2_w_out, ffn2_norm_post):
    stacked = dict(ffn1_norm_pre=ffn1_norm_pre, ffn1_w_in=ffn1_w_in, ffn1_w_out=ffn1_w_out,
                   ffn1_norm_post=ffn1_norm_post, mix_norm_pre=mix_norm_pre, w_in_mix=w_in_mix,
                   conv_w=conv_w, conv_b=conv_b, dt_bias=dt_bias, a_log=a_log, d_skip=d_skip,
                   ssm_norm_w=ssm_norm_w, proj_a=proj_a, proj_b=proj_b, w_out_mix=w_out_mix,
                   mix_norm_post=mix_norm_post, ffn2_norm_pre=ffn2_norm_pre, ffn2_w_in=ffn2_w_in,
                   ffn2_w_out=ffn2_w_out, ffn2_norm_post=ffn2_norm_post)
    batch, seq, _ = x.shape
    assert seq % ATTN_TILE == 0 and seq // MOBA_BLOCK <= MAX_BLOCKS and seq % SSM_CHUNK == 0
    depth = w_ada.shape[0]
    outs = []
    for b in range(batch):
        h = x[b]
        for l in range(depth):
            mod = _mod(c[b:b + 1], w_ada[l], b_ada[l])
            h = _layer(h, mod, rel_bias, {k: stacked[k][l] for k in _LAYER_KEYS})
        outs.append(h)
    return outs[0][None] if batch == 1 else jnp.stack(outs)
```

```python
import functools
import math

import jax
import jax.numpy as jnp
from jax import lax
from jax.experimental import pallas as pl
from jax.experimental.pallas import tpu as pltpu

F32 = jnp.float32
BF16 = jnp.bfloat16
HIGHEST = lax.Precision.HIGHEST

D_MODEL = 1024
N_MOD = 9
RMS_EPS = 1e-6
FFN_HIDDEN = 2816
FFN_RES = 0.5
FFN_CHUNK = FFN_HIDDEN // 2

ATTN_HEADS = 8
HEAD_DIM = 128
ATTN_WIDTH = ATTN_HEADS * HEAD_DIM
MOBA_BLOCK = 256
MOBA_TOPK = 3
MAX_BLOCKS = 128
AUG_DIM = HEAD_DIM + MAX_BLOCKS
REL_BUCKETS = 32
REL_MAX_DIST = 128
MASKED = -1e30
LOG2E = math.log2(math.e)
ATTN_TILE = 512
BLOCKS_PER_TILE = ATTN_TILE // MOBA_BLOCK
BF16_SUBLANES = 16
V_ROWS = HEAD_DIM + BF16_SUBLANES
ATTN_TILES_PER_STEP = 2
FAR_UNROLL = 4
PREP_TILES_PER_STEP = 4

SSM_INNER = 2048
SSM_HEAD_DIM = 64
SSM_GROUPS = 8
SSM_HEADS = SSM_INNER // SSM_HEAD_DIM
SSM_HPG = SSM_HEADS // SSM_GROUPS
SSM_GROUP_W = SSM_INNER // SSM_GROUPS
SSM_STATE = 128
SSM_CONV = 4
SSM_CHUNK = 256
CONV_HALO = 8
SMALL_ROWS = 128
SSD_GROUPS_PER_STEP = 2

COL_Q = 0
COL_K = COL_Q + ATTN_WIDTH
COL_V = COL_K + ATTN_WIDTH
COL_Z = COL_V + ATTN_WIDTH
COL_X = COL_Z + SSM_INNER
COL_B = COL_X + SSM_INNER
COL_C = COL_B + SSM_GROUPS * SSM_STATE
COL_GA = COL_C + SSM_GROUPS * SSM_STATE
COL_GB = COL_GA + D_MODEL
PROJ_W = COL_GB + D_MODEL
PROJ_TILE = 1024
MAIN_TILES = COL_GA // PROJ_TILE
DT_PAD = 128

VMEM_LIMIT = 56 * 1024 * 1024


def _params(sem):
    return pltpu.CompilerParams(dimension_semantics=sem, vmem_limit_bytes=VMEM_LIMIT)


def _sigmoid(x):
    return 0.5 + 0.5 * jnp.tanh(0.5 * x)


def _silu(x):
    return x * _sigmoid(x)


def _softplus(x):
    return jnp.maximum(x, 0.0) + jnp.log(1.0 + jnp.exp(-jnp.abs(x)))


def _rms(x, g):
    return x * lax.rsqrt(jnp.mean(x * x, axis=-1, keepdims=True) + RMS_EPS) * g


def _dot(a, b, **kw):
    return jnp.dot(a, b, preferred_element_type=F32, **kw)


def _dot_nt(a, b, **kw):
    return lax.dot_general(a, b, (((1,), (1,)), ((), ())), preferred_element_type=F32, **kw)


def _dot_tn(a, b, **kw):
    return lax.dot_general(a, b, (((0,), (0,)), ((), ())), preferred_element_type=F32, **kw)


def _mod_kernel(c_ref, w_ref, b_ref, o_ref):
    cs = _silu(c_ref[...])
    o_ref[...] = _dot(cs, w_ref[...], precision=HIGHEST) + b_ref[...]


def _mod(c, w_ada, b_ada):
    n = w_ada.shape[1]
    tn = 1024
    c8 = jnp.broadcast_to(c, (8, D_MODEL))
    out = pl.pallas_call(
        _mod_kernel,
        out_shape=jax.ShapeDtypeStruct((8, n), F32),
        grid=(n // tn,),
        in_specs=[pl.BlockSpec((8, D_MODEL), lambda j: (0, 0)),
                  pl.BlockSpec((D_MODEL, tn), lambda j: (0, j)),
                  pl.BlockSpec((1, tn), lambda j: (0, j))],
        out_specs=pl.BlockSpec((8, tn), lambda j: (0, j)),
        compiler_params=_params(("arbitrary",)),
        name="mod",
    )(c8, w_ada, b_ada.reshape(1, n))
    return out[0].reshape(N_MOD, 1, D_MODEL)


def _ffn_kernel(h_ref, gpre_ref, sh_ref, sc_ref, gate_ref, gpost_ref, wi_ref, wo_ref, o_ref):
    h = h_ref[...]
    u = (_rms(h, gpre_ref[...]) * (1.0 + sc_ref[...]) + sh_ref[...]).astype(BF16)
    acc = None
    for lo in range(0, FFN_HIDDEN, FFN_CHUNK):
        a = _dot(u, wi_ref[:, lo:lo + FFN_CHUNK])
        b = _dot(u, wi_ref[:, FFN_HIDDEN + lo:FFN_HIDDEN + lo + FFN_CHUNK])
        part = _dot((_silu(a) * b).astype(BF16), wo_ref[lo:lo + FFN_CHUNK, :])
        acc = part if acc is None else acc + part
    o_ref[...] = h + (FFN_RES * gate_ref[...]) * _rms(acc, gpost_ref[...])


def _ffn(h, gpre, sh, sc, gate, gpost, wi, wo, tm):
    s = h.shape[0]
    row = lambda i: (i, 0)
    vec = pl.BlockSpec((1, D_MODEL), lambda i: (0, 0))
    resident = lambda a: pl.BlockSpec(a.shape, lambda i: (0, 0), pipeline_mode=pl.Buffered(1))
    return pl.pallas_call(
        _ffn_kernel,
        out_shape=jax.ShapeDtypeStruct((s, D_MODEL), F32),
        grid=(s // tm,),
        in_specs=[pl.BlockSpec((tm, D_MODEL), row), vec, vec, vec, vec, vec,
                  resident(wi), resident(wo)],
        out_specs=pl.BlockSpec((tm, D_MODEL), row),
        compiler_params=_params(("parallel",)),
        name="ffn",
    )(h, gpre, sh, sc, gate, gpost, wi, wo)


def _inproj_kernel(h_ref, gpre_ref, sh_ref, sc_ref, w_ref, wg_ref, wdt_ref, o_ref, dt_ref, u_sc):
    j = pl.program_id(1)

    @pl.when(j == 0)
    def _():
        u = (_rms(h_ref[...], gpre_ref[...]) * (1.0 + sc_ref[...]) + sh_ref[...]).astype(BF16)
        u_sc[...] = u
        dt_ref[...] = _dot(u, wdt_ref[...])
        o_ref[...] = _dot(u, w_ref[...]).astype(o_ref.dtype)

    @pl.when(jnp.logical_and(j > 0, j < MAIN_TILES))
    def _():
        o_ref[...] = _dot(u_sc[...], w_ref[...]).astype(o_ref.dtype)

    @pl.when(j >= MAIN_TILES)
    def _():
        o_ref[...] = _dot(u_sc[...], wg_ref[...]).astype(o_ref.dtype)


def _inproj(h, gpre, sh, sc, w, wg, wdt, tm):
    s = h.shape[0]
    tn = PROJ_TILE
    vec = pl.BlockSpec((1, D_MODEL), lambda i, j: (0, 0))
    return pl.pallas_call(
        _inproj_kernel,
        out_shape=(jax.ShapeDtypeStruct((s, PROJ_W), BF16),
                   jax.ShapeDtypeStruct((s, DT_PAD), F32)),
        grid=(s // tm, PROJ_W // tn),
        in_specs=[pl.BlockSpec((tm, D_MODEL), lambda i, j: (i, 0)), vec, vec, vec,
                  pl.BlockSpec((D_MODEL, tn), lambda i, j: (0, jnp.minimum(j, MAIN_TILES - 1))),
                  pl.BlockSpec((D_MODEL, tn), lambda i, j: (0, jnp.maximum(j - MAIN_TILES, 0))),
                  pl.BlockSpec((D_MODEL, DT_PAD), lambda i, j: (0, 0))],
        out_specs=(pl.BlockSpec((tm, tn), lambda i, j: (i, j)),
                   pl.BlockSpec((tm, DT_PAD), lambda i, j: (i, 0))),
        scratch_shapes=[pltpu.VMEM((tm, D_MODEL), BF16)],
        compiler_params=_params(("parallel", "arbitrary")),
        name="inproj",
    )(h, gpre, sh, sc, w, wg, wdt)


def _prep_kernel(n_sel, n_tiles, q_ref, k_ref, v_ref, qt_ref, ka_ref, vt_ref, km_sc):
    @pl.when(pl.program_id(1) == 0)
    def _():
        km_sc[...] = jnp.zeros_like(km_sc)

    for u in range(n_tiles):
        t = pl.program_id(1) * n_tiles + u
        rows = slice(u * ATTN_TILE, (u + 1) * ATTN_TILE)
        k = k_ref[rows, :].astype(F32)
        for b in range(BLOCKS_PER_TILE):
            km_sc[pl.ds(t * BLOCKS_PER_TILE + b, 1), :] = jnp.mean(
                k[b * MOBA_BLOCK:(b + 1) * MOBA_BLOCK], axis=0, keepdims=True)

        qt = (q_ref[rows, :].astype(F32) * (HEAD_DIM ** -0.5 * LOG2E)).T
        score = _dot(km_sc[:n_sel, :], qt, precision=HIGHEST)
        blk = lax.broadcasted_iota(jnp.int32, score.shape, 0)
        q_blk = t * BLOCKS_PER_TILE + lax.broadcasted_iota(jnp.int32, score.shape, 1) // MOBA_BLOCK
        s = jnp.where(blk < q_blk, score, -jnp.inf)
        pen = jnp.full(score.shape, MASKED, F32)
        for _ in range(MOBA_TOPK):
            m = jnp.max(s, axis=0, keepdims=True)
            first = jnp.min(jnp.where(s == m, blk, n_sel), axis=0, keepdims=True)
            first = jnp.where(m > -jnp.inf, first, n_sel)
            pick = blk == first
            pen = jnp.where(pick, 0.0, pen)
            s = jnp.where(pick, -jnp.inf, s)
        pen = jnp.where(blk == q_blk, 0.0, pen)
        qt_ref[0, u, :HEAD_DIM, :] = qt.astype(BF16)
        qt_ref[0, u, HEAD_DIM:HEAD_DIM + n_sel, :] = pen.astype(BF16)
        if n_sel < MAX_BLOCKS:
            qt_ref[0, u, HEAD_DIM + n_sel:, :] = jnp.zeros((MAX_BLOCKS - n_sel, ATTN_TILE), BF16)

        lane = lax.broadcasted_iota(jnp.int32, (ATTN_TILE, MAX_BLOCKS), 1)
        k_blk = t * BLOCKS_PER_TILE + lax.broadcasted_iota(jnp.int32, lane.shape, 0) // MOBA_BLOCK
        ka_ref[0, rows, :HEAD_DIM] = k.astype(BF16)
        ka_ref[0, rows, HEAD_DIM:] = jnp.where(lane == k_blk, 1.0, 0.0).astype(BF16)

        ones_row = lax.broadcasted_iota(jnp.int32, (V_ROWS - HEAD_DIM, ATTN_TILE), 0) == 0
        vt_ref[0, u, :HEAD_DIM, :] = v_ref[rows, :].astype(F32).T.astype(BF16)
        vt_ref[0, u, HEAD_DIM:, :] = jnp.where(ones_row, 1.0, 0.0).astype(BF16)


def _prep(proj):
    s = proj.shape[0]
    nt = s // ATTN_TILE
    n = PREP_TILES_PER_STEP if nt % PREP_TILES_PER_STEP == 0 else 1
    blk = lambda col: pl.BlockSpec((n * ATTN_TILE, HEAD_DIM), lambda h, t: (t, col // HEAD_DIM + h))
    n_sel = -(-(s // MOBA_BLOCK) // BF16_SUBLANES) * BF16_SUBLANES
    return pl.pallas_call(
        functools.partial(_prep_kernel, n_sel, n),
        out_shape=(jax.ShapeDtypeStruct((ATTN_HEADS, nt, AUG_DIM, ATTN_TILE), BF16),
                   jax.ShapeDtypeStruct((ATTN_HEADS, s, AUG_DIM), BF16),
                   jax.ShapeDtypeStruct((ATTN_HEADS, nt, V_ROWS, ATTN_TILE), BF16)),
        grid=(ATTN_HEADS, nt // n),
        in_specs=[blk(COL_Q), blk(COL_K), blk(COL_V)],
        out_specs=(pl.BlockSpec((1, n, AUG_DIM, ATTN_TILE), lambda h, t: (h, t, 0, 0)),
                   pl.BlockSpec((1, n * ATTN_TILE, AUG_DIM), lambda h, t: (h, t, 0)),
                   pl.BlockSpec((1, n, V_ROWS, ATTN_TILE), lambda h, t: (h, t, 0, 0))),
        scratch_shapes=[pltpu.VMEM((MAX_BLOCKS, HEAD_DIM), F32)],
        compiler_params=_params(("parallel", "arbitrary")),
        name="prep",
    )(proj, proj, proj)


def _t5_bucket(rel):
    n = jnp.maximum(rel, 0)
    max_exact = REL_BUCKETS // 2
    nf = jnp.maximum(n, 1).astype(F32)
    large = max_exact + (jnp.log(nf / max_exact) / math.log(REL_MAX_DIST / max_exact)
                         * (REL_BUCKETS - max_exact)).astype(jnp.int32)
    large = jnp.minimum(large, REL_BUCKETS - 1)
    return jnp.where(n < max_exact, n, large)


def _bias_kernel(tab_ref, o_ref):
    h = pl.program_id(0)
    shape = (ATTN_TILE, ATTN_TILE)
    far = tab_ref[h * REL_BUCKETS + REL_BUCKETS - 1]
    bucket = _t5_bucket(lax.broadcasted_iota(jnp.int32, (1, ATTN_TILE), 1))
    val = jnp.zeros((1, ATTN_TILE), F32)
    for b in range(REL_BUCKETS):
        val = jnp.where(bucket == b, tab_ref[h * REL_BUCKETS + b], val)
    val = (val - far) * LOG2E
    toeplitz = pltpu.roll(jnp.broadcast_to(val, shape), 0, 1, stride=1, stride_axis=0)
    ki = lax.broadcasted_iota(jnp.int32, shape, 0)
    qi = lax.broadcasted_iota(jnp.int32, shape, 1)
    o_ref[0, 0] = jnp.where(qi >= ki, toeplitz, MASKED)
    o_ref[0, 1] = jnp.where(qi < ki, toeplitz, 0.0)


def _bias_tiles(tab_flat):
    return pl.pallas_call(
        _bias_kernel,
        out_shape=jax.ShapeDtypeStruct((ATTN_HEADS, 2, ATTN_TILE, ATTN_TILE), F32),
        grid=(ATTN_HEADS,),
        in_specs=[pl.BlockSpec(memory_space=pltpu.SMEM)],
        out_specs=pl.BlockSpec((1, 2, ATTN_TILE, ATTN_TILE), lambda h: (h, 0, 0, 0)),
        compiler_params=_params(("parallel",)),
        name="bias",
    )(tab_flat)


def _attn_tile(t, qt, ka_ref, vt_ref, t_ref, m_sc, acc_sc, s_sc, p_sc, p2_sc):
    def scores(j):
        start = pl.multiple_of(j * ATTN_TILE, ATTN_TILE)
        return _dot(ka_ref[0, pl.ds(start, ATTN_TILE), :], qt)

    m_sc[...] = jnp.full(m_sc.shape, 4.0 * MASKED, F32)
    acc_sc[...] = jnp.zeros_like(acc_sc)

    def colmax(s_ref):
        return jnp.max(s_ref[...], axis=0, keepdims=True)

    def accumulate(alpha, terms):
        acc = alpha * acc_sc[...]
        for j, p_ref in terms:
            acc = acc + _dot(vt_ref[0, j], p_ref[...])
        acc_sc[...] = acc

    def consume(s_ref, s_max, j):
        m_old = m_sc[...]
        m_new = jnp.maximum(m_old, s_max)
        m_sc[...] = m_new
        p_sc[...] = jnp.exp2(s_ref[...] - m_new).astype(BF16)
        accumulate(jnp.exp2(m_old - m_new), [(j, p_sc)])

    def consume_near(s_prev, s_diag):
        m_old = m_sc[...]
        m_new = jnp.maximum(m_old, jnp.maximum(colmax(s_prev), colmax(s_diag)))
        m_sc[...] = m_new
        p_sc[...] = jnp.exp2(s_prev[...] - m_new).astype(BF16)
        p2_sc[...] = jnp.exp2(s_diag[...] - m_new).astype(BF16)
        accumulate(jnp.exp2(m_old - m_new), [(t - 1, p_sc), (t, p2_sc)])

    def far_step(j, cur, nxt, max_cur):
        nxt[...] = scores(j + 1)
        consume(cur, max_cur, j)
        return colmax(nxt)

    s_a, s_b = s_sc.at[0], s_sc.at[1]

    @pl.when(t == 0)
    def _():
        s_a[...] = scores(0) + t_ref[0, 0]
        consume(s_a, colmax(s_a), 0)

    @pl.when(t >= 1)
    def _():
        n_far = t - 1
        s_a[...] = scores(0)

        bufs = (s_a, s_b)

        def steps(first, count, max_cur):
            for k in range(count):
                max_cur = far_step(first + k, bufs[k % 2], bufs[(k + 1) % 2], max_cur)
            return max_cur

        max_cur = lax.fori_loop(0, n_far // FAR_UNROLL,
                                lambda i, m: steps(FAR_UNROLL * i, FAR_UNROLL, m), colmax(s_a))
        done = n_far // FAR_UNROLL * FAR_UNROLL

        for rem in range(FAR_UNROLL):
            @pl.when(n_far - done == rem)
            def _(rem=rem):
                steps(done, rem, max_cur)
                prev, free = bufs[rem % 2], bufs[(rem + 1) % 2]
                free[...] = scores(t) + t_ref[0, 0]
                prev[...] = prev[...] + t_ref[0, 1]
                consume_near(prev, free)

    acc = acc_sc[...]
    out = acc[:HEAD_DIM] / acc[HEAD_DIM:HEAD_DIM + 1]
    return out.T


def _attn_kernel(n_tiles, qt_ref, ka_ref, vt_ref, t_ref, o_ref, *scratch):
    for u in range(n_tiles):
        out = _attn_tile(pl.program_id(1) * n_tiles + u, qt_ref[0, u], ka_ref, vt_ref, t_ref, *scratch)
        o_ref[u * ATTN_TILE:(u + 1) * ATTN_TILE, :] = out.astype(o_ref.dtype)


def _attn(qt, ka, vt, tiles):
    nt = qt.shape[1]
    s = nt * ATTN_TILE
    n = ATTN_TILES_PER_STEP if nt % ATTN_TILES_PER_STEP == 0 else 1
    return pl.pallas_call(
        functools.partial(_attn_kernel, n),
        out_shape=jax.ShapeDtypeStruct((s, ATTN_WIDTH), BF16),
        grid=(ATTN_HEADS, nt // n),
        in_specs=[pl.BlockSpec((1, n, AUG_DIM, ATTN_TILE), lambda h, t: (h, t, 0, 0)),
                  pl.BlockSpec((1, s, AUG_DIM), lambda h, t: (h, 0, 0)),
                  pl.BlockSpec((1, nt, V_ROWS, ATTN_TILE), lambda h, t: (h, 0, 0, 0)),
                  pl.BlockSpec((1, 2, ATTN_TILE, ATTN_TILE), lambda h, t: (h, 0, 0, 0))],
        out_specs=pl.BlockSpec((n * ATTN_TILE, HEAD_DIM), lambda h, t: (t, h)),
        scratch_shapes=[pltpu.VMEM((1, ATTN_TILE), F32), pltpu.VMEM((V_ROWS, ATTN_TILE), F32),
                        pltpu.VMEM((2, ATTN_TILE, ATTN_TILE), F32),
                        pltpu.VMEM((ATTN_TILE, ATTN_TILE), BF16), pltpu.VMEM((ATTN_TILE, ATTN_TILE), BF16)],
        compiler_params=_params(("parallel", "arbitrary")),
        name="attn",
    )(qt, ka, vt, tiles)


def _expand_heads(d, lane_head):
    out = d[:, SSM_HPG - 1:SSM_HPG]
    for hg in range(SSM_HPG - 2, -1, -1):
        out = jnp.where(lane_head == hg, d[:, hg:hg + 1], out)
    return out


def _cumsum_lanes(x, triu):
    hi = x.astype(BF16)
    rest = x - hi.astype(F32)
    mid = rest.astype(BF16)
    lo = (rest - mid.astype(F32)).astype(BF16)
    return _dot(hi, triu) + _dot(mid, triu) + _dot(lo, triu)


def _ssd_kernel(z_ref, x_ref, b_ref, c_ref, dtc_ref, wx_ref, wb_ref, wc_ref, bx_ref, bb_ref, bc_ref,
                dtb_ref, alog_ref, dskip_ref, nw_ref, o_ref, xpx_sc, xpb_sc, xpc_sc, st_sc):
    c = pl.program_id(1)
    L, GW, NS, hp = SSM_CHUNK, SSM_GROUP_W, SSM_STATE, SSM_HPG

    @pl.when(c == 0)
    def _():
        xpx_sc[0:CONV_HALO, :] = jnp.zeros((CONV_HALO, xpx_sc.shape[1]), F32)
        xpb_sc[0:CONV_HALO, :] = jnp.zeros((CONV_HALO, xpb_sc.shape[1]), F32)
        xpc_sc[0:CONV_HALO, :] = jnp.zeros((CONV_HALO, xpc_sc.shape[1]), F32)
        st_sc[...] = jnp.zeros_like(st_sc)

    def conv(src_ref, pad_sc, w_ref, bias_ref):
        pad_sc[CONV_HALO:CONV_HALO + L, :] = src_ref[...].astype(F32)
        acc = bias_ref[...]
        for j in range(SSM_CONV):
            lo = CONV_HALO - (SSM_CONV - 1) + j
            acc = acc + pad_sc[lo:lo + L, :] * w_ref[j:j + 1, :]
        pad_sc[0:CONV_HALO, :] = pad_sc[L:L + CONV_HALO, :]
        return _silu(acc)

    x_all = conv(x_ref, xpx_sc, wx_ref, bx_ref)
    b_all = conv(b_ref, xpb_sc, wb_ref, bb_ref)
    c_all = conv(c_ref, xpc_sc, wc_ref, bc_ref)

    row = lax.broadcasted_iota(jnp.int32, (L, L), 0)
    col = lax.broadcasted_iota(jnp.int32, (L, L), 1)
    causal = row >= col
    triu = jnp.where(row <= col, 1.0, 0.0).astype(BF16)
    lane_head = lax.broadcasted_iota(jnp.int32, (1, GW), 1) // SSM_HEAD_DIM

    for gi in range(SSD_GROUPS_PER_STEP):
        x = x_all[:, gi * GW:(gi + 1) * GW]
        bm = b_all[:, gi * NS:(gi + 1) * NS].astype(BF16)
        cm = c_all[:, gi * NS:(gi + 1) * NS].astype(BF16)

        dt_c = _softplus(dtc_ref[gi] + dtb_ref[gi])
        a_c = -jnp.exp(alog_ref[gi])
        cum_c = _cumsum_lanes(dt_c * a_c, triu)
        last_c = cum_c[:, L - 1:L]
        small = jnp.concatenate(
            [cum_c, jnp.exp(cum_c), jnp.exp(last_c - cum_c) * dt_c,
             jnp.zeros((SMALL_ROWS - 3 * hp, L), F32)], axis=0).T
        cum_r = small[:, 0:hp]
        grow_x = _expand_heads(small[:, hp:2 * hp], lane_head)
        end_x = _expand_heads(small[:, 2 * hp:3 * hp], lane_head)
        last_x = _expand_heads(small[L - 1:L, hp:2 * hp], lane_head)
        src_c = cum_c - jnp.log(dt_c)

        cb = _dot_nt(cm, bm)
        w_parts, x_parts = [], []
        for hg in range(hp):
            seg = cum_r[:, hg:hg + 1] - src_c[hg:hg + 1, :]
            w_parts.append(cb * jnp.exp(jnp.where(causal, seg, -jnp.inf)))
            x_parts.append(jnp.where(lane_head == hg, x, 0.0))
        y = _dot(jnp.concatenate(w_parts, axis=1).astype(BF16),
                 jnp.concatenate(x_parts, axis=0).astype(BF16))

        st = st_sc[gi]
        y = y + _dot(cm, st.astype(BF16)) * grow_x
        st_sc[gi] = last_x * st + _dot_tn(bm, (x * end_x).astype(BF16))
        y = y + x * dskip_ref[gi]

        g = y * _silu(z_ref[:, gi * GW:(gi + 1) * GW].astype(F32))
        g = g * lax.rsqrt(jnp.mean(g * g, axis=-1, keepdims=True) + RMS_EPS)
        o_ref[:, gi * GW:(gi + 1) * GW] = (g * nw_ref[gi]).astype(o_ref.dtype)


def _ssd(proj, dt_cols, conv_w, conv_b, dtb, alog, dskip_x, norm_w, nc):
    s = proj.shape[0]
    n = SSD_GROUPS_PER_STEP
    L, GW, NS = SSM_CHUNK, n * SSM_GROUP_W, n * SSM_STATE
    xoff, boff, coff = 0, SSM_INNER, SSM_INNER + SSM_GROUPS * SSM_STATE
    per_step = lambda a: pl.BlockSpec((n,) + a.shape[1:], lambda g, c: (g, 0, 0))
    return pl.pallas_call(
        _ssd_kernel,
        out_shape=jax.ShapeDtypeStruct((s, SSM_INNER), BF16),
        grid=(SSM_GROUPS // n, nc),
        in_specs=[
            pl.BlockSpec((L, GW), lambda g, c: (c, COL_Z // GW + g)),
            pl.BlockSpec((L, GW), lambda g, c: (c, COL_X // GW + g)),
            pl.BlockSpec((L, NS), lambda g, c: (c, COL_B // NS + g)),
            pl.BlockSpec((L, NS), lambda g, c: (c, COL_C // NS + g)),
            pl.BlockSpec((n, SSM_HPG, L), lambda g, c: (g, 0, c)),
            pl.BlockSpec((SSM_CONV, GW), lambda g, c: (0, xoff // GW + g)),
            pl.BlockSpec((SSM_CONV, NS), lambda g, c: (0, boff // NS + g)),
            pl.BlockSpec((SSM_CONV, NS), lambda g, c: (0, coff // NS + g)),
            pl.BlockSpec((1, GW), lambda g, c: (0, xoff // GW + g)),
            pl.BlockSpec((1, NS), lambda g, c: (0, boff // NS + g)),
            pl.BlockSpec((1, NS), lambda g, c: (0, coff // NS + g)),
            per_step(dtb), per_step(alog), per_step(dskip_x), per_step(norm_w),
        ],
        out_specs=pl.BlockSpec((L, GW), lambda g, c: (c, g)),
        scratch_shapes=[pltpu.VMEM((L + CONV_HALO, GW), F32), pltpu.VMEM((L + CONV_HALO, NS), F32),
                        pltpu.VMEM((L + CONV_HALO, NS), F32),
                        pltpu.VMEM((n, SSM_STATE, SSM_GROUP_W), F32)],
        compiler_params=_params(("parallel", "arbitrary")),
        name="ssd",
    )(proj, proj, proj, proj, dt_cols, conv_w, conv_w, conv_w, conv_b, conv_b, conv_b,
      dtb, alog, dskip_x, norm_w)


def _merge_kernel(h_ref, a_ref, b_ref, ga_ref, gb_ref, pa_ref, pb_ref, wo_ref, gate_ref, gpost_ref,
                  o_ref):
    ya = _dot(a_ref[...], pa_ref[...])
    yb = _dot(b_ref[...], pb_ref[...])
    mix = _sigmoid(ga_ref[...].astype(F32)) * ya + _sigmoid(gb_ref[...].astype(F32)) * yb
    y = _dot(mix.astype(BF16), wo_ref[...])
    o_ref[...] = h_ref[...] + gate_ref[...] * _rms(y, gpost_ref[...])


def _merge(h, attn, ssd, proj, pa, pb, wo, gate, gpost, tm):
    s = h.shape[0]
    vec = pl.BlockSpec((1, D_MODEL), lambda i: (0, 0))
    full = lambda a: pl.BlockSpec(a.shape, lambda i: (0, 0))
    return pl.pallas_call(
        _merge_kernel,
        out_shape=jax.ShapeDtypeStruct((s, D_MODEL), F32),
        grid=(s // tm,),
        in_specs=[pl.BlockSpec((tm, D_MODEL), lambda i: (i, 0)),
                  pl.BlockSpec((tm, ATTN_WIDTH), lambda i: (i, 0)),
                  pl.BlockSpec((tm, SSM_INNER), lambda i: (i, 0)),
                  pl.BlockSpec((tm, D_MODEL), lambda i: (i, COL_GA // D_MODEL)),
                  pl.BlockSpec((tm, D_MODEL), lambda i: (i, COL_GB // D_MODEL)),
                  full(pa), full(pb), full(wo), vec, vec],
        out_specs=pl.BlockSpec((tm, D_MODEL), lambda i: (i, 0)),
        compiler_params=_params(("parallel",)),
        name="merge",
    )(h, attn, ssd, proj, proj, pa, pb, wo, gate, gpost)


def _layer(h, mod, rel_bias, p):
    s = h.shape[0]
    nc = s // SSM_CHUNK
    tm = min(512, s)
    tm_wide = min(1024, s)
    sh1, sc1, g1, shm, scm, gm, sh2, sc2, g2 = [mod[k] for k in range(N_MOD)]
    vec = lambda a: a.reshape(1, -1)

    h = _ffn(h, vec(p["ffn1_norm_pre"]), sh1, sc1, g1, vec(p["ffn1_norm_post"]),
             p["ffn1_w_in"].astype(BF16), p["ffn1_w_out"].astype(BF16), tm=tm_wide)

    w_in = p["w_in_mix"].astype(BF16)
    dt_lo = COL_GA
    w_gates = w_in[:, dt_lo + SSM_HEADS:]
    w_dt = jnp.pad(w_in[:, dt_lo:dt_lo + SSM_HEADS], ((0, 0), (0, DT_PAD - SSM_HEADS)))
    proj, dt_raw = _inproj(h, vec(p["mix_norm_pre"]), shm, scm, w_in, w_gates, w_dt, tm=min(2048, s))

    tab_flat = rel_bias.T.reshape(-1)
    qt, ka, vt = _prep(proj)
    tiles = _bias_tiles(tab_flat)
    attn = _attn(qt, ka, vt, tiles)

    dt_cols = dt_raw[:, :SSM_HEADS].reshape(s, SSM_GROUPS, SSM_HPG).transpose(1, 2, 0)
    per_group = lambda a: a.reshape(SSM_GROUPS, SSM_HPG, 1)
    dskip_x = jnp.repeat(p["d_skip"], SSM_HEAD_DIM).reshape(SSM_GROUPS, 1, SSM_GROUP_W)
    ssd = _ssd(proj, dt_cols, p["conv_w"], vec(p["conv_b"]),
               per_group(p["dt_bias"]), per_group(p["a_log"]),
               dskip_x, p["ssm_norm_w"].reshape(SSM_GROUPS, 1, SSM_GROUP_W), nc)

    h = _merge(h, attn, ssd, proj, p["proj_a"].astype(BF16), p["proj_b"].astype(BF16),
               p["w_out_mix"].astype(BF16), gm, vec(p["mix_norm_post"]), tm=tm)

    h = _ffn(h, vec(p["ffn2_norm_pre"]), sh2, sc2, g2, vec(p["ffn2_norm_post"]),
             p["ffn2_w_in"].astype(BF16), p["ffn2_w_out"].astype(BF16), tm=tm_wide)
    return h


_LAYER_KEYS = ("ffn1_norm_pre", "ffn1_w_in", "ffn1_w_out", "ffn1_norm_post", "mix_norm_pre",
               "w_in_mix", "conv_w", "conv_b", "dt_bias", "a_log", "d_skip", "ssm_norm_w",
               "proj_a", "proj_b", "w_out_mix", "mix_norm_post",
               "ffn2_norm_pre", "ffn2_w_in", "ffn2_w_out", "ffn2_norm_post")


def kernel(x, c, w_ada, b_ada, ffn1_norm_pre, ffn1_w_in, ffn1_w_out, ffn1_norm_post, mix_norm_pre,
           w_in_mix, rel_bias, conv_w, conv_b, dt_bias, a_log, d_skip, ssm_norm_w, proj_a, proj_b,
           w_out_mix, mix_norm_post, ffn2_norm_pre, ffn2_w_in, ffn2_w_out, ffn2_norm_post):
    stacked = dict(ffn1_norm_pre=ffn1_norm_pre, ffn1_w_in=ffn1_w_in, ffn1_w_out=ffn1_w_out,
                   ffn1_norm_post=ffn1_norm_post, mix_norm_pre=mix_norm_pre, w_in_mix=w_in_mix,
                   conv_w=conv_w, conv_b=conv_b, dt_bias=dt_bias, a_log=a_log, d_skip=d_skip,
                   ssm_norm_w=ssm_norm_w, proj_a=proj_a, proj_b=proj_b, w_out_mix=w_out_mix,
                   mix_norm_post=mix_norm_post, ffn2_norm_pre=ffn2_norm_pre, ffn2_w_in=ffn2_w_in,
                   ffn2_w_out=ffn2_w_out, ffn2_norm_post=ffn2_norm_post)
    batch, seq, _ = x.shape
    assert seq % ATTN_TILE == 0 and seq // MOBA_BLOCK <= MAX_BLOCKS and seq % SSM_CHUNK == 0
    depth = w_ada.shape[0]
    outs = []
    for b in range(batch):
        h = x[b]
        for l in range(depth):
            mod = _mod(c[b:b + 1], w_ada[l], b_ada[l])
            h = _layer(h, mod, rel_bias, {k: stacked[k][l] for k in _LAYER_KEYS})
        outs.append(h)
    return outs[0][None] if batch == 1 else jnp.stack(outs)
```

```python
import functools
import math

import jax
import jax.numpy as jnp
from jax import lax
from jax.experimental import pallas as pl
from jax.experimental.pallas import tpu as pltpu

F32 = jnp.float32
BF16 = jnp.bfloat16
HIGHEST = lax.Precision.HIGHEST

D_MODEL = 1024
N_MOD = 9
RMS_EPS = 1e-6
FFN_HIDDEN = 2816
FFN_RES = 0.5
FFN_CHUNK = FFN_HIDDEN // 2

ATTN_HEADS = 8
HEAD_DIM = 128
ATTN_WIDTH = ATTN_HEADS * HEAD_DIM
MOBA_BLOCK = 256
MOBA_TOPK = 3
MAX_BLOCKS = 128
AUG_DIM = HEAD_DIM + MAX_BLOCKS
REL_BUCKETS = 32
REL_MAX_DIST = 128
MASKED = -1e30
LOG2E = math.log2(math.e)
ATTN_TILE = 512
BLOCKS_PER_TILE = ATTN_TILE // MOBA_BLOCK
BF16_SUBLANES = 16
V_ROWS = HEAD_DIM + BF16_SUBLANES
ATTN_TILES_PER_STEP = 1
FAR_UNROLL = 4
PREP_TILES_PER_STEP = 4

SSM_INNER = 2048
SSM_HEAD_DIM = 64
SSM_GROUPS = 8
SSM_HEADS = SSM_INNER // SSM_HEAD_DIM
SSM_HPG = SSM_HEADS // SSM_GROUPS
SSM_GROUP_W = SSM_INNER // SSM_GROUPS
SSM_STATE = 128
SSM_CONV = 4
SSM_CHUNK = 256
CONV_HALO = 8
SMALL_ROWS = 128
SSD_GROUPS_PER_STEP = 2

COL_Q = 0
COL_K = COL_Q + ATTN_WIDTH
COL_V = COL_K + ATTN_WIDTH
COL_Z = COL_V + ATTN_WIDTH
COL_X = COL_Z + SSM_INNER
COL_B = COL_X + SSM_INNER
COL_C = COL_B + SSM_GROUPS * SSM_STATE
COL_GA = COL_C + SSM_GROUPS * SSM_STATE
COL_GB = COL_GA + D_MODEL
PROJ_W = COL_GB + D_MODEL
PROJ_TILE = 1024
MAIN_TILES = COL_GA // PROJ_TILE
DT_PAD = 128

VMEM_LIMIT = 56 * 1024 * 1024


def _params(sem):
    return pltpu.CompilerParams(dimension_semantics=sem, vmem_limit_bytes=VMEM_LIMIT)


def _sigmoid(x):
    return 0.5 + 0.5 * jnp.tanh(0.5 * x)


def _silu(x):
    return x * _sigmoid(x)


def _softplus(x):
    return jnp.maximum(x, 0.0) + jnp.log(1.0 + jnp.exp(-jnp.abs(x)))


def _rms(x, g):
    return x * lax.rsqrt(jnp.mean(x * x, axis=-1, keepdims=True) + RMS_EPS) * g


def _dot(a, b, **kw):
    return jnp.dot(a, b, preferred_element_type=F32, **kw)


def _dot_nt(a, b, **kw):
    return lax.dot_general(a, b, (((1,), (1,)), ((), ())), preferred_element_type=F32, **kw)


def _dot_tn(a, b, **kw):
    return lax.dot_general(a, b, (((0,), (0,)), ((), ())), preferred_element_type=F32, **kw)


def _mod_kernel(c_ref, w_ref, b_ref, o_ref):
    cs = _silu(c_ref[...])
    o_ref[...] = _dot(cs, w_ref[...], precision=HIGHEST) + b_ref[...]


def _mod(c, w_ada, b_ada):
    n = w_ada.shape[1]
    tn = 1024
    c8 = jnp.broadcast_to(c, (8, D_MODEL))
    out = pl.pallas_call(
        _mod_kernel,
        out_shape=jax.ShapeDtypeStruct((8, n), F32),
        grid=(n // tn,),
        in_specs=[pl.BlockSpec((8, D_MODEL), lambda j: (0, 0)),
                  pl.BlockSpec((D_MODEL, tn), lambda j: (0, j)),
                  pl.BlockSpec((1, tn), lambda j: (0, j))],
        out_specs=pl.BlockSpec((8, tn), lambda j: (0, j)),
        compiler_params=_params(("arbitrary",)),
        name="mod",
    )(c8, w_ada, b_ada.reshape(1, n))
    return out[0].reshape(N_MOD, 1, D_MODEL)


def _ffn_kernel(h_ref, gpre_ref, sh_ref, sc_ref, gate_ref, gpost_ref, wi_ref, wo_ref, o_ref):
    h = h_ref[...]
    u = (_rms(h, gpre_ref[...]) * (1.0 + sc_ref[...]) + sh_ref[...]).astype(BF16)
    acc = None
    for lo in range(0, FFN_HIDDEN, FFN_CHUNK):
        a = _dot(u, wi_ref[:, lo:lo + FFN_CHUNK])
        b = _dot(u, wi_ref[:, FFN_HIDDEN + lo:FFN_HIDDEN + lo + FFN_CHUNK])
        part = _dot((_silu(a) * b).astype(BF16), wo_ref[lo:lo + FFN_CHUNK, :])
        acc = part if acc is None else acc + part
    o_ref[...] = h + (FFN_RES * gate_ref[...]) * _rms(acc, gpost_ref[...])


def _ffn(h, gpre, sh, sc, gate, gpost, wi, wo, tm):
    s = h.shape[0]
    row = lambda i: (i, 0)
    vec = pl.BlockSpec((1, D_MODEL), lambda i: (0, 0))
    resident = lambda a: pl.BlockSpec(a.shape, lambda i: (0, 0), pipeline_mode=pl.Buffered(1))
    return pl.pallas_call(
        _ffn_kernel,
        out_shape=jax.ShapeDtypeStruct((s, D_MODEL), F32),
        grid=(s // tm,),
        in_specs=[pl.BlockSpec((tm, D_MODEL), row), vec, vec, vec, vec, vec,
                  resident(wi), resident(wo)],
        out_specs=pl.BlockSpec((tm, D_MODEL), row),
        compiler_params=_params(("parallel",)),
        name="ffn",
    )(h, gpre, sh, sc, gate, gpost, wi, wo)


def _inproj_kernel(h_ref, gpre_ref, sh_ref, sc_ref, w_ref, wg_ref, wdt_ref, o_ref, dt_ref, u_sc):
    j = pl.program_id(1)

    @pl.when(j == 0)
    def _():
        u = (_rms(h_ref[...], gpre_ref[...]) * (1.0 + sc_ref[...]) + sh_ref[...]).astype(BF16)
        u_sc[...] = u
        dt_ref[...] = _dot(u, wdt_ref[...])
        o_ref[...] = _dot(u, w_ref[...]).astype(o_ref.dtype)

    @pl.when(jnp.logical_and(j > 0, j < MAIN_TILES))
    def _():
        o_ref[...] = _dot(u_sc[...], w_ref[...]).astype(o_ref.dtype)

    @pl.when(j >= MAIN_TILES)
    def _():
        o_ref[...] = _dot(u_sc[...], wg_ref[...]).astype(o_ref.dtype)


def _inproj(h, gpre, sh, sc, w, wg, wdt, tm):
    s = h.shape[0]
    tn = PROJ_TILE
    vec = pl.BlockSpec((1, D_MODEL), lambda i, j: (0, 0))
    return pl.pallas_call(
        _inproj_kernel,
        out_shape=(jax.ShapeDtypeStruct((s, PROJ_W), BF16),
                   jax.ShapeDtypeStruct((s, DT_PAD), F32)),
        grid=(s // tm, PROJ_W // tn),
        in_specs=[pl.BlockSpec((tm, D_MODEL), lambda i, j: (i, 0)), vec, vec, vec,
                  pl.BlockSpec((D_MODEL, tn), lambda i, j: (0, jnp.minimum(j, MAIN_TILES - 1))),
                  pl.BlockSpec((D_MODEL, tn), lambda i, j: (0, jnp.maximum(j - MAIN_TILES, 0))),
                  pl.BlockSpec((D_MODEL, DT_PAD), lambda i, j: (0, 0))],
        out_specs=(pl.BlockSpec((tm, tn), lambda i, j: (i, j)),
                   pl.BlockSpec((tm, DT_PAD), lambda i, j: (i, 0))),
        scratch_shapes=[pltpu.VMEM((tm, D_MODEL), BF16)],
        compiler_params=_params(("parallel", "arbitrary")),
        name="inproj",
    )(h, gpre, sh, sc, w, wg, wdt)


def _prep_kernel(n_sel, n_tiles, q_ref, k_ref, v_ref, qt_ref, ka_ref, vt_ref, km_sc):
    @pl.when(pl.program_id(1) == 0)
    def _():
        km_sc[...] = jnp.zeros_like(km_sc)

    for u in range(n_tiles):
        t = pl.program_id(1) * n_tiles + u
        rows = slice(u * ATTN_TILE, (u + 1) * ATTN_TILE)
        k = k_ref[rows, :].astype(F32)
        for b in range(BLOCKS_PER_TILE):
            km_sc[pl.ds(t * BLOCKS_PER_TILE + b, 1), :] = jnp.mean(
                k[b * MOBA_BLOCK:(b + 1) * MOBA_BLOCK], axis=0, keepdims=True)

        qt = (q_ref[rows, :].astype(F32) * (HEAD_DIM ** -0.5 * LOG2E)).T
        score = _dot(km_sc[:n_sel, :], qt, precision=HIGHEST)
        blk = lax.broadcasted_iota(jnp.int32, score.shape, 0)
        q_blk = t * BLOCKS_PER_TILE + lax.broadcasted_iota(jnp.int32, score.shape, 1) // MOBA_BLOCK
        s = jnp.where(blk < q_blk, score, -jnp.inf)
        pen = jnp.full(score.shape, MASKED, F32)
        for _ in range(MOBA_TOPK):
            m = jnp.max(s, axis=0, keepdims=True)
            first = jnp.min(jnp.where(s == m, blk, n_sel), axis=0, keepdims=True)
            first = jnp.where(m > -jnp.inf, first, n_sel)
            pick = blk == first
            pen = jnp.where(pick, 0.0, pen)
            s = jnp.where(pick, -jnp.inf, s)
        pen = jnp.where(blk == q_blk, 0.0, pen)
        qt_ref[0, u, :HEAD_DIM, :] = qt.astype(BF16)
        qt_ref[0, u, HEAD_DIM:HEAD_DIM + n_sel, :] = pen.astype(BF16)
        if n_sel < MAX_BLOCKS:
            qt_ref[0, u, HEAD_DIM + n_sel:, :] = jnp.zeros((MAX_BLOCKS - n_sel, ATTN_TILE), BF16)

        lane = lax.broadcasted_iota(jnp.int32, (ATTN_TILE, MAX_BLOCKS), 1)
        k_blk = t * BLOCKS_PER_TILE + lax.broadcasted_iota(jnp.int32, lane.shape, 0) // MOBA_BLOCK
        ka_ref[0, rows, :HEAD_DIM] = k.astype(BF16)
        ka_ref[0, rows, HEAD_DIM:] = jnp.where(lane == k_blk, 1.0, 0.0).astype(BF16)

        ones_row = lax.broadcasted_iota(jnp.int32, (V_ROWS - HEAD_DIM, ATTN_TILE), 0) == 0
        vt_ref[0, u, :HEAD_DIM, :] = v_ref[rows, :].astype(F32).T.astype(BF16)
        vt_ref[0, u, HEAD_DIM:, :] = jnp.where(ones_row, 1.0, 0.0).astype(BF16)


def _prep(proj):
    s = proj.shape[0]
    nt = s // ATTN_TILE
    n = PREP_TILES_PER_STEP if nt % PREP_TILES_PER_STEP == 0 else 1
    blk = lambda col: pl.BlockSpec((n * ATTN_TILE, HEAD_DIM), lambda h, t: (t, col // HEAD_DIM + h))
    n_sel = -(-(s // MOBA_BLOCK) // BF16_SUBLANES) * BF16_SUBLANES
    return pl.pallas_call(
        functools.partial(_prep_kernel, n_sel, n),
        out_shape=(jax.ShapeDtypeStruct((ATTN_HEADS, nt, AUG_DIM, ATTN_TILE), BF16),
                   jax.ShapeDtypeStruct((ATTN_HEADS, s, AUG_DIM), BF16),
                   jax.ShapeDtypeStruct((ATTN_HEADS, nt, V_ROWS, ATTN_TILE), BF16)),
        grid=(ATTN_HEADS, nt // n),
        in_specs=[blk(COL_Q), blk(COL_K), blk(COL_V)],
        out_specs=(pl.BlockSpec((1, n, AUG_DIM, ATTN_TILE), lambda h, t: (h, t, 0, 0)),
                   pl.BlockSpec((1, n * ATTN_TILE, AUG_DIM), lambda h, t: (h, t, 0)),
                   pl.BlockSpec((1, n, V_ROWS, ATTN_TILE), lambda h, t: (h, t, 0, 0))),
        scratch_shapes=[pltpu.VMEM((MAX_BLOCKS, HEAD_DIM), F32)],
        compiler_params=_params(("parallel", "arbitrary")),
        name="prep",
    )(proj, proj, proj)


def _t5_bucket(rel):
    n = jnp.maximum(rel, 0)
    max_exact = REL_BUCKETS // 2
    nf = jnp.maximum(n, 1).astype(F32)
    large = max_exact + (jnp.log(nf / max_exact) / math.log(REL_MAX_DIST / max_exact)
                         * (REL_BUCKETS - max_exact)).astype(jnp.int32)
    large = jnp.minimum(large, REL_BUCKETS - 1)
    return jnp.where(n < max_exact, n, large)


def _bias_kernel(tab_ref, o_ref):
    h = pl.program_id(0)
    shape = (ATTN_TILE, ATTN_TILE)
    far = tab_ref[h * REL_BUCKETS + REL_BUCKETS - 1]
    bucket = _t5_bucket(lax.broadcasted_iota(jnp.int32, (1, ATTN_TILE), 1))
    val = jnp.zeros((1, ATTN_TILE), F32)
    for b in range(REL_BUCKETS):
        val = jnp.where(bucket == b, tab_ref[h * REL_BUCKETS + b], val)
    val = (val - far) * LOG2E
    toeplitz = pltpu.roll(jnp.broadcast_to(val, shape), 0, 1, stride=1, stride_axis=0)
    ki = lax.broadcasted_iota(jnp.int32, shape, 0)
    qi = lax.broadcasted_iota(jnp.int32, shape, 1)
    o_ref[0, 0] = jnp.where(qi >= ki, toeplitz, MASKED)
    o_ref[0, 1] = jnp.where(qi < ki, toeplitz, 0.0)


def _bias_tiles(tab_flat):
    return pl.pallas_call(
        _bias_kernel,
        out_shape=jax.ShapeDtypeStruct((ATTN_HEADS, 2, ATTN_TILE, ATTN_TILE), F32),
        grid=(ATTN_HEADS,),
        in_specs=[pl.BlockSpec(memory_space=pltpu.SMEM)],
        out_specs=pl.BlockSpec((1, 2, ATTN_TILE, ATTN_TILE), lambda h: (h, 0, 0, 0)),
        compiler_params=_params(("parallel",)),
        name="bias",
    )(tab_flat)


def _attn_tile(t, qt, ka_ref, vt_ref, t_ref, m_sc, acc_sc, s_sc, p_sc, p2_sc):
    def scores(j):
        start = pl.multiple_of(j * ATTN_TILE, ATTN_TILE)
        return _dot(ka_ref[0, pl.ds(start, ATTN_TILE), :], qt)

    m_sc[...] = jnp.full(m_sc.shape, 4.0 * MASKED, F32)
    acc_sc[...] = jnp.zeros_like(acc_sc)

    def colmax(s_ref):
        return jnp.max(s_ref[...], axis=0, keepdims=True)

    def accumulate(alpha, terms):
        acc = alpha * acc_sc[...]
        for j, p_ref in terms:
            acc = acc + _dot(vt_ref[0, j], p_ref[...])
        acc_sc[...] = acc

    def consume(s_ref, s_max, j):
        m_old = m_sc[...]
        m_new = jnp.maximum(m_old, s_max)
        m_sc[...] = m_new
        p_sc[...] = jnp.exp2(s_ref[...] - m_new).astype(BF16)
        accumulate(jnp.exp2(m_old - m_new), [(j, p_sc)])

    def consume_near(s_prev, s_diag):
        m_old = m_sc[...]
        m_new = jnp.maximum(m_old, jnp.maximum(colmax(s_prev), colmax(s_diag)))
        m_sc[...] = m_new
        p_sc[...] = jnp.exp2(s_prev[...] - m_new).astype(BF16)
        p2_sc[...] = jnp.exp2(s_diag[...] - m_new).astype(BF16)
        accumulate(jnp.exp2(m_old - m_new), [(t - 1, p_sc), (t, p2_sc)])

    def far_step(j, cur, nxt, max_cur):
        nxt[...] = scores(j + 1)
        consume(cur, max_cur, j)
        return colmax(nxt)

    s_a, s_b = s_sc.at[0], s_sc.at[1]

    @pl.when(t == 0)
    def _():
        s_a[...] = scores(0) + t_ref[0, 0]
        consume(s_a, colmax(s_a), 0)

    @pl.when(t >= 1)
    def _():
        n_far = t - 1
        s_a[...] = scores(0)

        bufs = (s_a, s_b)

        def steps(first, count, max_cur):
            for k in range(count):
                max_cur = far_step(first + k, bufs[k % 2], bufs[(k + 1) % 2], max_cur)
            return max_cur

        max_cur = lax.fori_loop(0, n_far // FAR_UNROLL,
                                lambda i, m: steps(FAR_UNROLL * i, FAR_UNROLL, m), colmax(s_a))
        done = n_far // FAR_UNROLL * FAR_UNROLL

        for rem in range(FAR_UNROLL):
            @pl.when(n_far - done == rem)
            def _(rem=rem):
                steps(done, rem, max_cur)
                prev, free = bufs[rem % 2], bufs[(rem + 1) % 2]
                free[...] = scores(t) + t_ref[0, 0]
                prev[...] = prev[...] + t_ref[0, 1]
                consume_near(prev, free)

    acc = acc_sc[...]
    out = acc[:HEAD_DIM] / acc[HEAD_DIM:HEAD_DIM + 1]
    return out.T


def _attn_kernel(n_tiles, qt_ref, ka_ref, vt_ref, t_ref, o_ref, *scratch):
    for u in range(n_tiles):
        out = _attn_tile(pl.program_id(1) * n_tiles + u, qt_ref[0, u], ka_ref, vt_ref, t_ref, *scratch)
        o_ref[u * ATTN_TILE:(u + 1) * ATTN_TILE, :] = out.astype(o_ref.dtype)


def _attn(qt, ka, vt, tiles):
    nt = qt.shape[1]
    s = nt * ATTN_TILE
    n = ATTN_TILES_PER_STEP if nt % ATTN_TILES_PER_STEP == 0 else 1
    return pl.pallas_call(
        functools.partial(_attn_kernel, n),
        out_shape=jax.ShapeDtypeStruct((s, ATTN_WIDTH), BF16),
        grid=(ATTN_HEADS, nt // n),
        in_specs=[pl.BlockSpec((1, n, AUG_DIM, ATTN_TILE), lambda h, t: (h, t, 0, 0)),
                  pl.BlockSpec((1, s, AUG_DIM), lambda h, t: (h, 0, 0)),
                  pl.BlockSpec((1, nt, V_ROWS, ATTN_TILE), lambda h, t: (h, 0, 0, 0)),
                  pl.BlockSpec((1, 2, ATTN_TILE, ATTN_TILE), lambda h, t: (h, 0, 0, 0))],
        out_specs=pl.BlockSpec((n * ATTN_TILE, HEAD_DIM), lambda h, t: (t, h)),
        scratch_shapes=[pltpu.VMEM((1, ATTN_TILE), F32), pltpu.VMEM((V_ROWS, ATTN_TILE), F32),
                        pltpu.VMEM((2, ATTN_TILE, ATTN_TILE), F32),
                        pltpu.VMEM((ATTN_TILE, ATTN_TILE), BF16), pltpu.VMEM((ATTN_TILE, ATTN_TILE), BF16)],
        compiler_params=_params(("parallel", "arbitrary")),
        name="attn",
    )(qt, ka, vt, tiles)


def _expand_heads(d, lane_head):
    out = d[:, SSM_HPG - 1:SSM_HPG]
    for hg in range(SSM_HPG - 2, -1, -1):
        out = jnp.where(lane_head == hg, d[:, hg:hg + 1], out)
    return out


def _cumsum_lanes(x, triu):
    hi = x.astype(BF16)
    rest = x - hi.astype(F32)
    mid = rest.astype(BF16)
    lo = (rest - mid.astype(F32)).astype(BF16)
    return _dot(hi, triu) + _dot(mid, triu) + _dot(lo, triu)


def _ssd_kernel(z_ref, x_ref, b_ref, c_ref, dtc_ref, wx_ref, wb_ref, wc_ref, bx_ref, bb_ref, bc_ref,
                dtb_ref, alog_ref, dskip_ref, nw_ref, o_ref, xpx_sc, xpb_sc, xpc_sc, st_sc):
    c = pl.program_id(1)
    L, GW, NS, hp = SSM_CHUNK, SSM_GROUP_W, SSM_STATE, SSM_HPG

    @pl.when(c == 0)
    def _():
        xpx_sc[0:CONV_HALO, :] = jnp.zeros((CONV_HALO, xpx_sc.shape[1]), F32)
        xpb_sc[0:CONV_HALO, :] = jnp.zeros((CONV_HALO, xpb_sc.shape[1]), F32)
        xpc_sc[0:CONV_HALO, :] = jnp.zeros((CONV_HALO, xpc_sc.shape[1]), F32)
        st_sc[...] = jnp.zeros_like(st_sc)

    def conv(src_ref, pad_sc, w_ref, bias_ref):
        pad_sc[CONV_HALO:CONV_HALO + L, :] = src_ref[...].astype(F32)
        acc = bias_ref[...]
        for j in range(SSM_CONV):
            lo = CONV_HALO - (SSM_CONV - 1) + j
            acc = acc + pad_sc[lo:lo + L, :] * w_ref[j:j + 1, :]
        pad_sc[0:CONV_HALO, :] = pad_sc[L:L + CONV_HALO, :]
        return _silu(acc)

    x_all = conv(x_ref, xpx_sc, wx_ref, bx_ref)
    b_all = conv(b_ref, xpb_sc, wb_ref, bb_ref)
    c_all = conv(c_ref, xpc_sc, wc_ref, bc_ref)

    row = lax.broadcasted_iota(jnp.int32, (L, L), 0)
    col = lax.broadcasted_iota(jnp.int32, (L, L), 1)
    causal = row >= col
    triu = jnp.where(row <= col, 1.0, 0.0).astype(BF16)
    lane_head = lax.broadcasted_iota(jnp.int32, (1, GW), 1) // SSM_HEAD_DIM

    for gi in range(SSD_GROUPS_PER_STEP):
        x = x_all[:, gi * GW:(gi + 1) * GW]
        bm = b_all[:, gi * NS:(gi + 1) * NS].astype(BF16)
        cm = c_all[:, gi * NS:(gi + 1) * NS].astype(BF16)

        dt_c = _softplus(dtc_ref[gi] + dtb_ref[gi])
        a_c = -jnp.exp(alog_ref[gi])
        cum_c = _cumsum_lanes(dt_c * a_c, triu)
        last_c = cum_c[:, L - 1:L]
        small = jnp.concatenate(
            [cum_c, jnp.exp(cum_c), jnp.exp(last_c - cum_c) * dt_c,
             jnp.zeros((SMALL_ROWS - 3 * hp, L), F32)], axis=0).T
        cum_r = small[:, 0:hp]
        grow_x = _expand_heads(small[:, hp:2 * hp], lane_head)
        end_x = _expand_heads(small[:, 2 * hp:3 * hp], lane_head)
        last_x = _expand_heads(small[L - 1:L, hp:2 * hp], lane_head)
        src_c = cum_c - jnp.log(dt_c)

        cb = _dot_nt(cm, bm)
        w_parts, x_parts = [], []
        for hg in range(hp):
            seg = cum_r[:, hg:hg + 1] - src_c[hg:hg + 1, :]
            w_parts.append(cb * jnp.exp(jnp.where(causal, seg, -jnp.inf)))
            x_parts.append(jnp.where(lane_head == hg, x, 0.0))
        y = _dot(jnp.concatenate(w_parts, axis=1).astype(BF16),
                 jnp.concatenate(x_parts, axis=0).astype(BF16))

        st = st_sc[gi]
        y = y + _dot(cm, st.astype(BF16)) * grow_x
        st_sc[gi] = last_x * st + _dot_tn(bm, (x * end_x).astype(BF16))
        y = y + x * dskip_ref[gi]

        g = y * _silu(z_ref[:, gi * GW:(gi + 1) * GW].astype(F32))
        g = g * lax.rsqrt(jnp.mean(g * g, axis=-1, keepdims=True) + RMS_EPS)
        o_ref[:, gi * GW:(gi + 1) * GW] = (g * nw_ref[gi]).astype(o_ref.dtype)


def _ssd(proj, dt_cols, conv_w, conv_b, dtb, alog, dskip_x, norm_w, nc):
    s = proj.shape[0]
    n = SSD_GROUPS_PER_STEP
    L, GW, NS = SSM_CHUNK, n * SSM_GROUP_W, n * SSM_STATE
    xoff, boff, coff = 0, SSM_INNER, SSM_INNER + SSM_GROUPS * SSM_STATE
    per_step = lambda a: pl.BlockSpec((n,) + a.shape[1:], lambda g, c: (g, 0, 0))
    return pl.pallas_call(
        _ssd_kernel,
        out_shape=jax.ShapeDtypeStruct((s, SSM_INNER), BF16),
        grid=(SSM_GROUPS // n, nc),
        in_specs=[
            pl.BlockSpec((L, GW), lambda g, c: (c, COL_Z // GW + g)),
            pl.BlockSpec((L, GW), lambda g, c: (c, COL_X // GW + g)),
            pl.BlockSpec((L, NS), lambda g, c: (c, COL_B // NS + g)),
            pl.BlockSpec((L, NS), lambda g, c: (c, COL_C // NS + g)),
            pl.BlockSpec((n, SSM_HPG, L), lambda g, c: (g, 0, c)),
            pl.BlockSpec((SSM_CONV, GW), lambda g, c: (0, xoff // GW + g)),
            pl.BlockSpec((SSM_CONV, NS), lambda g, c: (0, boff // NS + g)),
            pl.BlockSpec((SSM_CONV, NS), lambda g, c: (0, coff // NS + g)),
            pl.BlockSpec((1, GW), lambda g, c: (0, xoff // GW + g)),
            pl.BlockSpec((1, NS), lambda g, c: (0, boff // NS + g)),
            pl.BlockSpec((1, NS), lambda g, c: (0, coff // NS + g)),
            per_step(dtb), per_step(alog), per_step(dskip_x), per_step(norm_w),
        ],
        out_specs=pl.BlockSpec((L, GW), lambda g, c: (c, g)),
        scratch_shapes=[pltpu.VMEM((L + CONV_HALO, GW), F32), pltpu.VMEM((L + CONV_HALO, NS), F32),
                        pltpu.VMEM((L + CONV_HALO, NS), F32),
                        pltpu.VMEM((n, SSM_STATE, SSM_GROUP_W), F32)],
        compiler_params=_params(("parallel", "arbitrary")),
        name="ssd",
    )(proj, proj, proj, proj, dt_cols, conv_w, conv_w, conv_w, conv_b, conv_b, conv_b,
      dtb, alog, dskip_x, norm_w)


def _merge_kernel(h_ref, a_ref, b_ref, ga_ref, gb_ref, pa_ref, pb_ref, wo_ref, gate_ref, gpost_ref,
                  o_ref):
    ya = _dot(a_ref[...], pa_ref[...])
    yb = _dot(b_ref[...], pb_ref[...])
    mix = _sigmoid(ga_ref[...].astype(F32)) * ya + _sigmoid(gb_ref[...].astype(F32)) * yb
    y = _dot(mix.astype(BF16), wo_ref[...])
    o_ref[...] = h_ref[...] + gate_ref[...] * _rms(y, gpost_ref[...])


def _merge(h, attn, ssd, proj, pa, pb, wo, gate, gpost, tm):
    s = h.shape[0]
    vec = pl.BlockSpec((1, D_MODEL), lambda i: (0, 0))
    full = lambda a: pl.BlockSpec(a.shape, lambda i: (0, 0))
    return pl.pallas_call(
        _merge_kernel,
        out_shape=jax.ShapeDtypeStruct((s, D_MODEL), F32),
        grid=(s // tm,),
        in_specs=[pl.BlockSpec((tm, D_MODEL), lambda i: (i, 0)),
                  pl.BlockSpec((tm, ATTN_WIDTH), lambda i: (i, 0)),
                  pl.BlockSpec((tm, SSM_INNER), lambda i: (i, 0)),
                  pl.BlockSpec((tm, D_MODEL), lambda i: (i, COL_GA // D_MODEL)),
                  pl.BlockSpec((tm, D_MODEL), lambda i: (i, COL_GB // D_MODEL)),
                  full(pa), full(pb), full(wo), vec, vec],
        out_specs=pl.BlockSpec((tm, D_MODEL), lambda i: (i, 0)),
        compiler_params=_params(("parallel",)),
        name="merge",
    )(h, attn, ssd, proj, proj, pa, pb, wo, gate, gpost)


def _layer(h, mod, rel_bias, p):
    s = h.shape[0]
    nc = s // SSM_CHUNK
    tm = min(512, s)
    tm_wide = min(1024, s)
    sh1, sc1, g1, shm, scm, gm, sh2, sc2, g2 = [mod[k] for k in range(N_MOD)]
    vec = lambda a: a.reshape(1, -1)

    h = _ffn(h, vec(p["ffn1_norm_pre"]), sh1, sc1, g1, vec(p["ffn1_norm_post"]),
             p["ffn1_w_in"].astype(BF16), p["ffn1_w_out"].astype(BF16), tm=tm_wide)

    w_in = p["w_in_mix"].astype(BF16)
    dt_lo = COL_GA
    w_gates = w_in[:, dt_lo + SSM_HEADS:]
    w_dt = jnp.pad(w_in[:, dt_lo:dt_lo + SSM_HEADS], ((0, 0), (0, DT_PAD - SSM_HEADS)))
    proj, dt_raw = _inproj(h, vec(p["mix_norm_pre"]), shm, scm, w_in, w_gates, w_dt, tm=min(2048, s))

    tab_flat = rel_bias.T.reshape(-1)
    qt, ka, vt = _prep(proj)
    tiles = _bias_tiles(tab_flat)
    attn = _attn(qt, ka, vt, tiles)

    dt_cols = dt_raw[:, :SSM_HEADS].reshape(s, SSM_GROUPS, SSM_HPG).transpose(1, 2, 0)
    per_group = lambda a: a.reshape(SSM_GROUPS, SSM_HPG, 1)
    dskip_x = jnp.repeat(p["d_skip"], SSM_HEAD_DIM).reshape(SSM_GROUPS, 1, SSM_GROUP_W)
    ssd = _ssd(proj, dt_cols, p["conv_w"], vec(p["conv_b"]),
               per_group(p["dt_bias"]), per_group(p["a_log"]),
               dskip_x, p["ssm_norm_w"].reshape(SSM_GROUPS, 1, SSM_GROUP_W), nc)

    h = _merge(h, attn, ssd, proj, p["proj_a"].astype(BF16), p["proj_b"].astype(BF16),
               p["w_out_mix"].astype(BF16), gm, vec(p["mix_norm_post"]), tm=tm)

    h = _ffn(h, vec(p["ffn2_norm_pre"]), sh2, sc2, g2, vec(p["ffn2_norm_post"]),
             p["ffn2_w_in"].astype(BF16), p["ffn2_w_out"].astype(BF16), tm=tm_wide)
    return h


_LAYER_KEYS = ("ffn1_norm_pre", "ffn1_w_in", "ffn1_w_out", "ffn1_norm_post", "mix_norm_pre",
               "w_in_mix", "conv_w", "conv_b", "dt_bias", "a_log", "d_skip", "ssm_norm_w",
               "proj_a", "proj_b", "w_out_mix", "mix_norm_post",
               "ffn2_norm_pre", "ffn2_w_in", "ffn2_w_out", "ffn2_norm_post")


def kernel(x, c, w_ada, b_ada, ffn1_norm_pre, ffn1_w_in, ffn1_w_out, ffn1_norm_post, mix_norm_pre,
           w_in_mix, rel_bias, conv_w, conv_b, dt_bias, a_log, d_skip, ssm_norm_w, proj_a, proj_b,
           w_out_mix, mix_norm_post, ffn2_norm_pre, ffn2_w_in, ffn2_w_out, ffn2_norm_post):
    stacked = dict(ffn1_norm_pre=ffn1_norm_pre, ffn1_w_in=ffn1_w_in, ffn1_w_out=ffn1_w_out,
                   ffn1_norm_post=ffn1_norm_post, mix_norm_pre=mix_norm_pre, w_in_mix=w_in_mix,
                   conv_w=conv_w, conv_b=conv_b, dt_bias=dt_bias, a_log=a_log, d_skip=d_skip,
                   ssm_norm_w=ssm_norm_w, proj_a=proj_a, proj_b=proj_b, w_out_mix=w_out_mix,
                   mix_norm_post=mix_norm_post, ffn2_norm_pre=ffn2_norm_pre, ffn2_w_in=ffn2_w_in,
                   ffn2_w_out=ffn2_w_out, ffn2_norm_post=ffn2_norm_post)
    batch, seq, _ = x.shape
    assert seq % ATTN_TILE == 0 and seq // MOBA_BLOCK <= MAX_BLOCKS and seq % SSM_CHUNK == 0
    depth = w_ada.shape[0]
    outs = []
    for b in range(batch):
        h = x[b]
        for l in range(depth):
            mod = _mod(c[b:b + 1], w_ada[l], b_ada[l])
            h = _layer(h, mod, rel_bias, {k: stacked[k][l] for k in _LAYER_KEYS})
        outs.append(h)
    return outs[0][None] if batch == 1 else jnp.stack(outs)
```

```python
import functools
import math

import jax
import jax.numpy as jnp
from jax import lax
from jax.experimental import pallas as pl
from jax.experimental.pallas import tpu as pltpu

F32 = jnp.float32
BF16 = jnp.bfloat16
HIGHEST = lax.Precision.HIGHEST

D_MODEL = 1024
N_MOD = 9
RMS_EPS = 1e-6
FFN_HIDDEN = 2816
FFN_RES = 0.5
FFN_CHUNK = FFN_HIDDEN // 2

ATTN_HEADS = 8
HEAD_DIM = 128
ATTN_WIDTH = ATTN_HEADS * HEAD_DIM
MOBA_BLOCK = 256
MOBA_TOPK = 3
MAX_BLOCKS = 128
AUG_DIM = HEAD_DIM + MAX_BLOCKS
REL_BUCKETS = 32
REL_MAX_DIST = 128
MASKED = -1e30
LOG2E = math.log2(math.e)
ATTN_TILE = 512
BLOCKS_PER_TILE = ATTN_TILE // MOBA_BLOCK
BF16_SUBLANES = 16
V_ROWS = HEAD_DIM + BF16_SUBLANES
ATTN_TILES_PER_STEP = 1
FAR_UNROLL = 4
PREP_TILES_PER_STEP = 4

SSM_INNER = 2048
SSM_HEAD_DIM = 64
SSM_GROUPS = 8
SSM_HEADS = SSM_INNER // SSM_HEAD_DIM
SSM_HPG = SSM_HEADS // SSM_GROUPS
SSM_GROUP_W = SSM_INNER // SSM_GROUPS
SSM_STATE = 128
SSM_CONV = 4
SSM_CHUNK = 256
CONV_HALO = 8
SMALL_ROWS = 128
SSD_GROUPS_PER_STEP = 2

COL_Q = 0
COL_K = COL_Q + ATTN_WIDTH
COL_V = COL_K + ATTN_WIDTH
COL_Z = COL_V + ATTN_WIDTH
COL_X = COL_Z + SSM_INNER
COL_B = COL_X + SSM_INNER
COL_C = COL_B + SSM_GROUPS * SSM_STATE
COL_GA = COL_C + SSM_GROUPS * SSM_STATE
COL_GB = COL_GA + D_MODEL
PROJ_W = COL_GB + D_MODEL
PROJ_TILE = 1024
MAIN_TILES = COL_GA // PROJ_TILE
DT_PAD = 128

VMEM_LIMIT = 56 * 1024 * 1024


def _params(sem):
    return pltpu.CompilerParams(dimension_semantics=sem, vmem_limit_bytes=VMEM_LIMIT)


def _sigmoid(x):
    return 0.5 + 0.5 * jnp.tanh(0.5 * x)


def _silu(x):
    return x * _sigmoid(x)


def _softplus(x):
    return jnp.maximum(x, 0.0) + jnp.log(1.0 + jnp.exp(-jnp.abs(x)))


def _rms(x, g):
    return x * lax.rsqrt(jnp.mean(x * x, axis=-1, keepdims=True) + RMS_EPS) * g


def _dot(a, b, **kw):
    return jnp.dot(a, b, preferred_element_type=F32, **kw)


def _dot_nt(a, b, **kw):
    return lax.dot_general(a, b, (((1,), (1,)), ((), ())), preferred_element_type=F32, **kw)


def _dot_tn(a, b, **kw):
    return lax.dot_general(a, b, (((0,), (0,)), ((), ())), preferred_element_type=F32, **kw)


def _mod_kernel(c_ref, w_ref, b_ref, o_ref):
    cs = _silu(c_ref[...])
    o_ref[...] = _dot(cs, w_ref[...], precision=HIGHEST) + b_ref[...]


def _mod(c, w_ada, b_ada):
    n = w_ada.shape[1]
    tn = 1024
    c8 = jnp.broadcast_to(c, (8, D_MODEL))
    out = pl.pallas_call(
        _mod_kernel,
        out_shape=jax.ShapeDtypeStruct((8, n), F32),
        grid=(n // tn,),
        in_specs=[pl.BlockSpec((8, D_MODEL), lambda j: (0, 0)),
                  pl.BlockSpec((D_MODEL, tn), lambda j: (0, j)),
                  pl.BlockSpec((1, tn), lambda j: (0, j))],
        out_specs=pl.BlockSpec((8, tn), lambda j: (0, j)),
        compiler_params=_params(("arbitrary",)),
        name="mod",
    )(c8, w_ada, b_ada.reshape(1, n))
    return out[0].reshape(N_MOD, 1, D_MODEL)


def _ffn_kernel(h_ref, gpre_ref, sh_ref, sc_ref, gate_ref, gpost_ref, wi_ref, wo_ref, o_ref):
    h = h_ref[...]
    u = (_rms(h, gpre_ref[...]) * (1.0 + sc_ref[...]) + sh_ref[...]).astype(BF16)
    acc = None
    for lo in range(0, FFN_HIDDEN, FFN_CHUNK):
        a = _dot(u, wi_ref[:, lo:lo + FFN_CHUNK])
        b = _dot(u, wi_ref[:, FFN_HIDDEN + lo:FFN_HIDDEN + lo + FFN_CHUNK])
        part = _dot((_silu(a) * b).astype(BF16), wo_ref[lo:lo + FFN_CHUNK, :])
        acc = part if acc is None else acc + part
    o_ref[...] = h + (FFN_RES * gate_ref[...]) * _rms(acc, gpost_ref[...])


def _ffn(h, gpre, sh, sc, gate, gpost, wi, wo, tm):
    s = h.shape[0]
    row = lambda i: (i, 0)
    vec = pl.BlockSpec((1, D_MODEL), lambda i: (0, 0))
    resident = lambda a: pl.BlockSpec(a.shape, lambda i: (0, 0), pipeline_mode=pl.Buffered(1))
    return pl.pallas_call(
        _ffn_kernel,
        out_shape=jax.ShapeDtypeStruct((s, D_MODEL), F32),
        grid=(s // tm,),
        in_specs=[pl.BlockSpec((tm, D_MODEL), row), vec, vec, vec, vec, vec,
                  resident(wi), resident(wo)],
        out_specs=pl.BlockSpec((tm, D_MODEL), row),
        compiler_params=_params(("parallel",)),
        name="ffn",
    )(h, gpre, sh, sc, gate, gpost, wi, wo)


def _inproj_kernel(h_ref, gpre_ref, sh_ref, sc_ref, w_ref, wg_ref, wdt_ref, o_ref, dt_ref, u_sc):
    j = pl.program_id(1)

    @pl.when(j == 0)
    def _():
        u = (_rms(h_ref[...], gpre_ref[...]) * (1.0 + sc_ref[...]) + sh_ref[...]).astype(BF16)
        u_sc[...] = u
        dt_ref[...] = _dot(u, wdt_ref[...])
        o_ref[...] = _dot(u, w_ref[...]).astype(o_ref.dtype)

    @pl.when(jnp.logical_and(j > 0, j < MAIN_TILES))
    def _():
        o_ref[...] = _dot(u_sc[...], w_ref[...]).astype(o_ref.dtype)

    @pl.when(j >= MAIN_TILES)
    def _():
        o_ref[...] = _dot(u_sc[...], wg_ref[...]).astype(o_ref.dtype)


def _inproj(h, gpre, sh, sc, w, wg, wdt, tm):
    s = h.shape[0]
    tn = PROJ_TILE
    vec = pl.BlockSpec((1, D_MODEL), lambda i, j: (0, 0))
    return pl.pallas_call(
        _inproj_kernel,
        out_shape=(jax.ShapeDtypeStruct((s, PROJ_W), BF16),
                   jax.ShapeDtypeStruct((s, DT_PAD), F32)),
        grid=(s // tm, PROJ_W // tn),
        in_specs=[pl.BlockSpec((tm, D_MODEL), lambda i, j: (i, 0)), vec, vec, vec,
                  pl.BlockSpec((D_MODEL, tn), lambda i, j: (0, jnp.minimum(j, MAIN_TILES - 1))),
                  pl.BlockSpec((D_MODEL, tn), lambda i, j: (0, jnp.maximum(j - MAIN_TILES, 0))),
                  pl.BlockSpec((D_MODEL, DT_PAD), lambda i, j: (0, 0))],
        out_specs=(pl.BlockSpec((tm, tn), lambda i, j: (i, j)),
                   pl.BlockSpec((tm, DT_PAD), lambda i, j: (i, 0))),
        scratch_shapes=[pltpu.VMEM((tm, D_MODEL), BF16)],
        compiler_params=_params(("parallel", "arbitrary")),
        name="inproj",
    )(h, gpre, sh, sc, w, wg, wdt)


def _prep_kernel(n_sel, n_tiles, q_ref, k_ref, v_ref, qt_ref, ka_ref, vt_ref, km_sc):
    @pl.when(pl.program_id(1) == 0)
    def _():
        km_sc[...] = jnp.zeros_like(km_sc)

    for u in range(n_tiles):
        t = pl.program_id(1) * n_tiles + u
        rows = slice(u * ATTN_TILE, (u + 1) * ATTN_TILE)
        k = k_ref[rows, :].astype(F32)
        for b in range(BLOCKS_PER_TILE):
            km_sc[pl.ds(t * BLOCKS_PER_TILE + b, 1), :] = jnp.mean(
                k[b * MOBA_BLOCK:(b + 1) * MOBA_BLOCK], axis=0, keepdims=True)

        qt = (q_ref[rows, :].astype(F32) * (HEAD_DIM ** -0.5 * LOG2E)).T
        score = _dot(km_sc[:n_sel, :], qt, precision=HIGHEST)
        blk = lax.broadcasted_iota(jnp.int32, score.shape, 0)
        q_blk = t * BLOCKS_PER_TILE + lax.broadcasted_iota(jnp.int32, score.shape, 1) // MOBA_BLOCK
        s = jnp.where(blk < q_blk, score, -jnp.inf)
        pen = jnp.full(score.shape, MASKED, F32)
        for _ in range(MOBA_TOPK):
            m = jnp.max(s, axis=0, keepdims=True)
            first = jnp.min(jnp.where(s == m, blk, n_sel), axis=0, keepdims=True)
            first = jnp.where(m > -jnp.inf, first, n_sel)
            pick = blk == first
            pen = jnp.where(pick, 0.0, pen)
            s = jnp.where(pick, -jnp.inf, s)
        pen = jnp.where(blk == q_blk, 0.0, pen)
        qt_ref[0, u, :HEAD_DIM, :] = qt.astype(BF16)
        qt_ref[0, u, HEAD_DIM:HEAD_DIM + n_sel, :] = pen.astype(BF16)
        if n_sel < MAX_BLOCKS:
            qt_ref[0, u, HEAD_DIM + n_sel:, :] = jnp.zeros((MAX_BLOCKS - n_sel, ATTN_TILE), BF16)

        lane = lax.broadcasted_iota(jnp.int32, (ATTN_TILE, MAX_BLOCKS), 1)
        k_blk = t * BLOCKS_PER_TILE + lax.broadcasted_iota(jnp.int32, lane.shape, 0) // MOBA_BLOCK
        ka_ref[0, rows, :HEAD_DIM] = k.astype(BF16)
        ka_ref[0, rows, HEAD_DIM:] = jnp.where(lane == k_blk, 1.0, 0.0).astype(BF16)

        ones_row = lax.broadcasted_iota(jnp.int32, (V_ROWS - HEAD_DIM, ATTN_TILE), 0) == 0
        vt_ref[0, u, :HEAD_DIM, :] = v_ref[rows, :].astype(F32).T.astype(BF16)
        vt_ref[0, u, HEAD_DIM:, :] = jnp.where(ones_row, 1.0, 0.0).astype(BF16)


def _prep(proj):
    s = proj.shape[0]
    nt = s // ATTN_TILE
    n = PREP_TILES_PER_STEP if nt % PREP_TILES_PER_STEP == 0 else 1
    blk = lambda col: pl.BlockSpec((n * ATTN_TILE, HEAD_DIM), lambda h, t: (t, col // HEAD_DIM + h))
    n_sel = -(-(s // MOBA_BLOCK) // BF16_SUBLANES) * BF16_SUBLANES
    return pl.pallas_call(
        functools.partial(_prep_kernel, n_sel, n),
        out_shape=(jax.ShapeDtypeStruct((ATTN_HEADS, nt, AUG_DIM, ATTN_TILE), BF16),
                   jax.ShapeDtypeStruct((ATTN_HEADS, s, AUG_DIM), BF16),
                   jax.ShapeDtypeStruct((ATTN_HEADS, nt, V_ROWS, ATTN_TILE), BF16)),
        grid=(ATTN_HEADS, nt // n),
        in_specs=[blk(COL_Q), blk(COL_K), blk(COL_V)],
        out_specs=(pl.BlockSpec((1, n, AUG_DIM, ATTN_TILE), lambda h, t: (h, t, 0, 0)),
                   pl.BlockSpec((1, n * ATTN_TILE, AUG_DIM), lambda h, t: (h, t, 0)),
                   pl.BlockSpec((1, n, V_ROWS, ATTN_TILE), lambda h, t: (h, t, 0, 0))),
        scratch_shapes=[pltpu.VMEM((MAX_BLOCKS, HEAD_DIM), F32)],
        compiler_params=_params(("parallel", "arbitrary")),
        name="prep",
    )(proj, proj, proj)


def _t5_bucket(rel):
    n = jnp.maximum(rel, 0)
    max_exact = REL_BUCKETS // 2
    nf = jnp.maximum(n, 1).astype(F32)
    large = max_exact + (jnp.log(nf / max_exact) / math.log(REL_MAX_DIST / max_exact)
                         * (REL_BUCKETS - max_exact)).astype(jnp.int32)
    large = jnp.minimum(large, REL_BUCKETS - 1)
    return jnp.where(n < max_exact, n, large)


def _bias_kernel(tab_ref, o_ref):
    h = pl.program_id(0)
    shape = (ATTN_TILE, ATTN_TILE)
    far = tab_ref[h * REL_BUCKETS + REL_BUCKETS - 1]
    bucket = _t5_bucket(lax.broadcasted_iota(jnp.int32, (1, ATTN_TILE), 1))
    val = jnp.zeros((1, ATTN_TILE), F32)
    for b in range(REL_BUCKETS):
        val = jnp.where(bucket == b, tab_ref[h * REL_BUCKETS + b], val)
    val = (val - far) * LOG2E
    toeplitz = pltpu.roll(jnp.broadcast_to(val, shape), 0, 1, stride=1, stride_axis=0)
    ki = lax.broadcasted_iota(jnp.int32, shape, 0)
    qi = lax.broadcasted_iota(jnp.int32, shape, 1)
    o_ref[0, 0] = jnp.where(qi >= ki, toeplitz, MASKED)
    o_ref[0, 1] = jnp.where(qi < ki, toeplitz, 0.0)


def _bias_tiles(tab_flat):
    return pl.pallas_call(
        _bias_kernel,
        out_shape=jax.ShapeDtypeStruct((ATTN_HEADS, 2, ATTN_TILE, ATTN_TILE), F32),
        grid=(ATTN_HEADS,),
        in_specs=[pl.BlockSpec(memory_space=pltpu.SMEM)],
        out_specs=pl.BlockSpec((1, 2, ATTN_TILE, ATTN_TILE), lambda h: (h, 0, 0, 0)),
        compiler_params=_params(("parallel",)),
        name="bias",
    )(tab_flat)


def _attn_tile(t, qt, ka_ref, vt_ref, t_ref, m_sc, acc_sc, s_sc):
    def scores(j):
        start = pl.multiple_of(j * ATTN_TILE, ATTN_TILE)
        return _dot(ka_ref[0, pl.ds(start, ATTN_TILE), :], qt)

    m_sc[...] = jnp.full(m_sc.shape, 4.0 * MASKED, F32)
    acc_sc[...] = jnp.zeros_like(acc_sc)

    def colmax(s_ref):
        return jnp.max(s_ref[...], axis=0, keepdims=True)

    def consume(s_ref, s_max, j):
        m_old = m_sc[...]
        m_new = jnp.maximum(m_old, s_max)
        m_sc[...] = m_new
        p = jnp.exp2(s_ref[...] - m_new).astype(BF16)
        acc_sc[...] = jnp.exp2(m_old - m_new) * acc_sc[...] + _dot(vt_ref[0, j], p)

    def consume_near(s_prev, s_diag):
        m_old = m_sc[...]
        m_new = jnp.maximum(m_old, jnp.maximum(colmax(s_prev), colmax(s_diag)))
        m_sc[...] = m_new
        p_prev = jnp.exp2(s_prev[...] - m_new).astype(BF16)
        p_diag = jnp.exp2(s_diag[...] - m_new).astype(BF16)
        acc_sc[...] = (jnp.exp2(m_old - m_new) * acc_sc[...]
                       + _dot(vt_ref[0, t - 1], p_prev) + _dot(vt_ref[0, t], p_diag))

    def far_step(j, cur, nxt, max_cur):
        nxt[...] = scores(j + 1)
        consume(cur, max_cur, j)
        return colmax(nxt)

    s_a, s_b = s_sc.at[0], s_sc.at[1]

    @pl.when(t == 0)
    def _():
        s_a[...] = scores(0) + t_ref[0, 0]
        consume(s_a, colmax(s_a), 0)

    @pl.when(t >= 1)
    def _():
        n_far = t - 1
        s_a[...] = scores(0)

        bufs = (s_a, s_b)

        def steps(first, count, max_cur):
            for k in range(count):
                max_cur = far_step(first + k, bufs[k % 2], bufs[(k + 1) % 2], max_cur)
            return max_cur

        max_cur = lax.fori_loop(0, n_far // FAR_UNROLL,
                                lambda i, m: steps(FAR_UNROLL * i, FAR_UNROLL, m), colmax(s_a))
        done = n_far // FAR_UNROLL * FAR_UNROLL

        for rem in range(FAR_UNROLL):
            @pl.when(n_far - done == rem)
            def _(rem=rem):
                steps(done, rem, max_cur)
                prev, free = bufs[rem % 2], bufs[(rem + 1) % 2]
                free[...] = scores(t) + t_ref[0, 0]
                prev[...] = prev[...] + t_ref[0, 1]
                consume_near(prev, free)

    acc = acc_sc[...]
    out = acc[:HEAD_DIM] / acc[HEAD_DIM:HEAD_DIM + 1]
    return out.T


def _attn_kernel(n_tiles, qt_ref, ka_ref, vt_ref, t_ref, o_ref, *scratch):
    for u in range(n_tiles):
        out = _attn_tile(pl.program_id(1) * n_tiles + u, qt_ref[0, u], ka_ref, vt_ref, t_ref, *scratch)
        o_ref[u * ATTN_TILE:(u + 1) * ATTN_TILE, :] = out.astype(o_ref.dtype)


def _attn(qt, ka, vt, tiles):
    nt = qt.shape[1]
    s = nt * ATTN_TILE
    n = ATTN_TILES_PER_STEP if nt % ATTN_TILES_PER_STEP == 0 else 1
    return pl.pallas_call(
        functools.partial(_attn_kernel, n),
        out_shape=jax.ShapeDtypeStruct((s, ATTN_WIDTH), BF16),
        grid=(ATTN_HEADS, nt // n),
        in_specs=[pl.BlockSpec((1, n, AUG_DIM, ATTN_TILE), lambda h, t: (h, t, 0, 0)),
                  pl.BlockSpec((1, s, AUG_DIM), lambda h, t: (h, 0, 0)),
                  pl.BlockSpec((1, nt, V_ROWS, ATTN_TILE), lambda h, t: (h, 0, 0, 0)),
                  pl.BlockSpec((1, 2, ATTN_TILE, ATTN_TILE), lambda h, t: (h, 0, 0, 0))],
        out_specs=pl.BlockSpec((n * ATTN_TILE, HEAD_DIM), lambda h, t: (t, h)),
        scratch_shapes=[pltpu.VMEM((1, ATTN_TILE), F32), pltpu.VMEM((V_ROWS, ATTN_TILE), F32),
                        pltpu.VMEM((2, ATTN_TILE, ATTN_TILE), F32)],
        compiler_params=_params(("parallel", "arbitrary")),
        name="attn",
    )(qt, ka, vt, tiles)


def _expand_heads(d, lane_head):
    out = d[:, SSM_HPG - 1:SSM_HPG]
    for hg in range(SSM_HPG - 2, -1, -1):
        out = jnp.where(lane_head == hg, d[:, hg:hg + 1], out)
    return out


def _cumsum_lanes(x, triu):
    hi = x.astype(BF16)
    rest = x - hi.astype(F32)
    mid = rest.astype(BF16)
    lo = (rest - mid.astype(F32)).astype(BF16)
    return _dot(hi, triu) + _dot(mid, triu) + _dot(lo, triu)


def _ssd_kernel(z_ref, x_ref, b_ref, c_ref, dtc_ref, wx_ref, wb_ref, wc_ref, bx_ref, bb_ref, bc_ref,
                dtb_ref, alog_ref, dskip_ref, nw_ref, o_ref, xpx_sc, xpb_sc, xpc_sc, st_sc):
    c = pl.program_id(1)
    L, GW, NS, hp = SSM_CHUNK, SSM_GROUP_W, SSM_STATE, SSM_HPG

    @pl.when(c == 0)
    def _():
        xpx_sc[0:CONV_HALO, :] = jnp.zeros((CONV_HALO, xpx_sc.shape[1]), F32)
        xpb_sc[0:CONV_HALO, :] = jnp.zeros((CONV_HALO, xpb_sc.shape[1]), F32)
        xpc_sc[0:CONV_HALO, :] = jnp.zeros((CONV_HALO, xpc_sc.shape[1]), F32)
        st_sc[...] = jnp.zeros_like(st_sc)

    def conv(src_ref, pad_sc, w_ref, bias_ref):
        pad_sc[CONV_HALO:CONV_HALO + L, :] = src_ref[...].astype(F32)
        acc = bias_ref[...]
        for j in range(SSM_CONV):
            lo = CONV_HALO - (SSM_CONV - 1) + j
            acc = acc + pad_sc[lo:lo + L, :] * w_ref[j:j + 1, :]
        pad_sc[0:CONV_HALO, :] = pad_sc[L:L + CONV_HALO, :]
        return _silu(acc)

    x_all = conv(x_ref, xpx_sc, wx_ref, bx_ref)
    b_all = conv(b_ref, xpb_sc, wb_ref, bb_ref)
    c_all = conv(c_ref, xpc_sc, wc_ref, bc_ref)

    row = lax.broadcasted_iota(jnp.int32, (L, L), 0)
    col = lax.broadcasted_iota(jnp.int32, (L, L), 1)
    causal = row >= col
    triu = jnp.where(row <= col, 1.0, 0.0).astype(BF16)
    lane_head = lax.broadcasted_iota(jnp.int32, (1, GW), 1) // SSM_HEAD_DIM

    for gi in range(SSD_GROUPS_PER_STEP):
        x = x_all[:, gi * GW:(gi + 1) * GW]
        bm = b_all[:, gi * NS:(gi + 1) * NS].astype(BF16)
        cm = c_all[:, gi * NS:(gi + 1) * NS].astype(BF16)

        dt_c = _softplus(dtc_ref[gi] + dtb_ref[gi])
        a_c = -jnp.exp(alog_ref[gi])
        cum_c = _cumsum_lanes(dt_c * a_c, triu)
        last_c = cum_c[:, L - 1:L]
        small = jnp.concatenate(
            [cum_c, jnp.exp(cum_c), jnp.exp(last_c - cum_c) * dt_c,
             jnp.zeros((SMALL_ROWS - 3 * hp, L), F32)], axis=0).T
        cum_r = small[:, 0:hp]
        grow_x = _expand_heads(small[:, hp:2 * hp], lane_head)
        end_x = _expand_heads(small[:, 2 * hp:3 * hp], lane_head)
        last_x = _expand_heads(small[L - 1:L, hp:2 * hp], lane_head)
        src_c = cum_c - jnp.log(dt_c)

        cb = _dot_nt(cm, bm)
        w_parts, x_parts = [], []
        for hg in range(hp):
            seg = cum_r[:, hg:hg + 1] - src_c[hg:hg + 1, :]
            w_parts.append(cb * jnp.exp(jnp.where(causal, seg, -jnp.inf)))
            x_parts.append(jnp.where(lane_head == hg, x, 0.0))
        y = _dot(jnp.concatenate(w_parts, axis=1).astype(BF16),
                 jnp.concatenate(x_parts, axis=0).astype(BF16))

        st = st_sc[gi]
        y = y + _dot(cm, st.astype(BF16)) * grow_x
        st_sc[gi] = last_x * st + _dot_tn(bm, (x * end_x).astype(BF16))
        y = y + x * dskip_ref[gi]

        g = y * _silu(z_ref[:, gi * GW:(gi + 1) * GW].astype(F32))
        g = g * lax.rsqrt(jnp.mean(g * g, axis=-1, keepdims=True) + RMS_EPS)
        o_ref[:, gi * GW:(gi + 1) * GW] = (g * nw_ref[gi]).astype(o_ref.dtype)


def _ssd(proj, dt_cols, conv_w, conv_b, dtb, alog, dskip_x, norm_w, nc):
    s = proj.shape[0]
    n = SSD_GROUPS_PER_STEP
    L, GW, NS = SSM_CHUNK, n * SSM_GROUP_W, n * SSM_STATE
    xoff, boff, coff = 0, SSM_INNER, SSM_INNER + SSM_GROUPS * SSM_STATE
    per_step = lambda a: pl.BlockSpec((n,) + a.shape[1:], lambda g, c: (g, 0, 0))
    return pl.pallas_call(
        _ssd_kernel,
        out_shape=jax.ShapeDtypeStruct((s, SSM_INNER), BF16),
        grid=(SSM_GROUPS // n, nc),
        in_specs=[
            pl.BlockSpec((L, GW), lambda g, c: (c, COL_Z // GW + g)),
            pl.BlockSpec((L, GW), lambda g, c: (c, COL_X // GW + g)),
            pl.BlockSpec((L, NS), lambda g, c: (c, COL_B // NS + g)),
            pl.BlockSpec((L, NS), lambda g, c: (c, COL_C // NS + g)),
            pl.BlockSpec((n, SSM_HPG, L), lambda g, c: (g, 0, c)),
            pl.BlockSpec((SSM_CONV, GW), lambda g, c: (0, xoff // GW + g)),
            pl.BlockSpec((SSM_CONV, NS), lambda g, c: (0, boff // NS + g)),
            pl.BlockSpec((SSM_CONV, NS), lambda g, c: (0, coff // NS + g)),
            pl.BlockSpec((1, GW), lambda g, c: (0, xoff // GW + g)),
            pl.BlockSpec((1, NS), lambda g, c: (0, boff // NS + g)),
            pl.BlockSpec((1, NS), lambda g, c: (0, coff // NS + g)),
            per_step(dtb), per_step(alog), per_step(dskip_x), per_step(norm_w),
        ],
        out_specs=pl.BlockSpec((L, GW), lambda g, c: (c, g)),
        scratch_shapes=[pltpu.VMEM((L + CONV_HALO, GW), F32), pltpu.VMEM((L + CONV_HALO, NS), F32),
                        pltpu.VMEM((L + CONV_HALO, NS), F32),
                        pltpu.VMEM((n, SSM_STATE, SSM_GROUP_W), F32)],
        compiler_params=_params(("parallel", "arbitrary")),
        name="ssd",
    )(proj, proj, proj, proj, dt_cols, conv_w, conv_w, conv_w, conv_b, conv_b, conv_b,
      dtb, alog, dskip_x, norm_w)


def _merge_kernel(h_ref, a_ref, b_ref, ga_ref, gb_ref, pa_ref, pb_ref, wo_ref, gate_ref, gpost_ref,
                  o_ref):
    ya = _dot(a_ref[...], pa_ref[...])
    yb = _dot(b_ref[...], pb_ref[...])
    mix = _sigmoid(ga_ref[...].astype(F32)) * ya + _sigmoid(gb_ref[...].astype(F32)) * yb
    y = _dot(mix.astype(BF16), wo_ref[...])
    o_ref[...] = h_ref[...] + gate_ref[...] * _rms(y, gpost_ref[...])


def _merge(h, attn, ssd, proj, pa, pb, wo, gate, gpost, tm):
    s = h.shape[0]
    vec = pl.BlockSpec((1, D_MODEL), lambda i: (0, 0))
    full = lambda a: pl.BlockSpec(a.shape, lambda i: (0, 0))
    return pl.pallas_call(
        _merge_kernel,
        out_shape=jax.ShapeDtypeStruct((s, D_MODEL), F32),
        grid=(s // tm,),
        in_specs=[pl.BlockSpec((tm, D_MODEL), lambda i: (i, 0)),
                  pl.BlockSpec((tm, ATTN_WIDTH), lambda i: (i, 0)),
                  pl.BlockSpec((tm, SSM_INNER), lambda i: (i, 0)),
                  pl.BlockSpec((tm, D_MODEL), lambda i: (i, COL_GA // D_MODEL)),
                  pl.BlockSpec((tm, D_MODEL), lambda i: (i, COL_GB // D_MODEL)),
                  full(pa), full(pb), full(wo), vec, vec],
        out_specs=pl.BlockSpec((tm, D_MODEL), lambda i: (i, 0)),
        compiler_params=_params(("parallel",)),
        name="merge",
    )(h, attn, ssd, proj, proj, pa, pb, wo, gate, gpost)


def _layer(h, mod, rel_bias, p):
    s = h.shape[0]
    nc = s // SSM_CHUNK
    tm = min(512, s)
    tm_wide = min(1024, s)
    sh1, sc1, g1, shm, scm, gm, sh2, sc2, g2 = [mod[k] for k in range(N_MOD)]
    vec = lambda a: a.reshape(1, -1)

    h = _ffn(h, vec(p["ffn1_norm_pre"]), sh1, sc1, g1, vec(p["ffn1_norm_post"]),
             p["ffn1_w_in"].astype(BF16), p["ffn1_w_out"].astype(BF16), tm=tm_wide)

    w_in = p["w_in_mix"].astype(BF16)
    dt_lo = COL_GA
    w_gates = w_in[:, dt_lo + SSM_HEADS:]
    w_dt = jnp.pad(w_in[:, dt_lo:dt_lo + SSM_HEADS], ((0, 0), (0, DT_PAD - SSM_HEADS)))
    proj, dt_raw = _inproj(h, vec(p["mix_norm_pre"]), shm, scm, w_in, w_gates, w_dt, tm=min(2048, s))

    tab_flat = rel_bias.T.reshape(-1)
    qt, ka, vt = _prep(proj)
    tiles = _bias_tiles(tab_flat)
    attn = _attn(qt, ka, vt, tiles)

    dt_cols = dt_raw[:, :SSM_HEADS].reshape(s, SSM_GROUPS, SSM_HPG).transpose(1, 2, 0)
    per_group = lambda a: a.reshape(SSM_GROUPS, SSM_HPG, 1)
    dskip_x = jnp.repeat(p["d_skip"], SSM_HEAD_DIM).reshape(SSM_GROUPS, 1, SSM_GROUP_W)
    ssd = _ssd(proj, dt_cols, p["conv_w"], vec(p["conv_b"]),
               per_group(p["dt_bias"]), per_group(p["a_log"]),
               dskip_x, p["ssm_norm_w"].reshape(SSM_GROUPS, 1, SSM_GROUP_W), nc)

    h = _merge(h, attn, ssd, proj, p["proj_a"].astype(BF16), p["proj_b"].astype(BF16),
               p["w_out_mix"].astype(BF16), gm, vec(p["mix_norm_post"]), tm=tm)

    h = _ffn(h, vec(p["ffn2_norm_pre"]), sh2, sc2, g2, vec(p["ffn2_norm_post"]),
             p["ffn2_w_in"].astype(BF16), p["ffn2_w_out"].astype(BF16), tm=tm_wide)
    return h


_LAYER_KEYS = ("ffn1_norm_pre", "ffn1_w_in", "ffn1_w_out", "ffn1_norm_post", "mix_norm_pre",
               "w_in_mix", "conv_w", "conv_b", "dt_bias", "a_log", "d_skip", "ssm_norm_w",
               "proj_a", "proj_b", "w_out_mix", "mix_norm_post",
               "ffn2_norm_pre", "ffn2_w_in", "ffn2_w_out", "ffn2_norm_post")


def kernel(x, c, w_ada, b_ada, ffn1_norm_pre, ffn1_w_in, ffn1_w_out, ffn1_norm_post, mix_norm_pre,
           w_in_mix, rel_bias, conv_w, conv_b, dt_bias, a_log, d_skip, ssm_norm_w, proj_a, proj_b,
           w_out_mix, mix_norm_post, ffn2_norm_pre, ffn2_w_in, ffn2_w_out, ffn2_norm_post):
    stacked = dict(ffn1_norm_pre=ffn1_norm_pre, ffn1_w_in=ffn1_w_in, ffn1_w_out=ffn1_w_out,
                   ffn1_norm_post=ffn1_norm_post, mix_norm_pre=mix_norm_pre, w_in_mix=w_in_mix,
                   conv_w=conv_w, conv_b=conv_b, dt_bias=dt_bias, a_log=a_log, d_skip=d_skip,
                   ssm_norm_w=ssm_norm_w, proj_a=proj_a, proj_b=proj_b, w_out_mix=w_out_mix,
                   mix_norm_post=mix_norm_post, ffn2_norm_pre=ffn2_norm_pre, ffn2_w_in=ffn2_w_in,
                   ffn2_w_out=ffn2_w_out, ffn2_norm_post=ffn2_norm_post)
    batch, seq, _ = x.shape
    assert seq % ATTN_TILE == 0 and seq // MOBA_BLOCK <= MAX_BLOCKS and seq % SSM_CHUNK == 0
    depth = w_ada.shape[0]
    outs = []
    for b in range(batch):
        h = x[b]
        for l in range(depth):
            mod = _mod(c[b:b + 1], w_ada[l], b_ada[l])
            h = _layer(h, mod, rel_bias, {k: stacked[k][l] for k in _LAYER_KEYS})
        outs.append(h)
    return outs[0][None] if batch == 1 else jnp.stack(outs)
```

```python
import functools
import math

import jax
import jax.numpy as jnp
from jax import lax
from jax.experimental import pallas as pl
from jax.experimental.pallas import tpu as pltpu

F32 = jnp.float32
BF16 = jnp.bfloat16
HIGHEST = lax.Precision.HIGHEST

D_MODEL = 1024
N_MOD = 9
RMS_EPS = 1e-6
FFN_HIDDEN = 2816
FFN_RES = 0.5
FFN_CHUNK = FFN_HIDDEN // 2

ATTN_HEADS = 8
HEAD_DIM = 128
ATTN_WIDTH = ATTN_HEADS * HEAD_DIM
MOBA_BLOCK = 256
MOBA_TOPK = 3
MAX_BLOCKS = 128
AUG_DIM = HEAD_DIM + MAX_BLOCKS
REL_BUCKETS = 32
REL_MAX_DIST = 128
MASKED = -1e30
LOG2E = math.log2(math.e)
ATTN_TILE = 512
BLOCKS_PER_TILE = ATTN_TILE // MOBA_BLOCK
BF16_SUBLANES = 16
V_ROWS = HEAD_DIM + BF16_SUBLANES
ATTN_TILES_PER_STEP = 1
FAR_UNROLL = 8
PREP_TILES_PER_STEP = 4

SSM_INNER = 2048
SSM_HEAD_DIM = 64
SSM_GROUPS = 8
SSM_HEADS = SSM_INNER // SSM_HEAD_DIM
SSM_HPG = SSM_HEADS // SSM_GROUPS
SSM_GROUP_W = SSM_INNER // SSM_GROUPS
SSM_STATE = 128
SSM_CONV = 4
SSM_CHUNK = 256
CONV_HALO = 8
SMALL_ROWS = 128
SSD_GROUPS_PER_STEP = 2

COL_Q = 0
COL_K = COL_Q + ATTN_WIDTH
COL_V = COL_K + ATTN_WIDTH
COL_Z = COL_V + ATTN_WIDTH
COL_X = COL_Z + SSM_INNER
COL_B = COL_X + SSM_INNER
COL_C = COL_B + SSM_GROUPS * SSM_STATE
COL_GA = COL_C + SSM_GROUPS * SSM_STATE
COL_GB = COL_GA + D_MODEL
PROJ_W = COL_GB + D_MODEL
PROJ_TILE = 1024
MAIN_TILES = COL_GA // PROJ_TILE
DT_PAD = 128

VMEM_LIMIT = 56 * 1024 * 1024


def _params(sem):
    return pltpu.CompilerParams(dimension_semantics=sem, vmem_limit_bytes=VMEM_LIMIT)


def _sigmoid(x):
    return 0.5 + 0.5 * jnp.tanh(0.5 * x)


def _silu(x):
    return x * _sigmoid(x)


def _softplus(x):
    return jnp.maximum(x, 0.0) + jnp.log(1.0 + jnp.exp(-jnp.abs(x)))


def _rms(x, g):
    return x * lax.rsqrt(jnp.mean(x * x, axis=-1, keepdims=True) + RMS_EPS) * g


def _dot(a, b, **kw):
    return jnp.dot(a, b, preferred_element_type=F32, **kw)


def _dot_nt(a, b, **kw):
    return lax.dot_general(a, b, (((1,), (1,)), ((), ())), preferred_element_type=F32, **kw)


def _dot_tn(a, b, **kw):
    return lax.dot_general(a, b, (((0,), (0,)), ((), ())), preferred_element_type=F32, **kw)


def _mod_kernel(c_ref, w_ref, b_ref, o_ref):
    cs = _silu(c_ref[...])
    o_ref[...] = _dot(cs, w_ref[...], precision=HIGHEST) + b_ref[...]


def _mod(c, w_ada, b_ada):
    n = w_ada.shape[1]
    tn = 1024
    c8 = jnp.broadcast_to(c, (8, D_MODEL))
    out = pl.pallas_call(
        _mod_kernel,
        out_shape=jax.ShapeDtypeStruct((8, n), F32),
        grid=(n // tn,),
        in_specs=[pl.BlockSpec((8, D_MODEL), lambda j: (0, 0)),
                  pl.BlockSpec((D_MODEL, tn), lambda j: (0, j)),
                  pl.BlockSpec((1, tn), lambda j: (0, j))],
        out_specs=pl.BlockSpec((8, tn), lambda j: (0, j)),
        compiler_params=_params(("arbitrary",)),
        name="mod",
    )(c8, w_ada, b_ada.reshape(1, n))
    return out[0].reshape(N_MOD, 1, D_MODEL)


def _ffn_kernel(h_ref, gpre_ref, sh_ref, sc_ref, gate_ref, gpost_ref, wi_ref, wo_ref, o_ref):
    h = h_ref[...]
    u = (_rms(h, gpre_ref[...]) * (1.0 + sc_ref[...]) + sh_ref[...]).astype(BF16)
    acc = None
    for lo in range(0, FFN_HIDDEN, FFN_CHUNK):
        a = _dot(u, wi_ref[:, lo:lo + FFN_CHUNK])
        b = _dot(u, wi_ref[:, FFN_HIDDEN + lo:FFN_HIDDEN + lo + FFN_CHUNK])
        part = _dot((_silu(a) * b).astype(BF16), wo_ref[lo:lo + FFN_CHUNK, :])
        acc = part if acc is None else acc + part
    o_ref[...] = h + (FFN_RES * gate_ref[...]) * _rms(acc, gpost_ref[...])


def _ffn(h, gpre, sh, sc, gate, gpost, wi, wo, tm):
    s = h.shape[0]
    row = lambda i: (i, 0)
    vec = pl.BlockSpec((1, D_MODEL), lambda i: (0, 0))
    resident = lambda a: pl.BlockSpec(a.shape, lambda i: (0, 0), pipeline_mode=pl.Buffered(1))
    return pl.pallas_call(
        _ffn_kernel,
        out_shape=jax.ShapeDtypeStruct((s, D_MODEL), F32),
        grid=(s // tm,),
        in_specs=[pl.BlockSpec((tm, D_MODEL), row), vec, vec, vec, vec, vec,
                  resident(wi), resident(wo)],
        out_specs=pl.BlockSpec((tm, D_MODEL), row),
        compiler_params=_params(("parallel",)),
        name="ffn",
    )(h, gpre, sh, sc, gate, gpost, wi, wo)


def _inproj_kernel(h_ref, gpre_ref, sh_ref, sc_ref, w_ref, wg_ref, wdt_ref, o_ref, dt_ref, u_sc):
    j = pl.program_id(1)

    @pl.when(j == 0)
    def _():
        u = (_rms(h_ref[...], gpre_ref[...]) * (1.0 + sc_ref[...]) + sh_ref[...]).astype(BF16)
        u_sc[...] = u
        dt_ref[...] = _dot(u, wdt_ref[...])
        o_ref[...] = _dot(u, w_ref[...]).astype(o_ref.dtype)

    @pl.when(jnp.logical_and(j > 0, j < MAIN_TILES))
    def _():
        o_ref[...] = _dot(u_sc[...], w_ref[...]).astype(o_ref.dtype)

    @pl.when(j >= MAIN_TILES)
    def _():
        o_ref[...] = _dot(u_sc[...], wg_ref[...]).astype(o_ref.dtype)


def _inproj(h, gpre, sh, sc, w, wg, wdt, tm):
    s = h.shape[0]
    tn = PROJ_TILE
    vec = pl.BlockSpec((1, D_MODEL), lambda i, j: (0, 0))
    return pl.pallas_call(
        _inproj_kernel,
        out_shape=(jax.ShapeDtypeStruct((s, PROJ_W), BF16),
                   jax.ShapeDtypeStruct((s, DT_PAD), F32)),
        grid=(s // tm, PROJ_W // tn),
        in_specs=[pl.BlockSpec((tm, D_MODEL), lambda i, j: (i, 0)), vec, vec, vec,
                  pl.BlockSpec((D_MODEL, tn), lambda i, j: (0, jnp.minimum(j, MAIN_TILES - 1))),
                  pl.BlockSpec((D_MODEL, tn), lambda i, j: (0, jnp.maximum(j - MAIN_TILES, 0))),
                  pl.BlockSpec((D_MODEL, DT_PAD), lambda i, j: (0, 0))],
        out_specs=(pl.BlockSpec((tm, tn), lambda i, j: (i, j)),
                   pl.BlockSpec((tm, DT_PAD), lambda i, j: (i, 0))),
        scratch_shapes=[pltpu.VMEM((tm, D_MODEL), BF16)],
        compiler_params=_params(("parallel", "arbitrary")),
        name="inproj",
    )(h, gpre, sh, sc, w, wg, wdt)


def _prep_kernel(n_sel, n_tiles, q_ref, k_ref, v_ref, qt_ref, ka_ref, vt_ref, km_sc):
    @pl.when(pl.program_id(1) == 0)
    def _():
        km_sc[...] = jnp.zeros_like(km_sc)

    for u in range(n_tiles):
        t = pl.program_id(1) * n_tiles + u
        rows = slice(u * ATTN_TILE, (u + 1) * ATTN_TILE)
        k = k_ref[rows, :].astype(F32)
        for b in range(BLOCKS_PER_TILE):
            km_sc[pl.ds(t * BLOCKS_PER_TILE + b, 1), :] = jnp.mean(
                k[b * MOBA_BLOCK:(b + 1) * MOBA_BLOCK], axis=0, keepdims=True)

        qt = (q_ref[rows, :].astype(F32) * (HEAD_DIM ** -0.5 * LOG2E)).T
        score = _dot(km_sc[:n_sel, :], qt, precision=HIGHEST)
        blk = lax.broadcasted_iota(jnp.int32, score.shape, 0)
        q_blk = t * BLOCKS_PER_TILE + lax.broadcasted_iota(jnp.int32, score.shape, 1) // MOBA_BLOCK
        s = jnp.where(blk < q_blk, score, -jnp.inf)
        pen = jnp.full(score.shape, MASKED, F32)
        for _ in range(MOBA_TOPK):
            m = jnp.max(s, axis=0, keepdims=True)
            first = jnp.min(jnp.where(s == m, blk, n_sel), axis=0, keepdims=True)
            first = jnp.where(m > -jnp.inf, first, n_sel)
            pick = blk == first
            pen = jnp.where(pick, 0.0, pen)
            s = jnp.where(pick, -jnp.inf, s)
        pen = jnp.where(blk == q_blk, 0.0, pen)
        qt_ref[0, u, :HEAD_DIM, :] = qt.astype(BF16)
        qt_ref[0, u, HEAD_DIM:HEAD_DIM + n_sel, :] = pen.astype(BF16)
        if n_sel < MAX_BLOCKS:
            qt_ref[0, u, HEAD_DIM + n_sel:, :] = jnp.zeros((MAX_BLOCKS - n_sel, ATTN_TILE), BF16)

        lane = lax.broadcasted_iota(jnp.int32, (ATTN_TILE, MAX_BLOCKS), 1)
        k_blk = t * BLOCKS_PER_TILE + lax.broadcasted_iota(jnp.int32, lane.shape, 0) // MOBA_BLOCK
        ka_ref[0, rows, :HEAD_DIM] = k.astype(BF16)
        ka_ref[0, rows, HEAD_DIM:] = jnp.where(lane == k_blk, 1.0, 0.0).astype(BF16)

        ones_row = lax.broadcasted_iota(jnp.int32, (V_ROWS - HEAD_DIM, ATTN_TILE), 0) == 0
        vt_ref[0, u, :HEAD_DIM, :] = v_ref[rows, :].astype(F32).T.astype(BF16)
        vt_ref[0, u, HEAD_DIM:, :] = jnp.where(ones_row, 1.0, 0.0).astype(BF16)


def _prep(proj):
    s = proj.shape[0]
    nt = s // ATTN_TILE
    n = PREP_TILES_PER_STEP if nt % PREP_TILES_PER_STEP == 0 else 1
    blk = lambda col: pl.BlockSpec((n * ATTN_TILE, HEAD_DIM), lambda h, t: (t, col // HEAD_DIM + h))
    n_sel = -(-(s // MOBA_BLOCK) // BF16_SUBLANES) * BF16_SUBLANES
    return pl.pallas_call(
        functools.partial(_prep_kernel, n_sel, n),
        out_shape=(jax.ShapeDtypeStruct((ATTN_HEADS, nt, AUG_DIM, ATTN_TILE), BF16),
                   jax.ShapeDtypeStruct((ATTN_HEADS, s, AUG_DIM), BF16),
                   jax.ShapeDtypeStruct((ATTN_HEADS, nt, V_ROWS, ATTN_TILE), BF16)),
        grid=(ATTN_HEADS, nt // n),
        in_specs=[blk(COL_Q), blk(COL_K), blk(COL_V)],
        out_specs=(pl.BlockSpec((1, n, AUG_DIM, ATTN_TILE), lambda h, t: (h, t, 0, 0)),
                   pl.BlockSpec((1, n * ATTN_TILE, AUG_DIM), lambda h, t: (h, t, 0)),
                   pl.BlockSpec((1, n, V_ROWS, ATTN_TILE), lambda h, t: (h, t, 0, 0))),
        scratch_shapes=[pltpu.VMEM((MAX_BLOCKS, HEAD_DIM), F32)],
        compiler_params=_params(("parallel", "arbitrary")),
        name="prep",
    )(proj, proj, proj)


def _t5_bucket(rel):
    n = jnp.maximum(rel, 0)
    max_exact = REL_BUCKETS // 2
    nf = jnp.maximum(n, 1).astype(F32)
    large = max_exact + (jnp.log(nf / max_exact) / math.log(REL_MAX_DIST / max_exact)
                         * (REL_BUCKETS - max_exact)).astype(jnp.int32)
    large = jnp.minimum(large, REL_BUCKETS - 1)
    return jnp.where(n < max_exact, n, large)


def _bias_kernel(tab_ref, o_ref):
    h = pl.program_id(0)
    shape = (ATTN_TILE, ATTN_TILE)
    far = tab_ref[h * REL_BUCKETS + REL_BUCKETS - 1]
    bucket = _t5_bucket(lax.broadcasted_iota(jnp.int32, (1, ATTN_TILE), 1))
    val = jnp.zeros((1, ATTN_TILE), F32)
    for b in range(REL_BUCKETS):
        val = jnp.where(bucket == b, tab_ref[h * REL_BUCKETS + b], val)
    val = (val - far) * LOG2E
    toeplitz = pltpu.roll(jnp.broadcast_to(val, shape), 0, 1, stride=1, stride_axis=0)
    ki = lax.broadcasted_iota(jnp.int32, shape, 0)
    qi = lax.broadcasted_iota(jnp.int32, shape, 1)
    o_ref[0, 0] = jnp.where(qi >= ki, toeplitz, MASKED)
    o_ref[0, 1] = jnp.where(qi < ki, toeplitz, 0.0)


def _bias_tiles(tab_flat):
    return pl.pallas_call(
        _bias_kernel,
        out_shape=jax.ShapeDtypeStruct((ATTN_HEADS, 2, ATTN_TILE, ATTN_TILE), F32),
        grid=(ATTN_HEADS,),
        in_specs=[pl.BlockSpec(memory_space=pltpu.SMEM)],
        out_specs=pl.BlockSpec((1, 2, ATTN_TILE, ATTN_TILE), lambda h: (h, 0, 0, 0)),
        compiler_params=_params(("parallel",)),
        name="bias",
    )(tab_flat)


def _attn_tile(t, qt, ka_ref, vt_ref, t_ref, m_sc, acc_sc, s_sc):
    def scores(j):
        start = pl.multiple_of(j * ATTN_TILE, ATTN_TILE)
        return _dot(ka_ref[0, pl.ds(start, ATTN_TILE), :], qt)

    m_sc[...] = jnp.full(m_sc.shape, 4.0 * MASKED, F32)
    acc_sc[...] = jnp.zeros_like(acc_sc)

    def colmax(s_ref):
        return jnp.max(s_ref[...], axis=0, keepdims=True)

    def consume(s_ref, s_max, j):
        m_old = m_sc[...]
        m_new = jnp.maximum(m_old, s_max)
        m_sc[...] = m_new
        p = jnp.exp2(s_ref[...] - m_new).astype(BF16)
        acc_sc[...] = jnp.exp2(m_old - m_new) * acc_sc[...] + _dot(vt_ref[0, j], p)

    def consume_near(s_prev, s_diag):
        m_old = m_sc[...]
        m_new = jnp.maximum(m_old, jnp.maximum(colmax(s_prev), colmax(s_diag)))
        m_sc[...] = m_new
        p_prev = jnp.exp2(s_prev[...] - m_new).astype(BF16)
        p_diag = jnp.exp2(s_diag[...] - m_new).astype(BF16)
        acc_sc[...] = (jnp.exp2(m_old - m_new) * acc_sc[...]
                       + _dot(vt_ref[0, t - 1], p_prev) + _dot(vt_ref[0, t], p_diag))

    def far_step(j, cur, nxt, max_cur):
        nxt[...] = scores(j + 1)
        consume(cur, max_cur, j)
        return colmax(nxt)

    s_a, s_b = s_sc.at[0], s_sc.at[1]

    @pl.when(t == 0)
    def _():
        s_a[...] = scores(0) + t_ref[0, 0]
        consume(s_a, colmax(s_a), 0)

    @pl.when(t >= 1)
    def _():
        n_far = t - 1
        s_a[...] = scores(0)

        bufs = (s_a, s_b)

        def steps(first, count, max_cur):
            for k in range(count):
                max_cur = far_step(first + k, bufs[k % 2], bufs[(k + 1) % 2], max_cur)
            return max_cur

        max_cur = lax.fori_loop(0, n_far // FAR_UNROLL,
                                lambda i, m: steps(FAR_UNROLL * i, FAR_UNROLL, m), colmax(s_a))
        done = n_far // FAR_UNROLL * FAR_UNROLL

        for rem in range(FAR_UNROLL):
            @pl.when(n_far - done == rem)
            def _(rem=rem):
                steps(done, rem, max_cur)
                prev, free = bufs[rem % 2], bufs[(rem + 1) % 2]
                free[...] = scores(t) + t_ref[0, 0]
                prev[...] = prev[...] + t_ref[0, 1]
                consume_near(prev, free)

    acc = acc_sc[...]
    out = acc[:HEAD_DIM] / acc[HEAD_DIM:HEAD_DIM + 1]
    return out.T


def _attn_kernel(n_tiles, qt_ref, ka_ref, vt_ref, t_ref, o_ref, *scratch):
    for u in range(n_tiles):
        out = _attn_tile(pl.program_id(1) * n_tiles + u, qt_ref[0, u], ka_ref, vt_ref, t_ref, *scratch)
        o_ref[u * ATTN_TILE:(u + 1) * ATTN_TILE, :] = out.astype(o_ref.dtype)


def _attn(qt, ka, vt, tiles):
    nt = qt.shape[1]
    s = nt * ATTN_TILE
    n = ATTN_TILES_PER_STEP if nt % ATTN_TILES_PER_STEP == 0 else 1
    return pl.pallas_call(
        functools.partial(_attn_kernel, n),
        out_shape=jax.ShapeDtypeStruct((s, ATTN_WIDTH), BF16),
        grid=(ATTN_HEADS, nt // n),
        in_specs=[pl.BlockSpec((1, n, AUG_DIM, ATTN_TILE), lambda h, t: (h, t, 0, 0)),
                  pl.BlockSpec((1, s, AUG_DIM), lambda h, t: (h, 0, 0)),
                  pl.BlockSpec((1, nt, V_ROWS, ATTN_TILE), lambda h, t: (h, 0, 0, 0)),
                  pl.BlockSpec((1, 2, ATTN_TILE, ATTN_TILE), lambda h, t: (h, 0, 0, 0))],
        out_specs=pl.BlockSpec((n * ATTN_TILE, HEAD_DIM), lambda h, t: (t, h)),
        scratch_shapes=[pltpu.VMEM((1, ATTN_TILE), F32), pltpu.VMEM((V_ROWS, ATTN_TILE), F32),
                        pltpu.VMEM((2, ATTN_TILE, ATTN_TILE), F32)],
        compiler_params=_params(("parallel", "arbitrary")),
        name="attn",
    )(qt, ka, vt, tiles)


def _expand_heads(d, lane_head):
    out = d[:, SSM_HPG - 1:SSM_HPG]
    for hg in range(SSM_HPG - 2, -1, -1):
        out = jnp.where(lane_head == hg, d[:, hg:hg + 1], out)
    return out


def _cumsum_lanes(x, triu):
    hi = x.astype(BF16)
    rest = x - hi.astype(F32)
    mid = rest.astype(BF16)
    lo = (rest - mid.astype(F32)).astype(BF16)
    return _dot(hi, triu) + _dot(mid, triu) + _dot(lo, triu)


def _ssd_kernel(z_ref, x_ref, b_ref, c_ref, dtc_ref, wx_ref, wb_ref, wc_ref, bx_ref, bb_ref, bc_ref,
                dtb_ref, alog_ref, dskip_ref, nw_ref, o_ref, xpx_sc, xpb_sc, xpc_sc, st_sc):
    c = pl.program_id(1)
    L, GW, NS, hp = SSM_CHUNK, SSM_GROUP_W, SSM_STATE, SSM_HPG

    @pl.when(c == 0)
    def _():
        xpx_sc[0:CONV_HALO, :] = jnp.zeros((CONV_HALO, xpx_sc.shape[1]), F32)
        xpb_sc[0:CONV_HALO, :] = jnp.zeros((CONV_HALO, xpb_sc.shape[1]), F32)
        xpc_sc[0:CONV_HALO, :] = jnp.zeros((CONV_HALO, xpc_sc.shape[1]), F32)
        st_sc[...] = jnp.zeros_like(st_sc)

    def conv(src_ref, pad_sc, w_ref, bias_ref):
        pad_sc[CONV_HALO:CONV_HALO + L, :] = src_ref[...].astype(F32)
        acc = bias_ref[...]
        for j in range(SSM_CONV):
            lo = CONV_HALO - (SSM_CONV - 1) + j
            acc = acc + pad_sc[lo:lo + L, :] * w_ref[j:j + 1, :]
        pad_sc[0:CONV_HALO, :] = pad_sc[L:L + CONV_HALO, :]
        return _silu(acc)

    x_all = conv(x_ref, xpx_sc, wx_ref, bx_ref)
    b_all = conv(b_ref, xpb_sc, wb_ref, bb_ref)
    c_all = conv(c_ref, xpc_sc, wc_ref, bc_ref)

    row = lax.broadcasted_iota(jnp.int32, (L, L), 0)
    col = lax.broadcasted_iota(jnp.int32, (L, L), 1)
    causal = row >= col
    triu = jnp.where(row <= col, 1.0, 0.0).astype(BF16)
    lane_head = lax.broadcasted_iota(jnp.int32, (1, GW), 1) // SSM_HEAD_DIM

    for gi in range(SSD_GROUPS_PER_STEP):
        x = x_all[:, gi * GW:(gi + 1) * GW]
        bm = b_all[:, gi * NS:(gi + 1) * NS].astype(BF16)
        cm = c_all[:, gi * NS:(gi + 1) * NS].astype(BF16)

        dt_c = _softplus(dtc_ref[gi] + dtb_ref[gi])
        a_c = -jnp.exp(alog_ref[gi])
        cum_c = _cumsum_lanes(dt_c * a_c, triu)
        last_c = cum_c[:, L - 1:L]
        small = jnp.concatenate(
            [cum_c, jnp.exp(cum_c), jnp.exp(last_c - cum_c) * dt_c,
             jnp.zeros((SMALL_ROWS - 3 * hp, L), F32)], axis=0).T
        cum_r = small[:, 0:hp]
        grow_x = _expand_heads(small[:, hp:2 * hp], lane_head)
        end_x = _expand_heads(small[:, 2 * hp:3 * hp], lane_head)
        last_x = _expand_heads(small[L - 1:L, hp:2 * hp], lane_head)
        src_c = cum_c - jnp.log(dt_c)

        cb = _dot_nt(cm, bm)
        w_parts, x_parts = [], []
        for hg in range(hp):
            seg = cum_r[:, hg:hg + 1] - src_c[hg:hg + 1, :]
            w_parts.append(cb * jnp.exp(jnp.where(causal, seg, -jnp.inf)))
            x_parts.append(jnp.where(lane_head == hg, x, 0.0))
        y = _dot(jnp.concatenate(w_parts, axis=1).astype(BF16),
                 jnp.concatenate(x_parts, axis=0).astype(BF16))

        st = st_sc[gi]
        y = y + _dot(cm, st.astype(BF16)) * grow_x
        st_sc[gi] = last_x * st + _dot_tn(bm, (x * end_x).astype(BF16))
        y = y + x * dskip_ref[gi]

        g = y * _silu(z_ref[:, gi * GW:(gi + 1) * GW].astype(F32))
        g = g * lax.rsqrt(jnp.mean(g * g, axis=-1, keepdims=True) + RMS_EPS)
        o_ref[:, gi * GW:(gi + 1) * GW] = (g * nw_ref[gi]).astype(o_ref.dtype)


def _ssd(proj, dt_cols, conv_w, conv_b, dtb, alog, dskip_x, norm_w, nc):
    s = proj.shape[0]
    n = SSD_GROUPS_PER_STEP
    L, GW, NS = SSM_CHUNK, n * SSM_GROUP_W, n * SSM_STATE
    xoff, boff, coff = 0, SSM_INNER, SSM_INNER + SSM_GROUPS * SSM_STATE
    per_step = lambda a: pl.BlockSpec((n,) + a.shape[1:], lambda g, c: (g, 0, 0))
    return pl.pallas_call(
        _ssd_kernel,
        out_shape=jax.ShapeDtypeStruct((s, SSM_INNER), BF16),
        grid=(SSM_GROUPS // n, nc),
        in_specs=[
            pl.BlockSpec((L, GW), lambda g, c: (c, COL_Z // GW + g)),
            pl.BlockSpec((L, GW), lambda g, c: (c, COL_X // GW + g)),
            pl.BlockSpec((L, NS), lambda g, c: (c, COL_B // NS + g)),
            pl.BlockSpec((L, NS), lambda g, c: (c, COL_C // NS + g)),
            pl.BlockSpec((n, SSM_HPG, L), lambda g, c: (g, 0, c)),
            pl.BlockSpec((SSM_CONV, GW), lambda g, c: (0, xoff // GW + g)),
            pl.BlockSpec((SSM_CONV, NS), lambda g, c: (0, boff // NS + g)),
            pl.BlockSpec((SSM_CONV, NS), lambda g, c: (0, coff // NS + g)),
            pl.BlockSpec((1, GW), lambda g, c: (0, xoff // GW + g)),
            pl.BlockSpec((1, NS), lambda g, c: (0, boff // NS + g)),
            pl.BlockSpec((1, NS), lambda g, c: (0, coff // NS + g)),
            per_step(dtb), per_step(alog), per_step(dskip_x), per_step(norm_w),
        ],
        out_specs=pl.BlockSpec((L, GW), lambda g, c: (c, g)),
        scratch_shapes=[pltpu.VMEM((L + CONV_HALO, GW), F32), pltpu.VMEM((L + CONV_HALO, NS), F32),
                        pltpu.VMEM((L + CONV_HALO, NS), F32),
                        pltpu.VMEM((n, SSM_STATE, SSM_GROUP_W), F32)],
        compiler_params=_params(("parallel", "arbitrary")),
        name="ssd",
    )(proj, proj, proj, proj, dt_cols, conv_w, conv_w, conv_w, conv_b, conv_b, conv_b,
      dtb, alog, dskip_x, norm_w)


def _merge_kernel(h_ref, a_ref, b_ref, ga_ref, gb_ref, pa_ref, pb_ref, wo_ref, gate_ref, gpost_ref,
                  o_ref):
    ya = _dot(a_ref[...], pa_ref[...])
    yb = _dot(b_ref[...], pb_ref[...])
    mix = _sigmoid(ga_ref[...].astype(F32)) * ya + _sigmoid(gb_ref[...].astype(F32)) * yb
    y = _dot(mix.astype(BF16), wo_ref[...])
    o_ref[...] = h_ref[...] + gate_ref[...] * _rms(y, gpost_ref[...])


def _merge(h, attn, ssd, proj, pa, pb, wo, gate, gpost, tm):
    s = h.shape[0]
    vec = pl.BlockSpec((1, D_MODEL), lambda i: (0, 0))
    full = lambda a: pl.BlockSpec(a.shape, lambda i: (0, 0))
    return pl.pallas_call(
        _merge_kernel,
        out_shape=jax.ShapeDtypeStruct((s, D_MODEL), F32),
        grid=(s // tm,),
        in_specs=[pl.BlockSpec((tm, D_MODEL), lambda i: (i, 0)),
                  pl.BlockSpec((tm, ATTN_WIDTH), lambda i: (i, 0)),
                  pl.BlockSpec((tm, SSM_INNER), lambda i: (i, 0)),
                  pl.BlockSpec((tm, D_MODEL), lambda i: (i, COL_GA // D_MODEL)),
                  pl.BlockSpec((tm, D_MODEL), lambda i: (i, COL_GB // D_MODEL)),
                  full(pa), full(pb), full(wo), vec, vec],
        out_specs=pl.BlockSpec((tm, D_MODEL), lambda i: (i, 0)),
        compiler_params=_params(("parallel",)),
        name="merge",
    )(h, attn, ssd, proj, proj, pa, pb, wo, gate, gpost)


def _layer(h, mod, rel_bias, p):
    s = h.shape[0]
    nc = s // SSM_CHUNK
    tm = min(512, s)
    tm_wide = min(1024, s)
    sh1, sc1, g1, shm, scm, gm, sh2, sc2, g2 = [mod[k] for k in range(N_MOD)]
    vec = lambda a: a.reshape(1, -1)

    h = _ffn(h, vec(p["ffn1_norm_pre"]), sh1, sc1, g1, vec(p["ffn1_norm_post"]),
             p["ffn1_w_in"].astype(BF16), p["ffn1_w_out"].astype(BF16), tm=tm_wide)

    w_in = p["w_in_mix"].astype(BF16)
    dt_lo = COL_GA
    w_gates = w_in[:, dt_lo + SSM_HEADS:]
    w_dt = jnp.pad(w_in[:, dt_lo:dt_lo + SSM_HEADS], ((0, 0), (0, DT_PAD - SSM_HEADS)))
    proj, dt_raw = _inproj(h, vec(p["mix_norm_pre"]), shm, scm, w_in, w_gates, w_dt, tm=min(2048, s))

    tab_flat = rel_bias.T.reshape(-1)
    qt, ka, vt = _prep(proj)
    tiles = _bias_tiles(tab_flat)
    attn = _attn(qt, ka, vt, tiles)

    dt_cols = dt_raw[:, :SSM_HEADS].reshape(s, SSM_GROUPS, SSM_HPG).transpose(1, 2, 0)
    per_group = lambda a: a.reshape(SSM_GROUPS, SSM_HPG, 1)
    dskip_x = jnp.repeat(p["d_skip"], SSM_HEAD_DIM).reshape(SSM_GROUPS, 1, SSM_GROUP_W)
    ssd = _ssd(proj, dt_cols, p["conv_w"], vec(p["conv_b"]),
               per_group(p["dt_bias"]), per_group(p["a_log"]),
               dskip_x, p["ssm_norm_w"].reshape(SSM_GROUPS, 1, SSM_GROUP_W), nc)

    h = _merge(h, attn, ssd, proj, p["proj_a"].astype(BF16), p["proj_b"].astype(BF16),
               p["w_out_mix"].astype(BF16), gm, vec(p["mix_norm_post"]), tm=tm)

    h = _ffn(h, vec(p["ffn2_norm_pre"]), sh2, sc2, g2, vec(p["ffn2_norm_post"]),
             p["ffn2_w_in"].astype(BF16), p["ffn2_w_out"].astype(BF16), tm=tm_wide)
    return h


_LAYER_KEYS = ("ffn1_norm_pre", "ffn1_w_in", "ffn1_w_out", "ffn1_norm_post", "mix_norm_pre",
               "w_in_mix", "conv_w", "conv_b", "dt_bias", "a_log", "d_skip", "ssm_norm_w",
               "proj_a", "proj_b", "w_out_mix", "mix_norm_post",
               "ffn2_norm_pre", "ffn2_w_in", "ffn2_w_out", "ffn2_norm_post")


def kernel(x, c, w_ada, b_ada, ffn1_norm_pre, ffn1_w_in, ffn1_w_out, ffn1_norm_post, mix_norm_pre,
           w_in_mix, rel_bias, conv_w, conv_b, dt_bias, a_log, d_skip, ssm_norm_w, proj_a, proj_b,
           w_out_mix, mix_norm_post, ffn2_norm_pre, ffn2_w_in, ffn2_w_out, ffn2_norm_post):
    stacked = dict(ffn1_norm_pre=ffn1_norm_pre, ffn1_w_in=ffn1_w_in, ffn1_w_out=ffn1_w_out,
                   ffn1_norm_post=ffn1_norm_post, mix_norm_pre=mix_norm_pre, w_in_mix=w_in_mix,
                   conv_w=conv_w, conv_b=conv_b, dt_bias=dt_bias, a_log=a_log, d_skip=d_skip,
                   ssm_norm_w=ssm_norm_w, proj_a=proj_a, proj_b=proj_b, w_out_mix=w_out_mix,
                   mix_norm_post=mix_norm_post, ffn2_norm_pre=ffn2_norm_pre, ffn2_w_in=ffn2_w_in,
                   ffn2_w_out=ffn2_w_out, ffn2_norm_post=ffn2_norm_post)
    batch, seq, _ = x.shape
    assert seq % ATTN_TILE == 0 and seq // MOBA_BLOCK <= MAX_BLOCKS and seq % SSM_CHUNK == 0
    depth = w_ada.shape[0]
    outs = []
    for b in range(batch):
        h = x[b]
        for l in range(depth):
            mod = _mod(c[b:b + 1], w_ada[l], b_ada[l])
            h = _layer(h, mod, rel_bias, {k: stacked[k][l] for k in _LAYER_KEYS})
        outs.append(h)
    return outs[0][None] if batch == 1 else jnp.stack(outs)
```

```python
import functools
import math

import jax
import jax.numpy as jnp
from jax import lax
from jax.experimental import pallas as pl
from jax.experimental.pallas import tpu as pltpu

F32 = jnp.float32
BF16 = jnp.bfloat16
HIGHEST = lax.Precision.HIGHEST

D_MODEL = 1024
N_MOD = 9
RMS_EPS = 1e-6
FFN_HIDDEN = 2816
FFN_RES = 0.5
FFN_CHUNK = FFN_HIDDEN // 2

ATTN_HEADS = 8
HEAD_DIM = 128
ATTN_WIDTH = ATTN_HEADS * HEAD_DIM
MOBA_BLOCK = 256
MOBA_TOPK = 3
MAX_BLOCKS = 128
AUG_DIM = HEAD_DIM + MAX_BLOCKS
REL_BUCKETS = 32
REL_MAX_DIST = 128
MASKED = -1e30
LOG2E = math.log2(math.e)
ATTN_TILE = 512
BLOCKS_PER_TILE = ATTN_TILE // MOBA_BLOCK
BF16_SUBLANES = 16
V_ROWS = HEAD_DIM + BF16_SUBLANES
ATTN_TILES_PER_STEP = 1
FAR_UNROLL = 8
PREP_TILES_PER_STEP = 4

SSM_INNER = 2048
SSM_HEAD_DIM = 64
SSM_GROUPS = 8
SSM_HEADS = SSM_INNER // SSM_HEAD_DIM
SSM_HPG = SSM_HEADS // SSM_GROUPS
SSM_GROUP_W = SSM_INNER // SSM_GROUPS
SSM_STATE = 128
SSM_CONV = 4
SSM_CHUNK = 256
CONV_HALO = 8
SMALL_ROWS = 128
SSD_GROUPS_PER_STEP = 2

COL_Q = 0
COL_K = COL_Q + ATTN_WIDTH
COL_V = COL_K + ATTN_WIDTH
COL_Z = COL_V + ATTN_WIDTH
COL_X = COL_Z + SSM_INNER
COL_B = COL_X + SSM_INNER
COL_C = COL_B + SSM_GROUPS * SSM_STATE
COL_GA = COL_C + SSM_GROUPS * SSM_STATE
COL_GB = COL_GA + D_MODEL
PROJ_W = COL_GB + D_MODEL
PROJ_TILE = 1024
MAIN_TILES = COL_GA // PROJ_TILE
DT_PAD = 128

VMEM_LIMIT = 56 * 1024 * 1024


def _params(sem):
    return pltpu.CompilerParams(dimension_semantics=sem, vmem_limit_bytes=VMEM_LIMIT)


def _sigmoid(x):
    return 0.5 + 0.5 * jnp.tanh(0.5 * x)


def _silu(x):
    return x * _sigmoid(x)


def _softplus(x):
    return jnp.maximum(x, 0.0) + jnp.log(1.0 + jnp.exp(-jnp.abs(x)))


def _rms(x, g):
    return x * lax.rsqrt(jnp.mean(x * x, axis=-1, keepdims=True) + RMS_EPS) * g


def _dot(a, b, **kw):
    return jnp.dot(a, b, preferred_element_type=F32, **kw)


def _dot_nt(a, b, **kw):
    return lax.dot_general(a, b, (((1,), (1,)), ((), ())), preferred_element_type=F32, **kw)


def _dot_tn(a, b, **kw):
    return lax.dot_general(a, b, (((0,), (0,)), ((), ())), preferred_element_type=F32, **kw)


def _mod_kernel(c_ref, w_ref, b_ref, o_ref):
    cs = _silu(c_ref[...])
    o_ref[...] = _dot(cs, w_ref[...], precision=HIGHEST) + b_ref[...]


def _mod(c, w_ada, b_ada):
    n = w_ada.shape[1]
    tn = 1024
    c8 = jnp.broadcast_to(c, (8, D_MODEL))
    out = pl.pallas_call(
        _mod_kernel,
        out_shape=jax.ShapeDtypeStruct((8, n), F32),
        grid=(n // tn,),
        in_specs=[pl.BlockSpec((8, D_MODEL), lambda j: (0, 0)),
                  pl.BlockSpec((D_MODEL, tn), lambda j: (0, j)),
                  pl.BlockSpec((1, tn), lambda j: (0, j))],
        out_specs=pl.BlockSpec((8, tn), lambda j: (0, j)),
        compiler_params=_params(("arbitrary",)),
        name="mod",
    )(c8, w_ada, b_ada.reshape(1, n))
    return out[0].reshape(N_MOD, 1, D_MODEL)


def _ffn_kernel(h_ref, gpre_ref, sh_ref, sc_ref, gate_ref, gpost_ref, wi_ref, wo_ref, o_ref):
    h = h_ref[...]
    u = (_rms(h, gpre_ref[...]) * (1.0 + sc_ref[...]) + sh_ref[...]).astype(BF16)
    acc = None
    for lo in range(0, FFN_HIDDEN, FFN_CHUNK):
        a = _dot(u, wi_ref[:, lo:lo + FFN_CHUNK])
        b = _dot(u, wi_ref[:, FFN_HIDDEN + lo:FFN_HIDDEN + lo + FFN_CHUNK])
        part = _dot((_silu(a) * b).astype(BF16), wo_ref[lo:lo + FFN_CHUNK, :])
        acc = part if acc is None else acc + part
    o_ref[...] = h + (FFN_RES * gate_ref[...]) * _rms(acc, gpost_ref[...])


def _ffn(h, gpre, sh, sc, gate, gpost, wi, wo, tm):
    s = h.shape[0]
    row = lambda i: (i, 0)
    vec = pl.BlockSpec((1, D_MODEL), lambda i: (0, 0))
    resident = lambda a: pl.BlockSpec(a.shape, lambda i: (0, 0), pipeline_mode=pl.Buffered(1))
    return pl.pallas_call(
        _ffn_kernel,
        out_shape=jax.ShapeDtypeStruct((s, D_MODEL), F32),
        grid=(s // tm,),
        in_specs=[pl.BlockSpec((tm, D_MODEL), row), vec, vec, vec, vec, vec,
                  resident(wi), resident(wo)],
        out_specs=pl.BlockSpec((tm, D_MODEL), row),
        compiler_params=_params(("parallel",)),
        name="ffn",
    )(h, gpre, sh, sc, gate, gpost, wi, wo)


def _inproj_kernel(h_ref, gpre_ref, sh_ref, sc_ref, w_ref, wg_ref, wdt_ref, o_ref, dt_ref, u_sc):
    j = pl.program_id(1)

    @pl.when(j == 0)
    def _():
        u = (_rms(h_ref[...], gpre_ref[...]) * (1.0 + sc_ref[...]) + sh_ref[...]).astype(BF16)
        u_sc[...] = u
        dt_ref[...] = _dot(u, wdt_ref[...])
        o_ref[...] = _dot(u, w_ref[...]).astype(o_ref.dtype)

    @pl.when(jnp.logical_and(j > 0, j < MAIN_TILES))
    def _():
        o_ref[...] = _dot(u_sc[...], w_ref[...]).astype(o_ref.dtype)

    @pl.when(j >= MAIN_TILES)
    def _():
        o_ref[...] = _dot(u_sc[...], wg_ref[...]).astype(o_ref.dtype)


def _inproj(h, gpre, sh, sc, w, wg, wdt, tm):
    s = h.shape[0]
    tn = PROJ_TILE
    vec = pl.BlockSpec((1, D_MODEL), lambda i, j: (0, 0))
    return pl.pallas_call(
        _inproj_kernel,
        out_shape=(jax.ShapeDtypeStruct((s, PROJ_W), BF16),
                   jax.ShapeDtypeStruct((s, DT_PAD), F32)),
        grid=(s // tm, PROJ_W // tn),
        in_specs=[pl.BlockSpec((tm, D_MODEL), lambda i, j: (i, 0)), vec, vec, vec,
                  pl.BlockSpec((D_MODEL, tn), lambda i, j: (0, jnp.minimum(j, MAIN_TILES - 1))),
                  pl.BlockSpec((D_MODEL, tn), lambda i, j: (0, jnp.maximum(j - MAIN_TILES, 0))),
                  pl.BlockSpec((D_MODEL, DT_PAD), lambda i, j: (0, 0))],
        out_specs=(pl.BlockSpec((tm, tn), lambda i, j: (i, j)),
                   pl.BlockSpec((tm, DT_PAD), lambda i, j: (i, 0))),
        scratch_shapes=[pltpu.VMEM((tm, D_MODEL), BF16)],
        compiler_params=_params(("parallel", "arbitrary")),
        name="inproj",
    )(h, gpre, sh, sc, w, wg, wdt)


def _prep_kernel(n_sel, n_tiles, q_ref, k_ref, v_ref, qt_ref, ka_ref, vt_ref, km_sc):
    @pl.when(pl.program_id(1) == 0)
    def _():
        km_sc[...] = jnp.zeros_like(km_sc)

    for u in range(n_tiles):
        t = pl.program_id(1) * n_tiles + u
        rows = slice(u * ATTN_TILE, (u + 1) * ATTN_TILE)
        k = k_ref[rows, :].astype(F32)
        for b in range(BLOCKS_PER_TILE):
            km_sc[pl.ds(t * BLOCKS_PER_TILE + b, 1), :] = jnp.mean(
                k[b * MOBA_BLOCK:(b + 1) * MOBA_BLOCK], axis=0, keepdims=True)

        qt = (q_ref[rows, :].astype(F32) * (HEAD_DIM ** -0.5 * LOG2E)).T
        score = _dot(km_sc[:n_sel, :], qt, precision=HIGHEST)
        blk = lax.broadcasted_iota(jnp.int32, score.shape, 0)
        q_blk = t * BLOCKS_PER_TILE + lax.broadcasted_iota(jnp.int32, score.shape, 1) // MOBA_BLOCK
        s = jnp.where(blk < q_blk, score, -jnp.inf)
        pen = jnp.full(score.shape, MASKED, F32)
        for _ in range(MOBA_TOPK):
            m = jnp.max(s, axis=0, keepdims=True)
            first = jnp.min(jnp.where(s == m, blk, n_sel), axis=0, keepdims=True)
            first = jnp.where(m > -jnp.inf, first, n_sel)
            pick = blk == first
            pen = jnp.where(pick, 0.0, pen)
            s = jnp.where(pick, -jnp.inf, s)
        pen = jnp.where(blk == q_blk, 0.0, pen)
        qt_ref[0, u, :HEAD_DIM, :] = qt.astype(BF16)
        qt_ref[0, u, HEAD_DIM:HEAD_DIM + n_sel, :] = pen.astype(BF16)
        if n_sel < MAX_BLOCKS:
            qt_ref[0, u, HEAD_DIM + n_sel:, :] = jnp.zeros((MAX_BLOCKS - n_sel, ATTN_TILE), BF16)

        lane = lax.broadcasted_iota(jnp.int32, (ATTN_TILE, MAX_BLOCKS), 1)
        k_blk = t * BLOCKS_PER_TILE + lax.broadcasted_iota(jnp.int32, lane.shape, 0) // MOBA_BLOCK
        ka_ref[0, rows, :HEAD_DIM] = k.astype(BF16)
        ka_ref[0, rows, HEAD_DIM:] = jnp.where(lane == k_blk, 1.0, 0.0).astype(BF16)

        ones_row = lax.broadcasted_iota(jnp.int32, (V_ROWS - HEAD_DIM, ATTN_TILE), 0) == 0
        vt_ref[0, u, :HEAD_DIM, :] = v_ref[rows, :].astype(F32).T.astype(BF16)
        vt_ref[0, u, HEAD_DIM:, :] = jnp.where(ones_row, 1.0, 0.0).astype(BF16)


def _prep(proj):
    s = proj.shape[0]
    nt = s // ATTN_TILE
    n = PREP_TILES_PER_STEP if nt % PREP_TILES_PER_STEP == 0 else 1
    blk = lambda col: pl.BlockSpec((n * ATTN_TILE, HEAD_DIM), lambda h, t: (t, col // HEAD_DIM + h))
    n_sel = -(-(s // MOBA_BLOCK) // BF16_SUBLANES) * BF16_SUBLANES
    return pl.pallas_call(
        functools.partial(_prep_kernel, n_sel, n),
        out_shape=(jax.ShapeDtypeStruct((ATTN_HEADS, nt, AUG_DIM, ATTN_TILE), BF16),
                   jax.ShapeDtypeStruct((ATTN_HEADS, s, AUG_DIM), BF16),
                   jax.ShapeDtypeStruct((ATTN_HEADS, nt, V_ROWS, ATTN_TILE), BF16)),
        grid=(ATTN_HEADS, nt // n),
        in_specs=[blk(COL_Q), blk(COL_K), blk(COL_V)],
        out_specs=(pl.BlockSpec((1, n, AUG_DIM, ATTN_TILE), lambda h, t: (h, t, 0, 0)),
                   pl.BlockSpec((1, n * ATTN_TILE, AUG_DIM), lambda h, t: (h, t, 0)),
                   pl.BlockSpec((1, n, V_ROWS, ATTN_TILE), lambda h, t: (h, t, 0, 0))),
        scratch_shapes=[pltpu.VMEM((MAX_BLOCKS, HEAD_DIM), F32)],
        compiler_params=_params(("parallel", "arbitrary")),
        name="prep",
    )(proj, proj, proj)


def _t5_bucket(rel):
    n = jnp.maximum(rel, 0)
    max_exact = REL_BUCKETS // 2
    nf = jnp.maximum(n, 1).astype(F32)
    large = max_exact + (jnp.log(nf / max_exact) / math.log(REL_MAX_DIST / max_exact)
                         * (REL_BUCKETS - max_exact)).astype(jnp.int32)
    large = jnp.minimum(large, REL_BUCKETS - 1)
    return jnp.where(n < max_exact, n, large)


def _bias_kernel(tab_ref, o_ref):
    h = pl.program_id(0)
    shape = (ATTN_TILE, ATTN_TILE)
    far = tab_ref[h * REL_BUCKETS + REL_BUCKETS - 1]
    bucket = _t5_bucket(lax.broadcasted_iota(jnp.int32, (1, ATTN_TILE), 1))
    val = jnp.zeros((1, ATTN_TILE), F32)
    for b in range(REL_BUCKETS):
        val = jnp.where(bucket == b, tab_ref[h * REL_BUCKETS + b], val)
    val = (val - far) * LOG2E
    toeplitz = pltpu.roll(jnp.broadcast_to(val, shape), 0, 1, stride=1, stride_axis=0)
    ki = lax.broadcasted_iota(jnp.int32, shape, 0)
    qi = lax.broadcasted_iota(jnp.int32, shape, 1)
    o_ref[0, 0] = jnp.where(qi >= ki, toeplitz, MASKED)
    o_ref[0, 1] = jnp.where(qi < ki, toeplitz, 0.0)


def _bias_tiles(tab_flat):
    return pl.pallas_call(
        _bias_kernel,
        out_shape=jax.ShapeDtypeStruct((ATTN_HEADS, 2, ATTN_TILE, ATTN_TILE), F32),
        grid=(ATTN_HEADS,),
        in_specs=[pl.BlockSpec(memory_space=pltpu.SMEM)],
        out_specs=pl.BlockSpec((1, 2, ATTN_TILE, ATTN_TILE), lambda h: (h, 0, 0, 0)),
        compiler_params=_params(("parallel",)),
        name="bias",
    )(tab_flat)


def _attn_tile(t, qt, ka_ref, vt_ref, t_ref, m_sc, acc_sc, s_sc):
    def scores(j):
        start = pl.multiple_of(j * ATTN_TILE, ATTN_TILE)
        return _dot(ka_ref[0, pl.ds(start, ATTN_TILE), :], qt)

    m_sc[...] = jnp.full(m_sc.shape, 4.0 * MASKED, F32)
    acc_sc[...] = jnp.zeros_like(acc_sc)

    def colmax(s_ref):
        return jnp.max(s_ref[...], axis=0, keepdims=True)

    def consume(s_ref, s_max, j):
        m_old = m_sc[...]
        m_new = jnp.maximum(m_old, s_max)
        m_sc[...] = m_new
        p = jnp.exp2(s_ref[...] - m_new).astype(BF16)
        acc_sc[...] = jnp.exp2(m_old - m_new) * acc_sc[...] + _dot(vt_ref[0, j], p)

    def consume_near(s_prev, s_diag):
        m_old = m_sc[...]
        m_new = jnp.maximum(m_old, jnp.maximum(colmax(s_prev), colmax(s_diag)))
        m_sc[...] = m_new
        p_prev = jnp.exp2(s_prev[...] - m_new).astype(BF16)
        p_diag = jnp.exp2(s_diag[...] - m_new).astype(BF16)
        acc_sc[...] = (jnp.exp2(m_old - m_new) * acc_sc[...]
                       + _dot(vt_ref[0, t - 1], p_prev) + _dot(vt_ref[0, t], p_diag))

    def far_step(j, cur, nxt, max_cur):
        nxt[...] = scores(j + 1)
        consume(cur, max_cur, j)
        return colmax(nxt)

    s_a, s_b = s_sc.at[0], s_sc.at[1]

    @pl.when(t == 0)
    def _():
        s_a[...] = scores(0) + t_ref[0, 0]
        consume(s_a, colmax(s_a), 0)

    @pl.when(t >= 1)
    def _():
        n_far = t - 1
        s_a[...] = scores(0)

        bufs = (s_a, s_b)

        def steps(first, count, max_cur):
            for k in range(count):
                max_cur = far_step(first + k, bufs[k % 2], bufs[(k + 1) % 2], max_cur)
            return max_cur

        max_cur = lax.fori_loop(0, n_far // FAR_UNROLL,
                                lambda i, m: steps(FAR_UNROLL * i, FAR_UNROLL, m), colmax(s_a))
        done = n_far // FAR_UNROLL * FAR_UNROLL

        for rem in range(FAR_UNROLL):
            @pl.when(n_far - done == rem)
            def _(rem=rem):
                steps(done, rem, max_cur)
                prev, free = bufs[rem % 2], bufs[(rem + 1) % 2]
                free[...] = scores(t) + t_ref[0, 0]
                lo = ATTN_TILE - REL_MAX_DIST
                prev[lo:, :REL_MAX_DIST] = prev[lo:, :REL_MAX_DIST] + t_ref[0, 1, lo:, :REL_MAX_DIST]
                consume_near(prev, free)

    acc = acc_sc[...]
    out = acc[:HEAD_DIM] / acc[HEAD_DIM:HEAD_DIM + 1]
    return out.T


def _attn_kernel(n_tiles, qt_ref, ka_ref, vt_ref, t_ref, o_ref, *scratch):
    for u in range(n_tiles):
        out = _attn_tile(pl.program_id(1) * n_tiles + u, qt_ref[0, u], ka_ref, vt_ref, t_ref, *scratch)
        o_ref[u * ATTN_TILE:(u + 1) * ATTN_TILE, :] = out.astype(o_ref.dtype)


def _attn(qt, ka, vt, tiles):
    nt = qt.shape[1]
    s = nt * ATTN_TILE
    n = ATTN_TILES_PER_STEP if nt % ATTN_TILES_PER_STEP == 0 else 1
    return pl.pallas_call(
        functools.partial(_attn_kernel, n),
        out_shape=jax.ShapeDtypeStruct((s, ATTN_WIDTH), BF16),
        grid=(ATTN_HEADS, nt // n),
        in_specs=[pl.BlockSpec((1, n, AUG_DIM, ATTN_TILE), lambda h, t: (h, t, 0, 0)),
                  pl.BlockSpec((1, s, AUG_DIM), lambda h, t: (h, 0, 0)),
                  pl.BlockSpec((1, nt, V_ROWS, ATTN_TILE), lambda h, t: (h, 0, 0, 0)),
                  pl.BlockSpec((1, 2, ATTN_TILE, ATTN_TILE), lambda h, t: (h, 0, 0, 0))],
        out_specs=pl.BlockSpec((n * ATTN_TILE, HEAD_DIM), lambda h, t: (t, h)),
        scratch_shapes=[pltpu.VMEM((1, ATTN_TILE), F32), pltpu.VMEM((V_ROWS, ATTN_TILE), F32),
                        pltpu.VMEM((2, ATTN_TILE, ATTN_TILE), F32)],
        compiler_params=_params(("parallel", "arbitrary")),
        name="attn",
    )(qt, ka, vt, tiles)


def _expand_heads(d, lane_head):
    out = d[:, SSM_HPG - 1:SSM_HPG]
    for hg in range(SSM_HPG - 2, -1, -1):
        out = jnp.where(lane_head == hg, d[:, hg:hg + 1], out)
    return out


def _cumsum_lanes(x, triu):
    hi = x.astype(BF16)
    rest = x - hi.astype(F32)
    mid = rest.astype(BF16)
    lo = (rest - mid.astype(F32)).astype(BF16)
    return _dot(hi, triu) + _dot(mid, triu) + _dot(lo, triu)


def _ssd_kernel(z_ref, x_ref, b_ref, c_ref, dtc_ref, wx_ref, wb_ref, wc_ref, bx_ref, bb_ref, bc_ref,
                dtb_ref, alog_ref, dskip_ref, nw_ref, o_ref, xpx_sc, xpb_sc, xpc_sc, st_sc):
    c = pl.program_id(1)
    L, GW, NS, hp = SSM_CHUNK, SSM_GROUP_W, SSM_STATE, SSM_HPG

    @pl.when(c == 0)
    def _():
        xpx_sc[0:CONV_HALO, :] = jnp.zeros((CONV_HALO, xpx_sc.shape[1]), F32)
        xpb_sc[0:CONV_HALO, :] = jnp.zeros((CONV_HALO, xpb_sc.shape[1]), F32)
        xpc_sc[0:CONV_HALO, :] = jnp.zeros((CONV_HALO, xpc_sc.shape[1]), F32)
        st_sc[...] = jnp.zeros_like(st_sc)

    def conv(src_ref, pad_sc, w_ref, bias_ref):
        pad_sc[CONV_HALO:CONV_HALO + L, :] = src_ref[...].astype(F32)
        acc = bias_ref[...]
        for j in range(SSM_CONV):
            lo = CONV_HALO - (SSM_CONV - 1) + j
            acc = acc + pad_sc[lo:lo + L, :] * w_ref[j:j + 1, :]
        pad_sc[0:CONV_HALO, :] = pad_sc[L:L + CONV_HALO, :]
        return _silu(acc)

    x_all = conv(x_ref, xpx_sc, wx_ref, bx_ref)
    b_all = conv(b_ref, xpb_sc, wb_ref, bb_ref)
    c_all = conv(c_ref, xpc_sc, wc_ref, bc_ref)

    row = lax.broadcasted_iota(jnp.int32, (L, L), 0)
    col = lax.broadcasted_iota(jnp.int32, (L, L), 1)
    causal = row >= col
    triu = jnp.where(row <= col, 1.0, 0.0).astype(BF16)
    lane_head = lax.broadcasted_iota(jnp.int32, (1, GW), 1) // SSM_HEAD_DIM

    for gi in range(SSD_GROUPS_PER_STEP):
        x = x_all[:, gi * GW:(gi + 1) * GW]
        bm = b_all[:, gi * NS:(gi + 1) * NS].astype(BF16)
        cm = c_all[:, gi * NS:(gi + 1) * NS].astype(BF16)

        dt_c = _softplus(dtc_ref[gi] + dtb_ref[gi])
        a_c = -jnp.exp(alog_ref[gi])
        cum_c = _cumsum_lanes(dt_c * a_c, triu)
        last_c = cum_c[:, L - 1:L]
        small = jnp.concatenate(
            [cum_c, jnp.exp(cum_c), jnp.exp(last_c - cum_c) * dt_c,
             jnp.zeros((SMALL_ROWS - 3 * hp, L), F32)], axis=0).T
        cum_r = small[:, 0:hp]
        grow_x = _expand_heads(small[:, hp:2 * hp], lane_head)
        end_x = _expand_heads(small[:, 2 * hp:3 * hp], lane_head)
        last_x = _expand_heads(small[L - 1:L, hp:2 * hp], lane_head)
        src_c = cum_c - jnp.log(dt_c)

        cb = _dot_nt(cm, bm)
        w_parts, x_parts = [], []
        for hg in range(hp):
            seg = cum_r[:, hg:hg + 1] - src_c[hg:hg + 1, :]
            w_parts.append(cb * jnp.exp(jnp.where(causal, seg, -jnp.inf)))
            x_parts.append(jnp.where(lane_head == hg, x, 0.0))
        y = _dot(jnp.concatenate(w_parts, axis=1).astype(BF16),
                 jnp.concatenate(x_parts, axis=0).astype(BF16))

        st = st_sc[gi]
        y = y + _dot(cm, st.astype(BF16)) * grow_x
        st_sc[gi] = last_x * st + _dot_tn(bm, (x * end_x).astype(BF16))
        y = y + x * dskip_ref[gi]

        g = y * _silu(z_ref[:, gi * GW:(gi + 1) * GW].astype(F32))
        g = g * lax.rsqrt(jnp.mean(g * g, axis=-1, keepdims=True) + RMS_EPS)
        o_ref[:, gi * GW:(gi + 1) * GW] = (g * nw_ref[gi]).astype(o_ref.dtype)


def _ssd(proj, dt_cols, conv_w, conv_b, dtb, alog, dskip_x, norm_w, nc):
    s = proj.shape[0]
    n = SSD_GROUPS_PER_STEP
    L, GW, NS = SSM_CHUNK, n * SSM_GROUP_W, n * SSM_STATE
    xoff, boff, coff = 0, SSM_INNER, SSM_INNER + SSM_GROUPS * SSM_STATE
    per_step = lambda a: pl.BlockSpec((n,) + a.shape[1:], lambda g, c: (g, 0, 0))
    return pl.pallas_call(
        _ssd_kernel,
        out_shape=jax.ShapeDtypeStruct((s, SSM_INNER), BF16),
        grid=(SSM_GROUPS // n, nc),
        in_specs=[
            pl.BlockSpec((L, GW), lambda g, c: (c, COL_Z // GW + g)),
            pl.BlockSpec((L, GW), lambda g, c: (c, COL_X // GW + g)),
            pl.BlockSpec((L, NS), lambda g, c: (c, COL_B // NS + g)),
            pl.BlockSpec((L, NS), lambda g, c: (c, COL_C // NS + g)),
            pl.BlockSpec((n, SSM_HPG, L), lambda g, c: (g, 0, c)),
            pl.BlockSpec((SSM_CONV, GW), lambda g, c: (0, xoff // GW + g)),
            pl.BlockSpec((SSM_CONV, NS), lambda g, c: (0, boff // NS + g)),
            pl.BlockSpec((SSM_CONV, NS), lambda g, c: (0, coff // NS + g)),
            pl.BlockSpec((1, GW), lambda g, c: (0, xoff // GW + g)),
            pl.BlockSpec((1, NS), lambda g, c: (0, boff // NS + g)),
            pl.BlockSpec((1, NS), lambda g, c: (0, coff // NS + g)),
            per_step(dtb), per_step(alog), per_step(dskip_x), per_step(norm_w),
        ],
        out_specs=pl.BlockSpec((L, GW), lambda g, c: (c, g)),
        scratch_shapes=[pltpu.VMEM((L + CONV_HALO, GW), F32), pltpu.VMEM((L + CONV_HALO, NS), F32),
                        pltpu.VMEM((L + CONV_HALO, NS), F32),
                        pltpu.VMEM((n, SSM_STATE, SSM_GROUP_W), F32)],
        compiler_params=_params(("parallel", "arbitrary")),
        name="ssd",
    )(proj, proj, proj, proj, dt_cols, conv_w, conv_w, conv_w, conv_b, conv_b, conv_b,
      dtb, alog, dskip_x, norm_w)


def _merge_kernel(h_ref, a_ref, b_ref, ga_ref, gb_ref, pa_ref, pb_ref, wo_ref, gate_ref, gpost_ref,
                  o_ref):
    ya = _dot(a_ref[...], pa_ref[...])
    yb = _dot(b_ref[...], pb_ref[...])
    mix = _sigmoid(ga_ref[...].astype(F32)) * ya + _sigmoid(gb_ref[...].astype(F32)) * yb
    y = _dot(mix.astype(BF16), wo_ref[...])
    o_ref[...] = h_ref[...] + gate_ref[...] * _rms(y, gpost_ref[...])


def _merge(h, attn, ssd, proj, pa, pb, wo, gate, gpost, tm):
    s = h.shape[0]
    vec = pl.BlockSpec((1, D_MODEL), lambda i: (0, 0))
    full = lambda a: pl.BlockSpec(a.shape, lambda i: (0, 0))
    return pl.pallas_call(
        _merge_kernel,
        out_shape=jax.ShapeDtypeStruct((s, D_MODEL), F32),
        grid=(s // tm,),
        in_specs=[pl.BlockSpec((tm, D_MODEL), lambda i: (i, 0)),
                  pl.BlockSpec((tm, ATTN_WIDTH), lambda i: (i, 0)),
                  pl.BlockSpec((tm, SSM_INNER), lambda i: (i, 0)),
                  pl.BlockSpec((tm, D_MODEL), lambda i: (i, COL_GA // D_MODEL)),
                  pl.BlockSpec((tm, D_MODEL), lambda i: (i, COL_GB // D_MODEL)),
                  full(pa), full(pb), full(wo), vec, vec],
        out_specs=pl.BlockSpec((tm, D_MODEL), lambda i: (i, 0)),
        compiler_params=_params(("parallel",)),
        name="merge",
    )(h, attn, ssd, proj, proj, pa, pb, wo, gate, gpost)


def _layer(h, mod, rel_bias, p):
    s = h.shape[0]
    nc = s // SSM_CHUNK
    tm = min(512, s)
    tm_wide = min(1024, s)
    sh1, sc1, g1, shm, scm, gm, sh2, sc2, g2 = [mod[k] for k in range(N_MOD)]
    vec = lambda a: a.reshape(1, -1)

    h = _ffn(h, vec(p["ffn1_norm_pre"]), sh1, sc1, g1, vec(p["ffn1_norm_post"]),
             p["ffn1_w_in"].astype(BF16), p["ffn1_w_out"].astype(BF16), tm=tm_wide)

    w_in = p["w_in_mix"].astype(BF16)
    dt_lo = COL_GA
    w_gates = w_in[:, dt_lo + SSM_HEADS:]
    w_dt = jnp.pad(w_in[:, dt_lo:dt_lo + SSM_HEADS], ((0, 0), (0, DT_PAD - SSM_HEADS)))
    proj, dt_raw = _inproj(h, vec(p["mix_norm_pre"]), shm, scm, w_in, w_gates, w_dt, tm=min(2048, s))

    tab_flat = rel_bias.T.reshape(-1)
    qt, ka, vt = _prep(proj)
    tiles = _bias_tiles(tab_flat)
    attn = _attn(qt, ka, vt, tiles)

    dt_cols = dt_raw[:, :SSM_HEADS].reshape(s, SSM_GROUPS, SSM_HPG).transpose(1, 2, 0)
    per_group = lambda a: a.reshape(SSM_GROUPS, SSM_HPG, 1)
    dskip_x = jnp.repeat(p["d_skip"], SSM_HEAD_DIM).reshape(SSM_GROUPS, 1, SSM_GROUP_W)
    ssd = _ssd(proj, dt_cols, p["conv_w"], vec(p["conv_b"]),
               per_group(p["dt_bias"]), per_group(p["a_log"]),
               dskip_x, p["ssm_norm_w"].reshape(SSM_GROUPS, 1, SSM_GROUP_W), nc)

    h = _merge(h, attn, ssd, proj, p["proj_a"].astype(BF16), p["proj_b"].astype(BF16),
               p["w_out_mix"].astype(BF16), gm, vec(p["mix_norm_post"]), tm=tm)

    h = _ffn(h, vec(p["ffn2_norm_pre"]), sh2, sc2, g2, vec(p["ffn2_norm_post"]),
             p["ffn2_w_in"].astype(BF16), p["ffn2_w_out"].astype(BF16), tm=tm_wide)
    return h


_LAYER_KEYS = ("ffn1_norm_pre", "ffn1_w_in", "ffn1_w_out", "ffn1_norm_post", "mix_norm_pre",
               "w_in_mix", "conv_w", "conv_b", "dt_bias", "a_log", "d_skip", "ssm_norm_w",
               "proj_a", "proj_b", "w_out_mix", "mix_norm_post",
               "ffn2_norm_pre", "ffn2_w_in", "ffn2_w_out", "ffn2_norm_post")


def kernel(x, c, w_ada, b_ada, ffn1_norm_pre, ffn1_w_in, ffn1_w_out, ffn1_norm_post, mix_norm_pre,
           w_in_mix, rel_bias, conv_w, conv_b, dt_bias, a_log, d_skip, ssm_norm_w, proj_a, proj_b,
           w_out_mix, mix_norm_post, ffn2_norm_pre, ffn2_w_in, ffn2_w_out, ffn2_norm_post):
    stacked = dict(ffn1_norm_pre=ffn1_norm_pre, ffn1_w_in=ffn1_w_in, ffn1_w_out=ffn1_w_out,
                   ffn1_norm_post=ffn1_norm_post, mix_norm_pre=mix_norm_pre, w_in_mix=w_in_mix,
                   conv_w=conv_w, conv_b=conv_b, dt_bias=dt_bias, a_log=a_log, d_skip=d_skip,
                   ssm_norm_w=ssm_norm_w, proj_a=proj_a, proj_b=proj_b, w_out_mix=w_out_mix,
                   mix_norm_post=mix_norm_post, ffn2_norm_pre=ffn2_norm_pre, ffn2_w_in=ffn2_w_in,
                   ffn2_w_out=ffn2_w_out, ffn2_norm_post=ffn2_norm_post)
    batch, seq, _ = x.shape
    assert seq % ATTN_TILE == 0 and seq // MOBA_BLOCK <= MAX_BLOCKS and seq % SSM_CHUNK == 0
    depth = w_ada.shape[0]
    outs = []
    for b in range(batch):
        h = x[b]
        for l in range(depth):
            mod = _mod(c[b:b + 1], w_ada[l], b_ada[l])
            h = _layer(h, mod, rel_bias, {k: stacked[k][l] for k in _LAYER_KEYS})
        outs.append(h)
    return outs[0][None] if batch == 1 else jnp.stack(outs)
```

```python
import functools
import math

import jax
import jax.numpy as jnp
from jax import lax
from jax.experimental import pallas as pl
from jax.experimental.pallas import tpu as pltpu

F32 = jnp.float32
BF16 = jnp.bfloat16
HIGHEST = lax.Precision.HIGHEST

D_MODEL = 1024
N_MOD = 9
RMS_EPS = 1e-6
FFN_HIDDEN = 2816
FFN_RES = 0.5
FFN_CHUNK = FFN_HIDDEN // 2

ATTN_HEADS = 8
HEAD_DIM = 128
ATTN_WIDTH = ATTN_HEADS * HEAD_DIM
MOBA_BLOCK = 256
MOBA_TOPK = 3
MAX_BLOCKS = 128
AUG_DIM = HEAD_DIM + MAX_BLOCKS
REL_BUCKETS = 32
REL_MAX_DIST = 128
MASKED = -1e30
LOG2E = math.log2(math.e)
ATTN_TILE = 512
BLOCKS_PER_TILE = ATTN_TILE // MOBA_BLOCK
BF16_SUBLANES = 16
V_ROWS = HEAD_DIM + BF16_SUBLANES
ATTN_TILES_PER_STEP = 1
FAR_UNROLL = 8
PREP_TILES_PER_STEP = 4

SSM_INNER = 2048
SSM_HEAD_DIM = 64
SSM_GROUPS = 8
SSM_HEADS = SSM_INNER // SSM_HEAD_DIM
SSM_HPG = SSM_HEADS // SSM_GROUPS
SSM_GROUP_W = SSM_INNER // SSM_GROUPS
SSM_STATE = 128
SSM_CONV = 4
SSM_CHUNK = 256
CONV_HALO = 8
SMALL_ROWS = 128
SSD_GROUPS_PER_STEP = 2

COL_Q = 0
COL_K = COL_Q + ATTN_WIDTH
COL_V = COL_K + ATTN_WIDTH
COL_Z = COL_V + ATTN_WIDTH
COL_X = COL_Z + SSM_INNER
COL_B = COL_X + SSM_INNER
COL_C = COL_B + SSM_GROUPS * SSM_STATE
COL_GA = COL_C + SSM_GROUPS * SSM_STATE
COL_GB = COL_GA + D_MODEL
PROJ_W = COL_GB + D_MODEL
PROJ_TILE = 1024
MAIN_TILES = COL_GA // PROJ_TILE
DT_PAD = 128

VMEM_LIMIT = 56 * 1024 * 1024


def _params(sem):
    return pltpu.CompilerParams(dimension_semantics=sem, vmem_limit_bytes=VMEM_LIMIT)


def _sigmoid(x):
    return 0.5 + 0.5 * jnp.tanh(0.5 * x)


def _silu(x):
    return x * _sigmoid(x)


def _softplus(x):
    return jnp.maximum(x, 0.0) + jnp.log(1.0 + jnp.exp(-jnp.abs(x)))


def _rms(x, g):
    return x * lax.rsqrt(jnp.mean(x * x, axis=-1, keepdims=True) + RMS_EPS) * g


def _dot(a, b, **kw):
    return jnp.dot(a, b, preferred_element_type=F32, **kw)


def _dot_nt(a, b, **kw):
    return lax.dot_general(a, b, (((1,), (1,)), ((), ())), preferred_element_type=F32, **kw)


def _dot_tn(a, b, **kw):
    return lax.dot_general(a, b, (((0,), (0,)), ((), ())), preferred_element_type=F32, **kw)


def _mod_kernel(c_ref, w_ref, b_ref, o_ref):
    cs = _silu(c_ref[...])
    o_ref[...] = _dot(cs, w_ref[...], precision=HIGHEST) + b_ref[...]


def _mod(c, w_ada, b_ada):
    n = w_ada.shape[1]
    tn = 1024
    c8 = jnp.broadcast_to(c, (8, D_MODEL))
    out = pl.pallas_call(
        _mod_kernel,
        out_shape=jax.ShapeDtypeStruct((8, n), F32),
        grid=(n // tn,),
        in_specs=[pl.BlockSpec((8, D_MODEL), lambda j: (0, 0)),
                  pl.BlockSpec((D_MODEL, tn), lambda j: (0, j)),
                  pl.BlockSpec((1, tn), lambda j: (0, j))],
        out_specs=pl.BlockSpec((8, tn), lambda j: (0, j)),
        compiler_params=_params(("arbitrary",)),
        name="mod",
    )(c8, w_ada, b_ada.reshape(1, n))
    return out[0].reshape(N_MOD, 1, D_MODEL)


def _ffn_kernel(h_ref, gpre_ref, sh_ref, sc_ref, gate_ref, gpost_ref, wi_ref, wo_ref, o_ref):
    h = h_ref[...]
    u = (_rms(h, gpre_ref[...]) * (1.0 + sc_ref[...]) + sh_ref[...]).astype(BF16)
    acc = None
    for lo in range(0, FFN_HIDDEN, FFN_CHUNK):
        a = _dot(u, wi_ref[:, lo:lo + FFN_CHUNK])
        b = _dot(u, wi_ref[:, FFN_HIDDEN + lo:FFN_HIDDEN + lo + FFN_CHUNK])
        part = _dot((_silu(a) * b).astype(BF16), wo_ref[lo:lo + FFN_CHUNK, :])
        acc = part if acc is None else acc + part
    o_ref[...] = h + (FFN_RES * gate_ref[...]) * _rms(acc, gpost_ref[...])


def _ffn(h, gpre, sh, sc, gate, gpost, wi, wo, tm):
    s = h.shape[0]
    row = lambda i: (i, 0)
    vec = pl.BlockSpec((1, D_MODEL), lambda i: (0, 0))
    resident = lambda a: pl.BlockSpec(a.shape, lambda i: (0, 0), pipeline_mode=pl.Buffered(1))
    return pl.pallas_call(
        _ffn_kernel,
        out_shape=jax.ShapeDtypeStruct((s, D_MODEL), F32),
        grid=(s // tm,),
        in_specs=[pl.BlockSpec((tm, D_MODEL), row), vec, vec, vec, vec, vec,
                  resident(wi), resident(wo)],
        out_specs=pl.BlockSpec((tm, D_MODEL), row),
        compiler_params=_params(("parallel",)),
        name="ffn",
    )(h, gpre, sh, sc, gate, gpost, wi, wo)


def _inproj_kernel(h_ref, gpre_ref, sh_ref, sc_ref, w_ref, wg_ref, wdt_ref, o_ref, dt_ref, u_sc):
    j = pl.program_id(1)

    @pl.when(j == 0)
    def _():
        u = (_rms(h_ref[...], gpre_ref[...]) * (1.0 + sc_ref[...]) + sh_ref[...]).astype(BF16)
        u_sc[...] = u
        dt_ref[...] = _dot(u, wdt_ref[...])
        o_ref[...] = _dot(u, w_ref[...]).astype(o_ref.dtype)

    @pl.when(jnp.logical_and(j > 0, j < MAIN_TILES))
    def _():
        o_ref[...] = _dot(u_sc[...], w_ref[...]).astype(o_ref.dtype)

    @pl.when(j >= MAIN_TILES)
    def _():
        o_ref[...] = _dot(u_sc[...], wg_ref[...]).astype(o_ref.dtype)


def _inproj(h, gpre, sh, sc, w, wg, wdt, tm):
    s = h.shape[0]
    tn = PROJ_TILE
    vec = pl.BlockSpec((1, D_MODEL), lambda i, j: (0, 0))
    return pl.pallas_call(
        _inproj_kernel,
        out_shape=(jax.ShapeDtypeStruct((s, PROJ_W), BF16),
                   jax.ShapeDtypeStruct((s, DT_PAD), F32)),
        grid=(s // tm, PROJ_W // tn),
        in_specs=[pl.BlockSpec((tm, D_MODEL), lambda i, j: (i, 0)), vec, vec, vec,
                  pl.BlockSpec((D_MODEL, tn), lambda i, j: (0, jnp.minimum(j, MAIN_TILES - 1))),
                  pl.BlockSpec((D_MODEL, tn), lambda i, j: (0, jnp.maximum(j - MAIN_TILES, 0))),
                  pl.BlockSpec((D_MODEL, DT_PAD), lambda i, j: (0, 0))],
        out_specs=(pl.BlockSpec((tm, tn), lambda i, j: (i, j)),
                   pl.BlockSpec((tm, DT_PAD), lambda i, j: (i, 0))),
        scratch_shapes=[pltpu.VMEM((tm, D_MODEL), BF16)],
        compiler_params=_params(("parallel", "arbitrary")),
        name="inproj",
    )(h, gpre, sh, sc, w, wg, wdt)


def _prep_kernel(n_sel, n_tiles, q_ref, k_ref, v_ref, qt_ref, ka_ref, vt_ref, km_sc):
    @pl.when(pl.program_id(1) == 0)
    def _():
        km_sc[...] = jnp.zeros_like(km_sc)

    for u in range(n_tiles):
        t = pl.program_id(1) * n_tiles + u
        rows = slice(u * ATTN_TILE, (u + 1) * ATTN_TILE)
        k = k_ref[rows, :].astype(F32)
        for b in range(BLOCKS_PER_TILE):
            km_sc[pl.ds(t * BLOCKS_PER_TILE + b, 1), :] = jnp.mean(
                k[b * MOBA_BLOCK:(b + 1) * MOBA_BLOCK], axis=0, keepdims=True)

        qt = (q_ref[rows, :].astype(F32) * (HEAD_DIM ** -0.5 * LOG2E)).T
        score = _dot(km_sc[:n_sel, :], qt, precision=HIGHEST)
        blk = lax.broadcasted_iota(jnp.int32, score.shape, 0)
        q_blk = t * BLOCKS_PER_TILE + lax.broadcasted_iota(jnp.int32, score.shape, 1) // MOBA_BLOCK
        s = jnp.where(blk < q_blk, score, -jnp.inf)
        pen = jnp.full(score.shape, MASKED, F32)
        for _ in range(MOBA_TOPK):
            m = jnp.max(s, axis=0, keepdims=True)
            first = jnp.min(jnp.where(s == m, blk, n_sel), axis=0, keepdims=True)
            first = jnp.where(m > -jnp.inf, first, n_sel)
            pick = blk == first
            pen = jnp.where(pick, 0.0, pen)
            s = jnp.where(pick, -jnp.inf, s)
        pen = jnp.where(blk == q_blk, 0.0, pen)
        qt_ref[0, u, :HEAD_DIM, :] = qt.astype(BF16)
        qt_ref[0, u, HEAD_DIM:HEAD_DIM + n_sel, :] = pen.astype(BF16)
        if n_sel < MAX_BLOCKS:
            qt_ref[0, u, HEAD_DIM + n_sel:, :] = jnp.zeros((MAX_BLOCKS - n_sel, ATTN_TILE), BF16)

        lane = lax.broadcasted_iota(jnp.int32, (ATTN_TILE, MAX_BLOCKS), 1)
        k_blk = t * BLOCKS_PER_TILE + lax.broadcasted_iota(jnp.int32, lane.shape, 0) // MOBA_BLOCK
        ka_ref[0, rows, :HEAD_DIM] = k.astype(BF16)
        ka_ref[0, rows, HEAD_DIM:] = jnp.where(lane == k_blk, 1.0, 0.0).astype(BF16)

        ones_row = lax.broadcasted_iota(jnp.int32, (V_ROWS - HEAD_DIM, ATTN_TILE), 0) == 0
        vt_ref[0, u, :HEAD_DIM, :] = v_ref[rows, :].astype(F32).T.astype(BF16)
        vt_ref[0, u, HEAD_DIM:, :] = jnp.where(ones_row, 1.0, 0.0).astype(BF16)


def _prep(proj):
    s = proj.shape[0]
    nt = s // ATTN_TILE
    n = PREP_TILES_PER_STEP if nt % PREP_TILES_PER_STEP == 0 else 1
    blk = lambda col: pl.BlockSpec((n * ATTN_TILE, HEAD_DIM), lambda h, t: (t, col // HEAD_DIM + h))
    n_sel = -(-(s // MOBA_BLOCK) // BF16_SUBLANES) * BF16_SUBLANES
    return pl.pallas_call(
        functools.partial(_prep_kernel, n_sel, n),
        out_shape=(jax.ShapeDtypeStruct((ATTN_HEADS, nt, AUG_DIM, ATTN_TILE), BF16),
                   jax.ShapeDtypeStruct((ATTN_HEADS, s, AUG_DIM), BF16),
                   jax.ShapeDtypeStruct((ATTN_HEADS, nt, V_ROWS, ATTN_TILE), BF16)),
        grid=(ATTN_HEADS, nt // n),
        in_specs=[blk(COL_Q), blk(COL_K), blk(COL_V)],
        out_specs=(pl.BlockSpec((1, n, AUG_DIM, ATTN_TILE), lambda h, t: (h, t, 0, 0)),
                   pl.BlockSpec((1, n * ATTN_TILE, AUG_DIM), lambda h, t: (h, t, 0)),
                   pl.BlockSpec((1, n, V_ROWS, ATTN_TILE), lambda h, t: (h, t, 0, 0))),
        scratch_shapes=[pltpu.VMEM((MAX_BLOCKS, HEAD_DIM), F32)],
        compiler_params=_params(("parallel", "arbitrary")),
        name="prep",
    )(proj, proj, proj)


def _t5_bucket(rel):
    n = jnp.maximum(rel, 0)
    max_exact = REL_BUCKETS // 2
    nf = jnp.maximum(n, 1).astype(F32)
    large = max_exact + (jnp.log(nf / max_exact) / math.log(REL_MAX_DIST / max_exact)
                         * (REL_BUCKETS - max_exact)).astype(jnp.int32)
    large = jnp.minimum(large, REL_BUCKETS - 1)
    return jnp.where(n < max_exact, n, large)


def _bias_kernel(tab_ref, o_ref):
    h = pl.program_id(0)
    shape = (ATTN_TILE, ATTN_TILE)
    far = tab_ref[h * REL_BUCKETS + REL_BUCKETS - 1]
    bucket = _t5_bucket(lax.broadcasted_iota(jnp.int32, (1, ATTN_TILE), 1))
    val = jnp.zeros((1, ATTN_TILE), F32)
    for b in range(REL_BUCKETS):
        val = jnp.where(bucket == b, tab_ref[h * REL_BUCKETS + b], val)
    val = (val - far) * LOG2E
    toeplitz = pltpu.roll(jnp.broadcast_to(val, shape), 0, 1, stride=1, stride_axis=0)
    ki = lax.broadcasted_iota(jnp.int32, shape, 0)
    qi = lax.broadcasted_iota(jnp.int32, shape, 1)
    o_ref[0, 0] = jnp.where(qi >= ki, toeplitz, MASKED)
    o_ref[0, 1] = jnp.where(qi < ki, toeplitz, 0.0)


def _bias_tiles(tab_flat):
    return pl.pallas_call(
        _bias_kernel,
        out_shape=jax.ShapeDtypeStruct((ATTN_HEADS, 2, ATTN_TILE, ATTN_TILE), F32),
        grid=(ATTN_HEADS,),
        in_specs=[pl.BlockSpec(memory_space=pltpu.SMEM)],
        out_specs=pl.BlockSpec((1, 2, ATTN_TILE, ATTN_TILE), lambda h: (h, 0, 0, 0)),
        compiler_params=_params(("parallel",)),
        name="bias",
    )(tab_flat)


def _attn_tile(t, qt, ka_ref, vt_ref, t_ref, m_sc, acc_sc, s_sc):
    def scores(j):
        start = pl.multiple_of(j * ATTN_TILE, ATTN_TILE)
        return _dot(ka_ref[0, pl.ds(start, ATTN_TILE), :], qt)

    m_sc[...] = jnp.full(m_sc.shape, 4.0 * MASKED, F32)
    acc_sc[...] = jnp.zeros_like(acc_sc)

    def colmax(s_ref):
        return jnp.max(s_ref[...], axis=0, keepdims=True)

    def consume(s_ref, s_max, j):
        m_old = m_sc[...]
        m_new = jnp.maximum(m_old, s_max)
        m_sc[...] = m_new
        p = jnp.exp2(s_ref[...] - m_new).astype(BF16)
        acc_sc[...] = jnp.exp2(m_old - m_new) * acc_sc[...] + _dot(vt_ref[0, j], p)

    def consume_near(s_prev, s_diag):
        m_old = m_sc[...]
        m_mid = jnp.maximum(m_old, colmax(s_prev))
        p_prev = jnp.exp2(s_prev[...] - m_mid).astype(BF16)
        acc = jnp.exp2(m_old - m_mid) * acc_sc[...] + _dot(vt_ref[0, t - 1], p_prev)
        m_new = jnp.maximum(m_mid, colmax(s_diag))
        m_sc[...] = m_new
        p_diag = jnp.exp2(s_diag[...] - m_new).astype(BF16)
        acc_sc[...] = jnp.exp2(m_mid - m_new) * acc + _dot(vt_ref[0, t], p_diag)

    def far_step(j, cur, nxt, max_cur):
        nxt[...] = scores(j + 1)
        consume(cur, max_cur, j)
        return colmax(nxt)

    s_a, s_b = s_sc.at[0], s_sc.at[1]

    @pl.when(t == 0)
    def _():
        s_a[...] = scores(0) + t_ref[0, 0]
        consume(s_a, colmax(s_a), 0)

    @pl.when(t >= 1)
    def _():
        n_far = t - 1
        s_a[...] = scores(0)

        bufs = (s_a, s_b)

        def steps(first, count, max_cur):
            for k in range(count):
                max_cur = far_step(first + k, bufs[k % 2], bufs[(k + 1) % 2], max_cur)
            return max_cur

        max_cur = lax.fori_loop(0, n_far // FAR_UNROLL,
                                lambda i, m: steps(FAR_UNROLL * i, FAR_UNROLL, m), colmax(s_a))
        done = n_far // FAR_UNROLL * FAR_UNROLL

        for rem in range(FAR_UNROLL):
            @pl.when(n_far - done == rem)
            def _(rem=rem):
                steps(done, rem, max_cur)
                prev, free = bufs[rem % 2], bufs[(rem + 1) % 2]
                free[...] = scores(t) + t_ref[0, 0]
                lo = ATTN_TILE - REL_MAX_DIST
                prev[lo:, :REL_MAX_DIST] = prev[lo:, :REL_MAX_DIST] + t_ref[0, 1, lo:, :REL_MAX_DIST]
                consume_near(prev, free)

    acc = acc_sc[...]
    out = acc[:HEAD_DIM] / acc[HEAD_DIM:HEAD_DIM + 1]
    return out.T


def _attn_kernel(n_tiles, qt_ref, ka_ref, vt_ref, t_ref, o_ref, *scratch):
    for u in range(n_tiles):
        out = _attn_tile(pl.program_id(1) * n_tiles + u, qt_ref[0, u], ka_ref, vt_ref, t_ref, *scratch)
        o_ref[u * ATTN_TILE:(u + 1) * ATTN_TILE, :] = out.astype(o_ref.dtype)


def _attn(qt, ka, vt, tiles):
    nt = qt.shape[1]
    s = nt * ATTN_TILE
    n = ATTN_TILES_PER_STEP if nt % ATTN_TILES_PER_STEP == 0 else 1
    return pl.pallas_call(
        functools.partial(_attn_kernel, n),
        out_shape=jax.ShapeDtypeStruct((s, ATTN_WIDTH), BF16),
        grid=(ATTN_HEADS, nt // n),
        in_specs=[pl.BlockSpec((1, n, AUG_DIM, ATTN_TILE), lambda h, t: (h, t, 0, 0)),
                  pl.BlockSpec((1, s, AUG_DIM), lambda h, t: (h, 0, 0)),
                  pl.BlockSpec((1, nt, V_ROWS, ATTN_TILE), lambda h, t: (h, 0, 0, 0)),
                  pl.BlockSpec((1, 2, ATTN_TILE, ATTN_TILE), lambda h, t: (h, 0, 0, 0))],
        out_specs=pl.BlockSpec((n * ATTN_TILE, HEAD_DIM), lambda h, t: (t, h)),
        scratch_shapes=[pltpu.VMEM((1, ATTN_TILE), F32), pltpu.VMEM((V_ROWS, ATTN_TILE), F32),
                        pltpu.VMEM((2, ATTN_TILE, ATTN_TILE), F32)],
        compiler_params=_params(("parallel", "arbitrary")),
        name="attn",
    )(qt, ka, vt, tiles)


def _expand_heads(d, lane_head):
    out = d[:, SSM_HPG - 1:SSM_HPG]
    for hg in range(SSM_HPG - 2, -1, -1):
        out = jnp.where(lane_head == hg, d[:, hg:hg + 1], out)
    return out


def _cumsum_lanes(x, triu):
    hi = x.astype(BF16)
    rest = x - hi.astype(F32)
    mid = rest.astype(BF16)
    lo = (rest - mid.astype(F32)).astype(BF16)
    return _dot(hi, triu) + _dot(mid, triu) + _dot(lo, triu)


def _ssd_kernel(z_ref, x_ref, b_ref, c_ref, dtc_ref, wx_ref, wb_ref, wc_ref, bx_ref, bb_ref, bc_ref,
                dtb_ref, alog_ref, dskip_ref, nw_ref, o_ref, xpx_sc, xpb_sc, xpc_sc, st_sc):
    c = pl.program_id(1)
    L, GW, NS, hp = SSM_CHUNK, SSM_GROUP_W, SSM_STATE, SSM_HPG

    @pl.when(c == 0)
    def _():
        xpx_sc[0:CONV_HALO, :] = jnp.zeros((CONV_HALO, xpx_sc.shape[1]), F32)
        xpb_sc[0:CONV_HALO, :] = jnp.zeros((CONV_HALO, xpb_sc.shape[1]), F32)
        xpc_sc[0:CONV_HALO, :] = jnp.zeros((CONV_HALO, xpc_sc.shape[1]), F32)
        st_sc[...] = jnp.zeros_like(st_sc)

    def conv(src_ref, pad_sc, w_ref, bias_ref):
        pad_sc[CONV_HALO:CONV_HALO + L, :] = src_ref[...].astype(F32)
        acc = bias_ref[...]
        for j in range(SSM_CONV):
            lo = CONV_HALO - (SSM_CONV - 1) + j
            acc = acc + pad_sc[lo:lo + L, :] * w_ref[j:j + 1, :]
        pad_sc[0:CONV_HALO, :] = pad_sc[L:L + CONV_HALO, :]
        return _silu(acc)

    x_all = conv(x_ref, xpx_sc, wx_ref, bx_ref)
    b_all = conv(b_ref, xpb_sc, wb_ref, bb_ref)
    c_all = conv(c_ref, xpc_sc, wc_ref, bc_ref)

    row = lax.broadcasted_iota(jnp.int32, (L, L), 0)
    col = lax.broadcasted_iota(jnp.int32, (L, L), 1)
    causal = row >= col
    triu = jnp.where(row <= col, 1.0, 0.0).astype(BF16)
    lane_head = lax.broadcasted_iota(jnp.int32, (1, GW), 1) // SSM_HEAD_DIM

    for gi in range(SSD_GROUPS_PER_STEP):
        x = x_all[:, gi * GW:(gi + 1) * GW]
        bm = b_all[:, gi * NS:(gi + 1) * NS].astype(BF16)
        cm = c_all[:, gi * NS:(gi + 1) * NS].astype(BF16)

        dt_c = _softplus(dtc_ref[gi] + dtb_ref[gi])
        a_c = -jnp.exp(alog_ref[gi])
        cum_c = _cumsum_lanes(dt_c * a_c, triu)
        last_c = cum_c[:, L - 1:L]
        small = jnp.concatenate(
            [cum_c, jnp.exp(cum_c), jnp.exp(last_c - cum_c) * dt_c,
             jnp.zeros((SMALL_ROWS - 3 * hp, L), F32)], axis=0).T
        cum_r = small[:, 0:hp]
        grow_x = _expand_heads(small[:, hp:2 * hp], lane_head)
        end_x = _expand_heads(small[:, 2 * hp:3 * hp], lane_head)
        last_x = _expand_heads(small[L - 1:L, hp:2 * hp], lane_head)
        src_c = cum_c - jnp.log(dt_c)

        cb = _dot_nt(cm, bm)
        w_parts, x_parts = [], []
        for hg in range(hp):
            seg = cum_r[:, hg:hg + 1] - src_c[hg:hg + 1, :]
            w_parts.append(cb * jnp.exp(jnp.where(causal, seg, -jnp.inf)))
            x_parts.append(jnp.where(lane_head == hg, x, 0.0))
        y = _dot(jnp.concatenate(w_parts, axis=1).astype(BF16),
                 jnp.concatenate(x_parts, axis=0).astype(BF16))

        st = st_sc[gi]
        y = y + _dot(cm, st.astype(BF16)) * grow_x
        st_sc[gi] = last_x * st + _dot_tn(bm, (x * end_x).astype(BF16))
        y = y + x * dskip_ref[gi]

        g = y * _silu(z_ref[:, gi * GW:(gi + 1) * GW].astype(F32))
        g = g * lax.rsqrt(jnp.mean(g * g, axis=-1, keepdims=True) + RMS_EPS)
        o_ref[:, gi * GW:(gi + 1) * GW] = (g * nw_ref[gi]).astype(o_ref.dtype)


def _ssd(proj, dt_cols, conv_w, conv_b, dtb, alog, dskip_x, norm_w, nc):
    s = proj.shape[0]
    n = SSD_GROUPS_PER_STEP
    L, GW, NS = SSM_CHUNK, n * SSM_GROUP_W, n * SSM_STATE
    xoff, boff, coff = 0, SSM_INNER, SSM_INNER + SSM_GROUPS * SSM_STATE
    per_step = lambda a: pl.BlockSpec((n,) + a.shape[1:], lambda g, c: (g, 0, 0))
    return pl.pallas_call(
        _ssd_kernel,
        out_shape=jax.ShapeDtypeStruct((s, SSM_INNER), BF16),
        grid=(SSM_GROUPS // n, nc),
        in_specs=[
            pl.BlockSpec((L, GW), lambda g, c: (c, COL_Z // GW + g)),
            pl.BlockSpec((L, GW), lambda g, c: (c, COL_X // GW + g)),
            pl.BlockSpec((L, NS), lambda g, c: (c, COL_B // NS + g)),
            pl.BlockSpec((L, NS), lambda g, c: (c, COL_C // NS + g)),
            pl.BlockSpec((n, SSM_HPG, L), lambda g, c: (g, 0, c)),
            pl.BlockSpec((SSM_CONV, GW), lambda g, c: (0, xoff // GW + g)),
            pl.BlockSpec((SSM_CONV, NS), lambda g, c: (0, boff // NS + g)),
            pl.BlockSpec((SSM_CONV, NS), lambda g, c: (0, coff // NS + g)),
            pl.BlockSpec((1, GW), lambda g, c: (0, xoff // GW + g)),
            pl.BlockSpec((1, NS), lambda g, c: (0, boff // NS + g)),
            pl.BlockSpec((1, NS), lambda g, c: (0, coff // NS + g)),
            per_step(dtb), per_step(alog), per_step(dskip_x), per_step(norm_w),
        ],
        out_specs=pl.BlockSpec((L, GW), lambda g, c: (c, g)),
        scratch_shapes=[pltpu.VMEM((L + CONV_HALO, GW), F32), pltpu.VMEM((L + CONV_HALO, NS), F32),
                        pltpu.VMEM((L + CONV_HALO, NS), F32),
                        pltpu.VMEM((n, SSM_STATE, SSM_GROUP_W), F32)],
        compiler_params=_params(("parallel", "arbitrary")),
        name="ssd",
    )(proj, proj, proj, proj, dt_cols, conv_w, conv_w, conv_w, conv_b, conv_b, conv_b,
      dtb, alog, dskip_x, norm_w)


def _merge_kernel(h_ref, a_ref, b_ref, ga_ref, gb_ref, pa_ref, pb_ref, wo_ref, gate_ref, gpost_ref,
                  o_ref):
    ya = _dot(a_ref[...], pa_ref[...])
    yb = _dot(b_ref[...], pb_ref[...])
    mix = _sigmoid(ga_ref[...].astype(F32)) * ya + _sigmoid(gb_ref[...].astype(F32)) * yb
    y = _dot(mix.astype(BF16), wo_ref[...])
    o_ref[...] = h_ref[...] + gate_ref[...] * _rms(y, gpost_ref[...])


def _merge(h, attn, ssd, proj, pa, pb, wo, gate, gpost, tm):
    s = h.shape[0]
    vec = pl.BlockSpec((1, D_MODEL), lambda i: (0, 0))
    full = lambda a: pl.BlockSpec(a.shape, lambda i: (0, 0))
    return pl.pallas_call(
        _merge_kernel,
        out_shape=jax.ShapeDtypeStruct((s, D_MODEL), F32),
        grid=(s // tm,),
        in_specs=[pl.BlockSpec((tm, D_MODEL), lambda i: (i, 0)),
                  pl.BlockSpec((tm, ATTN_WIDTH), lambda i: (i, 0)),
                  pl.BlockSpec((tm, SSM_INNER), lambda i: (i, 0)),
                  pl.BlockSpec((tm, D_MODEL), lambda i: (i, COL_GA // D_MODEL)),
                  pl.BlockSpec((tm, D_MODEL), lambda i: (i, COL_GB // D_MODEL)),
                  full(pa), full(pb), full(wo), vec, vec],
        out_specs=pl.BlockSpec((tm, D_MODEL), lambda i: (i, 0)),
        compiler_params=_params(("parallel",)),
        name="merge",
    )(h, attn, ssd, proj, proj, pa, pb, wo, gate, gpost)


def _layer(h, mod, rel_bias, p):
    s = h.shape[0]
    nc = s // SSM_CHUNK
    tm = min(512, s)
    tm_wide = min(1024, s)
    sh1, sc1, g1, shm, scm, gm, sh2, sc2, g2 = [mod[k] for k in range(N_MOD)]
    vec = lambda a: a.reshape(1, -1)

    h = _ffn(h, vec(p["ffn1_norm_pre"]), sh1, sc1, g1, vec(p["ffn1_norm_post"]),
             p["ffn1_w_in"].astype(BF16), p["ffn1_w_out"].astype(BF16), tm=tm_wide)

    w_in = p["w_in_mix"].astype(BF16)
    dt_lo = COL_GA
    w_gates = w_in[:, dt_lo + SSM_HEADS:]
    w_dt = jnp.pad(w_in[:, dt_lo:dt_lo + SSM_HEADS], ((0, 0), (0, DT_PAD - SSM_HEADS)))
    proj, dt_raw = _inproj(h, vec(p["mix_norm_pre"]), shm, scm, w_in, w_gates, w_dt, tm=min(2048, s))

    tab_flat = rel_bias.T.reshape(-1)
    qt, ka, vt = _prep(proj)
    tiles = _bias_tiles(tab_flat)
    attn = _attn(qt, ka, vt, tiles)

    dt_cols = dt_raw[:, :SSM_HEADS].reshape(s, SSM_GROUPS, SSM_HPG).transpose(1, 2, 0)
    per_group = lambda a: a.reshape(SSM_GROUPS, SSM_HPG, 1)
    dskip_x = jnp.repeat(p["d_skip"], SSM_HEAD_DIM).reshape(SSM_GROUPS, 1, SSM_GROUP_W)
    ssd = _ssd(proj, dt_cols, p["conv_w"], vec(p["conv_b"]),
               per_group(p["dt_bias"]), per_group(p["a_log"]),
               dskip_x, p["ssm_norm_w"].reshape(SSM_GROUPS, 1, SSM_GROUP_W), nc)

    h = _merge(h, attn, ssd, proj, p["proj_a"].astype(BF16), p["proj_b"].astype(BF16),
               p["w_out_mix"].astype(BF16), gm, vec(p["mix_norm_post"]), tm=tm)

    h = _ffn(h, vec(p["ffn2_norm_pre"]), sh2, sc2, g2, vec(p["ffn2_norm_post"]),
             p["ffn2_w_in"].astype(BF16), p["ffn2_w_out"].astype(BF16), tm=tm_wide)
    return h


_LAYER_KEYS = ("ffn1_norm_pre", "ffn1_w_in", "ffn1_w_out", "ffn1_norm_post", "mix_norm_pre",
               "w_in_mix", "conv_w", "conv_b", "dt_bias", "a_log", "d_skip", "ssm_norm_w",
               "proj_a", "proj_b", "w_out_mix", "mix_norm_post",
               "ffn2_norm_pre", "ffn2_w_in", "ffn2_w_out", "ffn2_norm_post")


def kernel(x, c, w_ada, b_ada, ffn1_norm_pre, ffn1_w_in, ffn1_w_out, ffn1_norm_post, mix_norm_pre,
           w_in_mix, rel_bias, conv_w, conv_b, dt_bias, a_log, d_skip, ssm_norm_w, proj_a, proj_b,
           w_out_mix, mix_norm_post, ffn2_norm_pre, ffn2_w_in, ffn2_w_out, ffn2_norm_post):
    stacked = dict(ffn1_norm_pre=ffn1_norm_pre, ffn1_w_in=ffn1_w_in, ffn1_w_out=ffn1_w_out,
                   ffn1_norm_post=ffn1_norm_post, mix_norm_pre=mix_norm_pre, w_in_mix=w_in_mix,
                   conv_w=conv_w, conv_b=conv_b, dt_bias=dt_bias, a_log=a_log, d_skip=d_skip,
                   ssm_norm_w=ssm_norm_w, proj_a=proj_a, proj_b=proj_b, w_out_mix=w_out_mix,
                   mix_norm_post=mix_norm_post, ffn2_norm_pre=ffn2_norm_pre, ffn2_w_in=ffn2_w_in,
                   ffn2_w_out=ffn2_w_out, ffn2_norm_post=ffn2_norm_post)
    batch, seq, _ = x.shape
    assert seq % ATTN_TILE == 0 and seq // MOBA_BLOCK <= MAX_BLOCKS and seq % SSM_CHUNK == 0
    depth = w_ada.shape[0]
    outs = []
    for b in range(batch):
        h = x[b]
        for l in range(depth):
            mod = _mod(c[b:b + 1], w_ada[l], b_ada[l])
            h = _layer(h, mod, rel_bias, {k: stacked[k][l] for k in _LAYER_KEYS})
        outs.append(h)
    return outs[0][None] if batch == 1 else jnp.stack(outs)
```

```python
import functools
import math

import jax
import jax.numpy as jnp
from jax import lax
from jax.experimental import pallas as pl
from jax.experimental.pallas import tpu as pltpu

F32 = jnp.float32
BF16 = jnp.bfloat16
HIGHEST = lax.Precision.HIGHEST

D_MODEL = 1024
N_MOD = 9
RMS_EPS = 1e-6
FFN_HIDDEN = 2816
FFN_RES = 0.5
FFN_CHUNK = FFN_HIDDEN // 2

ATTN_HEADS = 8
HEAD_DIM = 128
ATTN_WIDTH = ATTN_HEADS * HEAD_DIM
MOBA_BLOCK = 256
MOBA_TOPK = 3
MAX_BLOCKS = 128
AUG_DIM = HEAD_DIM + MAX_BLOCKS
REL_BUCKETS = 32
REL_MAX_DIST = 128
MASKED = -1e30
LOG2E = math.log2(math.e)
ATTN_TILE = 512
BLOCKS_PER_TILE = ATTN_TILE // MOBA_BLOCK
BF16_SUBLANES = 16
V_ROWS = HEAD_DIM + BF16_SUBLANES
ATTN_TILES_PER_STEP = 1
FAR_UNROLL = 8
PREP_TILES_PER_STEP = 8

SSM_INNER = 2048
SSM_HEAD_DIM = 64
SSM_GROUPS = 8
SSM_HEADS = SSM_INNER // SSM_HEAD_DIM
SSM_HPG = SSM_HEADS // SSM_GROUPS
SSM_GROUP_W = SSM_INNER // SSM_GROUPS
SSM_STATE = 128
SSM_CONV = 4
SSM_CHUNK = 256
CONV_HALO = 8
SMALL_ROWS = 128
SSD_GROUPS_PER_STEP = 2

COL_Q = 0
COL_K = COL_Q + ATTN_WIDTH
COL_V = COL_K + ATTN_WIDTH
COL_Z = COL_V + ATTN_WIDTH
COL_X = COL_Z + SSM_INNER
COL_B = COL_X + SSM_INNER
COL_C = COL_B + SSM_GROUPS * SSM_STATE
COL_GA = COL_C + SSM_GROUPS * SSM_STATE
COL_GB = COL_GA + D_MODEL
PROJ_W = COL_GB + D_MODEL
PROJ_TILE = 1024
MAIN_TILES = COL_GA // PROJ_TILE
DT_PAD = 128

VMEM_LIMIT = 56 * 1024 * 1024


def _params(sem):
    return pltpu.CompilerParams(dimension_semantics=sem, vmem_limit_bytes=VMEM_LIMIT)


def _sigmoid(x):
    return 0.5 + 0.5 * jnp.tanh(0.5 * x)


def _silu(x):
    return x * _sigmoid(x)


def _softplus(x):
    return jnp.maximum(x, 0.0) + jnp.log(1.0 + jnp.exp(-jnp.abs(x)))


def _rms(x, g):
    return x * lax.rsqrt(jnp.mean(x * x, axis=-1, keepdims=True) + RMS_EPS) * g


def _dot(a, b, **kw):
    return jnp.dot(a, b, preferred_element_type=F32, **kw)


def _dot_nt(a, b, **kw):
    return lax.dot_general(a, b, (((1,), (1,)), ((), ())), preferred_element_type=F32, **kw)


def _dot_tn(a, b, **kw):
    return lax.dot_general(a, b, (((0,), (0,)), ((), ())), preferred_element_type=F32, **kw)


def _mod_kernel(c_ref, w_ref, b_ref, o_ref):
    cs = _silu(c_ref[...])
    o_ref[...] = _dot(cs, w_ref[...], precision=HIGHEST) + b_ref[...]


def _mod(c, w_ada, b_ada):
    n = w_ada.shape[1]
    tn = 1024
    c8 = jnp.broadcast_to(c, (8, D_MODEL))
    out = pl.pallas_call(
        _mod_kernel,
        out_shape=jax.ShapeDtypeStruct((8, n), F32),
        grid=(n // tn,),
        in_specs=[pl.BlockSpec((8, D_MODEL), lambda j: (0, 0)),
                  pl.BlockSpec((D_MODEL, tn), lambda j: (0, j)),
                  pl.BlockSpec((1, tn), lambda j: (0, j))],
        out_specs=pl.BlockSpec((8, tn), lambda j: (0, j)),
        compiler_params=_params(("arbitrary",)),
        name="mod",
    )(c8, w_ada, b_ada.reshape(1, n))
    return out[0].reshape(N_MOD, 1, D_MODEL)


def _ffn_kernel(h_ref, gpre_ref, sh_ref, sc_ref, gate_ref, gpost_ref, wi_ref, wo_ref, o_ref):
    h = h_ref[...]
    u = (_rms(h, gpre_ref[...]) * (1.0 + sc_ref[...]) + sh_ref[...]).astype(BF16)
    acc = None
    for lo in range(0, FFN_HIDDEN, FFN_CHUNK):
        a = _dot(u, wi_ref[:, lo:lo + FFN_CHUNK])
        b = _dot(u, wi_ref[:, FFN_HIDDEN + lo:FFN_HIDDEN + lo + FFN_CHUNK])
        part = _dot((_silu(a) * b).astype(BF16), wo_ref[lo:lo + FFN_CHUNK, :])
        acc = part if acc is None else acc + part
    o_ref[...] = h + (FFN_RES * gate_ref[...]) * _rms(acc, gpost_ref[...])


def _ffn(h, gpre, sh, sc, gate, gpost, wi, wo, tm):
    s = h.shape[0]
    row = lambda i: (i, 0)
    vec = pl.BlockSpec((1, D_MODEL), lambda i: (0, 0))
    resident = lambda a: pl.BlockSpec(a.shape, lambda i: (0, 0), pipeline_mode=pl.Buffered(1))
    return pl.pallas_call(
        _ffn_kernel,
        out_shape=jax.ShapeDtypeStruct((s, D_MODEL), F32),
        grid=(s // tm,),
        in_specs=[pl.BlockSpec((tm, D_MODEL), row), vec, vec, vec, vec, vec,
                  resident(wi), resident(wo)],
        out_specs=pl.BlockSpec((tm, D_MODEL), row),
        compiler_params=_params(("parallel",)),
        name="ffn",
    )(h, gpre, sh, sc, gate, gpost, wi, wo)


def _inproj_kernel(h_ref, gpre_ref, sh_ref, sc_ref, w_ref, wg_ref, wdt_ref, o_ref, dt_ref, u_sc):
    j = pl.program_id(1)

    @pl.when(j == 0)
    def _():
        u = (_rms(h_ref[...], gpre_ref[...]) * (1.0 + sc_ref[...]) + sh_ref[...]).astype(BF16)
        u_sc[...] = u
        dt_ref[...] = _dot(u, wdt_ref[...])
        o_ref[...] = _dot(u, w_ref[...]).astype(o_ref.dtype)

    @pl.when(jnp.logical_and(j > 0, j < MAIN_TILES))
    def _():
        o_ref[...] = _dot(u_sc[...], w_ref[...]).astype(o_ref.dtype)

    @pl.when(j >= MAIN_TILES)
    def _():
        o_ref[...] = _dot(u_sc[...], wg_ref[...]).astype(o_ref.dtype)


def _inproj(h, gpre, sh, sc, w, wg, wdt, tm):
    s = h.shape[0]
    tn = PROJ_TILE
    vec = pl.BlockSpec((1, D_MODEL), lambda i, j: (0, 0))
    return pl.pallas_call(
        _inproj_kernel,
        out_shape=(jax.ShapeDtypeStruct((s, PROJ_W), BF16),
                   jax.ShapeDtypeStruct((s, DT_PAD), F32)),
        grid=(s // tm, PROJ_W // tn),
        in_specs=[pl.BlockSpec((tm, D_MODEL), lambda i, j: (i, 0)), vec, vec, vec,
                  pl.BlockSpec((D_MODEL, tn), lambda i, j: (0, jnp.minimum(j, MAIN_TILES - 1))),
                  pl.BlockSpec((D_MODEL, tn), lambda i, j: (0, jnp.maximum(j - MAIN_TILES, 0))),
                  pl.BlockSpec((D_MODEL, DT_PAD), lambda i, j: (0, 0))],
        out_specs=(pl.BlockSpec((tm, tn), lambda i, j: (i, j)),
                   pl.BlockSpec((tm, DT_PAD), lambda i, j: (i, 0))),
        scratch_shapes=[pltpu.VMEM((tm, D_MODEL), BF16)],
        compiler_params=_params(("parallel", "arbitrary")),
        name="inproj",
    )(h, gpre, sh, sc, w, wg, wdt)


def _prep_kernel(n_sel, n_tiles, q_ref, k_ref, v_ref, qt_ref, ka_ref, vt_ref, km_sc):
    @pl.when(pl.program_id(1) == 0)
    def _():
        km_sc[...] = jnp.zeros_like(km_sc)

    for u in range(n_tiles):
        t = pl.program_id(1) * n_tiles + u
        rows = slice(u * ATTN_TILE, (u + 1) * ATTN_TILE)
        k = k_ref[rows, :].astype(F32)
        for b in range(BLOCKS_PER_TILE):
            km_sc[pl.ds(t * BLOCKS_PER_TILE + b, 1), :] = jnp.mean(
                k[b * MOBA_BLOCK:(b + 1) * MOBA_BLOCK], axis=0, keepdims=True)

        qt = (q_ref[rows, :].astype(F32) * (HEAD_DIM ** -0.5 * LOG2E)).T
        score = _dot(km_sc[:n_sel, :], qt, precision=HIGHEST)
        blk = lax.broadcasted_iota(jnp.int32, score.shape, 0)
        q_blk = t * BLOCKS_PER_TILE + lax.broadcasted_iota(jnp.int32, score.shape, 1) // MOBA_BLOCK
        s = jnp.where(blk < q_blk, score, -jnp.inf)
        pen = jnp.full(score.shape, MASKED, F32)
        for _ in range(MOBA_TOPK):
            m = jnp.max(s, axis=0, keepdims=True)
            first = jnp.min(jnp.where(s == m, blk, n_sel), axis=0, keepdims=True)
            first = jnp.where(m > -jnp.inf, first, n_sel)
            pick = blk == first
            pen = jnp.where(pick, 0.0, pen)
            s = jnp.where(pick, -jnp.inf, s)
        pen = jnp.where(blk == q_blk, 0.0, pen)
        qt_ref[0, u, :HEAD_DIM, :] = qt.astype(BF16)
        qt_ref[0, u, HEAD_DIM:HEAD_DIM + n_sel, :] = pen.astype(BF16)
        if n_sel < MAX_BLOCKS:
            qt_ref[0, u, HEAD_DIM + n_sel:, :] = jnp.zeros((MAX_BLOCKS - n_sel, ATTN_TILE), BF16)

        lane = lax.broadcasted_iota(jnp.int32, (ATTN_TILE, MAX_BLOCKS), 1)
        k_blk = t * BLOCKS_PER_TILE + lax.broadcasted_iota(jnp.int32, lane.shape, 0) // MOBA_BLOCK
        ka_ref[0, rows, :HEAD_DIM] = k.astype(BF16)
        ka_ref[0, rows, HEAD_DIM:] = jnp.where(lane == k_blk, 1.0, 0.0).astype(BF16)

        ones_row = lax.broadcasted_iota(jnp.int32, (V_ROWS - HEAD_DIM, ATTN_TILE), 0) == 0
        vt_ref[0, u, :HEAD_DIM, :] = v_ref[rows, :].astype(F32).T.astype(BF16)
        vt_ref[0, u, HEAD_DIM:, :] = jnp.where(ones_row, 1.0, 0.0).astype(BF16)


def _prep(proj):
    s = proj.shape[0]
    nt = s // ATTN_TILE
    n = PREP_TILES_PER_STEP if nt % PREP_TILES_PER_STEP == 0 else 1
    blk = lambda col: pl.BlockSpec((n * ATTN_TILE, HEAD_DIM), lambda h, t: (t, col // HEAD_DIM + h))
    n_sel = -(-(s // MOBA_BLOCK) // BF16_SUBLANES) * BF16_SUBLANES
    return pl.pallas_call(
        functools.partial(_prep_kernel, n_sel, n),
        out_shape=(jax.ShapeDtypeStruct((ATTN_HEADS, nt, AUG_DIM, ATTN_TILE), BF16),
                   jax.ShapeDtypeStruct((ATTN_HEADS, s, AUG_DIM), BF16),
                   jax.ShapeDtypeStruct((ATTN_HEADS, nt, V_ROWS, ATTN_TILE), BF16)),
        grid=(ATTN_HEADS, nt // n),
        in_specs=[blk(COL_Q), blk(COL_K), blk(COL_V)],
        out_specs=(pl.BlockSpec((1, n, AUG_DIM, ATTN_TILE), lambda h, t: (h, t, 0, 0)),
                   pl.BlockSpec((1, n * ATTN_TILE, AUG_DIM), lambda h, t: (h, t, 0)),
                   pl.BlockSpec((1, n, V_ROWS, ATTN_TILE), lambda h, t: (h, t, 0, 0))),
        scratch_shapes=[pltpu.VMEM((MAX_BLOCKS, HEAD_DIM), F32)],
        compiler_params=_params(("parallel", "arbitrary")),
        name="prep",
    )(proj, proj, proj)


def _t5_bucket(rel):
    n = jnp.maximum(rel, 0)
    max_exact = REL_BUCKETS // 2
    nf = jnp.maximum(n, 1).astype(F32)
    large = max_exact + (jnp.log(nf / max_exact) / math.log(REL_MAX_DIST / max_exact)
                         * (REL_BUCKETS - max_exact)).astype(jnp.int32)
    large = jnp.minimum(large, REL_BUCKETS - 1)
    return jnp.where(n < max_exact, n, large)


def _bias_kernel(tab_ref, o_ref):
    h = pl.program_id(0)
    shape = (ATTN_TILE, ATTN_TILE)
    far = tab_ref[h * REL_BUCKETS + REL_BUCKETS - 1]
    bucket = _t5_bucket(lax.broadcasted_iota(jnp.int32, (1, ATTN_TILE), 1))
    val = jnp.zeros((1, ATTN_TILE), F32)
    for b in range(REL_BUCKETS):
        val = jnp.where(bucket == b, tab_ref[h * REL_BUCKETS + b], val)
    val = (val - far) * LOG2E
    toeplitz = pltpu.roll(jnp.broadcast_to(val, shape), 0, 1, stride=1, stride_axis=0)
    ki = lax.broadcasted_iota(jnp.int32, shape, 0)
    qi = lax.broadcasted_iota(jnp.int32, shape, 1)
    o_ref[0, 0] = jnp.where(qi >= ki, toeplitz, MASKED)
    o_ref[0, 1] = jnp.where(qi < ki, toeplitz, 0.0)


def _bias_tiles(tab_flat):
    return pl.pallas_call(
        _bias_kernel,
        out_shape=jax.ShapeDtypeStruct((ATTN_HEADS, 2, ATTN_TILE, ATTN_TILE), F32),
        grid=(ATTN_HEADS,),
        in_specs=[pl.BlockSpec(memory_space=pltpu.SMEM)],
        out_specs=pl.BlockSpec((1, 2, ATTN_TILE, ATTN_TILE), lambda h: (h, 0, 0, 0)),
        compiler_params=_params(("parallel",)),
        name="bias",
    )(tab_flat)


def _attn_tile(t, qt, ka_ref, vt_ref, t_ref, m_sc, acc_sc, s_sc):
    def scores(j):
        start = pl.multiple_of(j * ATTN_TILE, ATTN_TILE)
        return _dot(ka_ref[0, pl.ds(start, ATTN_TILE), :], qt)

    m_sc[...] = jnp.full(m_sc.shape, 4.0 * MASKED, F32)
    acc_sc[...] = jnp.zeros_like(acc_sc)

    def colmax(s_ref):
        return jnp.max(s_ref[...], axis=0, keepdims=True)

    def consume(s_ref, s_max, j):
        m_old = m_sc[...]
        m_new = jnp.maximum(m_old, s_max)
        m_sc[...] = m_new
        p = jnp.exp2(s_ref[...] - m_new).astype(BF16)
        acc_sc[...] = jnp.exp2(m_old - m_new) * acc_sc[...] + _dot(vt_ref[0, j], p)

    def consume_near(s_prev, s_diag):
        m_old = m_sc[...]
        m_mid = jnp.maximum(m_old, colmax(s_prev))
        p_prev = jnp.exp2(s_prev[...] - m_mid).astype(BF16)
        acc = jnp.exp2(m_old - m_mid) * acc_sc[...] + _dot(vt_ref[0, t - 1], p_prev)
        m_new = jnp.maximum(m_mid, colmax(s_diag))
        m_sc[...] = m_new
        p_diag = jnp.exp2(s_diag[...] - m_new).astype(BF16)
        acc_sc[...] = jnp.exp2(m_mid - m_new) * acc + _dot(vt_ref[0, t], p_diag)

    def far_step(j, cur, nxt, max_cur):
        nxt[...] = scores(j + 1)
        consume(cur, max_cur, j)
        return colmax(nxt)

    s_a, s_b = s_sc.at[0], s_sc.at[1]

    @pl.when(t == 0)
    def _():
        s_a[...] = scores(0) + t_ref[0, 0]
        consume(s_a, colmax(s_a), 0)

    @pl.when(t >= 1)
    def _():
        n_far = t - 1
        s_a[...] = scores(0)

        bufs = (s_a, s_b)

        def steps(first, count, max_cur):
            for k in range(count):
                max_cur = far_step(first + k, bufs[k % 2], bufs[(k + 1) % 2], max_cur)
            return max_cur

        max_cur = lax.fori_loop(0, n_far // FAR_UNROLL,
                                lambda i, m: steps(FAR_UNROLL * i, FAR_UNROLL, m), colmax(s_a))
        done = n_far // FAR_UNROLL * FAR_UNROLL

        for rem in range(FAR_UNROLL):
            @pl.when(n_far - done == rem)
            def _(rem=rem):
                steps(done, rem, max_cur)
                prev, free = bufs[rem % 2], bufs[(rem + 1) % 2]
                free[...] = scores(t) + t_ref[0, 0]
                lo = ATTN_TILE - REL_MAX_DIST
                prev[lo:, :REL_MAX_DIST] = prev[lo:, :REL_MAX_DIST] + t_ref[0, 1, lo:, :REL_MAX_DIST]
                consume_near(prev, free)

    acc = acc_sc[...]
    out = acc[:HEAD_DIM] / acc[HEAD_DIM:HEAD_DIM + 1]
    return out.T


def _attn_kernel(n_tiles, qt_ref, ka_ref, vt_ref, t_ref, o_ref, *scratch):
    for u in range(n_tiles):
        out = _attn_tile(pl.program_id(1) * n_tiles + u, qt_ref[0, u], ka_ref, vt_ref, t_ref, *scratch)
        o_ref[u * ATTN_TILE:(u + 1) * ATTN_TILE, :] = out.astype(o_ref.dtype)


def _attn(qt, ka, vt, tiles):
    nt = qt.shape[1]
    s = nt * ATTN_TILE
    n = ATTN_TILES_PER_STEP if nt % ATTN_TILES_PER_STEP == 0 else 1
    return pl.pallas_call(
        functools.partial(_attn_kernel, n),
        out_shape=jax.ShapeDtypeStruct((s, ATTN_WIDTH), BF16),
        grid=(ATTN_HEADS, nt // n),
        in_specs=[pl.BlockSpec((1, n, AUG_DIM, ATTN_TILE), lambda h, t: (h, t, 0, 0)),
                  pl.BlockSpec((1, s, AUG_DIM), lambda h, t: (h, 0, 0)),
                  pl.BlockSpec((1, nt, V_ROWS, ATTN_TILE), lambda h, t: (h, 0, 0, 0)),
                  pl.BlockSpec((1, 2, ATTN_TILE, ATTN_TILE), lambda h, t: (h, 0, 0, 0))],
        out_specs=pl.BlockSpec((n * ATTN_TILE, HEAD_DIM), lambda h, t: (t, h)),
        scratch_shapes=[pltpu.VMEM((1, ATTN_TILE), F32), pltpu.VMEM((V_ROWS, ATTN_TILE), F32),
                        pltpu.VMEM((2, ATTN_TILE, ATTN_TILE), F32)],
        compiler_params=_params(("parallel", "arbitrary")),
        name="attn",
    )(qt, ka, vt, tiles)


def _expand_heads(d, lane_head):
    out = d[:, SSM_HPG - 1:SSM_HPG]
    for hg in range(SSM_HPG - 2, -1, -1):
        out = jnp.where(lane_head == hg, d[:, hg:hg + 1], out)
    return out


def _cumsum_lanes(x, triu):
    hi = x.astype(BF16)
    rest = x - hi.astype(F32)
    mid = rest.astype(BF16)
    lo = (rest - mid.astype(F32)).astype(BF16)
    return _dot(hi, triu) + _dot(mid, triu) + _dot(lo, triu)


def _ssd_kernel(z_ref, x_ref, b_ref, c_ref, dtc_ref, wx_ref, wb_ref, wc_ref, bx_ref, bb_ref, bc_ref,
                dtb_ref, alog_ref, dskip_ref, nw_ref, o_ref, xpx_sc, xpb_sc, xpc_sc, st_sc):
    c = pl.program_id(1)
    L, GW, NS, hp = SSM_CHUNK, SSM_GROUP_W, SSM_STATE, SSM_HPG

    @pl.when(c == 0)
    def _():
        xpx_sc[0:CONV_HALO, :] = jnp.zeros((CONV_HALO, xpx_sc.shape[1]), F32)
        xpb_sc[0:CONV_HALO, :] = jnp.zeros((CONV_HALO, xpb_sc.shape[1]), F32)
        xpc_sc[0:CONV_HALO, :] = jnp.zeros((CONV_HALO, xpc_sc.shape[1]), F32)
        st_sc[...] = jnp.zeros_like(st_sc)

    def conv(src_ref, pad_sc, w_ref, bias_ref):
        pad_sc[CONV_HALO:CONV_HALO + L, :] = src_ref[...].astype(F32)
        acc = bias_ref[...]
        for j in range(SSM_CONV):
            lo = CONV_HALO - (SSM_CONV - 1) + j
            acc = acc + pad_sc[lo:lo + L, :] * w_ref[j:j + 1, :]
        pad_sc[0:CONV_HALO, :] = pad_sc[L:L + CONV_HALO, :]
        return _silu(acc)

    x_all = conv(x_ref, xpx_sc, wx_ref, bx_ref)
    b_all = conv(b_ref, xpb_sc, wb_ref, bb_ref)
    c_all = conv(c_ref, xpc_sc, wc_ref, bc_ref)

    row = lax.broadcasted_iota(jnp.int32, (L, L), 0)
    col = lax.broadcasted_iota(jnp.int32, (L, L), 1)
    causal = row >= col
    triu = jnp.where(row <= col, 1.0, 0.0).astype(BF16)
    lane_head = lax.broadcasted_iota(jnp.int32, (1, GW), 1) // SSM_HEAD_DIM

    for gi in range(SSD_GROUPS_PER_STEP):
        x = x_all[:, gi * GW:(gi + 1) * GW]
        bm = b_all[:, gi * NS:(gi + 1) * NS].astype(BF16)
        cm = c_all[:, gi * NS:(gi + 1) * NS].astype(BF16)

        dt_c = _softplus(dtc_ref[gi] + dtb_ref[gi])
        a_c = -jnp.exp(alog_ref[gi])
        cum_c = _cumsum_lanes(dt_c * a_c, triu)
        last_c = cum_c[:, L - 1:L]
        small = jnp.concatenate(
            [cum_c, jnp.exp(cum_c), jnp.exp(last_c - cum_c) * dt_c,
             jnp.zeros((SMALL_ROWS - 3 * hp, L), F32)], axis=0).T
        cum_r = small[:, 0:hp]
        grow_x = _expand_heads(small[:, hp:2 * hp], lane_head)
        end_x = _expand_heads(small[:, 2 * hp:3 * hp], lane_head)
        last_x = _expand_heads(small[L - 1:L, hp:2 * hp], lane_head)
        src_c = cum_c - jnp.log(dt_c)

        cb = _dot_nt(cm, bm)
        w_parts, x_parts = [], []
        for hg in range(hp):
            seg = cum_r[:, hg:hg + 1] - src_c[hg:hg + 1, :]
            w_parts.append(cb * jnp.exp(jnp.where(causal, seg, -jnp.inf)))
            x_parts.append(jnp.where(lane_head == hg, x, 0.0))
        y = _dot(jnp.concatenate(w_parts, axis=1).astype(BF16),
                 jnp.concatenate(x_parts, axis=0).astype(BF16))

        st = st_sc[gi]
        y = y + _dot(cm, st.astype(BF16)) * grow_x
        st_sc[gi] = last_x * st + _dot_tn(bm, (x * end_x).astype(BF16))
        y = y + x * dskip_ref[gi]

        g = y * _silu(z_ref[:, gi * GW:(gi + 1) * GW].astype(F32))
        g = g * lax.rsqrt(jnp.mean(g * g, axis=-1, keepdims=True) + RMS_EPS)
        o_ref[:, gi * GW:(gi + 1) * GW] = (g * nw_ref[gi]).astype(o_ref.dtype)


def _ssd(proj, dt_cols, conv_w, conv_b, dtb, alog, dskip_x, norm_w, nc):
    s = proj.shape[0]
    n = SSD_GROUPS_PER_STEP
    L, GW, NS = SSM_CHUNK, n * SSM_GROUP_W, n * SSM_STATE
    xoff, boff, coff = 0, SSM_INNER, SSM_INNER + SSM_GROUPS * SSM_STATE
    per_step = lambda a: pl.BlockSpec((n,) + a.shape[1:], lambda g, c: (g, 0, 0))
    return pl.pallas_call(
        _ssd_kernel,
        out_shape=jax.ShapeDtypeStruct((s, SSM_INNER), BF16),
        grid=(SSM_GROUPS // n, nc),
        in_specs=[
            pl.BlockSpec((L, GW), lambda g, c: (c, COL_Z // GW + g)),
            pl.BlockSpec((L, GW), lambda g, c: (c, COL_X // GW + g)),
            pl.BlockSpec((L, NS), lambda g, c: (c, COL_B // NS + g)),
            pl.BlockSpec((L, NS), lambda g, c: (c, COL_C // NS + g)),
            pl.BlockSpec((n, SSM_HPG, L), lambda g, c: (g, 0, c)),
            pl.BlockSpec((SSM_CONV, GW), lambda g, c: (0, xoff // GW + g)),
            pl.BlockSpec((SSM_CONV, NS), lambda g, c: (0, boff // NS + g)),
            pl.BlockSpec((SSM_CONV, NS), lambda g, c: (0, coff // NS + g)),
            pl.BlockSpec((1, GW), lambda g, c: (0, xoff // GW + g)),
            pl.BlockSpec((1, NS), lambda g, c: (0, boff // NS + g)),
            pl.BlockSpec((1, NS), lambda g, c: (0, coff // NS + g)),
            per_step(dtb), per_step(alog), per_step(dskip_x), per_step(norm_w),
        ],
        out_specs=pl.BlockSpec((L, GW), lambda g, c: (c, g)),
        scratch_shapes=[pltpu.VMEM((L + CONV_HALO, GW), F32), pltpu.VMEM((L + CONV_HALO, NS), F32),
                        pltpu.VMEM((L + CONV_HALO, NS), F32),
                        pltpu.VMEM((n, SSM_STATE, SSM_GROUP_W), F32)],
        compiler_params=_params(("parallel", "arbitrary")),
        name="ssd",
    )(proj, proj, proj, proj, dt_cols, conv_w, conv_w, conv_w, conv_b, conv_b, conv_b,
      dtb, alog, dskip_x, norm_w)


def _merge_kernel(h_ref, a_ref, b_ref, ga_ref, gb_ref, pa_ref, pb_ref, wo_ref, gate_ref, gpost_ref,
                  o_ref):
    ya = _dot(a_ref[...], pa_ref[...])
    yb = _dot(b_ref[...], pb_ref[...])
    mix = _sigmoid(ga_ref[...].astype(F32)) * ya + _sigmoid(gb_ref[...].astype(F32)) * yb
    y = _dot(mix.astype(BF16), wo_ref[...])
    o_ref[...] = h_ref[...] + gate_ref[...] * _rms(y, gpost_ref[...])


def _merge(h, attn, ssd, proj, pa, pb, wo, gate, gpost, tm):
    s = h.shape[0]
    vec = pl.BlockSpec((1, D_MODEL), lambda i: (0, 0))
    full = lambda a: pl.BlockSpec(a.shape, lambda i: (0, 0))
    return pl.pallas_call(
        _merge_kernel,
        out_shape=jax.ShapeDtypeStruct((s, D_MODEL), F32),
        grid=(s // tm,),
        in_specs=[pl.BlockSpec((tm, D_MODEL), lambda i: (i, 0)),
                  pl.BlockSpec((tm, ATTN_WIDTH), lambda i: (i, 0)),
                  pl.BlockSpec((tm, SSM_INNER), lambda i: (i, 0)),
                  pl.BlockSpec((tm, D_MODEL), lambda i: (i, COL_GA // D_MODEL)),
                  pl.BlockSpec((tm, D_MODEL), lambda i: (i, COL_GB // D_MODEL)),
                  full(pa), full(pb), full(wo), vec, vec],
        out_specs=pl.BlockSpec((tm, D_MODEL), lambda i: (i, 0)),
        compiler_params=_params(("parallel",)),
        name="merge",
    )(h, attn, ssd, proj, proj, pa, pb, wo, gate, gpost)


def _layer(h, mod, rel_bias, p):
    s = h.shape[0]
    nc = s // SSM_CHUNK
    tm = min(512, s)
    tm_wide = min(1024, s)
    sh1, sc1, g1, shm, scm, gm, sh2, sc2, g2 = [mod[k] for k in range(N_MOD)]
    vec = lambda a: a.reshape(1, -1)

    h = _ffn(h, vec(p["ffn1_norm_pre"]), sh1, sc1, g1, vec(p["ffn1_norm_post"]),
             p["ffn1_w_in"].astype(BF16), p["ffn1_w_out"].astype(BF16), tm=tm_wide)

    w_in = p["w_in_mix"].astype(BF16)
    dt_lo = COL_GA
    w_gates = w_in[:, dt_lo + SSM_HEADS:]
    w_dt = jnp.pad(w_in[:, dt_lo:dt_lo + SSM_HEADS], ((0, 0), (0, DT_PAD - SSM_HEADS)))
    proj, dt_raw = _inproj(h, vec(p["mix_norm_pre"]), shm, scm, w_in, w_gates, w_dt, tm=min(2048, s))

    tab_flat = rel_bias.T.reshape(-1)
    qt, ka, vt = _prep(proj)
    tiles = _bias_tiles(tab_flat)
    attn = _attn(qt, ka, vt, tiles)

    dt_cols = dt_raw[:, :SSM_HEADS].reshape(s, SSM_GROUPS, SSM_HPG).transpose(1, 2, 0)
    per_group = lambda a: a.reshape(SSM_GROUPS, SSM_HPG, 1)
    dskip_x = jnp.repeat(p["d_skip"], SSM_HEAD_DIM).reshape(SSM_GROUPS, 1, SSM_GROUP_W)
    ssd = _ssd(proj, dt_cols, p["conv_w"], vec(p["conv_b"]),
               per_group(p["dt_bias"]), per_group(p["a_log"]),
               dskip_x, p["ssm_norm_w"].reshape(SSM_GROUPS, 1, SSM_GROUP_W), nc)

    h = _merge(h, attn, ssd, proj, p["proj_a"].astype(BF16), p["proj_b"].astype(BF16),
               p["w_out_mix"].astype(BF16), gm, vec(p["mix_norm_post"]), tm=tm)

    h = _ffn(h, vec(p["ffn2_norm_pre"]), sh2, sc2, g2, vec(p["ffn2_norm_post"]),
             p["ffn2_w_in"].astype(BF16), p["ffn2_w_out"].astype(BF16), tm=tm_wide)
    return h


_LAYER_KEYS = ("ffn1_norm_pre", "ffn1_w_in", "ffn1_w_out", "ffn1_norm_post", "mix_norm_pre",
               "w_in_mix", "conv_w", "conv_b", "dt_bias", "a_log", "d_skip", "ssm_norm_w",
               "proj_a", "proj_b", "w_out_mix", "mix_norm_post",
               "ffn2_norm_pre", "ffn2_w_in", "ffn2_w_out", "ffn2_norm_post")


def kernel(x, c, w_ada, b_ada, ffn1_norm_pre, ffn1_w_in, ffn1_w_out, ffn1_norm_post, mix_norm_pre,
           w_in_mix, rel_bias, conv_w, conv_b, dt_bias, a_log, d_skip, ssm_norm_w, proj_a, proj_b,
           w_out_mix, mix_norm_post, ffn2_norm_pre, ffn2_w_in, ffn2_w_out, ffn2_norm_post):
    stacked = dict(ffn1_norm_pre=ffn1_norm_pre, ffn1_w_in=ffn1_w_in, ffn1_w_out=ffn1_w_out,
                   ffn1_norm_post=ffn1_norm_post, mix_norm_pre=mix_norm_pre, w_in_mix=w_in_mix,
                   conv_w=conv_w, conv_b=conv_b, dt_bias=dt_bias, a_log=a_log, d_skip=d_skip,
                   ssm_norm_w=ssm_norm_w, proj_a=proj_a, proj_b=proj_b, w_out_mix=w_out_mix,
                   mix_norm_post=mix_norm_post, ffn2_norm_pre=ffn2_norm_pre, ffn2_w_in=ffn2_w_in,
                   ffn2_w_out=ffn2_w_out, ffn2_norm_post=ffn2_norm_post)
    batch, seq, _ = x.shape
    assert seq % ATTN_TILE == 0 and seq // MOBA_BLOCK <= MAX_BLOCKS and seq % SSM_CHUNK == 0
    depth = w_ada.shape[0]
    outs = []
    for b in range(batch):
        h = x[b]
        for l in range(depth):
            mod = _mod(c[b:b + 1], w_ada[l], b_ada[l])
            h = _layer(h, mod, rel_bias, {k: stacked[k][l] for k in _LAYER_KEYS})
        outs.append(h)
    return outs[0][None] if batch == 1 else jnp.stack(outs)
```
